```python
import math
import jax, jax.numpy as jnp
from jax import lax
import numpy as np

D_MODEL = 1024
BATCH = 4
SEQ = 4096
DEPTH = 4

N_EVEN = (DEPTH + 1) // 2
N_ODD = DEPTH // 2

POOL_WINDOWS = (2, 4, 8, 16)
POOL_GROUPS = 4
POOL_WIDTH = D_MODEL // 2
POOL_GDIM = POOL_WIDTH // POOL_GROUPS
SGU_GROUPS = 4
SGU_WIDTH = D_MODEL // 2
SGU_GDIM = SGU_WIDTH // SGU_GROUPS
SGU_CHUNK = 128
EVEN_IN = POOL_WIDTH + 2 * SGU_WIDTH
EVEN_MIX = POOL_WIDTH + SGU_WIDTH
NSA_HEADS = 8
NSA_KV_GROUPS = 2
NSA_HPG = NSA_HEADS // NSA_KV_GROUPS
NSA_HEAD_DIM = 64
NSA_WIDTH = NSA_HEADS * NSA_HEAD_DIM
NSA_KV_WIDTH = NSA_KV_GROUPS * NSA_HEAD_DIM
CMP_BLOCK = 32
CMP_STRIDE = 16
CMP_HIDDEN = 128
SEL_BLOCK = 64
SEL_TOPK = 16
WINDOW = 512
Q_BLOCK = 128
FORCE_SCORE = 1.0e4
ROPE_THETA = 500000.0
ROPE_DIM = NSA_HEAD_DIM // 4
RET_HEADS = 4
RET_HEAD_DIM = 128
RET_WIDTH = RET_HEADS * RET_HEAD_DIM
RET_CHUNK = 128
RET_THETA = 10000.0
ODD_IN = NSA_WIDTH + 6 * NSA_KV_WIDTH + 3 * NSA_HEADS + 4 * RET_WIDTH
ODD_MIX = NSA_WIDTH + RET_WIDTH
FFN_HIDDEN = -(-8 * D_MODEL // (3 * 256)) * 256
NEG_INF = -1.0e30

kernel_name = "hybrid_pool_sgu_nsa_retention_trunk"


def rms_norm(x, g, eps=1e-6):
    xf = x.astype(jnp.float32)
    y = xf * lax.rsqrt(jnp.mean(xf * xf, axis=-1, keepdims=True) + eps)
    return (y * g.astype(jnp.float32)).astype(x.dtype)


def layer_norm(x, g, b, eps=1e-5):
    xf = x.astype(jnp.float32)
    mu = jnp.mean(xf, axis=-1, keepdims=True)
    var = jnp.mean(jnp.square(xf - mu), axis=-1, keepdims=True)
    return ((xf - mu) * lax.rsqrt(var + eps) * g.astype(jnp.float32) + b.astype(jnp.float32)).astype(x.dtype)


def masked_softmax(s, mask):
    return jax.nn.softmax(jnp.where(mask, s.astype(jnp.float32), NEG_INF), axis=-1)


def rope_tables(positions, rot_dim, theta):
    inv_freq = 1.0 / (theta ** (jnp.arange(0, rot_dim, 2, dtype=jnp.float32) / rot_dim))
    ang = positions.astype(jnp.float32)[..., None] * inv_freq
    return jnp.cos(ang)[:, None], jnp.sin(ang)[:, None]


def apply_rope(x, cos, sin):
    half = cos.shape[-1]
    x1, x2, rest = x[..., :half], x[..., half:2 * half], x[..., 2 * half:]
    c, s = cos.astype(x.dtype), sin.astype(x.dtype)
    return jnp.concatenate([x1 * c - x2 * s, x2 * c + x1 * s, rest], axis=-1)


def to_heads(x, n):
    b, t, _ = x.shape
    return x.reshape(b, t, n, -1).transpose(0, 2, 1, 3)


def swiglu(h, w_gate, w_up, w_down):
    return (jax.nn.silu(h @ w_gate) * (h @ w_up)) @ w_down


def pool_mixer(a, pool_w, pool_scale):
    b, t_len, _ = a.shape
    af = a.astype(jnp.float32).reshape(b, t_len, POOL_GROUPS, POOL_GDIM)
    cs = jnp.concatenate([jnp.zeros_like(af[:, :1]), jnp.cumsum(af, axis=1)], axis=1)
    t = jnp.arange(t_len)
    pooled = []
    for gi, w in enumerate(POOL_WINDOWS):
        lo = jnp.maximum(t + 1 - w, 0)
        win_sum = cs[:, 1:, gi] - cs[:, lo, gi]
        pooled.append(win_sum / (t + 1 - lo).astype(jnp.float32)[None, :, None])
    pooled = jnp.stack(pooled, axis=2)
    diff = (pooled - af).astype(a.dtype)
    y = jnp.einsum('btgc,gcd->btgd', diff, pool_w)
    return y.reshape(b, t_len, POOL_WIDTH) * pool_scale


def sgu_mixer(u, v, ln_g, ln_b, w_s, b_s):
    b, t_len, _ = u.shape
    u = jax.nn.gelu(u)
    v = layer_norm(jax.nn.gelu(v), ln_g, ln_b)
    n_chunk = t_len // SGU_CHUNK
    vc = v.reshape(b, n_chunk, SGU_CHUNK, SGU_GROUPS, SGU_GDIM)
    causal = jnp.tril(jnp.ones((SGU_CHUNK, SGU_CHUNK), dtype=bool))
    w_masked = jnp.where(causal[None], w_s, 0.0).astype(v.dtype)
    mixed = jnp.einsum('gts,bcsgd->bctgd', w_masked, vc) + b_s.T[None, None, :, :, None]
    return u * mixed.reshape(b, t_len, SGU_WIDTH)


def even_mixer(h, w_in, pool_w, pool_scale, sgu_ln_g, sgu_ln_b, sgu_w, sgu_b, w_out):
    z = h @ w_in
    a = z[..., :POOL_WIDTH]
    u = z[..., POOL_WIDTH:POOL_WIDTH + SGU_WIDTH]
    v = z[..., POOL_WIDTH + SGU_WIDTH:]
    y_a = pool_mixer(a, pool_w, pool_scale)
    y_b = sgu_mixer(u, v, sgu_ln_g, sgu_ln_b, sgu_w, sgu_b)
    return jnp.concatenate([y_a, y_b], axis=-1) @ w_out


def compress_blocks(k, pos_emb, w1, w2):
    t_len = k.shape[2]
    n_cmp = (t_len - CMP_BLOCK) // CMP_STRIDE + 1
    idx = np.arange(n_cmp)[:, None] * CMP_STRIDE + np.arange(CMP_BLOCK)[None, :]
    blocks = k[:, :, idx] + pos_emb
    flat = blocks.reshape(blocks.shape[:3] + (CMP_BLOCK * NSA_HEAD_DIM,))
    return jax.nn.gelu(flat @ w1) @ w2


def nsa_attention(q, k_cmp, v_cmp, k_sel, v_sel, k_win, v_win, gates, positions,
                  cmp_k_pos, cmp_k_w1, cmp_k_w2, cmp_v_pos, cmp_v_w1, cmp_v_w2):
    b, t_len, _ = q.shape
    G, M, d = NSA_KV_GROUPS, NSA_HPG, NSA_HEAD_DIM
    cos, sin = rope_tables(positions, ROPE_DIM, ROPE_THETA)
    q = apply_rope(to_heads(q, NSA_HEADS), cos, sin) * (d ** -0.5)
    k_cmp = apply_rope(to_heads(k_cmp, G), cos, sin)
    k_sel = apply_rope(to_heads(k_sel, G), cos, sin)
    k_win = apply_rope(to_heads(k_win, G), cos, sin)
    v_cmp, v_sel, v_win = to_heads(v_cmp, G), to_heads(v_sel, G), to_heads(v_win, G)

    kc = compress_blocks(k_cmp, cmp_k_pos, cmp_k_w1, cmp_k_w2)
    vc = compress_blocks(v_cmp, cmp_v_pos, cmp_v_w1, cmp_v_w2)
    n_cmp = kc.shape[2]
    cmp_end = jnp.arange(n_cmp) * CMP_STRIDE + CMP_BLOCK - 1

    n_sel = t_len // SEL_BLOCK
    top_k = min(SEL_TOPK, n_sel)
    c_start = np.arange(n_cmp) * CMP_STRIDE
    s_start = np.arange(n_sel) * SEL_BLOCK
    cmp_to_sel = ((c_start[:, None] < s_start[None, :] + SEL_BLOCK)
                  & (c_start[:, None] + CMP_BLOCK > s_start[None, :])).astype(np.float32)
    ks_blk = k_sel.reshape(b, G, n_sel, SEL_BLOCK, d)
    vs_blk = v_sel.reshape(b, G, n_sel, SEL_BLOCK, d)
    kw_pad = jnp.pad(k_win, ((0, 0), (0, 0), (WINDOW, 0), (0, 0)))
    vw_pad = jnp.pad(v_win, ((0, 0), (0, 0), (WINDOW, 0), (0, 0)))
    g = jax.nn.sigmoid(gates.astype(jnp.float32)).reshape(b, t_len, NSA_HEADS, 3)
    g = g.transpose(0, 2, 1, 3).astype(q.dtype)
    b_idx = jnp.arange(b)[:, None, None, None]
    g_idx = jnp.arange(G)[None, :, None, None]
    j_sel = jnp.arange(n_sel)

    def block(i):
        t0 = i * Q_BLOCK
        tq = t0 + jnp.arange(Q_BLOCK)
        qb = lax.dynamic_slice_in_dim(q, t0, Q_BLOCK, axis=2).reshape(b, G, M, Q_BLOCK, d)
        gb = lax.dynamic_slice_in_dim(g, t0, Q_BLOCK, axis=2)
        s = jnp.einsum('bgmqd,bgnd->bgmqn', qb, kc)
        p_cmp = masked_softmax(s, cmp_end[None, :] <= tq[:, None])
        p_cmp = p_cmp * (tq >= CMP_BLOCK - 1).astype(jnp.float32)[:, None]
        o_cmp = jnp.einsum('bgmqn,bgnd->bgmqd', p_cmp.astype(vc.dtype), vc)
        p_slc = jnp.einsum('bgmqn,nj->bgqj', p_cmp, cmp_to_sel)
        cur = tq // SEL_BLOCK
        sel_ok = j_sel[None, :] <= cur[:, None]
        forced = (j_sel[None, :] == 0) | (j_sel[None, :] == cur[:, None]) | (j_sel[None, :] == cur[:, None] - 1)
        score = jnp.where(sel_ok, jnp.where(forced, FORCE_SCORE, p_slc), -1.0)
        top_val, top_idx = lax.top_k(score, top_k)
        ks = ks_blk[b_idx, g_idx, top_idx]
        vs = vs_blk[b_idx, g_idx, top_idx]
        tok = top_idx[..., None] * SEL_BLOCK + jnp.arange(SEL_BLOCK)
        tok_ok = (top_val >= 0.0)[..., None] & (tok <= tq[:, None, None])
        s = jnp.einsum('bgmqd,bgqkld->bgmqkl', qb, ks).reshape(b, G, M, Q_BLOCK, top_k * SEL_BLOCK)
        p = masked_softmax(s, tok_ok.reshape(b, G, 1, Q_BLOCK, top_k * SEL_BLOCK))
        p = p.reshape(b, G, M, Q_BLOCK, top_k, SEL_BLOCK).astype(vs.dtype)
        o_sel = jnp.einsum('bgmqkl,bgqkld->bgmqd', p, vs)
        kwb = lax.dynamic_slice_in_dim(kw_pad, t0, WINDOW + Q_BLOCK, axis=2)
        vwb = lax.dynamic_slice_in_dim(vw_pad, t0, WINDOW + Q_BLOCK, axis=2)
        pos = t0 - WINDOW + jnp.arange(WINDOW + Q_BLOCK)
        win_ok = (pos[None, :] >= 0) & (pos[None, :] <= tq[:, None]) & (pos[None, :] > tq[:, None] - WINDOW)
        s = jnp.einsum('bgmqd,bgsd->bgmqs', qb, kwb)
        o_win = jnp.einsum('bgmqs,bgsd->bgmqd', masked_softmax(s, win_ok).astype(vwb.dtype), vwb)
        o = jnp.stack([o_cmp, o_sel, o_win], axis=-1).reshape(b, NSA_HEADS, Q_BLOCK, d, 3)
        return jnp.einsum('bhqdc,bhqc->bqhd', o, gb)

    out = lax.map(block, jnp.arange(t_len // Q_BLOCK))
    return out.transpose(1, 0, 2, 3, 4).reshape(b, t_len, NSA_WIDTH)


def retention(q, k, v, gate, positions, gn_g):
    b, t_len, _ = q.shape
    H, d, C = RET_HEADS, RET_HEAD_DIM, RET_CHUNK
    n_chunk = t_len // C
    cos, sin = rope_tables(positions, d, RET_THETA)
    q = apply_rope(to_heads(q, H), cos, sin)
    k = apply_rope(to_heads(k, H), cos, sin) * (d ** -0.5)
    v = to_heads(v, H)
    log_gamma = jnp.log1p(-jnp.exp2(-5.0 - jnp.arange(H, dtype=jnp.float32)))
    idx = jnp.arange(C, dtype=jnp.float32)
    rel = idx[:, None] - idx[None, :]
    decay = jnp.where(rel >= 0, jnp.exp(jnp.maximum(rel, 0.0)[None] * log_gamma[:, None, None]), 0.0)
    qc = q.reshape(b, H, n_chunk, C, d)
    kc = k.reshape(b, H, n_chunk, C, d)
    vc = v.reshape(b, H, n_chunk, C, d)
    s = jnp.einsum('bhcid,bhcjd->bhcij', qc, kc) * decay[None, :, None].astype(q.dtype)
    o_intra = jnp.einsum('bhcij,bhcje->bhcie', s, vc)
    zeta = jnp.exp((C - 1 - idx)[None, :] * log_gamma[:, None])
    chunk_kv = jnp.einsum('bhcjd,bhcje->bhcde', kc * zeta[None, :, None, :, None].astype(k.dtype), vc)
    g_chunk = jnp.exp(C * log_gamma)[None, :, None, None]

    def step(state, kv):
        return state * g_chunk + kv, state

    _, prev = lax.scan(step, jnp.zeros((b, H, d, d), jnp.float32),
                       jnp.moveaxis(chunk_kv, 2, 0).astype(jnp.float32))
    prev = jnp.moveaxis(prev, 0, 2)
    xi = jnp.exp((idx + 1.0)[None, :] * log_gamma[:, None])
    o_cross = jnp.einsum('bhcid,bhcde->bhcie', qc.astype(jnp.float32) * xi[None, :, None, :, None], prev)
    o = (o_intra.astype(jnp.float32) + o_cross).reshape(b, H, t_len, d).transpose(0, 2, 1, 3)
    mu = jnp.mean(o, axis=-1, keepdims=True)
    var = jnp.mean(jnp.square(o - mu), axis=-1, keepdims=True)
    o = ((o - mu) * lax.rsqrt(var + 1e-5)).reshape(b, t_len, RET_WIDTH) * gn_g.astype(jnp.float32)
    return (jax.nn.silu(gate.astype(jnp.float32)) * o).astype(gate.dtype)


def odd_mixer(h, positions, w_in, cmp_k_pos, cmp_k_w1, cmp_k_w2, cmp_v_pos, cmp_v_w1, cmp_v_w2,
              ret_gn_g, w_out):
    z = h @ w_in
    sizes = [NSA_WIDTH] + [NSA_KV_WIDTH] * 6 + [3 * NSA_HEADS] + [RET_WIDTH] * 4
    splits = np.cumsum(sizes)[:-1].tolist()
    q, kc, vc, ks, vs, kw, vw, gt, rq, rk, rv, rg = jnp.split(z, splits, axis=-1)
    y_c = nsa_attention(q, kc, vc, ks, vs, kw, vw, gt, positions,
                        cmp_k_pos, cmp_k_w1, cmp_k_w2, cmp_v_pos, cmp_v_w1, cmp_v_w2)
    y_d = retention(rq, rk, rv, rg, positions, ret_gn_g)
    return jnp.concatenate([y_c, y_d], axis=-1) @ w_out


def setup_inputs(seed: int = 0) -> dict:
    key = jax.random.key(seed)
    ks = jax.random.split(key, 32)

    def nrm(k, shape, scale):
        return jax.random.normal(k, shape, jnp.float32) * scale

    def gain(k, shape):
        return 1.0 + 0.05 * jax.random.normal(k, shape, jnp.float32)

    NE, NO, D, F = N_EVEN, N_ODD, D_MODEL, FFN_HIDDEN
    x = nrm(ks[0], (BATCH, SEQ, D), 1.0)
    start = jax.random.randint(ks[1], (BATCH, 1), 0, 2048, dtype=jnp.int32)
    positions = start + jnp.arange(SEQ, dtype=jnp.int32)[None, :]
    cmp_in = CMP_BLOCK * NSA_HEAD_DIM
    return {
        "x": x,
        "positions": positions,
        "ln_mix_pre": gain(ks[2], (DEPTH, D)),
        "ln_mix_post": gain(ks[3], (DEPTH, D)),
        "ln_ffn_pre": gain(ks[4], (DEPTH, D)),
        "ln_ffn_post": gain(ks[5], (DEPTH, D)),
        "ffn_w_gate": nrm(ks[6], (DEPTH, D, F), D ** -0.5),
        "ffn_w_up": nrm(ks[7], (DEPTH, D, F), D ** -0.5),
        "ffn_w_down": nrm(ks[8], (DEPTH, F, D), F ** -0.5),
        "ev_w_in": nrm(ks[9], (NE, D, EVEN_IN), D ** -0.5),
        "ev_pool_w": nrm(ks[10], (NE, POOL_GROUPS, POOL_GDIM, POOL_GDIM), POOL_GDIM ** -0.5),
        "ev_pool_scale": 1.0 + 0.1 * jax.random.normal(ks[11], (NE, POOL_WIDTH), jnp.float32),
        "ev_sgu_ln_g": gain(ks[12], (NE, SGU_WIDTH)),
        "ev_sgu_ln_b": nrm(ks[13], (NE, SGU_WIDTH), 0.02),
        "ev_sgu_w": nrm(ks[14], (NE, SGU_GROUPS, SGU_CHUNK, SGU_CHUNK), SGU_CHUNK ** -0.5),
        "ev_sgu_b": 1.0 + 0.1 * jax.random.normal(ks[15], (NE, SGU_GROUPS, SGU_CHUNK), jnp.float32),
        "ev_w_out": nrm(ks[16], (NE, EVEN_MIX, D), EVEN_MIX ** -0.5),
        "od_w_in": nrm(ks[17], (NO, D, ODD_IN), D ** -0.5),
        "od_cmp_k_pos": nrm(ks[18], (NO, CMP_BLOCK, NSA_HEAD_DIM), 0.1),
        "od_cmp_k_w1": nrm(ks[19], (NO, cmp_in, CMP_HIDDEN), cmp_in ** -0.5),
        "od_cmp_k_w2": nrm(ks[20], (NO, CMP_HIDDEN, NSA_HEAD_DIM), CMP_HIDDEN ** -0.5),
        "od_cmp_v_pos": nrm(ks[21], (NO, CMP_BLOCK, NSA_HEAD_DIM), 0.1),
        "od_cmp_v_w1": nrm(ks[22], (NO, cmp_in, CMP_HIDDEN), cmp_in ** -0.5),
        "od_cmp_v_w2": nrm(ks[23], (NO, CMP_HIDDEN, NSA_HEAD_DIM), CMP_HIDDEN ** -0.5),
        "od_ret_gn_g": gain(ks[24], (NO, RET_WIDTH)),
        "od_w_out": nrm(ks[25], (NO, ODD_MIX, D), ODD_MIX ** -0.5),
    }


def reference(x, positions, ln_mix_pre, ln_mix_post, ln_ffn_pre, ln_ffn_post,
              ffn_w_gate, ffn_w_up, ffn_w_down,
              ev_w_in, ev_pool_w, ev_pool_scale, ev_sgu_ln_g, ev_sgu_ln_b, ev_sgu_w, ev_sgu_b, ev_w_out,
              od_w_in, od_cmp_k_pos, od_cmp_k_w1, od_cmp_k_w2, od_cmp_v_pos, od_cmp_v_w1, od_cmp_v_w2,
              od_ret_gn_g, od_w_out):
    for layer in range(DEPTH):
        h = rms_norm(x, ln_mix_pre[layer])
        if layer % 2 == 0:
            e = layer // 2
            m = even_mixer(h, ev_w_in[e], ev_pool_w[e], ev_pool_scale[e], ev_sgu_ln_g[e],
                           ev_sgu_ln_b[e], ev_sgu_w[e], ev_sgu_b[e], ev_w_out[e])
        else:
            o = layer // 2
            m = odd_mixer(h, positions, od_w_in[o], od_cmp_k_pos[o], od_cmp_k_w1[o], od_cmp_k_w2[o],
                          od_cmp_v_pos[o], od_cmp_v_w1[o], od_cmp_v_w2[o], od_ret_gn_g[o], od_w_out[o])
        x = x + rms_norm(m, ln_mix_post[layer])
        h = rms_norm(x, ln_ffn_pre[layer])
        x = x + rms_norm(swiglu(h, ffn_w_gate[layer], ffn_w_up[layer], ffn_w_down[layer]), ln_ffn_post[layer])
    return x
```

```python
import functools
import math

import numpy as np
import jax
import jax.numpy as jnp
from jax import lax
from jax.experimental import pallas as pl
from jax.experimental.pallas import tpu as pltpu

F32 = jnp.float32
BF16 = jnp.bfloat16

D_MODEL = 1024
POOL_WINDOWS = (2, 4, 8, 16)
POOL_GROUPS = 4
POOL_WIDTH = D_MODEL // 2
POOL_GDIM = POOL_WIDTH // POOL_GROUPS
POOL_HIST = 16
SGU_GROUPS = 4
SGU_WIDTH = D_MODEL // 2
SGU_GDIM = SGU_WIDTH // SGU_GROUPS
SGU_CHUNK = 128
EVEN_IN = POOL_WIDTH + 2 * SGU_WIDTH
NSA_HEADS = 8
NSA_KV_GROUPS = 2
NSA_HPG = NSA_HEADS // NSA_KV_GROUPS
NSA_HEAD_DIM = 64
NSA_WIDTH = NSA_HEADS * NSA_HEAD_DIM
NSA_GROUP_WIDTH = NSA_HPG * NSA_HEAD_DIM
NSA_KV_WIDTH = NSA_KV_GROUPS * NSA_HEAD_DIM
CMP_BLOCK = 32
CMP_STRIDE = 16
CMP_HIDDEN = 128
SEL_BLOCK = 64
SEL_TOPK = 16
SEL_PAD = 64
WINDOW = 512
Q_BLOCK = 128
FORCE_SCORE = 1.0e4
ROPE_THETA = 500000.0
ROPE_DIM = NSA_HEAD_DIM // 4
RET_HEADS = 4
RET_HEAD_DIM = 128
RET_WIDTH = RET_HEADS * RET_HEAD_DIM
RET_CHUNK = 128
RET_THETA = 10000.0
ODD_MIX = NSA_WIDTH + RET_WIDTH
NEG_INF = -1.0e30
LANES = 128

VMEM_LIMIT = 56 * 1024 * 1024
FFN_ROWS = 512
FFN_CHUNKS = 2
EVEN_ROWS = 512
PROJ_ROWS = 512
SEL_KEYS = 256


def _cparams(sem):
    return pltpu.CompilerParams(dimension_semantics=sem, vmem_limit_bytes=VMEM_LIMIT)


def _const_spec(shape):
    nd = len(shape)
    return pl.BlockSpec(shape, lambda *_: (0,) * nd, pipeline_mode=pl.Buffered(1))


def _dot(a, b):
    return jnp.dot(a, b, preferred_element_type=F32)


def _dot_nt(a, b):
    return lax.dot_general(a, b, (((1,), (1,)), ((), ())), preferred_element_type=F32)


def _dot_tn(a, b):
    return lax.dot_general(a, b, (((0,), (0,)), ((), ())), preferred_element_type=F32)


def _rms(x, g, eps=1e-6):
    return x * lax.rsqrt(jnp.mean(x * x, axis=-1, keepdims=True) + eps) * g


def _softmax_rows(s):
    e = jnp.exp(s - jnp.max(s, axis=-1, keepdims=True))
    return e / jnp.sum(e, axis=-1, keepdims=True)


def _ffn_body(x_ref, gpre_ref, gpost_ref, wg_ref, wu_ref, wd_ref, o_ref):
    x = x_ref[...]
    h = _rms(x, gpre_ref[...]).astype(BF16)
    f_total = wg_ref.shape[1]
    fc = f_total // FFN_CHUNKS
    acc = None
    for c in range(FFN_CHUNKS):
        gate = _dot(h, wg_ref[:, c * fc:(c + 1) * fc])
        up = _dot(h, wu_ref[:, c * fc:(c + 1) * fc])
        act = (gate * jax.nn.sigmoid(gate) * up).astype(BF16)
        part = _dot(act, wd_ref[c * fc:(c + 1) * fc, :])
        acc = part if acc is None else acc + part
    o_ref[...] = x + _rms(acc, gpost_ref[...])


def _ffn(x2, gpre, gpost, wg, wu, wd):
    n, d = x2.shape
    f = wg.shape[1]
    row = pl.BlockSpec((FFN_ROWS, d), lambda i: (i, 0))
    return pl.pallas_call(
        _ffn_body,
        grid=(n // FFN_ROWS,),
        in_specs=[row, _const_spec((1, d)), _const_spec((1, d)),
                  _const_spec((d, f)), _const_spec((d, f)), _const_spec((f, d))],
        out_specs=row,
        out_shape=jax.ShapeDtypeStruct((n, d), F32),
        compiler_params=_cparams(("arbitrary",)),
        name="ffn",
    )(x2, gpre, gpost, wg, wu, wd)


def _even_body(x_ref, gpre_ref, win_ref, poolw_ref, pscale_ref, lng_ref, lnb_ref, sguw_ref, sgub_ref,
               wout_ref, gpost_ref, o_ref, hist_ref):
    tt = x_ref.shape[1]
    j = pl.program_id(1)

    @pl.when(j == 0)
    def _():
        hist_ref[0:POOL_HIST, :] = jnp.zeros((POOL_HIST, POOL_WIDTH), F32)

    x = x_ref[0]
    h = _rms(x, gpre_ref[...]).astype(BF16)
    z = _dot(h, win_ref[...])
    a = z[:, :POOL_WIDTH]
    u = z[:, POOL_WIDTH:POOL_WIDTH + SGU_WIDTH]
    v = z[:, POOL_WIDTH + SGU_WIDTH:]

    hist_ref[POOL_HIST:POOL_HIST + tt, :] = a
    t_pos = j * tt + lax.broadcasted_iota(jnp.int32, (tt, 1), 0)
    ya = []
    for gi, w in enumerate(POOL_WINDOWS):
        cols = slice(gi * POOL_GDIM, (gi + 1) * POOL_GDIM)
        win_sum = a[:, cols]
        for s in range(1, w):
            win_sum = win_sum + hist_ref[POOL_HIST - s:POOL_HIST - s + tt, cols]
        cnt = jnp.minimum(t_pos + 1, w).astype(F32)
        diff = (win_sum / cnt - a[:, cols]).astype(BF16)
        ya.append(_dot(diff, poolw_ref[gi]))
    ya = jnp.concatenate(ya, axis=1) * pscale_ref[...]
    hist_ref[0:POOL_HIST, :] = hist_ref[tt:tt + POOL_HIST, :]

    ug = jax.nn.gelu(u)
    vg = jax.nn.gelu(v)
    mu = jnp.mean(vg, axis=-1, keepdims=True)
    var = jnp.mean(jnp.square(vg - mu), axis=-1, keepdims=True)
    vn = ((vg - mu) * lax.rsqrt(var + 1e-5) * lng_ref[...] + lnb_ref[...]).astype(BF16)
    r_i = lax.broadcasted_iota(jnp.int32, (SGU_CHUNK, SGU_CHUNK), 0)
    c_i = lax.broadcasted_iota(jnp.int32, (SGU_CHUNK, SGU_CHUNK), 1)
    yb = []
    for g in range(SGU_GROUPS):
        cols = slice(g * SGU_GDIM, (g + 1) * SGU_GDIM)
        wm = jnp.where(r_i >= c_i, sguw_ref[g], 0.0).astype(BF16)
        bcol = sgub_ref[:, g:g + 1]
        parts = []
        for c in range(tt // SGU_CHUNK):
            rows = slice(c * SGU_CHUNK, (c + 1) * SGU_CHUNK)
            parts.append(ug[rows, cols] * (_dot(wm, vn[rows, cols]) + bcol))
        yb.append(jnp.concatenate(parts, axis=0))
    yb = jnp.concatenate(yb, axis=1)

    m = _dot(ya.astype(BF16), wout_ref[0:POOL_WIDTH, :]) + _dot(yb.astype(BF16), wout_ref[POOL_WIDTH:, :])
    o_ref[0] = x + _rms(m, gpost_ref[...])


def _even_layer(x, gpre, win, poolw, pscale, lng, lnb, sguw, sgub_t, wout, gpost):
    b, t, d = x.shape
    tt = EVEN_ROWS
    row = pl.BlockSpec((1, tt, d), lambda bi, j: (bi, j, 0))
    return pl.pallas_call(
        _even_body,
        grid=(b, t // tt),
        in_specs=[row, _const_spec((1, d)), _const_spec(win.shape), _const_spec(poolw.shape),
                  _const_spec(pscale.shape), _const_spec(lng.shape), _const_spec(lnb.shape),
                  _const_spec(sguw.shape), _const_spec(sgub_t.shape), _const_spec(wout.shape),
                  _const_spec((1, d))],
        out_specs=row,
        out_shape=jax.ShapeDtypeStruct((b, t, d), F32),
        scratch_shapes=[pltpu.VMEM((POOL_HIST + tt, POOL_WIDTH), F32)],
        compiler_params=_cparams(("arbitrary", "arbitrary")),
        name="even_mixer",
    )(x, gpre, win, poolw, pscale, lng, lnb, sguw, sgub_t, wout, gpost)


_C_Q = 0
_C_KC = _C_Q + NSA_WIDTH
_C_VC = _C_KC + NSA_KV_WIDTH
_C_KS = _C_VC + NSA_KV_WIDTH
_C_VS = _C_KS + NSA_KV_WIDTH
_C_KW = _C_VS + NSA_KV_WIDTH
_C_VW = _C_KW + NSA_KV_WIDTH
_C_RQ = _C_VW + NSA_KV_WIDTH
_C_RK = _C_RQ + RET_WIDTH
_C_RV = _C_RK + RET_WIDTH
_C_RG = _C_RV + RET_WIDTH
_C_GT = _C_RG + RET_WIDTH
ODD_COLS = _C_GT + NSA_KV_GROUPS * LANES


def _rope_nsa(z, c, sa, sb):
    half = ROPE_DIM // 2
    return z * c + pltpu.roll(z, LANES - half, 1) * sa + pltpu.roll(z, half, 1) * sb


def _proj_body(x_ref, gpre_ref, w_ref, nc_ref, nsa_ref, nsb_ref, rc_ref, rs_ref,
               q_ref, kc_ref, vc_ref, ksa_ref, vs_ref, kw_ref, vw_ref, gt_ref,
               rq_ref, rk_ref, rv_ref, rg_ref):
    tm = x_ref.shape[1]
    j = pl.program_id(1)
    x = x_ref[0]
    h = _rms(x, gpre_ref[...]).astype(BF16)
    nc, nsa, nsb = nc_ref[0], nsa_ref[0], nsb_ref[0]
    rc, rs = rc_ref[0], rs_ref[0]

    def cols(start, width):
        return _dot(h, w_ref[:, start:start + width])

    zq = cols(_C_Q, NSA_WIDTH)
    q_scale = NSA_HEAD_DIM ** -0.5
    for s in range(NSA_WIDTH // LANES):
        sl = slice(s * LANES, (s + 1) * LANES)
        q_ref[0, :, sl] = (_rope_nsa(zq[:, sl], nc, nsa, nsb) * q_scale).astype(BF16)

    kc_ref[0] = _rope_nsa(cols(_C_KC, LANES), nc, nsa, nsb)
    vc_ref[0] = cols(_C_VC, LANES)

    ks = _rope_nsa(cols(_C_KS, LANES), nc, nsa, nsb)
    vs = cols(_C_VS, LANES)
    kw = _rope_nsa(cols(_C_KW, LANES), nc, nsa, nsb)
    vw = cols(_C_VW, LANES)
    t_pos = j * tm + lax.broadcasted_iota(jnp.int32, (tm, SEL_PAD), 0)
    blk = lax.broadcasted_iota(jnp.int32, (tm, SEL_PAD), 1)
    onehot = jnp.where(t_pos // SEL_BLOCK == blk, 1.0, 0.0).astype(BF16)
    for g in range(NSA_KV_GROUPS):
        sl = slice(g * NSA_HEAD_DIM, (g + 1) * NSA_HEAD_DIM)
        ksa_ref[0, g] = jnp.concatenate([ks[:, sl].astype(BF16), onehot], axis=1)
        vs_ref[0, g] = vs[:, sl].astype(BF16)
        kw_ref[0, g] = kw[:, sl].astype(BF16)
        vw_ref[0, g] = vw[:, sl].astype(BF16)

    gt_ref[0] = jax.nn.sigmoid(cols(_C_GT, NSA_KV_GROUPS * LANES))

    k_scale = RET_HEAD_DIM ** -0.5
    zrq = cols(_C_RQ, RET_WIDTH)
    zrk = cols(_C_RK, RET_WIDTH)
    for hh in range(RET_HEADS):
        sl = slice(hh * RET_HEAD_DIM, (hh + 1) * RET_HEAD_DIM)
        zq_h, zk_h = zrq[:, sl], zrk[:, sl]
        rq_ref[0, :, sl] = (zq_h * rc + pltpu.roll(zq_h, RET_HEAD_DIM // 2, 1) * rs).astype(BF16)
        rk_ref[0, :, sl] = ((zk_h * rc + pltpu.roll(zk_h, RET_HEAD_DIM // 2, 1) * rs) * k_scale).astype(BF16)
    rv_ref[0] = cols(_C_RV, RET_WIDTH).astype(BF16)
    rg_ref[0] = cols(_C_RG, RET_WIDTH)


def _odd_proj(x, gpre, w, nc, nsa, nsb, rc, rs):
    b, t, d = x.shape
    tm = PROJ_ROWS
    G = NSA_KV_GROUPS

    def row(width):
        return pl.BlockSpec((1, tm, width), lambda bi, j: (bi, j, 0))

    def grp(width):
        return pl.BlockSpec((1, G, tm, width), lambda bi, j: (bi, 0, j, 0))

    out_shape = [
        jax.ShapeDtypeStruct((b, t, NSA_WIDTH), BF16),
        jax.ShapeDtypeStruct((b, t, LANES), F32),
        jax.ShapeDtypeStruct((b, t, LANES), F32),
        jax.ShapeDtypeStruct((b, G, t, LANES), BF16),
        jax.ShapeDtypeStruct((b, G, t, NSA_HEAD_DIM), BF16),
        jax.ShapeDtypeStruct((b, G, t, NSA_HEAD_DIM), BF16),
        jax.ShapeDtypeStruct((b, G, t, NSA_HEAD_DIM), BF16),
        jax.ShapeDtypeStruct((b, t, G * LANES), F32),
        jax.ShapeDtypeStruct((b, t, RET_WIDTH), BF16),
        jax.ShapeDtypeStruct((b, t, RET_WIDTH), BF16),
        jax.ShapeDtypeStruct((b, t, RET_WIDTH), BF16),
        jax.ShapeDtypeStruct((b, t, RET_WIDTH), F32),
    ]
    out_specs = [row(NSA_WIDTH), row(LANES), row(LANES), grp(LANES), grp(NSA_HEAD_DIM), grp(NSA_HEAD_DIM),
                 grp(NSA_HEAD_DIM), row(G * LANES), row(RET_WIDTH), row(RET_WIDTH), row(RET_WIDTH),
                 row(RET_WIDTH)]
    return pl.pallas_call(
        _proj_body,
        grid=(b, t // tm),
        in_specs=[row(d), _const_spec((1, d)), _const_spec(w.shape)] + [row(LANES)] * 5,
        out_specs=out_specs,
        out_shape=out_shape,
        compiler_params=_cparams(("arbitrary", "arbitrary")),
        name="odd_proj",
    )(x, gpre, w, nc, nsa, nsb, rc, rs)


def _compress_body(k_ref, v_ref, kpos_ref, vpos_ref, kw1_ref, vw1_ref, kw2t_ref, vw2_ref, kct_ref, vcm_ref):
    nc = k_ref.shape[1]

    def hidden(x16, pos_ref, w1_ref, g):
        first = _dot((x16 + pos_ref[0:1, :]).astype(BF16), w1_ref[0, g])
        second = _dot((x16 + pos_ref[1:2, :]).astype(BF16), w1_ref[1, g])
        return jax.nn.gelu(first + pltpu.roll(second, nc - 1, 0)).astype(BF16)

    k16 = k_ref[0]
    v16 = v_ref[0]
    for g in range(NSA_KV_GROUPS):
        kct_ref[0, g] = _dot_nt(kw2t_ref[...], hidden(k16, kpos_ref, kw1_ref, g)).astype(BF16)
        vcm_ref[0, g] = _dot(hidden(v16, vpos_ref, vw1_ref, g), vw2_ref[...]).astype(BF16)


def _compress(k16, v16, kpos, vpos, kw1, vw1, kw2t, vw2):
    b, nc, width = k16.shape
    G = NSA_KV_GROUPS
    row = pl.BlockSpec((1, nc, width), lambda bi: (bi, 0, 0))
    return pl.pallas_call(
        _compress_body,
        grid=(b,),
        in_specs=[row, row, _const_spec(kpos.shape), _const_spec(vpos.shape), _const_spec(kw1.shape),
                  _const_spec(vw1.shape), _const_spec(kw2t.shape), _const_spec(vw2.shape)],
        out_specs=[pl.BlockSpec((1, G, NSA_HEAD_DIM, nc), lambda bi: (bi, 0, 0, 0)),
                   pl.BlockSpec((1, G, nc, NSA_HEAD_DIM), lambda bi: (bi, 0, 0, 0))],
        out_shape=[jax.ShapeDtypeStruct((b, G, NSA_HEAD_DIM, nc), BF16),
                   jax.ShapeDtypeStruct((b, G, nc, NSA_HEAD_DIM), BF16)],
        compiler_params=_cparams(("arbitrary",)),
        name="compress",
    )(k16, v16, kpos, vpos, kw1, vw1, kw2t, vw2)


def _nsa_body(q_ref, kct_ref, vcm_ref, c2st_ref, ksa_ref, vs_ref, kw_ref, vw_ref, gt_ref, o_ref, score_ref,
              *, top_k):
    M, Q, dh = NSA_HPG, Q_BLOCK, NSA_HEAD_DIM
    rows = M * Q
    nc = kct_ref.shape[3]
    i = pl.program_id(2)
    t0 = i * Q

    qb = q_ref[0]
    q_heads = [qb[:, m * dh:(m + 1) * dh] for m in range(M)]
    q_rows = jnp.concatenate(q_heads, axis=0)
    tq = t0 + (lax.broadcasted_iota(jnp.int32, (rows, 1), 0) & (Q - 1))

    n_idx = lax.broadcasted_iota(jnp.int32, (1, nc), 1)
    cmp_ok = (n_idx * CMP_STRIDE + (CMP_BLOCK - 1) <= tq) & (n_idx < nc - 1)
    p_cmp = _softmax_rows(jnp.where(cmp_ok, _dot(q_rows, kct_ref[0, 0]), NEG_INF))
    p_cmp = p_cmp * (tq >= CMP_BLOCK - 1).astype(F32)
    o_cmp = _dot(p_cmp.astype(BF16), vcm_ref[0, 0])

    p_grp = p_cmp[0:Q]
    for m in range(1, M):
        p_grp = p_grp + p_cmp[m * Q:(m + 1) * Q]
    p_hi = p_grp.astype(BF16)
    p_lo = (p_grp - p_hi.astype(F32)).astype(BF16)
    imp = _dot_nt(c2st_ref[...], p_hi) + _dot_nt(c2st_ref[...], p_lo)
    j_idx = lax.broadcasted_iota(jnp.int32, (SEL_PAD, Q), 0)
    lane = lax.broadcasted_iota(jnp.int32, (SEL_PAD, Q), 1)
    cur = (t0 + lane) // SEL_BLOCK
    forced = (j_idx == 0) | (j_idx == cur) | (j_idx == cur - 1)
    score = jnp.where(j_idx <= cur, jnp.where(forced, FORCE_SCORE, imp), -1.0)
    score_ref[...] = score

    def rank_step(k, rank):
        other = score_ref[pl.ds(k, 1), :]
        beats = (other > score) | ((other == score) & (k < j_idx))
        return rank + beats.astype(jnp.int32)

    n_causal = (t0 + Q) // SEL_BLOCK
    rank = lax.fori_loop(0, n_causal, rank_step, jnp.zeros((SEL_PAD, Q), jnp.int32))
    chosen = jnp.where((rank < top_k) & (score >= 0.0), 1.0, 0.0).astype(BF16)
    eye = jnp.where(lax.broadcasted_iota(jnp.int32, (Q, Q), 0) == lax.broadcasted_iota(jnp.int32, (Q, Q), 1),
                    1.0, 0.0).astype(BF16)
    chosen_t = _dot_nt(eye, chosen)
    bias = jnp.where(chosen_t > 0.5, 0.0, NEG_INF).astype(BF16)

    lhs = jnp.concatenate([jnp.concatenate([q_heads[m], bias], axis=1) for m in range(M)], axis=0)

    def sel_tile(kt, carry, diagonal):
        m_i, l_i, acc = carry
        k0 = pl.multiple_of(kt * SEL_KEYS, SEL_KEYS)
        kk = ksa_ref[0, 0, pl.ds(k0, SEL_KEYS), :]
        vv = vs_ref[0, 0, pl.ds(k0, SEL_KEYS), :]
        s = _dot_nt(lhs, kk)
        if diagonal:
            kpos = k0 + lax.broadcasted_iota(jnp.int32, (1, SEL_KEYS), 1)
            s = jnp.where(kpos <= tq, s, NEG_INF)
        m_new = jnp.maximum(m_i, jnp.max(s, axis=-1, keepdims=True))
        alpha = jnp.exp(m_i - m_new)
        p = jnp.exp(s - m_new)
        l_new = alpha * l_i + jnp.sum(p, axis=-1, keepdims=True)
        acc_new = alpha * acc + _dot(p.astype(BF16), vv)
        return m_new, l_new, acc_new

    n_full = t0 // SEL_KEYS
    init = (jnp.full((rows, 1), NEG_INF, F32), jnp.zeros((rows, 1), F32), jnp.zeros((rows, dh), F32))
    carry = lax.fori_loop(0, n_full, functools.partial(sel_tile, diagonal=False), init)
    _, l_sel, acc_sel = sel_tile(n_full, carry, diagonal=True)
    o_sel = acc_sel / l_sel

    span = WINDOW + Q
    w0 = pl.multiple_of(jnp.maximum(t0 - WINDOW, 0), Q)
    s = _dot_nt(q_rows, kw_ref[0, 0, pl.ds(w0, span), :])
    wpos = w0 + lax.broadcasted_iota(jnp.int32, (1, span), 1)
    win_ok = (wpos <= tq) & (wpos > tq - WINDOW)
    p_win = _softmax_rows(jnp.where(win_ok, s, NEG_INF))
    o_win = _dot(p_win.astype(BF16), vw_ref[0, 0, pl.ds(w0, span), :])

    gt = gt_ref[0]
    pieces = []
    for m in range(M):
        r = slice(m * Q, (m + 1) * Q)
        pieces.append(o_cmp[r] * gt[:, 3 * m:3 * m + 1] + o_sel[r] * gt[:, 3 * m + 1:3 * m + 2]
                      + o_win[r] * gt[:, 3 * m + 2:3 * m + 3])
    o_ref[0] = jnp.concatenate(pieces, axis=1).astype(BF16)


def _nsa(q, kct, vcm, c2st, ksa, vs, kw, vw, gt):
    b, t, _ = q.shape
    G = NSA_KV_GROUPS
    nc = kct.shape[3]
    n_sel = t // SEL_BLOCK
    assert n_sel <= SEL_PAD and t >= WINDOW + Q_BLOCK and t % SEL_KEYS == 0
    top_k = min(SEL_TOPK, n_sel)

    def per_group(shape):
        return pl.BlockSpec((1, 1) + shape, lambda bi, g, i: (bi, g, 0, 0))

    return pl.pallas_call(
        functools.partial(_nsa_body, top_k=top_k),
        grid=(b, G, t // Q_BLOCK),
        in_specs=[pl.BlockSpec((1, Q_BLOCK, NSA_GROUP_WIDTH), lambda bi, g, i: (bi, i, g)),
                  per_group((NSA_HEAD_DIM, nc)), per_group((nc, NSA_HEAD_DIM)), _const_spec(c2st.shape),
                  per_group((t, LANES)), per_group((t, NSA_HEAD_DIM)), per_group((t, NSA_HEAD_DIM)),
                  per_group((t, NSA_HEAD_DIM)),
                  pl.BlockSpec((1, Q_BLOCK, LANES), lambda bi, g, i: (bi, i, g))],
        out_specs=pl.BlockSpec((1, Q_BLOCK, NSA_GROUP_WIDTH), lambda bi, g, i: (bi, i, g)),
        out_shape=jax.ShapeDtypeStruct((b, t, NSA_WIDTH), BF16),
        scratch_shapes=[pltpu.VMEM((SEL_PAD, Q_BLOCK), F32)],
        compiler_params=_cparams(("arbitrary", "arbitrary", "arbitrary")),
        name="nsa",
    )(q, kct, vcm, c2st, ksa, vs, kw, vw, gt)


def _ret_body(q_ref, k_ref, v_ref, g_ref, decay_ref, zeta_ref, xi_ref, gchunk_ref, gn_ref, o_ref, state_ref):
    c = pl.program_id(2)

    @pl.when(c == 0)
    def _():
        state_ref[...] = jnp.zeros_like(state_ref)

    q, k, v = q_ref[0], k_ref[0], v_ref[0]
    s = (_dot_nt(q, k) * decay_ref[0]).astype(BF16)
    o_intra = _dot(s, v)
    state = state_ref[...]
    o_cross = _dot((q.astype(F32) * xi_ref[0]).astype(BF16), state.astype(BF16))
    kz = (k.astype(F32) * zeta_ref[0]).astype(BF16)
    state_ref[...] = state * gchunk_ref[0] + _dot_tn(kz, v)
    o = o_intra + o_cross
    mu = jnp.mean(o, axis=-1, keepdims=True)
    var = jnp.mean(jnp.square(o - mu), axis=-1, keepdims=True)
    o = (o - mu) * lax.rsqrt(var + 1e-5) * gn_ref[0]
    gate = g_ref[0]
    o_ref[0] = (gate * jax.nn.sigmoid(gate) * o).astype(BF16)


def _retention(rq, rk, rv, rg, decay, zeta, xi, gchunk, gn):
    b, t, _ = rq.shape
    C, d, H = RET_CHUNK, RET_HEAD_DIM, RET_HEADS
    tok = pl.BlockSpec((1, C, d), lambda bi, h, c: (bi, c, h))
    per_head = pl.BlockSpec((1, C, d), lambda bi, h, c: (h, 0, 0))
    return pl.pallas_call(
        _ret_body,
        grid=(b, H, t // C),
        in_specs=[tok, tok, tok, tok, per_head, per_head, per_head,
                  pl.BlockSpec((1, 1, d), lambda bi, h, c: (h, 0, 0)),
                  pl.BlockSpec((1, 1, d), lambda bi, h, c: (h, 0, 0))],
        out_specs=tok,
        out_shape=jax.ShapeDtypeStruct((b, t, RET_WIDTH), BF16),
        scratch_shapes=[pltpu.VMEM((d, d), F32)],
        compiler_params=_cparams(("arbitrary", "arbitrary", "arbitrary")),
        name="retention",
    )(rq, rk, rv, rg, decay, zeta, xi, gchunk, gn)


def _oproj_body(x_ref, yc_ref, yd_ref, w_ref, gpost_ref, o_ref):
    m = _dot(yc_ref[...], w_ref[0:NSA_WIDTH, :]) + _dot(yd_ref[...], w_ref[NSA_WIDTH:, :])
    o_ref[...] = x_ref[...] + _rms(m, gpost_ref[...])


def _odd_out(x2, yc2, yd2, w, gpost):
    n, d = x2.shape
    tm = PROJ_ROWS

    def row(width):
        return pl.BlockSpec((tm, width), lambda i: (i, 0))

    return pl.pallas_call(
        _oproj_body,
        grid=(n // tm,),
        in_specs=[row(d), row(NSA_WIDTH), row(RET_WIDTH), _const_spec(w.shape), _const_spec((1, d))],
        out_specs=row(d),
        out_shape=jax.ShapeDtypeStruct((n, d), F32),
        compiler_params=_cparams(("arbitrary",)),
        name="odd_out",
    )(x2, yc2, yd2, w, gpost)


def _odd_in_weight(w_in):
    sizes = [NSA_WIDTH] + [NSA_KV_WIDTH] * 6 + [3 * NSA_HEADS] + [RET_WIDTH] * 4
    offs = np.concatenate([[0], np.cumsum(sizes)])
    q, kc, vc, ks, vs, kw, vw, gt, rq, rk, rv, rg = [w_in[:, offs[n]:offs[n + 1]] for n in range(len(sizes))]
    per_group = 3 * NSA_HPG
    gates = [jnp.pad(gt[:, g * per_group:(g + 1) * per_group], ((0, 0), (0, LANES - per_group)))
             for g in range(NSA_KV_GROUPS)]
    return jnp.concatenate([q, kc, vc, ks, vs, kw, vw, rq, rk, rv, rg] + gates, axis=1).astype(BF16)


def _rope_tables(positions):
    pos = positions.astype(F32)[..., None]
    half = ROPE_DIM // 2
    inv = 1.0 / (ROPE_THETA ** (jnp.arange(0, ROPE_DIM, 2, dtype=F32) / ROPE_DIM))
    ang = pos * inv
    c, s = jnp.cos(ang), jnp.sin(ang)
    z = jnp.zeros(ang.shape[:-1] + (NSA_HEAD_DIM - ROPE_DIM,), F32)
    z8 = jnp.zeros_like(s)
    reps = LANES // NSA_HEAD_DIM
    nc = jnp.concatenate([c, c, z + 1.0] * reps, axis=-1)
    nsa = jnp.concatenate([-s, z8, z] * reps, axis=-1)
    nsb = jnp.concatenate([z8, s, z] * reps, axis=-1)
    inv_r = 1.0 / (RET_THETA ** (jnp.arange(0, RET_HEAD_DIM, 2, dtype=F32) / RET_HEAD_DIM))
    ang_r = pos * inv_r
    cr, sr = jnp.cos(ang_r), jnp.sin(ang_r)
    rc = jnp.concatenate([cr, cr], axis=-1)
    rs = jnp.concatenate([-sr, sr], axis=-1)
    return nc, nsa, nsb, rc, rs


def _retention_tables():
    H, C, d = RET_HEADS, RET_CHUNK, RET_HEAD_DIM
    log_gamma = jnp.log1p(-jnp.exp2(-5.0 - jnp.arange(H, dtype=F32)))
    idx = jnp.arange(C, dtype=F32)
    rel = idx[:, None] - idx[None, :]
    decay = jnp.where(rel >= 0, jnp.exp(jnp.maximum(rel, 0.0)[None] * log_gamma[:, None, None]), 0.0)
    zeta = jnp.exp((C - 1 - idx)[None, :] * log_gamma[:, None])
    xi = jnp.exp((idx + 1.0)[None, :] * log_gamma[:, None])
    gchunk = jnp.exp(C * log_gamma)
    zeta_b = jnp.broadcast_to(zeta[:, :, None], (H, C, d))
    xi_b = jnp.broadcast_to(xi[:, :, None], (H, C, d))
    gchunk_b = jnp.broadcast_to(gchunk[:, None, None], (H, 1, d))
    return decay, zeta_b, xi_b, gchunk_b


def _cmp_to_sel_t(t_len):
    nc = t_len // CMP_STRIDE
    n_cmp = (t_len - CMP_BLOCK) // CMP_STRIDE + 1
    n_sel = t_len // SEL_BLOCK
    c_start = np.arange(nc) * CMP_STRIDE
    s_start = np.arange(SEL_PAD) * SEL_BLOCK
    hit = ((c_start[None, :] < s_start[:, None] + SEL_BLOCK) & (c_start[None, :] + CMP_BLOCK > s_start[:, None])
           & (np.arange(nc)[None, :] < n_cmp) & (np.arange(SEL_PAD)[:, None] < n_sel))
    return jnp.asarray(hit.astype(np.float32), dtype=BF16)


def _compress_weights(pos, w1, w2):
    G, dh, half = NSA_KV_GROUPS, NSA_HEAD_DIM, CMP_BLOCK // 2
    pos_rows = jnp.broadcast_to(pos.reshape(2, half, 1, dh), (2, half, G, dh)).reshape(2, half * G * dh)
    w1h = w1.reshape(2, half, dh, CMP_HIDDEN)
    per_group = []
    for g in range(G):
        wg = jnp.zeros((2, half, G, dh, CMP_HIDDEN), w1.dtype).at[:, :, g].set(w1h)
        per_group.append(wg.reshape(2, half * G * dh, CMP_HIDDEN))
    return pos_rows, jnp.stack(per_group, axis=1).astype(BF16), w2.astype(BF16)


def _odd_layer(x, positions_tables, ret_tables, c2st, gpre, w_in, cmp_k_pos, cmp_k_w1, cmp_k_w2,
               cmp_v_pos, cmp_v_w1, cmp_v_w2, gn_g, w_out, gpost):
    b, t, d = x.shape
    (q, kc, vc, ksa, vs, kw, vw, gt, rq, rk, rv, rg) = _odd_proj(x, gpre, _odd_in_weight(w_in), *positions_tables)
    kpos, kw1, kw2 = _compress_weights(cmp_k_pos, cmp_k_w1, cmp_k_w2)
    vpos, vw1, vw2 = _compress_weights(cmp_v_pos, cmp_v_w1, cmp_v_w2)
    rows16 = (b, t // CMP_STRIDE, CMP_STRIDE * LANES)
    kct, vcm = _compress(kc.reshape(rows16), vc.reshape(rows16), kpos, vpos, kw1, vw1, kw2.T, vw2)
    yc = _nsa(q, kct, vcm, c2st, ksa, vs, kw, vw, gt)
    yd = _retention(rq, rk, rv, rg, *ret_tables, gn_g.reshape(RET_HEADS, 1, RET_HEAD_DIM))
    out = _odd_out(x.reshape(b * t, d), yc.reshape(b * t, NSA_WIDTH), yd.reshape(b * t, RET_WIDTH),
                   w_out.astype(BF16), gpost)
    return out.reshape(b, t, d)


def kernel(x, positions, ln_mix_pre, ln_mix_post, ln_ffn_pre, ln_ffn_post, ffn_w_gate, ffn_w_up, ffn_w_down,
           ev_w_in, ev_pool_w, ev_pool_scale, ev_sgu_ln_g, ev_sgu_ln_b, ev_sgu_w, ev_sgu_b, ev_w_out,
           od_w_in, od_cmp_k_pos, od_cmp_k_w1, od_cmp_k_w2, od_cmp_v_pos, od_cmp_v_w1, od_cmp_v_w2,
           od_ret_gn_g, od_w_out):
    b, t, d = x.shape
    depth = ln_mix_pre.shape[0]
    rope = _rope_tables(positions)
    ret_tables = _retention_tables()
    c2st = _cmp_to_sel_t(t)
    for layer in range(depth):
        gpre = ln_mix_pre[layer].reshape(1, d)
        gpost = ln_mix_post[layer].reshape(1, d)
        if layer % 2 == 0:
            e = layer // 2
            x = _even_layer(x, gpre, ev_w_in[e].astype(BF16), ev_pool_w[e].astype(BF16),
                            ev_pool_scale[e].reshape(1, POOL_WIDTH), ev_sgu_ln_g[e].reshape(1, SGU_WIDTH),
                            ev_sgu_ln_b[e].reshape(1, SGU_WIDTH), ev_sgu_w[e], ev_sgu_b[e].T,
                            ev_w_out[e].astype(BF16), gpost)
        else:
            o = layer // 2
            x = _odd_layer(x, rope, ret_tables, c2st, gpre, od_w_in[o], od_cmp_k_pos[o], od_cmp_k_w1[o],
                           od_cmp_k_w2[o], od_cmp_v_pos[o], od_cmp_v_w1[o], od_cmp_v_w2[o], od_ret_gn_g[o],
                           od_w_out[o], gpost)
        x = _ffn(x.reshape(b * t, d), ln_ffn_pre[layer].reshape(1, d), ln_ffn_post[layer].reshape(1, d),
                 ffn_w_gate[layer].astype(BF16), ffn_w_up[layer].astype(BF16),
                 ffn_w_down[layer].astype(BF16)).reshape(b, t, d)
    return x
```

```python
import functools

import numpy as np
import jax
import jax.numpy as jnp
from jax import lax
from jax.experimental import pallas as pl
from jax.experimental.pallas import tpu as pltpu

F32 = jnp.float32
BF16 = jnp.bfloat16

D_MODEL = 1024
POOL_WINDOWS = (2, 4, 8, 16)
POOL_GROUPS = 4
POOL_WIDTH = D_MODEL // 2
POOL_GDIM = POOL_WIDTH // POOL_GROUPS
POOL_HIST = 16
SGU_GROUPS = 4
SGU_WIDTH = D_MODEL // 2
SGU_GDIM = SGU_WIDTH // SGU_GROUPS
SGU_CHUNK = 128
EVEN_IN = POOL_WIDTH + 2 * SGU_WIDTH
NSA_HEADS = 8
NSA_KV_GROUPS = 2
NSA_HPG = NSA_HEADS // NSA_KV_GROUPS
NSA_HEAD_DIM = 64
NSA_WIDTH = NSA_HEADS * NSA_HEAD_DIM
NSA_GROUP_WIDTH = NSA_HPG * NSA_HEAD_DIM
NSA_KV_WIDTH = NSA_KV_GROUPS * NSA_HEAD_DIM
NSA_BRANCHES = 3
GATE_ROWS = 16
CMP_BLOCK = 32
CMP_STRIDE = 16
CMP_HIDDEN = 128
SEL_BLOCK = 64
SEL_TOPK = 16
SEL_PAD = 64
WINDOW = 512
Q_BLOCK = 128
FORCE_SCORE = 1.0e4
ROPE_THETA = 500000.0
ROPE_DIM = NSA_HEAD_DIM // 4
RET_HEADS = 4
RET_HEAD_DIM = 128
RET_WIDTH = RET_HEADS * RET_HEAD_DIM
RET_CHUNK = 128
RET_THETA = 10000.0
ODD_MIX = NSA_WIDTH + RET_WIDTH
NEG_INF = -1.0e30
LANES = 128

VMEM_LIMIT = 56 * 1024 * 1024
FFN_ROWS = 512
FFN_CHUNKS = 2
EVEN_ROWS = 512
PROJ_ROWS = 512
SEL_KEYS = 256
RET_ROWS = 512


def _cparams(sem):
    return pltpu.CompilerParams(dimension_semantics=sem, vmem_limit_bytes=VMEM_LIMIT)


def _const_spec(shape):
    nd = len(shape)
    return pl.BlockSpec(shape, lambda *_: (0,) * nd, pipeline_mode=pl.Buffered(1))


def _dot(a, b):
    return jnp.dot(a, b, preferred_element_type=F32)


def _dot_nt(a, b):
    return lax.dot_general(a, b, (((1,), (1,)), ((), ())), preferred_element_type=F32)


def _dot_tn(a, b):
    return lax.dot_general(a, b, (((0,), (0,)), ((), ())), preferred_element_type=F32)


def _rms(x, g, eps=1e-6):
    return x * lax.rsqrt(jnp.mean(x * x, axis=-1, keepdims=True) + eps) * g


def _softmax_cols(s):
    e = jnp.exp(s - jnp.max(s, axis=0, keepdims=True))
    return e / jnp.sum(e, axis=0, keepdims=True)


def _eye(n):
    return jnp.where(lax.broadcasted_iota(jnp.int32, (n, n), 0) == lax.broadcasted_iota(jnp.int32, (n, n), 1),
                     1.0, 0.0).astype(BF16)


def _ffn_body(x_ref, gpre_ref, gpost_ref, wg_ref, wu_ref, wd_ref, o_ref):
    x = x_ref[...]
    h = _rms(x, gpre_ref[...]).astype(BF16)
    f_total = wg_ref.shape[1]
    fc = f_total // FFN_CHUNKS
    acc = None
    for c in range(FFN_CHUNKS):
        gate = _dot(h, wg_ref[:, c * fc:(c + 1) * fc])
        up = _dot(h, wu_ref[:, c * fc:(c + 1) * fc])
        act = (gate * jax.nn.sigmoid(gate) * up).astype(BF16)
        part = _dot(act, wd_ref[c * fc:(c + 1) * fc, :])
        acc = part if acc is None else acc + part
    o_ref[...] = x + _rms(acc, gpost_ref[...])


def _ffn(x2, gpre, gpost, wg, wu, wd):
    n, d = x2.shape
    f = wg.shape[1]
    row = pl.BlockSpec((FFN_ROWS, d), lambda i: (i, 0))
    return pl.pallas_call(
        _ffn_body,
        grid=(n // FFN_ROWS,),
        in_specs=[row, _const_spec((1, d)), _const_spec((1, d)),
                  _const_spec((d, f)), _const_spec((d, f)), _const_spec((f, d))],
        out_specs=row,
        out_shape=jax.ShapeDtypeStruct((n, d), F32),
        compiler_params=_cparams(("arbitrary",)),
        name="ffn",
    )(x2, gpre, gpost, wg, wu, wd)


def _even_body(x_ref, gpre_ref, win_ref, poolw_ref, pscale_ref, lng_ref, lnb_ref, sguw_ref, sgub_ref,
               wout_ref, gpost_ref, o_ref, hist_ref):
    tt = x_ref.shape[1]
    j = pl.program_id(1)

    @pl.when(j == 0)
    def _():
        hist_ref[0:POOL_HIST, :] = jnp.zeros((POOL_HIST, POOL_WIDTH), F32)

    x = x_ref[0]
    h = _rms(x, gpre_ref[...]).astype(BF16)
    z = _dot(h, win_ref[...])
    a = z[:, :POOL_WIDTH]
    u = z[:, POOL_WIDTH:POOL_WIDTH + SGU_WIDTH]
    v = z[:, POOL_WIDTH + SGU_WIDTH:]

    hist_ref[POOL_HIST:POOL_HIST + tt, :] = a
    t_pos = j * tt + lax.broadcasted_iota(jnp.int32, (tt, 1), 0)
    ya = []
    for gi, w in enumerate(POOL_WINDOWS):
        cols = slice(gi * POOL_GDIM, (gi + 1) * POOL_GDIM)
        win_sum = a[:, cols]
        for s in range(1, w):
            win_sum = win_sum + hist_ref[POOL_HIST - s:POOL_HIST - s + tt, cols]
        cnt = jnp.minimum(t_pos + 1, w).astype(F32)
        diff = (win_sum / cnt - a[:, cols]).astype(BF16)
        ya.append(_dot(diff, poolw_ref[gi]))
    ya = jnp.concatenate(ya, axis=1) * pscale_ref[...]
    hist_ref[0:POOL_HIST, :] = hist_ref[tt:tt + POOL_HIST, :]

    ug = jax.nn.gelu(u)
    vg = jax.nn.gelu(v)
    mu = jnp.mean(vg, axis=-1, keepdims=True)
    var = jnp.mean(jnp.square(vg - mu), axis=-1, keepdims=True)
    vn = ((vg - mu) * lax.rsqrt(var + 1e-5) * lng_ref[...] + lnb_ref[...]).astype(BF16)
    r_i = lax.broadcasted_iota(jnp.int32, (SGU_CHUNK, SGU_CHUNK), 0)
    c_i = lax.broadcasted_iota(jnp.int32, (SGU_CHUNK, SGU_CHUNK), 1)
    yb = []
    for g in range(SGU_GROUPS):
        cols = slice(g * SGU_GDIM, (g + 1) * SGU_GDIM)
        wm = jnp.where(r_i >= c_i, sguw_ref[g], 0.0).astype(BF16)
        bcol = sgub_ref[:, g:g + 1]
        parts = []
        for c in range(tt // SGU_CHUNK):
            rows = slice(c * SGU_CHUNK, (c + 1) * SGU_CHUNK)
            parts.append(ug[rows, cols] * (_dot(wm, vn[rows, cols]) + bcol))
        yb.append(jnp.concatenate(parts, axis=0))
    yb = jnp.concatenate(yb, axis=1)

    m = _dot(ya.astype(BF16), wout_ref[0:POOL_WIDTH, :]) + _dot(yb.astype(BF16), wout_ref[POOL_WIDTH:, :])
    o_ref[0] = x + _rms(m, gpost_ref[...])


def _even_layer(x, gpre, win, poolw, pscale, lng, lnb, sguw, sgub_t, wout, gpost):
    b, t, d = x.shape
    tt = EVEN_ROWS
    row = pl.BlockSpec((1, tt, d), lambda bi, j: (bi, j, 0))
    return pl.pallas_call(
        _even_body,
        grid=(b, t // tt),
        in_specs=[row, _const_spec((1, d)), _const_spec(win.shape), _const_spec(poolw.shape),
                  _const_spec(pscale.shape), _const_spec(lng.shape), _const_spec(lnb.shape),
                  _const_spec(sguw.shape), _const_spec(sgub_t.shape), _const_spec(wout.shape),
                  _const_spec((1, d))],
        out_specs=row,
        out_shape=jax.ShapeDtypeStruct((b, t, d), F32),
        scratch_shapes=[pltpu.VMEM((POOL_HIST + tt, POOL_WIDTH), F32)],
        compiler_params=_cparams(("arbitrary", "arbitrary")),
        name="even_mixer",
    )(x, gpre, win, poolw, pscale, lng, lnb, sguw, sgub_t, wout, gpost)


_C_Q = 0
_C_KC = _C_Q + NSA_WIDTH
_C_VC = _C_KC + NSA_KV_WIDTH
_C_KS = _C_VC + NSA_KV_WIDTH
_C_KW = _C_KS + NSA_KV_WIDTH
_C_RQ = _C_KW + NSA_KV_WIDTH
_C_RK = _C_RQ + RET_WIDTH
_C_RV = _C_RK + RET_WIDTH
_C_RG = _C_RV + RET_WIDTH
ODD_COLS = _C_RG + RET_WIDTH
_R_VS = 0
_R_VW = _R_VS + NSA_KV_WIDTH
_R_GT = _R_VW + NSA_KV_WIDTH
ODD_TROWS = _R_GT + NSA_KV_GROUPS * GATE_ROWS


def _rope_nsa(z, c, sa, sb):
    half = ROPE_DIM // 2
    return z * c + pltpu.roll(z, LANES - half, 1) * sa + pltpu.roll(z, half, 1) * sb


def _proj_body(x_ref, gpre_ref, w_ref, wt_ref, nc_ref, nsa_ref, nsb_ref, rc_ref, rs_ref,
               q_ref, kc_ref, vc_ref, ksa_ref, kw_ref, vt_ref, gt_ref, rq_ref, rk_ref, rv_ref, rg_ref):
    tm = x_ref.shape[1]
    j = pl.program_id(1)
    x = x_ref[0]
    h = _rms(x, gpre_ref[...]).astype(BF16)
    nc, nsa, nsb = nc_ref[0], nsa_ref[0], nsb_ref[0]
    rc, rs = rc_ref[0], rs_ref[0]

    def cols(start, width):
        return _dot(h, w_ref[:, start:start + width])

    zq = cols(_C_Q, NSA_WIDTH)
    q_scale = NSA_HEAD_DIM ** -0.5
    for s in range(NSA_WIDTH // LANES):
        sl = slice(s * LANES, (s + 1) * LANES)
        q_ref[0, :, sl] = (_rope_nsa(zq[:, sl], nc, nsa, nsb) * q_scale).astype(BF16)

    kc_ref[0] = _rope_nsa(cols(_C_KC, LANES), nc, nsa, nsb)
    vc_ref[0] = cols(_C_VC, LANES)

    ks = _rope_nsa(cols(_C_KS, LANES), nc, nsa, nsb)
    kw = _rope_nsa(cols(_C_KW, LANES), nc, nsa, nsb)
    t_pos = j * tm + lax.broadcasted_iota(jnp.int32, (tm, SEL_PAD), 0)
    blk = lax.broadcasted_iota(jnp.int32, (tm, SEL_PAD), 1)
    onehot = jnp.where(t_pos // SEL_BLOCK == blk, 1.0, 0.0).astype(BF16)
    for g in range(NSA_KV_GROUPS):
        sl = slice(g * NSA_HEAD_DIM, (g + 1) * NSA_HEAD_DIM)
        ksa_ref[0, g] = jnp.concatenate([ks[:, sl].astype(BF16), onehot], axis=1)
        kw_ref[0, g] = kw[:, sl].astype(BF16)

    zt = _dot_nt(wt_ref[...], h)
    vt_ref[0] = zt[_R_VS:_R_GT].astype(BF16)
    gt_ref[0] = jax.nn.sigmoid(zt[_R_GT:])

    k_scale = RET_HEAD_DIM ** -0.5
    zrq = cols(_C_RQ, RET_WIDTH)
    zrk = cols(_C_RK, RET_WIDTH)
    for hh in range(RET_HEADS):
        sl = slice(hh * RET_HEAD_DIM, (hh + 1) * RET_HEAD_DIM)
        zq_h, zk_h = zrq[:, sl], zrk[:, sl]
        rq_ref[0, :, sl] = (zq_h * rc + pltpu.roll(zq_h, RET_HEAD_DIM // 2, 1) * rs).astype(BF16)
        rk_ref[0, :, sl] = ((zk_h * rc + pltpu.roll(zk_h, RET_HEAD_DIM // 2, 1) * rs) * k_scale).astype(BF16)
    rv_ref[0] = cols(_C_RV, RET_WIDTH).astype(BF16)
    rg_ref[0] = cols(_C_RG, RET_WIDTH)


def _odd_proj(x, gpre, w, wt, nc, nsa, nsb, rc, rs):
    b, t, d = x.shape
    tm = PROJ_ROWS
    G = NSA_KV_GROUPS

    def row(width):
        return pl.BlockSpec((1, tm, width), lambda bi, j: (bi, j, 0))

    def grp(width):
        return pl.BlockSpec((1, G, tm, width), lambda bi, j: (bi, 0, j, 0))

    def feat(rows):
        return pl.BlockSpec((1, rows, tm), lambda bi, j: (bi, 0, j))

    out_shape = [
        jax.ShapeDtypeStruct((b, t, NSA_WIDTH), BF16),
        jax.ShapeDtypeStruct((b, t, LANES), F32),
        jax.ShapeDtypeStruct((b, t, LANES), F32),
        jax.ShapeDtypeStruct((b, G, t, LANES), BF16),
        jax.ShapeDtypeStruct((b, G, t, NSA_HEAD_DIM), BF16),
        jax.ShapeDtypeStruct((b, _R_GT, t), BF16),
        jax.ShapeDtypeStruct((b, G * GATE_ROWS, t), F32),
        jax.ShapeDtypeStruct((b, t, RET_WIDTH), BF16),
        jax.ShapeDtypeStruct((b, t, RET_WIDTH), BF16),
        jax.ShapeDtypeStruct((b, t, RET_WIDTH), BF16),
        jax.ShapeDtypeStruct((b, t, RET_WIDTH), F32),
    ]
    out_specs = [row(NSA_WIDTH), row(LANES), row(LANES), grp(LANES), grp(NSA_HEAD_DIM), feat(_R_GT),
                 feat(G * GATE_ROWS), row(RET_WIDTH), row(RET_WIDTH), row(RET_WIDTH), row(RET_WIDTH)]
    return pl.pallas_call(
        _proj_body,
        grid=(b, t // tm),
        in_specs=[row(d), _const_spec((1, d)), _const_spec(w.shape), _const_spec(wt.shape)] + [row(LANES)] * 5,
        out_specs=out_specs,
        out_shape=out_shape,
        compiler_params=_cparams(("arbitrary", "arbitrary")),
        name="odd_proj",
    )(x, gpre, w, wt, nc, nsa, nsb, rc, rs)


def _compress_body(k_ref, v_ref, kpos_ref, vpos_ref, kw1_ref, vw1_ref, kw2_ref, vw2t_ref, kcm_ref, vct_ref):
    nc = k_ref.shape[1]

    def hidden(x16, pos_ref, w1_ref, g):
        first = _dot((x16 + pos_ref[0:1, :]).astype(BF16), w1_ref[0, g])
        second = _dot((x16 + pos_ref[1:2, :]).astype(BF16), w1_ref[1, g])
        return jax.nn.gelu(first + pltpu.roll(second, nc - 1, 0)).astype(BF16)

    k16 = k_ref[0]
    v16 = v_ref[0]
    for g in range(NSA_KV_GROUPS):
        kcm_ref[0, g] = _dot(hidden(k16, kpos_ref, kw1_ref, g), kw2_ref[...]).astype(BF16)
        vct_ref[0, g] = _dot_nt(vw2t_ref[...], hidden(v16, vpos_ref, vw1_ref, g)).astype(BF16)


def _compress(k16, v16, kpos, vpos, kw1, vw1, kw2, vw2t):
    b, nc, width = k16.shape
    G = NSA_KV_GROUPS
    row = pl.BlockSpec((1, nc, width), lambda bi: (bi, 0, 0))
    return pl.pallas_call(
        _compress_body,
        grid=(b,),
        in_specs=[row, row, _const_spec(kpos.shape), _const_spec(vpos.shape), _const_spec(kw1.shape),
                  _const_spec(vw1.shape), _const_spec(kw2.shape), _const_spec(vw2t.shape)],
        out_specs=[pl.BlockSpec((1, G, nc, NSA_HEAD_DIM), lambda bi: (bi, 0, 0, 0)),
                   pl.BlockSpec((1, G, NSA_HEAD_DIM, nc), lambda bi: (bi, 0, 0, 0))],
        out_shape=[jax.ShapeDtypeStruct((b, G, nc, NSA_HEAD_DIM), BF16),
                   jax.ShapeDtypeStruct((b, G, NSA_HEAD_DIM, nc), BF16)],
        compiler_params=_cparams(("arbitrary",)),
        name="compress",
    )(k16, v16, kpos, vpos, kw1, vw1, kw2, vw2t)


def _nsa_body(q_ref, kcm_ref, vct_ref, c2st_ref, ksa_ref, vst_ref, kw_ref, vwt_ref, gt_ref, o_ref, score_ref,
              *, top_k):
    M, Q, dh = NSA_HPG, Q_BLOCK, NSA_HEAD_DIM
    cols = M * Q
    nc = kcm_ref.shape[2]
    i = pl.program_id(2)
    t0 = i * Q

    qb = q_ref[0]
    q_heads = [qb[:, m * dh:(m + 1) * dh] for m in range(M)]
    q_rows = jnp.concatenate(q_heads, axis=0)
    tq = t0 + (lax.broadcasted_iota(jnp.int32, (1, cols), 1) & (Q - 1))

    n_idx = lax.broadcasted_iota(jnp.int32, (nc, 1), 0)
    cmp_ok = (n_idx * CMP_STRIDE + (CMP_BLOCK - 1) <= tq) & (n_idx < nc - 1)
    p_cmp = _softmax_cols(jnp.where(cmp_ok, _dot_nt(kcm_ref[0, 0], q_rows), NEG_INF))
    p_cmp = p_cmp * (tq >= CMP_BLOCK - 1).astype(F32)
    o_cmp = _dot(vct_ref[0, 0], p_cmp.astype(BF16))

    p_grp = p_cmp[:, 0:Q]
    for m in range(1, M):
        p_grp = p_grp + p_cmp[:, m * Q:(m + 1) * Q]
    p_hi = p_grp.astype(BF16)
    p_lo = (p_grp - p_hi.astype(F32)).astype(BF16)
    imp = _dot(c2st_ref[...], p_hi) + _dot(c2st_ref[...], p_lo)
    j_idx = lax.broadcasted_iota(jnp.int32, (SEL_PAD, Q), 0)
    lane = lax.broadcasted_iota(jnp.int32, (SEL_PAD, Q), 1)
    cur = (t0 + lane) // SEL_BLOCK
    forced = (j_idx == 0) | (j_idx == cur) | (j_idx == cur - 1)
    score = jnp.where(j_idx <= cur, jnp.where(forced, FORCE_SCORE, imp), -1.0)
    score_ref[...] = score

    def rank_step(k, rank):
        other = score_ref[pl.ds(k, 1), :]
        beats = (other > score) | ((other == score) & (k < j_idx))
        return rank + beats.astype(jnp.int32)

    n_causal = (t0 + Q) // SEL_BLOCK
    rank = lax.fori_loop(0, n_causal, rank_step, jnp.zeros((SEL_PAD, Q), jnp.int32))
    chosen = jnp.where((rank < top_k) & (score >= 0.0), 1.0, 0.0).astype(BF16)
    chosen_t = _dot_nt(_eye(Q), chosen)
    bias = jnp.where(chosen_t > 0.5, 0.0, NEG_INF).astype(BF16)

    lhs = jnp.concatenate([jnp.concatenate([q_heads[m], bias], axis=1) for m in range(M)], axis=0)

    def sel_tile(kt, carry, diagonal):
        m_i, l_i, acc = carry
        k0 = pl.multiple_of(kt * SEL_KEYS, SEL_KEYS)
        s = _dot_nt(ksa_ref[0, 0, pl.ds(k0, SEL_KEYS), :], lhs)
        if diagonal:
            kpos = k0 + lax.broadcasted_iota(jnp.int32, (SEL_KEYS, 1), 0)
            s = jnp.where(kpos <= tq, s, NEG_INF)
        m_new = jnp.maximum(m_i, jnp.max(s, axis=0, keepdims=True))
        alpha = jnp.exp(m_i - m_new)
        p = jnp.exp(s - m_new)
        l_new = alpha * l_i + jnp.sum(p, axis=0, keepdims=True)
        acc_new = alpha * acc + _dot(vst_ref[0, :, pl.ds(k0, SEL_KEYS)], p.astype(BF16))
        return m_new, l_new, acc_new

    n_full = t0 // SEL_KEYS
    init = (jnp.full((1, cols), NEG_INF, F32), jnp.zeros((1, cols), F32), jnp.zeros((dh, cols), F32))
    carry = lax.fori_loop(0, n_full, functools.partial(sel_tile, diagonal=False), init)
    _, l_sel, acc_sel = sel_tile(n_full, carry, diagonal=True)
    o_sel = acc_sel / l_sel

    span = WINDOW + Q
    w0 = pl.multiple_of(jnp.maximum(t0 - WINDOW, 0), Q)
    s = _dot_nt(kw_ref[0, 0, pl.ds(w0, span), :], q_rows)
    wpos = w0 + lax.broadcasted_iota(jnp.int32, (span, 1), 0)
    win_ok = (wpos <= tq) & (wpos > tq - WINDOW)
    p_win = _softmax_cols(jnp.where(win_ok, s, NEG_INF))
    o_win = _dot(vwt_ref[0, :, pl.ds(w0, span)], p_win.astype(BF16))

    gt = gt_ref[0]
    eye_q = _eye(Q)
    pieces = []
    for m in range(M):
        c = slice(m * Q, (m + 1) * Q)
        r = NSA_BRANCHES * m
        mixed = o_cmp[:, c] * gt[r:r + 1] + o_sel[:, c] * gt[r + 1:r + 2] + o_win[:, c] * gt[r + 2:r + 3]
        pieces.append(_dot_nt(eye_q, mixed.astype(BF16)))
    o_ref[0] = jnp.concatenate(pieces, axis=1).astype(BF16)


def _nsa(q, kcm, vct, c2st, ksa, kw, vt, gt):
    b, t, _ = q.shape
    G = NSA_KV_GROUPS
    nc = kcm.shape[2]
    n_sel = t // SEL_BLOCK
    assert n_sel <= SEL_PAD and t >= WINDOW + Q_BLOCK and t % SEL_KEYS == 0
    top_k = min(SEL_TOPK, n_sel)

    def per_group(shape):
        return pl.BlockSpec((1, 1) + shape, lambda bi, g, i: (bi, g, 0, 0))

    return pl.pallas_call(
        functools.partial(_nsa_body, top_k=top_k),
        grid=(b, G, t // Q_BLOCK),
        in_specs=[pl.BlockSpec((1, Q_BLOCK, NSA_GROUP_WIDTH), lambda bi, g, i: (bi, i, g)),
                  per_group((nc, NSA_HEAD_DIM)), per_group((NSA_HEAD_DIM, nc)), _const_spec(c2st.shape),
                  per_group((t, LANES)),
                  pl.BlockSpec((1, NSA_HEAD_DIM, t), lambda bi, g, i: (bi, g, 0)),
                  per_group((t, NSA_HEAD_DIM)),
                  pl.BlockSpec((1, NSA_HEAD_DIM, t), lambda bi, g, i: (bi, G + g, 0)),
                  pl.BlockSpec((1, GATE_ROWS, Q_BLOCK), lambda bi, g, i: (bi, g, i))],
        out_specs=pl.BlockSpec((1, Q_BLOCK, NSA_GROUP_WIDTH), lambda bi, g, i: (bi, i, g)),
        out_shape=jax.ShapeDtypeStruct((b, t, NSA_WIDTH), BF16),
        scratch_shapes=[pltpu.VMEM((SEL_PAD, Q_BLOCK), F32)],
        compiler_params=_cparams(("arbitrary", "arbitrary", "arbitrary")),
        name="nsa",
    )(q, kcm, vct, c2st, ksa, vt, kw, vt, gt)


def _ret_body(q_ref, k_ref, v_ref, g_ref, decay_ref, zeta_ref, xi_ref, gchunk_ref, gn_ref, o_ref, state_ref):
    C, d = RET_CHUNK, RET_HEAD_DIM

    @pl.when(pl.program_id(1) == 0)
    def _():
        state_ref[...] = jnp.zeros_like(state_ref)

    for c in range(q_ref.shape[1] // C):
        rows = slice(c * C, (c + 1) * C)
        for h in range(RET_HEADS):
            hc = slice(h * d, (h + 1) * d)
            q, k, v = q_ref[0, rows, hc], k_ref[0, rows, hc], v_ref[0, rows, hc]
            s = (_dot_nt(q, k) * decay_ref[h]).astype(BF16)
            state = state_ref[h]
            o = _dot(s, v) + _dot((q.astype(F32) * xi_ref[h]).astype(BF16), state.astype(BF16))
            kz = (k.astype(F32) * zeta_ref[h]).astype(BF16)
            state_ref[h] = state * gchunk_ref[h] + _dot_tn(kz, v)
            mu = jnp.mean(o, axis=-1, keepdims=True)
            var = jnp.mean(jnp.square(o - mu), axis=-1, keepdims=True)
            o = (o - mu) * lax.rsqrt(var + 1e-5) * gn_ref[h]
            gate = g_ref[0, rows, hc]
            o_ref[0, rows, hc] = (gate * jax.nn.sigmoid(gate) * o).astype(BF16)


def _retention(rq, rk, rv, rg, decay, zeta, xi, gchunk, gn):
    b, t, width = rq.shape
    tok = pl.BlockSpec((1, RET_ROWS, width), lambda bi, c: (bi, c, 0))
    return pl.pallas_call(
        _ret_body,
        grid=(b, t // RET_ROWS),
        in_specs=[tok, tok, tok, tok, _const_spec(decay.shape), _const_spec(zeta.shape), _const_spec(xi.shape),
                  _const_spec(gchunk.shape), _const_spec(gn.shape)],
        out_specs=tok,
        out_shape=jax.ShapeDtypeStruct((b, t, width), BF16),
        scratch_shapes=[pltpu.VMEM((RET_HEADS, RET_HEAD_DIM, RET_HEAD_DIM), F32)],
        compiler_params=_cparams(("arbitrary", "arbitrary")),
        name="retention",
    )(rq, rk, rv, rg, decay, zeta, xi, gchunk, gn)


def _oproj_body(x_ref, yc_ref, yd_ref, w_ref, gpost_ref, o_ref):
    m = _dot(yc_ref[...], w_ref[0:NSA_WIDTH, :]) + _dot(yd_ref[...], w_ref[NSA_WIDTH:, :])
    o_ref[...] = x_ref[...] + _rms(m, gpost_ref[...])


def _odd_out(x2, yc2, yd2, w, gpost):
    n, d = x2.shape
    tm = PROJ_ROWS

    def row(width):
        return pl.BlockSpec((tm, width), lambda i: (i, 0))

    return pl.pallas_call(
        _oproj_body,
        grid=(n // tm,),
        in_specs=[row(d), row(NSA_WIDTH), row(RET_WIDTH), _const_spec(w.shape), _const_spec((1, d))],
        out_specs=row(d),
        out_shape=jax.ShapeDtypeStruct((n, d), F32),
        compiler_params=_cparams(("arbitrary",)),
        name="odd_out",
    )(x2, yc2, yd2, w, gpost)


def _odd_in_weight(w_in):
    sizes = [NSA_WIDTH] + [NSA_KV_WIDTH] * 6 + [NSA_BRANCHES * NSA_HEADS] + [RET_WIDTH] * 4
    offs = np.concatenate([[0], np.cumsum(sizes)])
    q, kc, vc, ks, vs, kw, vw, gt, rq, rk, rv, rg = [w_in[:, offs[n]:offs[n + 1]] for n in range(len(sizes))]
    w = jnp.concatenate([q, kc, vc, ks, kw, rq, rk, rv, rg], axis=1).astype(BF16)
    per_group = NSA_BRANCHES * NSA_HPG
    gates = [jnp.pad(gt[:, g * per_group:(g + 1) * per_group], ((0, 0), (0, GATE_ROWS - per_group)))
             for g in range(NSA_KV_GROUPS)]
    wt = jnp.concatenate([vs, vw] + gates, axis=1).T.astype(BF16)
    return w, wt


def _rope_tables(positions):
    pos = positions.astype(F32)[..., None]
    inv = 1.0 / (ROPE_THETA ** (jnp.arange(0, ROPE_DIM, 2, dtype=F32) / ROPE_DIM))
    ang = pos * inv
    c, s = jnp.cos(ang), jnp.sin(ang)
    z = jnp.zeros(ang.shape[:-1] + (NSA_HEAD_DIM - ROPE_DIM,), F32)
    z8 = jnp.zeros_like(s)
    reps = LANES // NSA_HEAD_DIM
    nc = jnp.concatenate([c, c, z + 1.0] * reps, axis=-1)
    nsa = jnp.concatenate([-s, z8, z] * reps, axis=-1)
    nsb = jnp.concatenate([z8, s, z] * reps, axis=-1)
    inv_r = 1.0 / (RET_THETA ** (jnp.arange(0, RET_HEAD_DIM, 2, dtype=F32) / RET_HEAD_DIM))
    ang_r = pos * inv_r
    cr, sr = jnp.cos(ang_r), jnp.sin(ang_r)
    rc = jnp.concatenate([cr, cr], axis=-1)
    rs = jnp.concatenate([-sr, sr], axis=-1)
    return nc, nsa, nsb, rc, rs


def _retention_tables():
    H, C, d = RET_HEADS, RET_CHUNK, RET_HEAD_DIM
    log_gamma = jnp.log1p(-jnp.exp2(-5.0 - jnp.arange(H, dtype=F32)))
    idx = jnp.arange(C, dtype=F32)
    rel = idx[:, None] - idx[None, :]
    decay = jnp.where(rel >= 0, jnp.exp(jnp.maximum(rel, 0.0)[None] * log_gamma[:, None, None]), 0.0)
    zeta = jnp.exp((C - 1 - idx)[None, :] * log_gamma[:, None])
    xi = jnp.exp((idx + 1.0)[None, :] * log_gamma[:, None])
    gchunk = jnp.exp(C * log_gamma)
    zeta_b = jnp.broadcast_to(zeta[:, :, None], (H, C, d))
    xi_b = jnp.broadcast_to(xi[:, :, None], (H, C, d))
    gchunk_b = jnp.broadcast_to(gchunk[:, None, None], (H, 1, d))
    return decay, zeta_b, xi_b, gchunk_b


def _cmp_to_sel_t(t_len):
    nc = t_len // CMP_STRIDE
    n_cmp = (t_len - CMP_BLOCK) // CMP_STRIDE + 1
    n_sel = t_len // SEL_BLOCK
    c_start = np.arange(nc) * CMP_STRIDE
    s_start = np.arange(SEL_PAD) * SEL_BLOCK
    hit = ((c_start[None, :] < s_start[:, None] + SEL_BLOCK) & (c_start[None, :] + CMP_BLOCK > s_start[:, None])
           & (np.arange(nc)[None, :] < n_cmp) & (np.arange(SEL_PAD)[:, None] < n_sel))
    return jnp.asarray(hit.astype(np.float32), dtype=BF16)


def _compress_weights(pos, w1, w2):
    G, dh, half = NSA_KV_GROUPS, NSA_HEAD_DIM, CMP_BLOCK // 2
    pos_rows = jnp.broadcast_to(pos.reshape(2, half, 1, dh), (2, half, G, dh)).reshape(2, half * G * dh)
    w1h = w1.reshape(2, half, dh, CMP_HIDDEN)
    per_group = []
    for g in range(G):
        wg = jnp.zeros((2, half, G, dh, CMP_HIDDEN), w1.dtype).at[:, :, g].set(w1h)
        per_group.append(wg.reshape(2, half * G * dh, CMP_HIDDEN))
    return pos_rows, jnp.stack(per_group, axis=1).astype(BF16), w2.astype(BF16)


def _odd_layer(x, positions_tables, ret_tables, c2st, gpre, w_in, cmp_k_pos, cmp_k_w1, cmp_k_w2,
               cmp_v_pos, cmp_v_w1, cmp_v_w2, gn_g, w_out, gpost):
    b, t, d = x.shape
    w, wt = _odd_in_weight(w_in)
    (q, kc, vc, ksa, kw, vt, gt, rq, rk, rv, rg) = _odd_proj(x, gpre, w, wt, *positions_tables)
    kpos, kw1, kw2 = _compress_weights(cmp_k_pos, cmp_k_w1, cmp_k_w2)
    vpos, vw1, vw2 = _compress_weights(cmp_v_pos, cmp_v_w1, cmp_v_w2)
    rows16 = (b, t // CMP_STRIDE, CMP_STRIDE * LANES)
    kcm, vct = _compress(kc.reshape(rows16), vc.reshape(rows16), kpos, vpos, kw1, vw1, kw2, vw2.T)
    yc = _nsa(q, kcm, vct, c2st, ksa, kw, vt, gt)
    yd = _retention(rq, rk, rv, rg, *ret_tables, gn_g.reshape(RET_HEADS, 1, RET_HEAD_DIM))
    out = _odd_out(x.reshape(b * t, d), yc.reshape(b * t, NSA_WIDTH), yd.reshape(b * t, RET_WIDTH),
                   w_out.astype(BF16), gpost)
    return out.reshape(b, t, d)


def kernel(x, positions, ln_mix_pre, ln_mix_post, ln_ffn_pre, ln_ffn_post, ffn_w_gate, ffn_w_up, ffn_w_down,
           ev_w_in, ev_pool_w, ev_pool_scale, ev_sgu_ln_g, ev_sgu_ln_b, ev_sgu_w, ev_sgu_b, ev_w_out,
           od_w_in, od_cmp_k_pos, od_cmp_k_w1, od_cmp_k_w2, od_cmp_v_pos, od_cmp_v_w1, od_cmp_v_w2,
           od_ret_gn_g, od_w_out):
    b, t, d = x.shape
    depth = ln_mix_pre.shape[0]
    rope = _rope_tables(positions)
    ret_tables = _retention_tables()
    c2st = _cmp_to_sel_t(t)
    for layer in range(depth):
        gpre = ln_mix_pre[layer].reshape(1, d)
        gpost = ln_mix_post[layer].reshape(1, d)
        if layer % 2 == 0:
            e = layer // 2
            x = _even_layer(x, gpre, ev_w_in[e].astype(BF16), ev_pool_w[e].astype(BF16),
                            ev_pool_scale[e].reshape(1, POOL_WIDTH), ev_sgu_ln_g[e].reshape(1, SGU_WIDTH),
                            ev_sgu_ln_b[e].reshape(1, SGU_WIDTH), ev_sgu_w[e], ev_sgu_b[e].T,
                            ev_w_out[e].astype(BF16), gpost)
        else:
            o = layer // 2
            x = _odd_layer(x, rope, ret_tables, c2st, gpre, od_w_in[o], od_cmp_k_pos[o], od_cmp_k_w1[o],
                           od_cmp_k_w2[o], od_cmp_v_pos[o], od_cmp_v_w1[o], od_cmp_v_w2[o], od_ret_gn_g[o],
                           od_w_out[o], gpost)
        x = _ffn(x.reshape(b * t, d), ln_ffn_pre[layer].reshape(1, d), ln_ffn_post[layer].reshape(1, d),
                 ffn_w_gate[layer].astype(BF16), ffn_w_up[layer].astype(BF16),
                 ffn_w_down[layer].astype(BF16)).reshape(b, t, d)
    return x
```

```python
import functools

import numpy as np
import jax
import jax.numpy as jnp
from jax import lax
from jax.experimental import pallas as pl
from jax.experimental.pallas import tpu as pltpu

F32 = jnp.float32
BF16 = jnp.bfloat16

D_MODEL = 1024
POOL_WINDOWS = (2, 4, 8, 16)
POOL_GROUPS = 4
POOL_WIDTH = D_MODEL // 2
POOL_GDIM = POOL_WIDTH // POOL_GROUPS
POOL_HIST = 16
SGU_GROUPS = 4
SGU_WIDTH = D_MODEL // 2
SGU_GDIM = SGU_WIDTH // SGU_GROUPS
SGU_CHUNK = 128
EVEN_IN = POOL_WIDTH + 2 * SGU_WIDTH
NSA_HEADS = 8
NSA_KV_GROUPS = 2
NSA_HPG = NSA_HEADS // NSA_KV_GROUPS
NSA_HEAD_DIM = 64
NSA_WIDTH = NSA_HEADS * NSA_HEAD_DIM
NSA_GROUP_WIDTH = NSA_HPG * NSA_HEAD_DIM
NSA_KV_WIDTH = NSA_KV_GROUPS * NSA_HEAD_DIM
NSA_BRANCHES = 3
GATE_ROWS = 16
CMP_BLOCK = 32
CMP_STRIDE = 16
CMP_HIDDEN = 128
SEL_BLOCK = 64
SEL_TOPK = 16
SEL_PAD = 64
WINDOW = 512
Q_BLOCK = 128
FORCE_SCORE = 1.0e4
ROPE_THETA = 500000.0
ROPE_DIM = NSA_HEAD_DIM // 4
RET_HEADS = 4
RET_HEAD_DIM = 128
RET_WIDTH = RET_HEADS * RET_HEAD_DIM
RET_CHUNK = 128
RET_THETA = 10000.0
ODD_MIX = NSA_WIDTH + RET_WIDTH
NEG_INF = -1.0e30
LOG2_E = 1.4426950408889634
LANES = 128

VMEM_LIMIT = 56 * 1024 * 1024
FFN_ROWS = 512
FFN_CHUNKS = 2
EVEN_ROWS = 512
PROJ_ROWS = 512
SEL_KEYS = 512
RET_ROWS = 512


def _cparams(sem):
    return pltpu.CompilerParams(dimension_semantics=sem, vmem_limit_bytes=VMEM_LIMIT)


def _const_spec(shape):
    nd = len(shape)
    return pl.BlockSpec(shape, lambda *_: (0,) * nd, pipeline_mode=pl.Buffered(1))


def _dot(a, b):
    return jnp.dot(a, b, preferred_element_type=F32)


def _dot_nt(a, b):
    return lax.dot_general(a, b, (((1,), (1,)), ((), ())), preferred_element_type=F32)


def _dot_tn(a, b):
    return lax.dot_general(a, b, (((0,), (0,)), ((), ())), preferred_element_type=F32)


def _rms(x, g, eps=1e-6):
    return x * lax.rsqrt(jnp.mean(x * x, axis=-1, keepdims=True) + eps) * g


def _softmax2_cols(s):
    e = jnp.exp2(s - jnp.max(s, axis=0, keepdims=True))
    return e / jnp.sum(e, axis=0, keepdims=True)


def _eye(n):
    return jnp.where(lax.broadcasted_iota(jnp.int32, (n, n), 0) == lax.broadcasted_iota(jnp.int32, (n, n), 1),
                     1.0, 0.0).astype(BF16)


def _ffn_body(x_ref, gpre_ref, gpost_ref, wg_ref, wu_ref, wd_ref, o_ref):
    x = x_ref[...]
    h = _rms(x, gpre_ref[...]).astype(BF16)
    f_total = wg_ref.shape[1]
    fc = f_total // FFN_CHUNKS
    acc = None
    for c in range(FFN_CHUNKS):
        gate = _dot(h, wg_ref[:, c * fc:(c + 1) * fc])
        up = _dot(h, wu_ref[:, c * fc:(c + 1) * fc])
        act = (gate * jax.nn.sigmoid(gate) * up).astype(BF16)
        part = _dot(act, wd_ref[c * fc:(c + 1) * fc, :])
        acc = part if acc is None else acc + part
    o_ref[...] = x + _rms(acc, gpost_ref[...])


def _ffn(x2, gpre, gpost, wg, wu, wd):
    n, d = x2.shape
    f = wg.shape[1]
    row = pl.BlockSpec((FFN_ROWS, d), lambda i: (i, 0))
    return pl.pallas_call(
        _ffn_body,
        grid=(n // FFN_ROWS,),
        in_specs=[row, _const_spec((1, d)), _const_spec((1, d)),
                  _const_spec((d, f)), _const_spec((d, f)), _const_spec((f, d))],
        out_specs=row,
        out_shape=jax.ShapeDtypeStruct((n, d), F32),
        compiler_params=_cparams(("arbitrary",)),
        name="ffn",
    )(x2, gpre, gpost, wg, wu, wd)


def _even_body(x_ref, gpre_ref, win_ref, poolw_ref, pscale_ref, lng_ref, lnb_ref, sguw_ref, sgub_ref,
               wout_ref, gpost_ref, o_ref, hist_ref):
    tt = x_ref.shape[1]
    j = pl.program_id(1)

    @pl.when(j == 0)
    def _():
        hist_ref[0:POOL_HIST, :] = jnp.zeros((POOL_HIST, POOL_WIDTH), F32)

    x = x_ref[0]
    h = _rms(x, gpre_ref[...]).astype(BF16)
    z = _dot(h, win_ref[...])
    a = z[:, :POOL_WIDTH]
    u = z[:, POOL_WIDTH:POOL_WIDTH + SGU_WIDTH]
    v = z[:, POOL_WIDTH + SGU_WIDTH:]

    hist_ref[POOL_HIST:POOL_HIST + tt, :] = a
    t_pos = j * tt + lax.broadcasted_iota(jnp.int32, (tt, 1), 0)
    ya = []
    for gi, w in enumerate(POOL_WINDOWS):
        cols = slice(gi * POOL_GDIM, (gi + 1) * POOL_GDIM)
        win_sum = a[:, cols]
        for s in range(1, w):
            win_sum = win_sum + hist_ref[POOL_HIST - s:POOL_HIST - s + tt, cols]
        cnt = jnp.minimum(t_pos + 1, w).astype(F32)
        diff = (win_sum / cnt - a[:, cols]).astype(BF16)
        ya.append(_dot(diff, poolw_ref[gi]))
    ya = jnp.concatenate(ya, axis=1) * pscale_ref[...]
    hist_ref[0:POOL_HIST, :] = hist_ref[tt:tt + POOL_HIST, :]

    ug = jax.nn.gelu(u)
    vg = jax.nn.gelu(v)
    mu = jnp.mean(vg, axis=-1, keepdims=True)
    var = jnp.mean(jnp.square(vg - mu), axis=-1, keepdims=True)
    vn = ((vg - mu) * lax.rsqrt(var + 1e-5) * lng_ref[...] + lnb_ref[...]).astype(BF16)
    r_i = lax.broadcasted_iota(jnp.int32, (SGU_CHUNK, SGU_CHUNK), 0)
    c_i = lax.broadcasted_iota(jnp.int32, (SGU_CHUNK, SGU_CHUNK), 1)
    yb = []
    for g in range(SGU_GROUPS):
        cols = slice(g * SGU_GDIM, (g + 1) * SGU_GDIM)
        wm = jnp.where(r_i >= c_i, sguw_ref[g], 0.0).astype(BF16)
        bcol = sgub_ref[:, g:g + 1]
        parts = []
        for c in range(tt // SGU_CHUNK):
            rows = slice(c * SGU_CHUNK, (c + 1) * SGU_CHUNK)
            parts.append(ug[rows, cols] * (_dot(wm, vn[rows, cols]) + bcol))
        yb.append(jnp.concatenate(parts, axis=0))
    yb = jnp.concatenate(yb, axis=1)

    m = _dot(ya.astype(BF16), wout_ref[0:POOL_WIDTH, :]) + _dot(yb.astype(BF16), wout_ref[POOL_WIDTH:, :])
    o_ref[0] = x + _rms(m, gpost_ref[...])


def _even_layer(x, gpre, win, poolw, pscale, lng, lnb, sguw, sgub_t, wout, gpost):
    b, t, d = x.shape
    tt = EVEN_ROWS
    row = pl.BlockSpec((1, tt, d), lambda bi, j: (bi, j, 0))
    return pl.pallas_call(
        _even_body,
        grid=(b, t // tt),
        in_specs=[row, _const_spec((1, d)), _const_spec(win.shape), _const_spec(poolw.shape),
                  _const_spec(pscale.shape), _const_spec(lng.shape), _const_spec(lnb.shape),
                  _const_spec(sguw.shape), _const_spec(sgub_t.shape), _const_spec(wout.shape),
                  _const_spec((1, d))],
        out_specs=row,
        out_shape=jax.ShapeDtypeStruct((b, t, d), F32),
        scratch_shapes=[pltpu.VMEM((POOL_HIST + tt, POOL_WIDTH), F32)],
        compiler_params=_cparams(("arbitrary", "arbitrary")),
        name="even_mixer",
    )(x, gpre, win, poolw, pscale, lng, lnb, sguw, sgub_t, wout, gpost)


_C_Q = 0
_C_KC = _C_Q + NSA_WIDTH
_C_VC = _C_KC + NSA_KV_WIDTH
_C_KS = _C_VC + NSA_KV_WIDTH
_C_KW = _C_KS + NSA_KV_WIDTH
_C_RQ = _C_KW + NSA_KV_WIDTH
_C_RK = _C_RQ + RET_WIDTH
_C_RV = _C_RK + RET_WIDTH
_C_RG = _C_RV + RET_WIDTH
ODD_COLS = _C_RG + RET_WIDTH
_R_VS = 0
_R_VW = _R_VS + NSA_KV_WIDTH
_R_GT = _R_VW + NSA_KV_WIDTH
ODD_TROWS = _R_GT + NSA_KV_GROUPS * GATE_ROWS
V_AUG = NSA_HEAD_DIM + 16
VT_ROWS = 2 * NSA_KV_GROUPS * V_AUG


def _rope_nsa(z, c, sa, sb):
    half = ROPE_DIM // 2
    return z * c + pltpu.roll(z, LANES - half, 1) * sa + pltpu.roll(z, half, 1) * sb


def _proj_body(x_ref, gpre_ref, w_ref, wt_ref, nc_ref, nsa_ref, nsb_ref, rc_ref, rs_ref,
               q_ref, kc_ref, vc_ref, ksa_ref, kw_ref, vt_ref, gt_ref, rq_ref, rk_ref, rv_ref, rg_ref):
    tm = x_ref.shape[1]
    j = pl.program_id(1)
    x = x_ref[0]
    h = _rms(x, gpre_ref[...]).astype(BF16)
    nc, nsa, nsb = nc_ref[0], nsa_ref[0], nsb_ref[0]
    rc, rs = rc_ref[0], rs_ref[0]

    def cols(start, width):
        return _dot(h, w_ref[:, start:start + width])

    zq = cols(_C_Q, NSA_WIDTH)
    q_scale = NSA_HEAD_DIM ** -0.5 * LOG2_E
    for s in range(NSA_WIDTH // LANES):
        sl = slice(s * LANES, (s + 1) * LANES)
        q_ref[0, :, sl] = (_rope_nsa(zq[:, sl], nc, nsa, nsb) * q_scale).astype(BF16)

    kc_ref[0] = _rope_nsa(cols(_C_KC, LANES), nc, nsa, nsb)
    vc_ref[0] = cols(_C_VC, LANES)

    ks = _rope_nsa(cols(_C_KS, LANES), nc, nsa, nsb)
    kw = _rope_nsa(cols(_C_KW, LANES), nc, nsa, nsb)
    t_pos = j * tm + lax.broadcasted_iota(jnp.int32, (tm, SEL_PAD), 0)
    blk = lax.broadcasted_iota(jnp.int32, (tm, SEL_PAD), 1)
    onehot = jnp.where(t_pos // SEL_BLOCK == blk, 1.0, 0.0).astype(BF16)
    for g in range(NSA_KV_GROUPS):
        sl = slice(g * NSA_HEAD_DIM, (g + 1) * NSA_HEAD_DIM)
        ksa_ref[0, g] = jnp.concatenate([ks[:, sl].astype(BF16), onehot], axis=1)
        kw_ref[0, g] = kw[:, sl].astype(BF16)

    zt = _dot_nt(wt_ref[...], h)
    ones_rows = jnp.where(lax.broadcasted_iota(jnp.int32, (V_AUG - NSA_HEAD_DIM, tm), 0) == 0, 1.0, 0.0).astype(BF16)
    for k in range(2 * NSA_KV_GROUPS):
        vt_ref[0, k * V_AUG:k * V_AUG + NSA_HEAD_DIM] = zt[k * NSA_HEAD_DIM:(k + 1) * NSA_HEAD_DIM].astype(BF16)
        vt_ref[0, k * V_AUG + NSA_HEAD_DIM:(k + 1) * V_AUG] = ones_rows
    gt_ref[0] = jax.nn.sigmoid(zt[_R_GT:])

    k_scale = RET_HEAD_DIM ** -0.5
    zrq = cols(_C_RQ, RET_WIDTH)
    zrk = cols(_C_RK, RET_WIDTH)
    for hh in range(RET_HEADS):
        sl = slice(hh * RET_HEAD_DIM, (hh + 1) * RET_HEAD_DIM)
        zq_h, zk_h = zrq[:, sl], zrk[:, sl]
        rq_ref[0, :, sl] = (zq_h * rc + pltpu.roll(zq_h, RET_HEAD_DIM // 2, 1) * rs).astype(BF16)
        rk_ref[0, :, sl] = ((zk_h * rc + pltpu.roll(zk_h, RET_HEAD_DIM // 2, 1) * rs) * k_scale).astype(BF16)
    rv_ref[0] = cols(_C_RV, RET_WIDTH).astype(BF16)
    rg_ref[0] = cols(_C_RG, RET_WIDTH)


def _odd_proj(x, gpre, w, wt, nc, nsa, nsb, rc, rs):
    b, t, d = x.shape
    tm = PROJ_ROWS
    G = NSA_KV_GROUPS

    def row(width):
        return pl.BlockSpec((1, tm, width), lambda bi, j: (bi, j, 0))

    def grp(width):
        return pl.BlockSpec((1, G, tm, width), lambda bi, j: (bi, 0, j, 0))

    def feat(rows):
        return pl.BlockSpec((1, rows, tm), lambda bi, j: (bi, 0, j))

    out_shape = [
        jax.ShapeDtypeStruct((b, t, NSA_WIDTH), BF16),
        jax.ShapeDtypeStruct((b, t, LANES), F32),
        jax.ShapeDtypeStruct((b, t, LANES), F32),
        jax.ShapeDtypeStruct((b, G, t, LANES), BF16),
        jax.ShapeDtypeStruct((b, G, t, NSA_HEAD_DIM), BF16),
        jax.ShapeDtypeStruct((b, VT_ROWS, t), BF16),
        jax.ShapeDtypeStruct((b, G * GATE_ROWS, t), F32),
        jax.ShapeDtypeStruct((b, t, RET_WIDTH), BF16),
        jax.ShapeDtypeStruct((b, t, RET_WIDTH), BF16),
        jax.ShapeDtypeStruct((b, t, RET_WIDTH), BF16),
        jax.ShapeDtypeStruct((b, t, RET_WIDTH), F32),
    ]
    out_specs = [row(NSA_WIDTH), row(LANES), row(LANES), grp(LANES), grp(NSA_HEAD_DIM), feat(VT_ROWS),
                 feat(G * GATE_ROWS), row(RET_WIDTH), row(RET_WIDTH), row(RET_WIDTH), row(RET_WIDTH)]
    return pl.pallas_call(
        _proj_body,
        grid=(b, t // tm),
        in_specs=[row(d), _const_spec((1, d)), _const_spec(w.shape), _const_spec(wt.shape)] + [row(LANES)] * 5,
        out_specs=out_specs,
        out_shape=out_shape,
        compiler_params=_cparams(("arbitrary", "arbitrary")),
        name="odd_proj",
    )(x, gpre, w, wt, nc, nsa, nsb, rc, rs)


def _compress_body(k_ref, v_ref, kpos_ref, vpos_ref, kw1_ref, vw1_ref, kw2_ref, vw2t_ref, kcm_ref, vct_ref):
    nc = k_ref.shape[1]

    def hidden(x16, pos_ref, w1_ref, g):
        first = _dot((x16 + pos_ref[0:1, :]).astype(BF16), w1_ref[0, g])
        second = _dot((x16 + pos_ref[1:2, :]).astype(BF16), w1_ref[1, g])
        return jax.nn.gelu(first + pltpu.roll(second, nc - 1, 0)).astype(BF16)

    k16 = k_ref[0]
    v16 = v_ref[0]
    for g in range(NSA_KV_GROUPS):
        kcm_ref[0, g] = _dot(hidden(k16, kpos_ref, kw1_ref, g), kw2_ref[...]).astype(BF16)
        vct_ref[0, g] = _dot_nt(vw2t_ref[...], hidden(v16, vpos_ref, vw1_ref, g)).astype(BF16)


def _compress(k16, v16, kpos, vpos, kw1, vw1, kw2, vw2t):
    b, nc, width = k16.shape
    G = NSA_KV_GROUPS
    row = pl.BlockSpec((1, nc, width), lambda bi: (bi, 0, 0))
    return pl.pallas_call(
        _compress_body,
        grid=(b,),
        in_specs=[row, row, _const_spec(kpos.shape), _const_spec(vpos.shape), _const_spec(kw1.shape),
                  _const_spec(vw1.shape), _const_spec(kw2.shape), _const_spec(vw2t.shape)],
        out_specs=[pl.BlockSpec((1, G, nc, NSA_HEAD_DIM), lambda bi: (bi, 0, 0, 0)),
                   pl.BlockSpec((1, G, NSA_HEAD_DIM, nc), lambda bi: (bi, 0, 0, 0))],
        out_shape=[jax.ShapeDtypeStruct((b, G, nc, NSA_HEAD_DIM), BF16),
                   jax.ShapeDtypeStruct((b, G, NSA_HEAD_DIM, nc), BF16)],
        compiler_params=_cparams(("arbitrary",)),
        name="compress",
    )(k16, v16, kpos, vpos, kw1, vw1, kw2, vw2t)


def _nsa_body(q_ref, kcm_ref, vct_ref, c2st_ref, ksa_ref, kw_ref, vt_ref, gt_ref, o_ref,
              score_ref, sc_ref, sw_ref, pw_ref, sa_ref, sb_ref, pa_ref, pb_ref, lhs_ref, m_ref, acc_ref, *, top_k):
    G, M, Q, dh = NSA_KV_GROUPS, NSA_HPG, Q_BLOCK, NSA_HEAD_DIM
    cols = M * Q
    nc = kcm_ref.shape[2]
    i = pl.program_id(1)
    t0 = i * Q
    tq = t0 + (lax.broadcasted_iota(jnp.int32, (1, cols), 1) & (Q - 1))
    groups = range(G)

    q_heads, q_rows = [], []
    for g in groups:
        qb = q_ref[0, :, g * NSA_GROUP_WIDTH:(g + 1) * NSA_GROUP_WIDTH]
        q_heads.append([qb[:, m * dh:(m + 1) * dh] for m in range(M)])
        q_rows.append(jnp.concatenate(q_heads[g], axis=0))

    n_idx = lax.broadcasted_iota(jnp.int32, (nc, 1), 0)
    cmp_ok = (n_idx * CMP_STRIDE + (CMP_BLOCK - 1) <= tq) & (n_idx < nc - 1)
    j_idx = lax.broadcasted_iota(jnp.int32, (SEL_PAD, Q), 0)
    cur = (t0 + lax.broadcasted_iota(jnp.int32, (SEL_PAD, Q), 1)) // SEL_BLOCK
    forced = (j_idx == 0) | (j_idx == cur) | (j_idx == cur - 1)
    valid_tok = (tq >= CMP_BLOCK - 1).astype(F32)
    o_cmp, score = [], []
    for g in groups:
        sc = sc_ref.at[g]
        sc[...] = jnp.where(cmp_ok, _dot_nt(kcm_ref[0, g], q_rows[g]), NEG_INF)
        sc[...] = jnp.exp2(sc[...] - jnp.max(sc[...], axis=0, keepdims=True))
        norm = valid_tok / jnp.sum(sc[...], axis=0, keepdims=True)
        o_cmp.append(_dot(vct_ref[0, g], sc[...].astype(BF16)) * norm)
        p_grp = sc[:, 0:Q] * norm[:, 0:Q]
        for m in range(1, M):
            p_grp = p_grp + sc[:, m * Q:(m + 1) * Q] * norm[:, m * Q:(m + 1) * Q]
        p_hi = p_grp.astype(BF16)
        p_lo = (p_grp - p_hi.astype(F32)).astype(BF16)
        imp = _dot(c2st_ref[...], p_hi) + _dot(c2st_ref[...], p_lo)
        score.append(jnp.where(j_idx <= cur, jnp.where(forced, FORCE_SCORE, imp), -1.0))
        score_ref[g] = score[g]

    def v_aug(branch, g, k0, n):
        r0 = (branch * G + g) * V_AUG
        return vt_ref[0, r0:r0 + V_AUG, pl.ds(k0, n)]

    span = WINDOW + Q
    w0 = pl.multiple_of(jnp.maximum(t0 - WINDOW, 0), Q)
    wpos = w0 + lax.broadcasted_iota(jnp.int32, (span, 1), 0)
    o_win = []
    for g in groups:
        sw = sw_ref.at[g]
        win_ok = (wpos <= tq) & (wpos > tq - WINDOW)
        sw[...] = jnp.where(win_ok, _dot_nt(kw_ref[0, g, pl.ds(w0, span), :], q_rows[g]), NEG_INF)
        pw_ref[g] = jnp.exp2(sw[...] - jnp.max(sw[...], axis=0, keepdims=True)).astype(BF16)
        acc_win = _dot(v_aug(1, g, w0, span), pw_ref[g])
        o_win.append(acc_win[0:dh] / acc_win[dh:dh + 1])

    def rank_step(k, ranks):
        out = []
        for g in groups:
            other = score_ref[g, pl.ds(k, 1), :]
            beats = (other > score[g]) | ((other == score[g]) & (k < j_idx))
            out.append(ranks[g] + beats.astype(jnp.int32))
        return tuple(out)

    n_causal = (t0 + Q) // SEL_BLOCK
    ranks = lax.fori_loop(0, n_causal, rank_step, tuple(jnp.zeros((SEL_PAD, Q), jnp.int32) for _ in groups))

    eye_q = _eye(Q)
    for g in groups:
        chosen = jnp.where((ranks[g] < top_k) & (score[g] >= 0.0), 1.0, 0.0).astype(BF16)
        chosen_t = _dot_nt(eye_q, chosen)
        bias = jnp.where(chosen_t > 0.5, 0.0, NEG_INF).astype(BF16)
        for m in range(M):
            lhs_ref[g, m * Q:(m + 1) * Q, :] = jnp.concatenate([q_heads[g][m], bias], axis=1)
        m_ref[g] = jnp.full((1, cols), NEG_INF, F32)
        acc_ref[g] = jnp.zeros((V_AUG, cols), F32)

    def put_scores(dst, g, kt):
        k0 = pl.multiple_of(kt * SEL_KEYS, SEL_KEYS)
        dst[g] = _dot_nt(ksa_ref[0, g, pl.ds(k0, SEL_KEYS), :], lhs_ref[g])

    def sel_update(src, p_dst, g, kt, mask=None):
        k0 = pl.multiple_of(kt * SEL_KEYS, SEL_KEYS)

        def scores():
            return src[g] if mask is None else jnp.where(mask, src[g], NEG_INF)

        m_i = m_ref[g]
        m_new = jnp.maximum(m_i, jnp.max(scores(), axis=0, keepdims=True))
        p_dst[g] = jnp.exp2(scores() - m_new).astype(BF16)
        acc_ref[g] = jnp.exp2(m_i - m_new) * acc_ref[g] + _dot(v_aug(0, g, k0, SEL_KEYS), p_dst[g])
        m_ref[g] = m_new

    n_full = t0 // SEL_KEYS
    odd = n_full & 1
    for g in groups:
        put_scores(sa_ref, g, 0)

    @pl.when(odd == 1)
    def _():
        for g in groups:
            sel_update(sa_ref, pa_ref, g, 0)
            put_scores(sa_ref, g, 1)

    def sel_pair(pair, _):
        ta = odd + 2 * pair
        for g in groups:
            put_scores(sb_ref, g, ta + 1)
        for g in groups:
            sel_update(sa_ref, pa_ref, g, ta)
        for g in groups:
            put_scores(sa_ref, g, ta + 2)
        for g in groups:
            sel_update(sb_ref, pb_ref, g, ta + 1)
        return 0

    lax.fori_loop(0, n_full // 2, sel_pair, 0)
    kpos = n_full * SEL_KEYS + lax.broadcasted_iota(jnp.int32, (SEL_KEYS, 1), 0)
    o_sel = []
    for g in groups:
        sel_update(sa_ref, pa_ref, g, n_full, mask=kpos <= tq)
        o_sel.append(acc_ref[g, 0:dh, :] / acc_ref[g, dh:dh + 1, :])

    pieces = []
    for g in groups:
        gt = gt_ref[0, g * GATE_ROWS:(g + 1) * GATE_ROWS, :]
        for m in range(M):
            c = slice(m * Q, (m + 1) * Q)
            r = NSA_BRANCHES * m
            mixed = (o_cmp[g][:, c] * gt[r:r + 1] + o_sel[g][:, c] * gt[r + 1:r + 2]
                     + o_win[g][:, c] * gt[r + 2:r + 3])
            pieces.append(_dot_nt(eye_q, mixed.astype(BF16)))
    o_ref[0] = jnp.concatenate(pieces, axis=1).astype(BF16)


def _nsa(q, kcm, vct, c2st, ksa, kw, vt, gt):
    b, t, _ = q.shape
    G = NSA_KV_GROUPS
    nc = kcm.shape[2]
    n_sel = t // SEL_BLOCK
    assert n_sel <= SEL_PAD and t >= WINDOW + Q_BLOCK and t % SEL_KEYS == 0
    top_k = min(SEL_TOPK, n_sel)
    cols, span = NSA_HPG * Q_BLOCK, WINDOW + Q_BLOCK

    def per_batch(shape):
        nd = len(shape)
        return pl.BlockSpec((1,) + shape, lambda bi, i: (bi,) + (0,) * nd)

    token_rows = pl.BlockSpec((1, Q_BLOCK, NSA_WIDTH), lambda bi, i: (bi, i, 0))
    return pl.pallas_call(
        functools.partial(_nsa_body, top_k=top_k),
        grid=(b, t // Q_BLOCK),
        in_specs=[token_rows, per_batch((G, nc, NSA_HEAD_DIM)), per_batch((G, NSA_HEAD_DIM, nc)),
                  _const_spec(c2st.shape), per_batch((G, t, LANES)), per_batch((G, t, NSA_HEAD_DIM)),
                  per_batch((vt.shape[1], t)),
                  pl.BlockSpec((1, G * GATE_ROWS, Q_BLOCK), lambda bi, i: (bi, 0, i))],
        out_specs=token_rows,
        out_shape=jax.ShapeDtypeStruct((b, t, NSA_WIDTH), BF16),
        scratch_shapes=[pltpu.VMEM((G, SEL_PAD, Q_BLOCK), F32),
                        pltpu.VMEM((G, nc, cols), F32),
                        pltpu.VMEM((G, span, cols), F32),
                        pltpu.VMEM((G, span, cols), BF16),
                        pltpu.VMEM((G, SEL_KEYS, cols), F32),
                        pltpu.VMEM((G, SEL_KEYS, cols), F32),
                        pltpu.VMEM((G, SEL_KEYS, cols), BF16),
                        pltpu.VMEM((G, SEL_KEYS, cols), BF16),
                        pltpu.VMEM((G, cols, 2 * NSA_HEAD_DIM), BF16),
                        pltpu.VMEM((G, 1, cols), F32),
                        pltpu.VMEM((G, V_AUG, cols), F32)],
        compiler_params=_cparams(("arbitrary", "arbitrary")),
        name="nsa",
    )(q, kcm, vct, c2st, ksa, kw, vt, gt)


def _ret_body(q_ref, k_ref, v_ref, g_ref, decay_ref, zeta_ref, xi_ref, gchunk_ref, gn_ref, o_ref, state_ref):
    C, d = RET_CHUNK, RET_HEAD_DIM

    @pl.when(pl.program_id(1) == 0)
    def _():
        state_ref[...] = jnp.zeros_like(state_ref)

    for c in range(q_ref.shape[1] // C):
        rows = slice(c * C, (c + 1) * C)
        for h in range(RET_HEADS):
            hc = slice(h * d, (h + 1) * d)
            q, k, v = q_ref[0, rows, hc], k_ref[0, rows, hc], v_ref[0, rows, hc]
            s = (_dot_nt(q, k) * decay_ref[h]).astype(BF16)
            state = state_ref[h]
            o = _dot(s, v) + _dot((q.astype(F32) * xi_ref[h]).astype(BF16), state.astype(BF16))
            kz = (k.astype(F32) * zeta_ref[h]).astype(BF16)
            state_ref[h] = state * gchunk_ref[h] + _dot_tn(kz, v)
            mu = jnp.mean(o, axis=-1, keepdims=True)
            var = jnp.mean(jnp.square(o - mu), axis=-1, keepdims=True)
            o = (o - mu) * lax.rsqrt(var + 1e-5) * gn_ref[h]
            gate = g_ref[0, rows, hc]
            o_ref[0, rows, hc] = (gate * jax.nn.sigmoid(gate) * o).astype(BF16)


def _retention(rq, rk, rv, rg, decay, zeta, xi, gchunk, gn):
    b, t, width = rq.shape
    tok = pl.BlockSpec((1, RET_ROWS, width), lambda bi, c: (bi, c, 0))
    return pl.pallas_call(
        _ret_body,
        grid=(b, t // RET_ROWS),
        in_specs=[tok, tok, tok, tok, _const_spec(decay.shape), _const_spec(zeta.shape), _const_spec(xi.shape),
                  _const_spec(gchunk.shape), _const_spec(gn.shape)],
        out_specs=tok,
        out_shape=jax.ShapeDtypeStruct((b, t, width), BF16),
        scratch_shapes=[pltpu.VMEM((RET_HEADS, RET_HEAD_DIM, RET_HEAD_DIM), F32)],
        compiler_params=_cparams(("arbitrary", "arbitrary")),
        name="retention",
    )(rq, rk, rv, rg, decay, zeta, xi, gchunk, gn)


def _oproj_body(x_ref, yc_ref, yd_ref, w_ref, gpost_ref, o_ref):
    m = _dot(yc_ref[...], w_ref[0:NSA_WIDTH, :]) + _dot(yd_ref[...], w_ref[NSA_WIDTH:, :])
    o_ref[...] = x_ref[...] + _rms(m, gpost_ref[...])


def _odd_out(x2, yc2, yd2, w, gpost):
    n, d = x2.shape
    tm = PROJ_ROWS

    def row(width):
        return pl.BlockSpec((tm, width), lambda i: (i, 0))

    return pl.pallas_call(
        _oproj_body,
        grid=(n // tm,),
        in_specs=[row(d), row(NSA_WIDTH), row(RET_WIDTH), _const_spec(w.shape), _const_spec((1, d))],
        out_specs=row(d),
        out_shape=jax.ShapeDtypeStruct((n, d), F32),
        compiler_params=_cparams(("arbitrary",)),
        name="odd_out",
    )(x2, yc2, yd2, w, gpost)


def _odd_in_weight(w_in):
    sizes = [NSA_WIDTH] + [NSA_KV_WIDTH] * 6 + [NSA_BRANCHES * NSA_HEADS] + [RET_WIDTH] * 4
    offs = np.concatenate([[0], np.cumsum(sizes)])
    q, kc, vc, ks, vs, kw, vw, gt, rq, rk, rv, rg = [w_in[:, offs[n]:offs[n + 1]] for n in range(len(sizes))]
    w = jnp.concatenate([q, kc, vc, ks, kw, rq, rk, rv, rg], axis=1).astype(BF16)
    per_group = NSA_BRANCHES * NSA_HPG
    gates = [jnp.pad(gt[:, g * per_group:(g + 1) * per_group], ((0, 0), (0, GATE_ROWS - per_group)))
             for g in range(NSA_KV_GROUPS)]
    wt = jnp.concatenate([vs, vw] + gates, axis=1).T.astype(BF16)
    return w, wt


def _rope_tables(positions):
    pos = positions.astype(F32)[..., None]
    inv = 1.0 / (ROPE_THETA ** (jnp.arange(0, ROPE_DIM, 2, dtype=F32) / ROPE_DIM))
    ang = pos * inv
    c, s = jnp.cos(ang), jnp.sin(ang)
    z = jnp.zeros(ang.shape[:-1] + (NSA_HEAD_DIM - ROPE_DIM,), F32)
    z8 = jnp.zeros_like(s)
    reps = LANES // NSA_HEAD_DIM
    nc = jnp.concatenate([c, c, z + 1.0] * reps, axis=-1)
    nsa = jnp.concatenate([-s, z8, z] * reps, axis=-1)
    nsb = jnp.concatenate([z8, s, z] * reps, axis=-1)
    inv_r = 1.0 / (RET_THETA ** (jnp.arange(0, RET_HEAD_DIM, 2, dtype=F32) / RET_HEAD_DIM))
    ang_r = pos * inv_r
    cr, sr = jnp.cos(ang_r), jnp.sin(ang_r)
    rc = jnp.concatenate([cr, cr], axis=-1)
    rs = jnp.concatenate([-sr, sr], axis=-1)
    return nc, nsa, nsb, rc, rs


def _retention_tables():
    H, C, d = RET_HEADS, RET_CHUNK, RET_HEAD_DIM
    log_gamma = jnp.log1p(-jnp.exp2(-5.0 - jnp.arange(H, dtype=F32)))
    idx = jnp.arange(C, dtype=F32)
    rel = idx[:, None] - idx[None, :]
    decay = jnp.where(rel >= 0, jnp.exp(jnp.maximum(rel, 0.0)[None] * log_gamma[:, None, None]), 0.0)
    zeta = jnp.exp((C - 1 - idx)[None, :] * log_gamma[:, None])
    xi = jnp.exp((idx + 1.0)[None, :] * log_gamma[:, None])
    gchunk = jnp.exp(C * log_gamma)
    zeta_b = jnp.broadcast_to(zeta[:, :, None], (H, C, d))
    xi_b = jnp.broadcast_to(xi[:, :, None], (H, C, d))
    gchunk_b = jnp.broadcast_to(gchunk[:, None, None], (H, 1, d))
    return decay, zeta_b, xi_b, gchunk_b


def _cmp_to_sel_t(t_len):
    nc = t_len // CMP_STRIDE
    n_cmp = (t_len - CMP_BLOCK) // CMP_STRIDE + 1
    n_sel = t_len // SEL_BLOCK
    c_start = np.arange(nc) * CMP_STRIDE
    s_start = np.arange(SEL_PAD) * SEL_BLOCK
    hit = ((c_start[None, :] < s_start[:, None] + SEL_BLOCK) & (c_start[None, :] + CMP_BLOCK > s_start[:, None])
           & (np.arange(nc)[None, :] < n_cmp) & (np.arange(SEL_PAD)[:, None] < n_sel))
    return jnp.asarray(hit.astype(np.float32), dtype=BF16)


def _compress_weights(pos, w1, w2):
    G, dh, half = NSA_KV_GROUPS, NSA_HEAD_DIM, CMP_BLOCK // 2
    pos_rows = jnp.broadcast_to(pos.reshape(2, half, 1, dh), (2, half, G, dh)).reshape(2, half * G * dh)
    w1h = w1.reshape(2, half, dh, CMP_HIDDEN)
    per_group = []
    for g in range(G):
        wg = jnp.zeros((2, half, G, dh, CMP_HIDDEN), w1.dtype).at[:, :, g].set(w1h)
        per_group.append(wg.reshape(2, half * G * dh, CMP_HIDDEN))
    return pos_rows, jnp.stack(per_group, axis=1).astype(BF16), w2.astype(BF16)


def _odd_layer(x, positions_tables, ret_tables, c2st, gpre, w_in, cmp_k_pos, cmp_k_w1, cmp_k_w2,
               cmp_v_pos, cmp_v_w1, cmp_v_w2, gn_g, w_out, gpost):
    b, t, d = x.shape
    w, wt = _odd_in_weight(w_in)
    (q, kc, vc, ksa, kw, vt, gt, rq, rk, rv, rg) = _odd_proj(x, gpre, w, wt, *positions_tables)
    kpos, kw1, kw2 = _compress_weights(cmp_k_pos, cmp_k_w1, cmp_k_w2)
    vpos, vw1, vw2 = _compress_weights(cmp_v_pos, cmp_v_w1, cmp_v_w2)
    rows16 = (b, t // CMP_STRIDE, CMP_STRIDE * LANES)
    kcm, vct = _compress(kc.reshape(rows16), vc.reshape(rows16), kpos, vpos, kw1, vw1, kw2, vw2.T)
    yc = _nsa(q, kcm, vct, c2st, ksa, kw, vt, gt)
    yd = _retention(rq, rk, rv, rg, *ret_tables, gn_g.reshape(RET_HEADS, 1, RET_HEAD_DIM))
    out = _odd_out(x.reshape(b * t, d), yc.reshape(b * t, NSA_WIDTH), yd.reshape(b * t, RET_WIDTH),
                   w_out.astype(BF16), gpost)
    return out.reshape(b, t, d)


def kernel(x, positions, ln_mix_pre, ln_mix_post, ln_ffn_pre, ln_ffn_post, ffn_w_gate, ffn_w_up, ffn_w_down,
           ev_w_in, ev_pool_w, ev_pool_scale, ev_sgu_ln_g, ev_sgu_ln_b, ev_sgu_w, ev_sgu_b, ev_w_out,
           od_w_in, od_cmp_k_pos, od_cmp_k_w1, od_cmp_k_w2, od_cmp_v_pos, od_cmp_v_w1, od_cmp_v_w2,
           od_ret_gn_g, od_w_out):
    b, t, d = x.shape
    depth = ln_mix_pre.shape[0]
    rope = _rope_tables(positions)
    ret_tables = _retention_tables()
    c2st = _cmp_to_sel_t(t)
    for layer in range(depth):
        gpre = ln_mix_pre[layer].reshape(1, d)
        gpost = ln_mix_post[layer].reshape(1, d)
        if layer % 2 == 0:
            e = layer // 2
            x = _even_layer(x, gpre, ev_w_in[e].astype(BF16), ev_pool_w[e].astype(BF16),
                            ev_pool_scale[e].reshape(1, POOL_WIDTH), ev_sgu_ln_g[e].reshape(1, SGU_WIDTH),
                            ev_sgu_ln_b[e].reshape(1, SGU_WIDTH), ev_sgu_w[e], ev_sgu_b[e].T,
                            ev_w_out[e].astype(BF16), gpost)
        else:
            o = layer // 2
            x = _odd_layer(x, rope, ret_tables, c2st, gpre, od_w_in[o], od_cmp_k_pos[o], od_cmp_k_w1[o],
                           od_cmp_k_w2[o], od_cmp_v_pos[o], od_cmp_v_w1[o], od_cmp_v_w2[o], od_ret_gn_g[o],
                           od_w_out[o], gpost)
        x = _ffn(x.reshape(b * t, d), ln_ffn_pre[layer].reshape(1, d), ln_ffn_post[layer].reshape(1, d),
                 ffn_w_gate[layer].astype(BF16), ffn_w_up[layer].astype(BF16),
                 ffn_w_down[layer].astype(BF16)).reshape(b, t, d)
    return x
```

```python
import functools

import numpy as np
import jax
import jax.numpy as jnp
from jax import lax
from jax.experimental import pallas as pl
from jax.experimental.pallas import tpu as pltpu

F32 = jnp.float32
BF16 = jnp.bfloat16

D_MODEL = 1024
POOL_WINDOWS = (2, 4, 8, 16)
POOL_GROUPS = 4
POOL_WIDTH = D_MODEL // 2
POOL_GDIM = POOL_WIDTH // POOL_GROUPS
POOL_HIST = 16
SGU_GROUPS = 4
SGU_WIDTH = D_MODEL // 2
SGU_GDIM = SGU_WIDTH // SGU_GROUPS
SGU_CHUNK = 128
EVEN_IN = POOL_WIDTH + 2 * SGU_WIDTH
NSA_HEADS = 8
NSA_KV_GROUPS = 2
NSA_HPG = NSA_HEADS // NSA_KV_GROUPS
NSA_HEAD_DIM = 64
NSA_WIDTH = NSA_HEADS * NSA_HEAD_DIM
NSA_GROUP_WIDTH = NSA_HPG * NSA_HEAD_DIM
NSA_KV_WIDTH = NSA_KV_GROUPS * NSA_HEAD_DIM
NSA_BRANCHES = 3
GATE_ROWS = 16
CMP_BLOCK = 32
CMP_STRIDE = 16
CMP_HIDDEN = 128
SEL_BLOCK = 64
SEL_TOPK = 16
SEL_PAD = 64
WINDOW = 512
Q_BLOCK = 256
FORCE_SCORE = 1.0e4
ROPE_THETA = 500000.0
ROPE_DIM = NSA_HEAD_DIM // 4
RET_HEADS = 4
RET_HEAD_DIM = 128
RET_WIDTH = RET_HEADS * RET_HEAD_DIM
RET_CHUNK = 128
RET_THETA = 10000.0
ODD_MIX = NSA_WIDTH + RET_WIDTH
NEG_INF = -1.0e30
LOG2_E = 1.4426950408889634
LANES = 128
SUBLANES = 8

VMEM_LIMIT = 56 * 1024 * 1024
FFN_ROWS = 512
FFN_CHUNKS = 2
EVEN_ROWS = 512
PROJ_ROWS = 512
SEL_KEYS = 512
RANK_CHUNK = 8
RET_ROWS = 512


def _cparams(sem):
    return pltpu.CompilerParams(dimension_semantics=sem, vmem_limit_bytes=VMEM_LIMIT)


def _const_spec(shape):
    nd = len(shape)
    return pl.BlockSpec(shape, lambda *_: (0,) * nd, pipeline_mode=pl.Buffered(1))


def _dot(a, b):
    return jnp.dot(a, b, preferred_element_type=F32)


def _dot_nt(a, b):
    return lax.dot_general(a, b, (((1,), (1,)), ((), ())), preferred_element_type=F32)


def _dot_tn(a, b):
    return lax.dot_general(a, b, (((0,), (0,)), ((), ())), preferred_element_type=F32)


def _rms(x, g, eps=1e-6):
    return x * lax.rsqrt(jnp.mean(x * x, axis=-1, keepdims=True) + eps) * g


def _softmax2_cols(s):
    e = jnp.exp2(s - jnp.max(s, axis=0, keepdims=True))
    return e / jnp.sum(e, axis=0, keepdims=True)


def _eye(n):
    return jnp.where(lax.broadcasted_iota(jnp.int32, (n, n), 0) == lax.broadcasted_iota(jnp.int32, (n, n), 1),
                     1.0, 0.0).astype(BF16)


def _ffn_body(x_ref, gpre_ref, gpost_ref, wg_ref, wu_ref, wd_ref, o_ref):
    x = x_ref[...]
    h = _rms(x, gpre_ref[...]).astype(BF16)
    f_total = wg_ref.shape[1]
    fc = f_total // FFN_CHUNKS
    acc = None
    for c in range(FFN_CHUNKS):
        gate = _dot(h, wg_ref[:, c * fc:(c + 1) * fc])
        up = _dot(h, wu_ref[:, c * fc:(c + 1) * fc])
        act = (gate * jax.nn.sigmoid(gate) * up).astype(BF16)
        part = _dot(act, wd_ref[c * fc:(c + 1) * fc, :])
        acc = part if acc is None else acc + part
    o_ref[...] = x + _rms(acc, gpost_ref[...])


def _ffn(x2, gpre, gpost, wg, wu, wd):
    n, d = x2.shape
    f = wg.shape[1]
    row = pl.BlockSpec((FFN_ROWS, d), lambda i: (i, 0))
    return pl.pallas_call(
        _ffn_body,
        grid=(n // FFN_ROWS,),
        in_specs=[row, _const_spec((1, d)), _const_spec((1, d)),
                  _const_spec((d, f)), _const_spec((d, f)), _const_spec((f, d))],
        out_specs=row,
        out_shape=jax.ShapeDtypeStruct((n, d), F32),
        compiler_params=_cparams(("arbitrary",)),
        name="ffn",
    )(x2, gpre, gpost, wg, wu, wd)


def _even_body(x_ref, gpre_ref, win_ref, poolw_ref, pscale_ref, lng_ref, lnb_ref, sguw_ref, sgub_ref,
               wout_ref, gpost_ref, o_ref, hist_ref):
    tt = x_ref.shape[1]
    j = pl.program_id(1)

    @pl.when(j == 0)
    def _():
        hist_ref[0:POOL_HIST, :] = jnp.zeros((POOL_HIST, POOL_WIDTH), F32)

    x = x_ref[0]
    h = _rms(x, gpre_ref[...]).astype(BF16)
    z = _dot(h, win_ref[...])
    a = z[:, :POOL_WIDTH]
    u = z[:, POOL_WIDTH:POOL_WIDTH + SGU_WIDTH]
    v = z[:, POOL_WIDTH + SGU_WIDTH:]

    hist_ref[POOL_HIST:POOL_HIST + tt, :] = a
    t_pos = j * tt + lax.broadcasted_iota(jnp.int32, (tt, 1), 0)
    ya = []
    for gi, w in enumerate(POOL_WINDOWS):
        cols = slice(gi * POOL_GDIM, (gi + 1) * POOL_GDIM)
        win_sum = a[:, cols]
        for s in range(1, w):
            win_sum = win_sum + hist_ref[POOL_HIST - s:POOL_HIST - s + tt, cols]
        cnt = jnp.minimum(t_pos + 1, w).astype(F32)
        diff = (win_sum / cnt - a[:, cols]).astype(BF16)
        ya.append(_dot(diff, poolw_ref[gi]))
    ya = jnp.concatenate(ya, axis=1) * pscale_ref[...]
    hist_ref[0:POOL_HIST, :] = hist_ref[tt:tt + POOL_HIST, :]

    ug = jax.nn.gelu(u)
    vg = jax.nn.gelu(v)
    mu = jnp.mean(vg, axis=-1, keepdims=True)
    var = jnp.mean(jnp.square(vg - mu), axis=-1, keepdims=True)
    vn = ((vg - mu) * lax.rsqrt(var + 1e-5) * lng_ref[...] + lnb_ref[...]).astype(BF16)
    r_i = lax.broadcasted_iota(jnp.int32, (SGU_CHUNK, SGU_CHUNK), 0)
    c_i = lax.broadcasted_iota(jnp.int32, (SGU_CHUNK, SGU_CHUNK), 1)
    yb = []
    for g in range(SGU_GROUPS):
        cols = slice(g * SGU_GDIM, (g + 1) * SGU_GDIM)
        wm = jnp.where(r_i >= c_i, sguw_ref[g], 0.0).astype(BF16)
        bcol = sgub_ref[:, g:g + 1]
        parts = []
        for c in range(tt // SGU_CHUNK):
            rows = slice(c * SGU_CHUNK, (c + 1) * SGU_CHUNK)
            parts.append(ug[rows, cols] * (_dot(wm, vn[rows, cols]) + bcol))
        yb.append(jnp.concatenate(parts, axis=0))
    yb = jnp.concatenate(yb, axis=1)

    m = _dot(ya.astype(BF16), wout_ref[0:POOL_WIDTH, :]) + _dot(yb.astype(BF16), wout_ref[POOL_WIDTH:, :])
    o_ref[0] = x + _rms(m, gpost_ref[...])


def _even_layer(x, gpre, win, poolw, pscale, lng, lnb, sguw, sgub_t, wout, gpost):
    b, t, d = x.shape
    tt = EVEN_ROWS
    row = pl.BlockSpec((1, tt, d), lambda bi, j: (bi, j, 0))
    return pl.pallas_call(
        _even_body,
        grid=(b, t // tt),
        in_specs=[row, _const_spec((1, d)), _const_spec(win.shape), _const_spec(poolw.shape),
                  _const_spec(pscale.shape), _const_spec(lng.shape), _const_spec(lnb.shape),
                  _const_spec(sguw.shape), _const_spec(sgub_t.shape), _const_spec(wout.shape),
                  _const_spec((1, d))],
        out_specs=row,
        out_shape=jax.ShapeDtypeStruct((b, t, d), F32),
        scratch_shapes=[pltpu.VMEM((POOL_HIST + tt, POOL_WIDTH), F32)],
        compiler_params=_cparams(("arbitrary", "arbitrary")),
        name="even_mixer",
    )(x, gpre, win, poolw, pscale, lng, lnb, sguw, sgub_t, wout, gpost)


_C_Q = 0
_C_KC = _C_Q + NSA_WIDTH
_C_VC = _C_KC + NSA_KV_WIDTH
_C_KS = _C_VC + NSA_KV_WIDTH
_C_KW = _C_KS + NSA_KV_WIDTH
_C_RQ = _C_KW + NSA_KV_WIDTH
_C_RK = _C_RQ + RET_WIDTH
_C_RV = _C_RK + RET_WIDTH
_C_RG = _C_RV + RET_WIDTH
ODD_COLS = _C_RG + RET_WIDTH
_R_VS = 0
_R_VW = _R_VS + NSA_KV_WIDTH
_R_GT = _R_VW + NSA_KV_WIDTH
ODD_TROWS = _R_GT + NSA_KV_GROUPS * GATE_ROWS
V_AUG = NSA_HEAD_DIM + 16
VT_ROWS = 2 * NSA_KV_GROUPS * V_AUG


def _rope_nsa(z, c, sa, sb):
    half = ROPE_DIM // 2
    return z * c + pltpu.roll(z, LANES - half, 1) * sa + pltpu.roll(z, half, 1) * sb


def _proj_body(x_ref, gpre_ref, w_ref, wt_ref, nc_ref, nsa_ref, nsb_ref, rc_ref, rs_ref,
               q_ref, kc_ref, vc_ref, ksa_ref, kw_ref, vt_ref, gt_ref, rq_ref, rk_ref, rv_ref, rg_ref):
    tm = x_ref.shape[1]
    j = pl.program_id(1)
    x = x_ref[0]
    h = _rms(x, gpre_ref[...]).astype(BF16)
    nc, nsa, nsb = nc_ref[0], nsa_ref[0], nsb_ref[0]
    rc, rs = rc_ref[0], rs_ref[0]

    def cols(start, width):
        return _dot(h, w_ref[:, start:start + width])

    zq = cols(_C_Q, NSA_WIDTH)
    q_scale = NSA_HEAD_DIM ** -0.5 * LOG2_E
    for s in range(NSA_WIDTH // LANES):
        sl = slice(s * LANES, (s + 1) * LANES)
        q_ref[0, :, sl] = (_rope_nsa(zq[:, sl], nc, nsa, nsb) * q_scale).astype(BF16)

    kc_ref[0] = _rope_nsa(cols(_C_KC, LANES), nc, nsa, nsb)
    vc_ref[0] = cols(_C_VC, LANES)

    ks = _rope_nsa(cols(_C_KS, LANES), nc, nsa, nsb)
    kw = _rope_nsa(cols(_C_KW, LANES), nc, nsa, nsb)
    t_pos = j * tm + lax.broadcasted_iota(jnp.int32, (tm, SEL_PAD), 0)
    blk = lax.broadcasted_iota(jnp.int32, (tm, SEL_PAD), 1)
    onehot = jnp.where(t_pos // SEL_BLOCK == blk, 1.0, 0.0).astype(BF16)
    for g in range(NSA_KV_GROUPS):
        sl = slice(g * NSA_HEAD_DIM, (g + 1) * NSA_HEAD_DIM)
        ksa_ref[0, g] = jnp.concatenate([ks[:, sl].astype(BF16), onehot], axis=1)
        kw_ref[0, g] = kw[:, sl].astype(BF16)

    zt = _dot_nt(wt_ref[...], h)
    ones_rows = jnp.where(lax.broadcasted_iota(jnp.int32, (V_AUG - NSA_HEAD_DIM, tm), 0) == 0, 1.0, 0.0).astype(BF16)
    for k in range(2 * NSA_KV_GROUPS):
        vt_ref[0, k * V_AUG:k * V_AUG + NSA_HEAD_DIM] = zt[k * NSA_HEAD_DIM:(k + 1) * NSA_HEAD_DIM].astype(BF16)
        vt_ref[0, k * V_AUG + NSA_HEAD_DIM:(k + 1) * V_AUG] = ones_rows
    gt_ref[0] = jax.nn.sigmoid(zt[_R_GT:])

    k_scale = RET_HEAD_DIM ** -0.5
    zrq = cols(_C_RQ, RET_WIDTH)
    zrk = cols(_C_RK, RET_WIDTH)
    for hh in range(RET_HEADS):
        sl = slice(hh * RET_HEAD_DIM, (hh + 1) * RET_HEAD_DIM)
        zq_h, zk_h = zrq[:, sl], zrk[:, sl]
        rq_ref[0, :, sl] = (zq_h * rc + pltpu.roll(zq_h, RET_HEAD_DIM // 2, 1) * rs).astype(BF16)
        rk_ref[0, :, sl] = ((zk_h * rc + pltpu.roll(zk_h, RET_HEAD_DIM // 2, 1) * rs) * k_scale).astype(BF16)
    rv_ref[0] = cols(_C_RV, RET_WIDTH).astype(BF16)
    rg_ref[0] = cols(_C_RG, RET_WIDTH)


def _odd_proj(x, gpre, w, wt, nc, nsa, nsb, rc, rs):
    b, t, d = x.shape
    tm = PROJ_ROWS
    G = NSA_KV_GROUPS

    def row(width):
        return pl.BlockSpec((1, tm, width), lambda bi, j: (bi, j, 0))

    def grp(width):
        return pl.BlockSpec((1, G, tm, width), lambda bi, j: (bi, 0, j, 0))

    def feat(rows):
        return pl.BlockSpec((1, rows, tm), lambda bi, j: (bi, 0, j))

    out_shape = [
        jax.ShapeDtypeStruct((b, t, NSA_WIDTH), BF16),
        jax.ShapeDtypeStruct((b, t, LANES), F32),
        jax.ShapeDtypeStruct((b, t, LANES), F32),
        jax.ShapeDtypeStruct((b, G, t, LANES), BF16),
        jax.ShapeDtypeStruct((b, G, t, NSA_HEAD_DIM), BF16),
        jax.ShapeDtypeStruct((b, VT_ROWS, t), BF16),
        jax.ShapeDtypeStruct((b, G * GATE_ROWS, t), F32),
        jax.ShapeDtypeStruct((b, t, RET_WIDTH), BF16),
        jax.ShapeDtypeStruct((b, t, RET_WIDTH), BF16),
        jax.ShapeDtypeStruct((b, t, RET_WIDTH), BF16),
        jax.ShapeDtypeStruct((b, t, RET_WIDTH), F32),
    ]
    out_specs = [row(NSA_WIDTH), row(LANES), row(LANES), grp(LANES), grp(NSA_HEAD_DIM), feat(VT_ROWS),
                 feat(G * GATE_ROWS), row(RET_WIDTH), row(RET_WIDTH), row(RET_WIDTH), row(RET_WIDTH)]
    return pl.pallas_call(
        _proj_body,
        grid=(b, t // tm),
        in_specs=[row(d), _const_spec((1, d)), _const_spec(w.shape), _const_spec(wt.shape)] + [row(LANES)] * 5,
        out_specs=out_specs,
        out_shape=out_shape,
        compiler_params=_cparams(("arbitrary", "arbitrary")),
        name="odd_proj",
    )(x, gpre, w, wt, nc, nsa, nsb, rc, rs)


def _compress_body(k_ref, v_ref, kpos_ref, vpos_ref, kw1_ref, vw1_ref, kw2_ref, vw2t_ref, kcm_ref, vct_ref):
    nc = k_ref.shape[1]

    def hidden(x16, pos_ref, w1_ref, g):
        first = _dot((x16 + pos_ref[0:1, :]).astype(BF16), w1_ref[0, g])
        second = _dot((x16 + pos_ref[1:2, :]).astype(BF16), w1_ref[1, g])
        return jax.nn.gelu(first + pltpu.roll(second, nc - 1, 0)).astype(BF16)

    k16 = k_ref[0]
    v16 = v_ref[0]
    for g in range(NSA_KV_GROUPS):
        kcm_ref[0, g] = _dot(hidden(k16, kpos_ref, kw1_ref, g), kw2_ref[...]).astype(BF16)
        vct_ref[0, g] = _dot_nt(vw2t_ref[...], hidden(v16, vpos_ref, vw1_ref, g)).astype(BF16)


def _compress(k16, v16, kpos, vpos, kw1, vw1, kw2, vw2t):
    b, nc, width = k16.shape
    G = NSA_KV_GROUPS
    row = pl.BlockSpec((1, nc, width), lambda bi: (bi, 0, 0))
    return pl.pallas_call(
        _compress_body,
        grid=(b,),
        in_specs=[row, row, _const_spec(kpos.shape), _const_spec(vpos.shape), _const_spec(kw1.shape),
                  _const_spec(vw1.shape), _const_spec(kw2.shape), _const_spec(vw2t.shape)],
        out_specs=[pl.BlockSpec((1, G, nc, NSA_HEAD_DIM), lambda bi: (bi, 0, 0, 0)),
                   pl.BlockSpec((1, G, NSA_HEAD_DIM, nc), lambda bi: (bi, 0, 0, 0))],
        out_shape=[jax.ShapeDtypeStruct((b, G, nc, NSA_HEAD_DIM), BF16),
                   jax.ShapeDtypeStruct((b, G, NSA_HEAD_DIM, nc), BF16)],
        compiler_params=_cparams(("arbitrary",)),
        name="compress",
    )(k16, v16, kpos, vpos, kw1, vw1, kw2, vw2t)


def _nsa_body(q_ref, kcm_ref, vct_ref, c2st_ref, ksa_ref, kw_ref, vt_ref, gt_ref, o_ref,
              score_ref, rank_ref, sc_ref, sw_ref, pw_ref, sa_ref, sb_ref, pa_ref, pb_ref, lhs_ref, m_ref, acc_ref, *, top_k):
    G, M, Q, dh = NSA_KV_GROUPS, NSA_HPG, Q_BLOCK, NSA_HEAD_DIM
    cols = M * Q
    nc = kcm_ref.shape[2]
    i = pl.program_id(1)
    t0 = i * Q
    tq = t0 + (lax.broadcasted_iota(jnp.int32, (1, cols), 1) & (Q - 1))
    groups = range(G)

    q_heads, q_rows = [], []
    for g in groups:
        qb = q_ref[0, :, g * NSA_GROUP_WIDTH:(g + 1) * NSA_GROUP_WIDTH]
        q_heads.append([qb[:, m * dh:(m + 1) * dh] for m in range(M)])
        q_rows.append(jnp.concatenate(q_heads[g], axis=0))

    n_idx = lax.broadcasted_iota(jnp.int32, (nc, 1), 0)
    cmp_ok = (n_idx * CMP_STRIDE + (CMP_BLOCK - 1) <= tq) & (n_idx < nc - 1)
    j_idx = lax.broadcasted_iota(jnp.int32, (SEL_PAD, Q), 0)
    cur = (t0 + lax.broadcasted_iota(jnp.int32, (SEL_PAD, Q), 1)) // SEL_BLOCK
    forced = (j_idx == 0) | (j_idx == cur) | (j_idx == cur - 1)
    valid_tok = (tq >= CMP_BLOCK - 1).astype(F32)
    o_cmp = []
    for g in groups:
        sc = sc_ref.at[g]
        sc[...] = jnp.where(cmp_ok, _dot_nt(kcm_ref[0, g], q_rows[g]), NEG_INF)
        sc[...] = jnp.exp2(sc[...] - jnp.max(sc[...], axis=0, keepdims=True))
        norm = valid_tok / jnp.sum(sc[...], axis=0, keepdims=True)
        o_cmp.append(_dot(vct_ref[0, g], sc[...].astype(BF16)) * norm)
        p_grp = sc[:, 0:Q] * norm[:, 0:Q]
        for m in range(1, M):
            p_grp = p_grp + sc[:, m * Q:(m + 1) * Q] * norm[:, m * Q:(m + 1) * Q]
        p_hi = p_grp.astype(BF16)
        p_lo = (p_grp - p_hi.astype(F32)).astype(BF16)
        imp = _dot(c2st_ref[...], p_hi) + _dot(c2st_ref[...], p_lo)
        score_ref[g] = jnp.where(j_idx <= cur, jnp.where(forced, FORCE_SCORE, imp), -1.0)

    def v_aug(branch, g, k0, n):
        r0 = (branch * G + g) * V_AUG
        return vt_ref[0, r0:r0 + V_AUG, pl.ds(k0, n)]

    span = WINDOW + Q
    w0 = pl.multiple_of(jnp.maximum(t0 - WINDOW, 0), Q)
    wpos = w0 + lax.broadcasted_iota(jnp.int32, (span, 1), 0)
    o_win = []
    for g in groups:
        sw = sw_ref.at[g]
        win_ok = (wpos <= tq) & (wpos > tq - WINDOW)
        sw[...] = jnp.where(win_ok, _dot_nt(kw_ref[0, g, pl.ds(w0, span), :], q_rows[g]), NEG_INF)
        pw_ref[g] = jnp.exp2(sw[...] - jnp.max(sw[...], axis=0, keepdims=True)).astype(BF16)
        acc_win = _dot(v_aug(1, g, w0, span), pw_ref[g])
        o_win.append(acc_win[0:dh] / acc_win[dh:dh + 1])

    n_causal = (t0 + Q) // SEL_BLOCK
    rank_ref[...] = jnp.zeros(rank_ref.shape, jnp.int32)
    row_in_tile = lax.broadcasted_iota(jnp.int32, (SUBLANES, LANES), 0)
    for c0 in range(0, SEL_PAD, RANK_CHUNK):
        @pl.when(c0 < n_causal)
        def _(c0=c0):
            for g in groups:
                for l0 in range(0, Q, LANES):
                    ln = slice(l0, l0 + LANES)
                    others = [score_ref[g, k:k + 1, ln] for k in range(c0, c0 + RANK_CHUNK)]
                    for r0 in range(0, SEL_PAD, SUBLANES):
                        mine = score_ref[g, r0:r0 + SUBLANES, ln]
                        count = rank_ref[g, r0:r0 + SUBLANES, ln]
                        for k, other in zip(range(c0, c0 + RANK_CHUNK), others):
                            if r0 > k:
                                beats = other >= mine
                            elif r0 + SUBLANES <= k:
                                beats = other > mine
                            else:
                                beats = (other > mine) | ((other == mine) & (k < r0 + row_in_tile))
                            count = count + beats.astype(jnp.int32)
                        rank_ref[g, r0:r0 + SUBLANES, ln] = count

    eye_q = _eye(Q)
    for g in groups:
        chosen = jnp.where((rank_ref[g] < top_k) & (score_ref[g] >= 0.0), 1.0, 0.0).astype(BF16)
        chosen_t = _dot_nt(eye_q, chosen)
        bias = jnp.where(chosen_t > 0.5, 0.0, NEG_INF).astype(BF16)
        for m in range(M):
            lhs_ref[g, m * Q:(m + 1) * Q, :] = jnp.concatenate([q_heads[g][m], bias], axis=1)
        m_ref[g] = jnp.full((1, cols), NEG_INF, F32)
        acc_ref[g] = jnp.zeros((V_AUG, cols), F32)

    def put_scores(dst, g, kt):
        k0 = pl.multiple_of(kt * SEL_KEYS, SEL_KEYS)
        dst[g] = _dot_nt(ksa_ref[0, g, pl.ds(k0, SEL_KEYS), :], lhs_ref[g])

    def sel_update(src, p_dst, g, kt, mask=None):
        k0 = pl.multiple_of(kt * SEL_KEYS, SEL_KEYS)

        def scores():
            return src[g] if mask is None else jnp.where(mask, src[g], NEG_INF)

        m_i = m_ref[g]
        m_new = jnp.maximum(m_i, jnp.max(scores(), axis=0, keepdims=True))
        p_dst[g] = jnp.exp2(scores() - m_new).astype(BF16)
        acc_ref[g] = jnp.exp2(m_i - m_new) * acc_ref[g] + _dot(v_aug(0, g, k0, SEL_KEYS), p_dst[g])
        m_ref[g] = m_new

    n_full = t0 // SEL_KEYS
    odd = n_full & 1
    for g in groups:
        put_scores(sa_ref, g, 0)

    @pl.when(odd == 1)
    def _():
        for g in groups:
            sel_update(sa_ref, pa_ref, g, 0)
            put_scores(sa_ref, g, 1)

    def sel_pair(pair, _):
        ta = odd + 2 * pair
        for g in groups:
            put_scores(sb_ref, g, ta + 1)
        for g in groups:
            sel_update(sa_ref, pa_ref, g, ta)
        for g in groups:
            put_scores(sa_ref, g, ta + 2)
        for g in groups:
            sel_update(sb_ref, pb_ref, g, ta + 1)
        return 0

    lax.fori_loop(0, n_full // 2, sel_pair, 0)
    kpos = n_full * SEL_KEYS + lax.broadcasted_iota(jnp.int32, (SEL_KEYS, 1), 0)
    o_sel = []
    for g in groups:
        sel_update(sa_ref, pa_ref, g, n_full, mask=kpos <= tq)
        o_sel.append(acc_ref[g, 0:dh, :] / acc_ref[g, dh:dh + 1, :])

    pieces = []
    for g in groups:
        gt = gt_ref[0, g * GATE_ROWS:(g + 1) * GATE_ROWS, :]
        for m in range(M):
            c = slice(m * Q, (m + 1) * Q)
            r = NSA_BRANCHES * m
            mixed = (o_cmp[g][:, c] * gt[r:r + 1] + o_sel[g][:, c] * gt[r + 1:r + 2]
                     + o_win[g][:, c] * gt[r + 2:r + 3])
            pieces.append(_dot_nt(eye_q, mixed.astype(BF16)))
    o_ref[0] = jnp.concatenate(pieces, axis=1).astype(BF16)


def _nsa(q, kcm, vct, c2st, ksa, kw, vt, gt):
    b, t, _ = q.shape
    G = NSA_KV_GROUPS
    nc = kcm.shape[2]
    n_sel = t // SEL_BLOCK
    assert n_sel <= SEL_PAD and t >= WINDOW + Q_BLOCK and t % SEL_KEYS == 0
    top_k = min(SEL_TOPK, n_sel)
    cols, span = NSA_HPG * Q_BLOCK, WINDOW + Q_BLOCK

    def per_batch(shape):
        nd = len(shape)
        return pl.BlockSpec((1,) + shape, lambda bi, i: (bi,) + (0,) * nd)

    token_rows = pl.BlockSpec((1, Q_BLOCK, NSA_WIDTH), lambda bi, i: (bi, i, 0))
    return pl.pallas_call(
        functools.partial(_nsa_body, top_k=top_k),
        grid=(b, t // Q_BLOCK),
        in_specs=[token_rows, per_batch((G, nc, NSA_HEAD_DIM)), per_batch((G, NSA_HEAD_DIM, nc)),
                  _const_spec(c2st.shape), per_batch((G, t, LANES)), per_batch((G, t, NSA_HEAD_DIM)),
                  per_batch((vt.shape[1], t)),
                  pl.BlockSpec((1, G * GATE_ROWS, Q_BLOCK), lambda bi, i: (bi, 0, i))],
        out_specs=token_rows,
        out_shape=jax.ShapeDtypeStruct((b, t, NSA_WIDTH), BF16),
        scratch_shapes=[pltpu.VMEM((G, SEL_PAD, Q_BLOCK), F32),
                        pltpu.VMEM((G, SEL_PAD, Q_BLOCK), jnp.int32),
                        pltpu.VMEM((G, nc, cols), F32),
                        pltpu.VMEM((G, span, cols), F32),
                        pltpu.VMEM((G, span, cols), BF16),
                        pltpu.VMEM((G, SEL_KEYS, cols), F32),
                        pltpu.VMEM((G, SEL_KEYS, cols), F32),
                        pltpu.VMEM((G, SEL_KEYS, cols), BF16),
                        pltpu.VMEM((G, SEL_KEYS, cols), BF16),
                        pltpu.VMEM((G, cols, 2 * NSA_HEAD_DIM), BF16),
                        pltpu.VMEM((G, 1, cols), F32),
                        pltpu.VMEM((G, V_AUG, cols), F32)],
        compiler_params=_cparams(("arbitrary", "arbitrary")),
        name="nsa",
    )(q, kcm, vct, c2st, ksa, kw, vt, gt)


def _ret_body(q_ref, k_ref, v_ref, g_ref, decay_ref, zeta_ref, xi_ref, gchunk_ref, gn_ref, o_ref, state_ref):
    C, d = RET_CHUNK, RET_HEAD_DIM

    @pl.when(pl.program_id(1) == 0)
    def _():
        state_ref[...] = jnp.zeros_like(state_ref)

    for c in range(q_ref.shape[1] // C):
        rows = slice(c * C, (c + 1) * C)
        for h in range(RET_HEADS):
            hc = slice(h * d, (h + 1) * d)
            q, k, v = q_ref[0, rows, hc], k_ref[0, rows, hc], v_ref[0, rows, hc]
            s = (_dot_nt(q, k) * decay_ref[h]).astype(BF16)
            state = state_ref[h]
            o = _dot(s, v) + _dot((q.astype(F32) * xi_ref[h]).astype(BF16), state.astype(BF16))
            kz = (k.astype(F32) * zeta_ref[h]).astype(BF16)
            state_ref[h] = state * gchunk_ref[h] + _dot_tn(kz, v)
            mu = jnp.mean(o, axis=-1, keepdims=True)
            var = jnp.mean(jnp.square(o - mu), axis=-1, keepdims=True)
            o = (o - mu) * lax.rsqrt(var + 1e-5) * gn_ref[h]
            gate = g_ref[0, rows, hc]
            o_ref[0, rows, hc] = (gate * jax.nn.sigmoid(gate) * o).astype(BF16)


def _retention(rq, rk, rv, rg, decay, zeta, xi, gchunk, gn):
    b, t, width = rq.shape
    tok = pl.BlockSpec((1, RET_ROWS, width), lambda bi, c: (bi, c, 0))
    return pl.pallas_call(
        _ret_body,
        grid=(b, t // RET_ROWS),
        in_specs=[tok, tok, tok, tok, _const_spec(decay.shape), _const_spec(zeta.shape), _const_spec(xi.shape),
                  _const_spec(gchunk.shape), _const_spec(gn.shape)],
        out_specs=tok,
        out_shape=jax.ShapeDtypeStruct((b, t, width), BF16),
        scratch_shapes=[pltpu.VMEM((RET_HEADS, RET_HEAD_DIM, RET_HEAD_DIM), F32)],
        compiler_params=_cparams(("arbitrary", "arbitrary")),
        name="retention",
    )(rq, rk, rv, rg, decay, zeta, xi, gchunk, gn)


def _oproj_body(x_ref, yc_ref, yd_ref, w_ref, gpost_ref, o_ref):
    m = _dot(yc_ref[...], w_ref[0:NSA_WIDTH, :]) + _dot(yd_ref[...], w_ref[NSA_WIDTH:, :])
    o_ref[...] = x_ref[...] + _rms(m, gpost_ref[...])


def _odd_out(x2, yc2, yd2, w, gpost):
    n, d = x2.shape
    tm = PROJ_ROWS

    def row(width):
        return pl.BlockSpec((tm, width), lambda i: (i, 0))

    return pl.pallas_call(
        _oproj_body,
        grid=(n // tm,),
        in_specs=[row(d), row(NSA_WIDTH), row(RET_WIDTH), _const_spec(w.shape), _const_spec((1, d))],
        out_specs=row(d),
        out_shape=jax.ShapeDtypeStruct((n, d), F32),
        compiler_params=_cparams(("arbitrary",)),
        name="odd_out",
    )(x2, yc2, yd2, w, gpost)


def _odd_in_weight(w_in):
    sizes = [NSA_WIDTH] + [NSA_KV_WIDTH] * 6 + [NSA_BRANCHES * NSA_HEADS] + [RET_WIDTH] * 4
    offs = np.concatenate([[0], np.cumsum(sizes)])
    q, kc, vc, ks, vs, kw, vw, gt, rq, rk, rv, rg = [w_in[:, offs[n]:offs[n + 1]] for n in range(len(sizes))]
    w = jnp.concatenate([q, kc, vc, ks, kw, rq, rk, rv, rg], axis=1).astype(BF16)
    per_group = NSA_BRANCHES * NSA_HPG
    gates = [jnp.pad(gt[:, g * per_group:(g + 1) * per_group], ((0, 0), (0, GATE_ROWS - per_group)))
             for g in range(NSA_KV_GROUPS)]
    wt = jnp.concatenate([vs, vw] + gates, axis=1).T.astype(BF16)
    return w, wt


def _rope_tables(positions):
    pos = positions.astype(F32)[..., None]
    inv = 1.0 / (ROPE_THETA ** (jnp.arange(0, ROPE_DIM, 2, dtype=F32) / ROPE_DIM))
    ang = pos * inv
    c, s = jnp.cos(ang), jnp.sin(ang)
    z = jnp.zeros(ang.shape[:-1] + (NSA_HEAD_DIM - ROPE_DIM,), F32)
    z8 = jnp.zeros_like(s)
    reps = LANES // NSA_HEAD_DIM
    nc = jnp.concatenate([c, c, z + 1.0] * reps, axis=-1)
    nsa = jnp.concatenate([-s, z8, z] * reps, axis=-1)
    nsb = jnp.concatenate([z8, s, z] * reps, axis=-1)
    inv_r = 1.0 / (RET_THETA ** (jnp.arange(0, RET_HEAD_DIM, 2, dtype=F32) / RET_HEAD_DIM))
    ang_r = pos * inv_r
    cr, sr = jnp.cos(ang_r), jnp.sin(ang_r)
    rc = jnp.concatenate([cr, cr], axis=-1)
    rs = jnp.concatenate([-sr, sr], axis=-1)
    return nc, nsa, nsb, rc, rs


def _retention_tables():
    H, C, d = RET_HEADS, RET_CHUNK, RET_HEAD_DIM
    log_gamma = jnp.log1p(-jnp.exp2(-5.0 - jnp.arange(H, dtype=F32)))
    idx = jnp.arange(C, dtype=F32)
    rel = idx[:, None] - idx[None, :]
    decay = jnp.where(rel >= 0, jnp.exp(jnp.maximum(rel, 0.0)[None] * log_gamma[:, None, None]), 0.0)
    zeta = jnp.exp((C - 1 - idx)[None, :] * log_gamma[:, None])
    xi = jnp.exp((idx + 1.0)[None, :] * log_gamma[:, None])
    gchunk = jnp.exp(C * log_gamma)
    zeta_b = jnp.broadcast_to(zeta[:, :, None], (H, C, d))
    xi_b = jnp.broadcast_to(xi[:, :, None], (H, C, d))
    gchunk_b = jnp.broadcast_to(gchunk[:, None, None], (H, 1, d))
    return decay, zeta_b, xi_b, gchunk_b


def _cmp_to_sel_t(t_len):
    nc = t_len // CMP_STRIDE
    n_cmp = (t_len - CMP_BLOCK) // CMP_STRIDE + 1
    n_sel = t_len // SEL_BLOCK
    c_start = np.arange(nc) * CMP_STRIDE
    s_start = np.arange(SEL_PAD) * SEL_BLOCK
    hit = ((c_start[None, :] < s_start[:, None] + SEL_BLOCK) & (c_start[None, :] + CMP_BLOCK > s_start[:, None])
           & (np.arange(nc)[None, :] < n_cmp) & (np.arange(SEL_PAD)[:, None] < n_sel))
    return jnp.asarray(hit.astype(np.float32), dtype=BF16)


def _compress_weights(pos, w1, w2):
    G, dh, half = NSA_KV_GROUPS, NSA_HEAD_DIM, CMP_BLOCK // 2
    pos_rows = jnp.broadcast_to(pos.reshape(2, half, 1, dh), (2, half, G, dh)).reshape(2, half * G * dh)
    w1h = w1.reshape(2, half, dh, CMP_HIDDEN)
    per_group = []
    for g in range(G):
        wg = jnp.zeros((2, half, G, dh, CMP_HIDDEN), w1.dtype).at[:, :, g].set(w1h)
        per_group.append(wg.reshape(2, half * G * dh, CMP_HIDDEN))
    return pos_rows, jnp.stack(per_group, axis=1).astype(BF16), w2.astype(BF16)


def _odd_layer(x, positions_tables, ret_tables, c2st, gpre, w_in, cmp_k_pos, cmp_k_w1, cmp_k_w2,
               cmp_v_pos, cmp_v_w1, cmp_v_w2, gn_g, w_out, gpost):
    b, t, d = x.shape
    w, wt = _odd_in_weight(w_in)
    (q, kc, vc, ksa, kw, vt, gt, rq, rk, rv, rg) = _odd_proj(x, gpre, w, wt, *positions_tables)
    kpos, kw1, kw2 = _compress_weights(cmp_k_pos, cmp_k_w1, cmp_k_w2)
    vpos, vw1, vw2 = _compress_weights(cmp_v_pos, cmp_v_w1, cmp_v_w2)
    rows16 = (b, t // CMP_STRIDE, CMP_STRIDE * LANES)
    kcm, vct = _compress(kc.reshape(rows16), vc.reshape(rows16), kpos, vpos, kw1, vw1, kw2, vw2.T)
    yc = _nsa(q, kcm, vct, c2st, ksa, kw, vt, gt)
    yd = _retention(rq, rk, rv, rg, *ret_tables, gn_g.reshape(RET_HEADS, 1, RET_HEAD_DIM))
    out = _odd_out(x.reshape(b * t, d), yc.reshape(b * t, NSA_WIDTH), yd.reshape(b * t, RET_WIDTH),
                   w_out.astype(BF16), gpost)
    return out.reshape(b, t, d)


def kernel(x, positions, ln_mix_pre, ln_mix_post, ln_ffn_pre, ln_ffn_post, ffn_w_gate, ffn_w_up, ffn_w_down,
           ev_w_in, ev_pool_w, ev_pool_scale, ev_sgu_ln_g, ev_sgu_ln_b, ev_sgu_w, ev_sgu_b, ev_w_out,
           od_w_in, od_cmp_k_pos, od_cmp_k_w1, od_cmp_k_w2, od_cmp_v_pos, od_cmp_v_w1, od_cmp_v_w2,
           od_ret_gn_g, od_w_out):
    b, t, d = x.shape
    depth = ln_mix_pre.shape[0]
    rope = _rope_tables(positions)
    ret_tables = _retention_tables()
    c2st = _cmp_to_sel_t(t)
    for layer in range(depth):
        gpre = ln_mix_pre[layer].reshape(1, d)
        gpost = ln_mix_post[layer].reshape(1, d)
        if layer % 2 == 0:
            e = layer // 2
            x = _even_layer(x, gpre, ev_w_in[e].astype(BF16), ev_pool_w[e].astype(BF16),
                            ev_pool_scale[e].reshape(1, POOL_WIDTH), ev_sgu_ln_g[e].reshape(1, SGU_WIDTH),
                            ev_sgu_ln_b[e].reshape(1, SGU_WIDTH), ev_sgu_w[e], ev_sgu_b[e].T,
                            ev_w_out[e].astype(BF16), gpost)
        else:
            o = layer // 2
            x = _odd_layer(x, rope, ret_tables, c2st, gpre, od_w_in[o], od_cmp_k_pos[o], od_cmp_k_w1[o],
                           od_cmp_k_w2[o], od_cmp_v_pos[o], od_cmp_v_w1[o], od_cmp_v_w2[o], od_ret_gn_g[o],
                           od_w_out[o], gpost)
        x = _ffn(x.reshape(b * t, d), ln_ffn_pre[layer].reshape(1, d), ln_ffn_post[layer].reshape(1, d),
                 ffn_w_gate[layer].astype(BF16), ffn_w_up[layer].astype(BF16),
                 ffn_w_down[layer].astype(BF16)).reshape(b, t, d)
    return x
```

```python
import functools

import numpy as np
import jax
import jax.numpy as jnp
from jax import lax
from jax.experimental import pallas as pl
from jax.experimental.pallas import tpu as pltpu

F32 = jnp.float32
BF16 = jnp.bfloat16

D_MODEL = 1024
POOL_WINDOWS = (2, 4, 8, 16)
POOL_GROUPS = 4
POOL_WIDTH = D_MODEL // 2
POOL_GDIM = POOL_WIDTH // POOL_GROUPS
POOL_HIST = 16
SGU_GROUPS = 4
SGU_WIDTH = D_MODEL // 2
SGU_GDIM = SGU_WIDTH // SGU_GROUPS
SGU_CHUNK = 128
EVEN_IN = POOL_WIDTH + 2 * SGU_WIDTH
NSA_HEADS = 8
NSA_KV_GROUPS = 2
NSA_HPG = NSA_HEADS // NSA_KV_GROUPS
NSA_HEAD_DIM = 64
NSA_WIDTH = NSA_HEADS * NSA_HEAD_DIM
NSA_GROUP_WIDTH = NSA_HPG * NSA_HEAD_DIM
NSA_KV_WIDTH = NSA_KV_GROUPS * NSA_HEAD_DIM
NSA_BRANCHES = 3
GATE_ROWS = 16
CMP_BLOCK = 32
CMP_STRIDE = 16
CMP_HIDDEN = 128
SEL_BLOCK = 64
SEL_TOPK = 16
SEL_PAD = 64
WINDOW = 512
Q_BLOCK = 256
FORCE_SCORE = 1.0e4
ROPE_THETA = 500000.0
ROPE_DIM = NSA_HEAD_DIM // 4
RET_HEADS = 4
RET_HEAD_DIM = 128
RET_WIDTH = RET_HEADS * RET_HEAD_DIM
RET_CHUNK = 128
RET_THETA = 10000.0
ODD_MIX = NSA_WIDTH + RET_WIDTH
NEG_INF = -1.0e30
LOG2_E = 1.4426950408889634
LANES = 128
SUBLANES = 8

VMEM_LIMIT = 56 * 1024 * 1024
FFN_ROWS = 512
FFN_CHUNKS = 2
EVEN_ROWS = 512
PROJ_ROWS = 512
SEL_KEYS = Q_BLOCK
RANK_CHUNK = 8
RET_ROWS = 512


def _cparams(sem):
    return pltpu.CompilerParams(dimension_semantics=sem, vmem_limit_bytes=VMEM_LIMIT)


def _const_spec(shape):
    nd = len(shape)
    return pl.BlockSpec(shape, lambda *_: (0,) * nd, pipeline_mode=pl.Buffered(1))


def _dot(a, b):
    return jnp.dot(a, b, preferred_element_type=F32)


def _dot_nt(a, b):
    return lax.dot_general(a, b, (((1,), (1,)), ((), ())), preferred_element_type=F32)


def _dot_tn(a, b):
    return lax.dot_general(a, b, (((0,), (0,)), ((), ())), preferred_element_type=F32)


def _rms(x, g, eps=1e-6):
    return x * lax.rsqrt(jnp.mean(x * x, axis=-1, keepdims=True) + eps) * g


def _softmax2_cols(s):
    e = jnp.exp2(s - jnp.max(s, axis=0, keepdims=True))
    return e / jnp.sum(e, axis=0, keepdims=True)


def _eye(n):
    return jnp.where(lax.broadcasted_iota(jnp.int32, (n, n), 0) == lax.broadcasted_iota(jnp.int32, (n, n), 1),
                     1.0, 0.0).astype(BF16)


def _ffn_body(x_ref, gpre_ref, gpost_ref, wg_ref, wu_ref, wd_ref, o_ref):
    x = x_ref[...]
    h = _rms(x, gpre_ref[...]).astype(BF16)
    f_total = wg_ref.shape[1]
    fc = f_total // FFN_CHUNKS
    acc = None
    for c in range(FFN_CHUNKS):
        gate = _dot(h, wg_ref[:, c * fc:(c + 1) * fc])
        up = _dot(h, wu_ref[:, c * fc:(c + 1) * fc])
        act = (gate * jax.nn.sigmoid(gate) * up).astype(BF16)
        part = _dot(act, wd_ref[c * fc:(c + 1) * fc, :])
        acc = part if acc is None else acc + part
    o_ref[...] = x + _rms(acc, gpost_ref[...])


def _ffn(x2, gpre, gpost, wg, wu, wd):
    n, d = x2.shape
    f = wg.shape[1]
    row = pl.BlockSpec((FFN_ROWS, d), lambda i: (i, 0))
    return pl.pallas_call(
        _ffn_body,
        grid=(n // FFN_ROWS,),
        in_specs=[row, _const_spec((1, d)), _const_spec((1, d)),
                  _const_spec((d, f)), _const_spec((d, f)), _const_spec((f, d))],
        out_specs=row,
        out_shape=jax.ShapeDtypeStruct((n, d), F32),
        compiler_params=_cparams(("arbitrary",)),
        name="ffn",
    )(x2, gpre, gpost, wg, wu, wd)


def _even_body(x_ref, gpre_ref, win_ref, poolw_ref, pscale_ref, lng_ref, lnb_ref, sguw_ref, sgub_ref,
               wout_ref, gpost_ref, o_ref, hist_ref):
    tt = x_ref.shape[1]
    j = pl.program_id(1)

    @pl.when(j == 0)
    def _():
        hist_ref[0:POOL_HIST, :] = jnp.zeros((POOL_HIST, POOL_WIDTH), F32)

    x = x_ref[0]
    h = _rms(x, gpre_ref[...]).astype(BF16)
    z = _dot(h, win_ref[...])
    a = z[:, :POOL_WIDTH]
    u = z[:, POOL_WIDTH:POOL_WIDTH + SGU_WIDTH]
    v = z[:, POOL_WIDTH + SGU_WIDTH:]

    hist_ref[POOL_HIST:POOL_HIST + tt, :] = a
    t_pos = j * tt + lax.broadcasted_iota(jnp.int32, (tt, 1), 0)
    ya = []
    for gi, w in enumerate(POOL_WINDOWS):
        cols = slice(gi * POOL_GDIM, (gi + 1) * POOL_GDIM)
        win_sum = a[:, cols]
        for s in range(1, w):
            win_sum = win_sum + hist_ref[POOL_HIST - s:POOL_HIST - s + tt, cols]
        cnt = jnp.minimum(t_pos + 1, w).astype(F32)
        diff = (win_sum / cnt - a[:, cols]).astype(BF16)
        ya.append(_dot(diff, poolw_ref[gi]))
    ya = jnp.concatenate(ya, axis=1) * pscale_ref[...]
    hist_ref[0:POOL_HIST, :] = hist_ref[tt:tt + POOL_HIST, :]

    ug = jax.nn.gelu(u)
    vg = jax.nn.gelu(v)
    mu = jnp.mean(vg, axis=-1, keepdims=True)
    var = jnp.mean(jnp.square(vg - mu), axis=-1, keepdims=True)
    vn = ((vg - mu) * lax.rsqrt(var + 1e-5) * lng_ref[...] + lnb_ref[...]).astype(BF16)
    r_i = lax.broadcasted_iota(jnp.int32, (SGU_CHUNK, SGU_CHUNK), 0)
    c_i = lax.broadcasted_iota(jnp.int32, (SGU_CHUNK, SGU_CHUNK), 1)
    yb = []
    for g in range(SGU_GROUPS):
        cols = slice(g * SGU_GDIM, (g + 1) * SGU_GDIM)
        wm = jnp.where(r_i >= c_i, sguw_ref[g], 0.0).astype(BF16)
        bcol = sgub_ref[:, g:g + 1]
        parts = []
        for c in range(tt // SGU_CHUNK):
            rows = slice(c * SGU_CHUNK, (c + 1) * SGU_CHUNK)
            parts.append(ug[rows, cols] * (_dot(wm, vn[rows, cols]) + bcol))
        yb.append(jnp.concatenate(parts, axis=0))
    yb = jnp.concatenate(yb, axis=1)

    m = _dot(ya.astype(BF16), wout_ref[0:POOL_WIDTH, :]) + _dot(yb.astype(BF16), wout_ref[POOL_WIDTH:, :])
    o_ref[0] = x + _rms(m, gpost_ref[...])


def _even_layer(x, gpre, win, poolw, pscale, lng, lnb, sguw, sgub_t, wout, gpost):
    b, t, d = x.shape
    tt = EVEN_ROWS
    row = pl.BlockSpec((1, tt, d), lambda bi, j: (bi, j, 0))
    return pl.pallas_call(
        _even_body,
        grid=(b, t // tt),
        in_specs=[row, _const_spec((1, d)), _const_spec(win.shape), _const_spec(poolw.shape),
                  _const_spec(pscale.shape), _const_spec(lng.shape), _const_spec(lnb.shape),
                  _const_spec(sguw.shape), _const_spec(sgub_t.shape), _const_spec(wout.shape),
                  _const_spec((1, d))],
        out_specs=row,
        out_shape=jax.ShapeDtypeStruct((b, t, d), F32),
        scratch_shapes=[pltpu.VMEM((POOL_HIST + tt, POOL_WIDTH), F32)],
        compiler_params=_cparams(("arbitrary", "arbitrary")),
        name="even_mixer",
    )(x, gpre, win, poolw, pscale, lng, lnb, sguw, sgub_t, wout, gpost)


_C_Q = 0
_C_KC = _C_Q + NSA_WIDTH
_C_VC = _C_KC + NSA_KV_WIDTH
_C_KS = _C_VC + NSA_KV_WIDTH
_C_KW = _C_KS + NSA_KV_WIDTH
_C_RQ = _C_KW + NSA_KV_WIDTH
_C_RK = _C_RQ + RET_WIDTH
_C_RV = _C_RK + RET_WIDTH
_C_RG = _C_RV + RET_WIDTH
ODD_COLS = _C_RG + RET_WIDTH
_R_VS = 0
_R_VW = _R_VS + NSA_KV_WIDTH
_R_GT = _R_VW + NSA_KV_WIDTH
ODD_TROWS = _R_GT + NSA_KV_GROUPS * GATE_ROWS
V_AUG = NSA_HEAD_DIM + 16
VT_ROWS = 2 * NSA_KV_GROUPS * V_AUG


def _rope_nsa(z, c, sa, sb):
    half = ROPE_DIM // 2
    return z * c + pltpu.roll(z, LANES - half, 1) * sa + pltpu.roll(z, half, 1) * sb


def _split3(x):
    a = x.astype(BF16)
    r = x - a.astype(F32)
    b = r.astype(BF16)
    return a, b, (r - b.astype(F32)).astype(BF16)


def _proj_body(x_ref, gpre_ref, w_ref, wt_ref, cs_ref, place_ref, base_ref, cr_ref, sr_ref,
               q_ref, kc_ref, vc_ref, ksa_ref, kw_ref, vt_ref, gt_ref, rq_ref, rk_ref, rv_ref, rg_ref):
    tm = x_ref.shape[1]
    j = pl.program_id(1)
    x = x_ref[0]
    h = _rms(x, gpre_ref[...]).astype(BF16)
    tables = base_ref[...]
    for part in _split3(cs_ref[0]):
        tables = tables + _dot(part, place_ref[...])
    nc, nsa, nsb = tables[:, 0:LANES], tables[:, LANES:2 * LANES], tables[:, 2 * LANES:3 * LANES]
    cr, sr = cr_ref[0], sr_ref[0]
    rc = jnp.concatenate([cr, cr], axis=1)
    rs = jnp.concatenate([-sr, sr], axis=1)

    def cols(start, width):
        return _dot(h, w_ref[:, start:start + width])

    zq = cols(_C_Q, NSA_WIDTH)
    q_scale = NSA_HEAD_DIM ** -0.5 * LOG2_E
    for s in range(NSA_WIDTH // LANES):
        sl = slice(s * LANES, (s + 1) * LANES)
        q_ref[0, :, sl] = (_rope_nsa(zq[:, sl], nc, nsa, nsb) * q_scale).astype(BF16)

    kc_ref[0] = _rope_nsa(cols(_C_KC, LANES), nc, nsa, nsb)
    vc_ref[0] = cols(_C_VC, LANES)

    ks = _rope_nsa(cols(_C_KS, LANES), nc, nsa, nsb)
    kw = _rope_nsa(cols(_C_KW, LANES), nc, nsa, nsb)
    t_pos = j * tm + lax.broadcasted_iota(jnp.int32, (tm, SEL_PAD), 0)
    blk = lax.broadcasted_iota(jnp.int32, (tm, SEL_PAD), 1)
    onehot = jnp.where(t_pos // SEL_BLOCK == blk, 1.0, 0.0).astype(BF16)
    for g in range(NSA_KV_GROUPS):
        sl = slice(g * NSA_HEAD_DIM, (g + 1) * NSA_HEAD_DIM)
        ksa_ref[0, g] = jnp.concatenate([ks[:, sl].astype(BF16), onehot], axis=1)
        kw_ref[0, g] = kw[:, sl].astype(BF16)

    zt = _dot_nt(wt_ref[...], h)
    ones_rows = jnp.where(lax.broadcasted_iota(jnp.int32, (V_AUG - NSA_HEAD_DIM, tm), 0) == 0, 1.0, 0.0).astype(BF16)
    for k in range(2 * NSA_KV_GROUPS):
        vt_ref[0, k * V_AUG:k * V_AUG + NSA_HEAD_DIM] = zt[k * NSA_HEAD_DIM:(k + 1) * NSA_HEAD_DIM].astype(BF16)
        vt_ref[0, k * V_AUG + NSA_HEAD_DIM:(k + 1) * V_AUG] = ones_rows
    gt_ref[0] = jax.nn.sigmoid(zt[_R_GT:])

    k_scale = RET_HEAD_DIM ** -0.5
    zrq = cols(_C_RQ, RET_WIDTH)
    zrk = cols(_C_RK, RET_WIDTH)
    for hh in range(RET_HEADS):
        sl = slice(hh * RET_HEAD_DIM, (hh + 1) * RET_HEAD_DIM)
        zq_h, zk_h = zrq[:, sl], zrk[:, sl]
        rq_ref[0, :, sl] = (zq_h * rc + pltpu.roll(zq_h, RET_HEAD_DIM // 2, 1) * rs).astype(BF16)
        rk_ref[0, :, sl] = ((zk_h * rc + pltpu.roll(zk_h, RET_HEAD_DIM // 2, 1) * rs) * k_scale).astype(BF16)
    rv_ref[0] = cols(_C_RV, RET_WIDTH).astype(BF16)
    rg_ref[0] = cols(_C_RG, RET_WIDTH)


def _odd_proj(x, gpre, w, wt, cs, place, base, cr, sr):
    b, t, d = x.shape
    tm = PROJ_ROWS
    G = NSA_KV_GROUPS

    def row(width):
        return pl.BlockSpec((1, tm, width), lambda bi, j: (bi, j, 0))

    def grp(width):
        return pl.BlockSpec((1, G, tm, width), lambda bi, j: (bi, 0, j, 0))

    def feat(rows):
        return pl.BlockSpec((1, rows, tm), lambda bi, j: (bi, 0, j))

    out_shape = [
        jax.ShapeDtypeStruct((b, t, NSA_WIDTH), BF16),
        jax.ShapeDtypeStruct((b, t, LANES), F32),
        jax.ShapeDtypeStruct((b, t, LANES), F32),
        jax.ShapeDtypeStruct((b, G, t, LANES), BF16),
        jax.ShapeDtypeStruct((b, G, t, NSA_HEAD_DIM), BF16),
        jax.ShapeDtypeStruct((b, VT_ROWS, t), BF16),
        jax.ShapeDtypeStruct((b, G * GATE_ROWS, t), F32),
        jax.ShapeDtypeStruct((b, t, RET_WIDTH), BF16),
        jax.ShapeDtypeStruct((b, t, RET_WIDTH), BF16),
        jax.ShapeDtypeStruct((b, t, RET_WIDTH), BF16),
        jax.ShapeDtypeStruct((b, t, RET_WIDTH), F32),
    ]
    out_specs = [row(NSA_WIDTH), row(LANES), row(LANES), grp(LANES), grp(NSA_HEAD_DIM), feat(VT_ROWS),
                 feat(G * GATE_ROWS), row(RET_WIDTH), row(RET_WIDTH), row(RET_WIDTH), row(RET_WIDTH)]
    return pl.pallas_call(
        _proj_body,
        grid=(b, t // tm),
        in_specs=[row(d), _const_spec((1, d)), _const_spec(w.shape), _const_spec(wt.shape), row(cs.shape[2]),
                  _const_spec(place.shape), _const_spec(base.shape), row(cr.shape[2]), row(sr.shape[2])],
        out_specs=out_specs,
        out_shape=out_shape,
        compiler_params=_cparams(("arbitrary", "arbitrary")),
        name="odd_proj",
    )(x, gpre, w, wt, cs, place, base, cr, sr)


def _compress_body(k_ref, v_ref, kpos_ref, vpos_ref, kw1_ref, vw1_ref, kw2_ref, vw2t_ref, kcm_ref, vct_ref):
    nc = k_ref.shape[1] // CMP_STRIDE
    half = CMP_BLOCK // CMP_STRIDE

    def hidden(x_ref, pos_ref, w1_ref):
        parts = [None] * half
        for off in range(CMP_STRIDE):
            tok = x_ref[0, pl.ds(off, nc, stride=CMP_STRIDE), :]
            for h in range(half):
                l = h * CMP_STRIDE + off
                term = _dot((tok + pos_ref[l:l + 1, :]).astype(BF16), w1_ref[l])
                parts[h] = term if parts[h] is None else parts[h] + term
        pre = parts[0]
        for h in range(1, half):
            pre = pre + pltpu.roll(parts[h], nc - h, 0)
        return jax.nn.gelu(pre).astype(BF16)

    hk = hidden(k_ref, kpos_ref, kw1_ref)
    hv = hidden(v_ref, vpos_ref, vw1_ref)
    for g in range(NSA_KV_GROUPS):
        sl = slice(g * CMP_HIDDEN, (g + 1) * CMP_HIDDEN)
        kcm_ref[0, g] = _dot(hk[:, sl], kw2_ref[...]).astype(BF16)
        vct_ref[0, g] = _dot_nt(vw2t_ref[...], hv[:, sl]).astype(BF16)


def _compress(k, v, kpos, vpos, kw1, vw1, kw2, vw2t):
    b, t, width = k.shape
    nc = t // CMP_STRIDE
    G = NSA_KV_GROUPS
    row = pl.BlockSpec((1, t, width), lambda bi: (bi, 0, 0))
    return pl.pallas_call(
        _compress_body,
        grid=(b,),
        in_specs=[row, row, _const_spec(kpos.shape), _const_spec(vpos.shape), _const_spec(kw1.shape),
                  _const_spec(vw1.shape), _const_spec(kw2.shape), _const_spec(vw2t.shape)],
        out_specs=[pl.BlockSpec((1, G, nc, NSA_HEAD_DIM), lambda bi: (bi, 0, 0, 0)),
                   pl.BlockSpec((1, G, NSA_HEAD_DIM, nc), lambda bi: (bi, 0, 0, 0))],
        out_shape=[jax.ShapeDtypeStruct((b, G, nc, NSA_HEAD_DIM), BF16),
                   jax.ShapeDtypeStruct((b, G, NSA_HEAD_DIM, nc), BF16)],
        compiler_params=_cparams(("arbitrary",)),
        name="compress",
    )(k, v, kpos, vpos, kw1, vw1, kw2, vw2t)


def _nsa_body(q_ref, kcm_ref, vct_ref, c2st_ref, tri_ref, ksa_ref, kw_ref, vt_ref, gt_ref, o_ref,
              score_ref, rank_ref, wbias_ref, sc_ref, sw_ref, pw_ref, sa_ref, sb_ref, pa_ref, pb_ref, lhs_ref, m_ref, acc_ref, *, top_k):
    G, M, Q, dh = NSA_KV_GROUPS, NSA_HPG, Q_BLOCK, NSA_HEAD_DIM
    cols = M * Q
    nc = kcm_ref.shape[2]
    i = pl.program_id(1)
    t0 = i * Q
    tq = t0 + (lax.broadcasted_iota(jnp.int32, (1, cols), 1) & (Q - 1))
    groups = range(G)

    q_heads, q_rows = [], []
    for g in groups:
        qb = q_ref[0, :, g * NSA_GROUP_WIDTH:(g + 1) * NSA_GROUP_WIDTH]
        q_heads.append([qb[:, m * dh:(m + 1) * dh] for m in range(M)])
        q_rows.append(jnp.concatenate(q_heads[g], axis=0))

    n_idx = lax.broadcasted_iota(jnp.int32, (nc, 1), 0)
    cmp_ok = (n_idx * CMP_STRIDE + (CMP_BLOCK - 1) <= tq) & (n_idx < nc - 1)
    j_idx = lax.broadcasted_iota(jnp.int32, (SEL_PAD, Q), 0)
    cur = (t0 + lax.broadcasted_iota(jnp.int32, (SEL_PAD, Q), 1)) // SEL_BLOCK
    forced = (j_idx == 0) | (j_idx == cur) | (j_idx == cur - 1)
    valid_tok = (tq >= CMP_BLOCK - 1).astype(F32)
    o_cmp = []
    for g in groups:
        sc = sc_ref.at[g]
        sc[...] = jnp.where(cmp_ok, _dot_nt(kcm_ref[0, g], q_rows[g]), NEG_INF)
        sc[...] = jnp.exp2(sc[...] - jnp.max(sc[...], axis=0, keepdims=True))
        norm = valid_tok / jnp.sum(sc[...], axis=0, keepdims=True)
        o_cmp.append(_dot(vct_ref[0, g], sc[...].astype(BF16)) * norm)
        p_grp = sc[:, 0:Q] * norm[:, 0:Q]
        for m in range(1, M):
            p_grp = p_grp + sc[:, m * Q:(m + 1) * Q] * norm[:, m * Q:(m + 1) * Q]
        p_hi = p_grp.astype(BF16)
        p_lo = (p_grp - p_hi.astype(F32)).astype(BF16)
        imp = _dot(c2st_ref[...], p_hi) + _dot(c2st_ref[...], p_lo)
        score_ref[g] = jnp.where(j_idx <= cur, jnp.where(forced, FORCE_SCORE, imp), -1.0)

    def v_aug(branch, g, k0, n):
        r0 = (branch * G + g) * V_AUG
        return vt_ref[0, r0:r0 + V_AUG, pl.ds(k0, n)]

    span = WINDOW + Q
    w0 = pl.multiple_of(jnp.maximum(t0 - WINDOW, 0), Q)
    n_wb = span // Q
    diag_b = jnp.minimum(i, n_wb - 1)
    tri_causal, tri_leaving = tri_ref[0], tri_ref[1]
    for wb in range(n_wb):
        blk_bias = jnp.where(wb == diag_b, tri_causal, jnp.where(wb > diag_b, NEG_INF, 0.0))
        if wb == 0:
            blk_bias = jnp.where(i >= n_wb - 1, tri_leaving, blk_bias)
        wbias_ref[wb * Q:(wb + 1) * Q, :] = blk_bias
    o_win = []
    for g in groups:
        sw = sw_ref.at[g]
        sw[...] = _dot_nt(kw_ref[0, g, pl.ds(w0, span), :], q_rows[g]) + wbias_ref[...]
        pw_ref[g] = jnp.exp2(sw[...] - jnp.max(sw[...], axis=0, keepdims=True)).astype(BF16)
        acc_win = _dot(v_aug(1, g, w0, span), pw_ref[g])
        o_win.append(acc_win[0:dh] / acc_win[dh:dh + 1])

    n_causal = (t0 + Q) // SEL_BLOCK
    rank_ref[...] = jnp.zeros(rank_ref.shape, jnp.int32)
    row_in_tile = lax.broadcasted_iota(jnp.int32, (SUBLANES, LANES), 0)
    for c0 in range(0, SEL_PAD, RANK_CHUNK):
        @pl.when(c0 < n_causal)
        def _(c0=c0):
            for g in groups:
                for l0 in range(0, Q, LANES):
                    ln = slice(l0, l0 + LANES)
                    others = [score_ref[g, k:k + 1, ln] for k in range(c0, c0 + RANK_CHUNK)]
                    for r0 in range(0, SEL_PAD, SUBLANES):
                        mine = score_ref[g, r0:r0 + SUBLANES, ln]
                        count = rank_ref[g, r0:r0 + SUBLANES, ln]
                        for k, other in zip(range(c0, c0 + RANK_CHUNK), others):
                            if r0 > k:
                                beats = other >= mine
                            elif r0 + SUBLANES <= k:
                                beats = other > mine
                            else:
                                beats = (other > mine) | ((other == mine) & (k < r0 + row_in_tile))
                            count = count + beats.astype(jnp.int32)
                        rank_ref[g, r0:r0 + SUBLANES, ln] = count

    eye_q = _eye(Q)
    for g in groups:
        chosen = jnp.where((rank_ref[g] < top_k) & (score_ref[g] >= 0.0), 1.0, 0.0).astype(BF16)
        chosen_t = _dot_nt(eye_q, chosen)
        bias = jnp.where(chosen_t > 0.5, 0.0, NEG_INF).astype(BF16)
        for m in range(M):
            lhs_ref[g, m * Q:(m + 1) * Q, :] = jnp.concatenate([q_heads[g][m], bias], axis=1)
        m_ref[g] = jnp.full((1, cols), NEG_INF, F32)
        acc_ref[g] = jnp.zeros((V_AUG, cols), F32)

    def put_scores(dst, g, kt):
        k0 = pl.multiple_of(kt * SEL_KEYS, SEL_KEYS)
        dst[g] = _dot_nt(ksa_ref[0, g, pl.ds(k0, SEL_KEYS), :], lhs_ref[g])

    def sel_update(src, p_dst, g, kt, bias=None):
        k0 = pl.multiple_of(kt * SEL_KEYS, SEL_KEYS)

        def scores():
            return src[g] if bias is None else src[g] + bias

        m_i = m_ref[g]
        m_new = jnp.maximum(m_i, jnp.max(scores(), axis=0, keepdims=True))
        p_dst[g] = jnp.exp2(scores() - m_new).astype(BF16)
        acc_ref[g] = jnp.exp2(m_i - m_new) * acc_ref[g] + _dot(v_aug(0, g, k0, SEL_KEYS), p_dst[g])
        m_ref[g] = m_new

    n_full = t0 // SEL_KEYS
    odd = n_full & 1
    for g in groups:
        put_scores(sa_ref, g, 0)

    @pl.when(odd == 1)
    def _():
        for g in groups:
            sel_update(sa_ref, pa_ref, g, 0)
            put_scores(sa_ref, g, 1)

    def sel_pair(pair, _):
        ta = odd + 2 * pair
        for g in groups:
            put_scores(sb_ref, g, ta + 1)
        for g in groups:
            sel_update(sa_ref, pa_ref, g, ta)
        for g in groups:
            put_scores(sa_ref, g, ta + 2)
        for g in groups:
            sel_update(sb_ref, pb_ref, g, ta + 1)
        return 0

    lax.fori_loop(0, n_full // 2, sel_pair, 0)
    o_sel = []
    for g in groups:
        sel_update(sa_ref, pa_ref, g, n_full, bias=tri_causal)
        o_sel.append(acc_ref[g, 0:dh, :] / acc_ref[g, dh:dh + 1, :])

    pieces = []
    for g in groups:
        gt = gt_ref[0, g * GATE_ROWS:(g + 1) * GATE_ROWS, :]
        for m in range(M):
            c = slice(m * Q, (m + 1) * Q)
            r = NSA_BRANCHES * m
            mixed = (o_cmp[g][:, c] * gt[r:r + 1] + o_sel[g][:, c] * gt[r + 1:r + 2]
                     + o_win[g][:, c] * gt[r + 2:r + 3])
            pieces.append(_dot_nt(eye_q, mixed.astype(BF16)))
    o_ref[0] = jnp.concatenate(pieces, axis=1).astype(BF16)


def _nsa(q, kcm, vct, c2st, tri, ksa, kw, vt, gt):
    b, t, _ = q.shape
    G = NSA_KV_GROUPS
    nc = kcm.shape[2]
    n_sel = t // SEL_BLOCK
    assert n_sel <= SEL_PAD and t >= WINDOW + Q_BLOCK and t % SEL_KEYS == 0
    top_k = min(SEL_TOPK, n_sel)
    cols, span = NSA_HPG * Q_BLOCK, WINDOW + Q_BLOCK

    def per_batch(shape):
        nd = len(shape)
        return pl.BlockSpec((1,) + shape, lambda bi, i: (bi,) + (0,) * nd)

    token_rows = pl.BlockSpec((1, Q_BLOCK, NSA_WIDTH), lambda bi, i: (bi, i, 0))
    return pl.pallas_call(
        functools.partial(_nsa_body, top_k=top_k),
        grid=(b, t // Q_BLOCK),
        in_specs=[token_rows, per_batch((G, nc, NSA_HEAD_DIM)), per_batch((G, NSA_HEAD_DIM, nc)),
                  _const_spec(c2st.shape), _const_spec(tri.shape), per_batch((G, t, LANES)), per_batch((G, t, NSA_HEAD_DIM)),
                  per_batch((vt.shape[1], t)),
                  pl.BlockSpec((1, G * GATE_ROWS, Q_BLOCK), lambda bi, i: (bi, 0, i))],
        out_specs=token_rows,
        out_shape=jax.ShapeDtypeStruct((b, t, NSA_WIDTH), BF16),
        scratch_shapes=[pltpu.VMEM((G, SEL_PAD, Q_BLOCK), F32),
                        pltpu.VMEM((G, SEL_PAD, Q_BLOCK), jnp.int32),
                        pltpu.VMEM((span, cols), F32),
                        pltpu.VMEM((G, nc, cols), F32),
                        pltpu.VMEM((G, span, cols), F32),
                        pltpu.VMEM((G, span, cols), BF16),
                        pltpu.VMEM((G, SEL_KEYS, cols), F32),
                        pltpu.VMEM((G, SEL_KEYS, cols), F32),
                        pltpu.VMEM((G, SEL_KEYS, cols), BF16),
                        pltpu.VMEM((G, SEL_KEYS, cols), BF16),
                        pltpu.VMEM((G, cols, 2 * NSA_HEAD_DIM), BF16),
                        pltpu.VMEM((G, 1, cols), F32),
                        pltpu.VMEM((G, V_AUG, cols), F32)],
        compiler_params=_cparams(("arbitrary", "arbitrary")),
        name="nsa",
    )(q, kcm, vct, c2st, tri, ksa, kw, vt, gt)


def _ret_body(q_ref, k_ref, v_ref, g_ref, decay_ref, zeta_ref, xi_ref, gchunk_ref, gn_ref, o_ref, state_ref):
    C, d = RET_CHUNK, RET_HEAD_DIM

    @pl.when(pl.program_id(1) == 0)
    def _():
        state_ref[...] = jnp.zeros_like(state_ref)

    for c in range(q_ref.shape[1] // C):
        rows = slice(c * C, (c + 1) * C)
        for h in range(RET_HEADS):
            hc = slice(h * d, (h + 1) * d)
            q, k, v = q_ref[0, rows, hc], k_ref[0, rows, hc], v_ref[0, rows, hc]
            s = (_dot_nt(q, k) * decay_ref[h]).astype(BF16)
            state = state_ref[h]
            o = _dot(s, v) + _dot((q.astype(F32) * xi_ref[h]).astype(BF16), state.astype(BF16))
            kz = (k.astype(F32) * zeta_ref[h]).astype(BF16)
            state_ref[h] = state * gchunk_ref[h] + _dot_tn(kz, v)
            mu = jnp.mean(o, axis=-1, keepdims=True)
            var = jnp.mean(jnp.square(o - mu), axis=-1, keepdims=True)
            o = (o - mu) * lax.rsqrt(var + 1e-5) * gn_ref[h]
            gate = g_ref[0, rows, hc]
            o_ref[0, rows, hc] = (gate * jax.nn.sigmoid(gate) * o).astype(BF16)


def _retention(rq, rk, rv, rg, decay, zeta, xi, gchunk, gn):
    b, t, width = rq.shape
    tok = pl.BlockSpec((1, RET_ROWS, width), lambda bi, c: (bi, c, 0))
    return pl.pallas_call(
        _ret_body,
        grid=(b, t // RET_ROWS),
        in_specs=[tok, tok, tok, tok, _const_spec(decay.shape), _const_spec(zeta.shape), _const_spec(xi.shape),
                  _const_spec(gchunk.shape), _const_spec(gn.shape)],
        out_specs=tok,
        out_shape=jax.ShapeDtypeStruct((b, t, width), BF16),
        scratch_shapes=[pltpu.VMEM((RET_HEADS, RET_HEAD_DIM, RET_HEAD_DIM), F32)],
        compiler_params=_cparams(("arbitrary", "arbitrary")),
        name="retention",
    )(rq, rk, rv, rg, decay, zeta, xi, gchunk, gn)


def _oproj_body(x_ref, yc_ref, yd_ref, w_ref, gpost_ref, o_ref):
    m = _dot(yc_ref[...], w_ref[0:NSA_WIDTH, :]) + _dot(yd_ref[...], w_ref[NSA_WIDTH:, :])
    o_ref[...] = x_ref[...] + _rms(m, gpost_ref[...])


def _odd_out(x2, yc2, yd2, w, gpost):
    n, d = x2.shape
    tm = PROJ_ROWS

    def row(width):
        return pl.BlockSpec((tm, width), lambda i: (i, 0))

    return pl.pallas_call(
        _oproj_body,
        grid=(n // tm,),
        in_specs=[row(d), row(NSA_WIDTH), row(RET_WIDTH), _const_spec(w.shape), _const_spec((1, d))],
        out_specs=row(d),
        out_shape=jax.ShapeDtypeStruct((n, d), F32),
        compiler_params=_cparams(("arbitrary",)),
        name="odd_out",
    )(x2, yc2, yd2, w, gpost)


def _odd_in_weight(w_in):
    sizes = [NSA_WIDTH] + [NSA_KV_WIDTH] * 6 + [NSA_BRANCHES * NSA_HEADS] + [RET_WIDTH] * 4
    offs = np.concatenate([[0], np.cumsum(sizes)])
    w_in = w_in.astype(BF16)
    q, kc, vc, ks, vs, kw, vw, gt, rq, rk, rv, rg = [w_in[:, offs[n]:offs[n + 1]] for n in range(len(sizes))]
    w = jnp.concatenate([q, kc, vc, ks, kw, rq, rk, rv, rg], axis=1)
    per_group = NSA_BRANCHES * NSA_HPG
    gates = [jnp.pad(gt[:, g * per_group:(g + 1) * per_group], ((0, 0), (0, GATE_ROWS - per_group)))
             for g in range(NSA_KV_GROUPS)]
    wt = jnp.concatenate([vs, vw] + gates, axis=1).T
    return w, wt


def _rope_tables(positions):
    pos = positions.astype(F32)[..., None]
    half = ROPE_DIM // 2
    inv = 1.0 / (ROPE_THETA ** (jnp.arange(0, ROPE_DIM, 2, dtype=F32) / ROPE_DIM))
    ang = pos * inv
    cs = jnp.concatenate([jnp.cos(ang), jnp.sin(ang)], axis=-1)
    place = np.zeros((ROPE_DIM, 3 * LANES), np.float32)
    base = np.zeros((1, 3 * LANES), np.float32)
    for head0 in range(0, LANES, NSA_HEAD_DIM):
        base[0, head0 + ROPE_DIM:head0 + NSA_HEAD_DIM] = 1.0
        for f in range(half):
            place[f, head0 + f] = 1.0
            place[f, head0 + half + f] = 1.0
            place[half + f, LANES + head0 + f] = -1.0
            place[half + f, 2 * LANES + head0 + half + f] = 1.0
    inv_r = 1.0 / (RET_THETA ** (jnp.arange(0, RET_HEAD_DIM, 2, dtype=F32) / RET_HEAD_DIM))
    ang_r = pos * inv_r
    return cs, jnp.asarray(place, dtype=BF16), jnp.asarray(base), jnp.cos(ang_r), jnp.sin(ang_r)


def _retention_tables():
    H, C, d = RET_HEADS, RET_CHUNK, RET_HEAD_DIM
    log_gamma = jnp.log1p(-jnp.exp2(-5.0 - jnp.arange(H, dtype=F32)))
    idx = jnp.arange(C, dtype=F32)
    rel = idx[:, None] - idx[None, :]
    decay = jnp.where(rel >= 0, jnp.exp(jnp.maximum(rel, 0.0)[None] * log_gamma[:, None, None]), 0.0)
    zeta = jnp.exp((C - 1 - idx)[None, :] * log_gamma[:, None])
    xi = jnp.exp((idx + 1.0)[None, :] * log_gamma[:, None])
    gchunk = jnp.exp(C * log_gamma)
    zeta_b = jnp.broadcast_to(zeta[:, :, None], (H, C, d))
    xi_b = jnp.broadcast_to(xi[:, :, None], (H, C, d))
    gchunk_b = jnp.broadcast_to(gchunk[:, None, None], (H, 1, d))
    return decay, zeta_b, xi_b, gchunk_b


def _cmp_to_sel_t(t_len):
    nc = t_len // CMP_STRIDE
    n_cmp = (t_len - CMP_BLOCK) // CMP_STRIDE + 1
    n_sel = t_len // SEL_BLOCK
    c_start = np.arange(nc) * CMP_STRIDE
    s_start = np.arange(SEL_PAD) * SEL_BLOCK
    hit = ((c_start[None, :] < s_start[:, None] + SEL_BLOCK) & (c_start[None, :] + CMP_BLOCK > s_start[:, None])
           & (np.arange(nc)[None, :] < n_cmp) & (np.arange(SEL_PAD)[:, None] < n_sel))
    return jnp.asarray(hit.astype(np.float32), dtype=BF16)


def _triangle_biases():
    r = np.arange(Q_BLOCK)[:, None]
    tok = np.arange(NSA_HPG * Q_BLOCK)[None, :] % Q_BLOCK
    causal = np.where(r <= tok, 0.0, NEG_INF)
    leaving = np.where(r > tok, 0.0, NEG_INF)
    return jnp.asarray(np.stack([causal, leaving]), dtype=F32)


def _compress_weights(pos, w1, w2):
    G, dh = NSA_KV_GROUPS, NSA_HEAD_DIM
    pos_rows = jnp.tile(pos, (1, G))
    w1r = w1.astype(BF16).reshape(CMP_BLOCK, dh, CMP_HIDDEN)
    w1_bd = jnp.einsum('ab,lij->laibj', jnp.eye(G, dtype=BF16), w1r).reshape(CMP_BLOCK, G * dh, G * CMP_HIDDEN)
    return pos_rows, w1_bd, w2.astype(BF16)


def _odd_layer(x, positions_tables, ret_tables, c2st, tri, gpre, w_in, cmp_k_pos, cmp_k_w1, cmp_k_w2,
               cmp_v_pos, cmp_v_w1, cmp_v_w2, gn_g, w_out, gpost):
    b, t, d = x.shape
    w, wt = _odd_in_weight(w_in)
    (q, kc, vc, ksa, kw, vt, gt, rq, rk, rv, rg) = _odd_proj(x, gpre, w, wt, *positions_tables)
    kpos, kw1, kw2 = _compress_weights(cmp_k_pos, cmp_k_w1, cmp_k_w2)
    vpos, vw1, vw2 = _compress_weights(cmp_v_pos, cmp_v_w1, cmp_v_w2)
    kcm, vct = _compress(kc, vc, kpos, vpos, kw1, vw1, kw2, vw2.T)
    yc = _nsa(q, kcm, vct, c2st, tri, ksa, kw, vt, gt)
    yd = _retention(rq, rk, rv, rg, *ret_tables, gn_g.reshape(RET_HEADS, 1, RET_HEAD_DIM))
    out = _odd_out(x.reshape(b * t, d), yc.reshape(b * t, NSA_WIDTH), yd.reshape(b * t, RET_WIDTH),
                   w_out.astype(BF16), gpost)
    return out.reshape(b, t, d)


def kernel(x, positions, ln_mix_pre, ln_mix_post, ln_ffn_pre, ln_ffn_post, ffn_w_gate, ffn_w_up, ffn_w_down,
           ev_w_in, ev_pool_w, ev_pool_scale, ev_sgu_ln_g, ev_sgu_ln_b, ev_sgu_w, ev_sgu_b, ev_w_out,
           od_w_in, od_cmp_k_pos, od_cmp_k_w1, od_cmp_k_w2, od_cmp_v_pos, od_cmp_v_w1, od_cmp_v_w2,
           od_ret_gn_g, od_w_out):
    b, t, d = x.shape
    depth = ln_mix_pre.shape[0]
    rope = _rope_tables(positions)
    ret_tables = _retention_tables()
    c2st = _cmp_to_sel_t(t)
    tri = _triangle_biases()
    for layer in range(depth):
        gpre = ln_mix_pre[layer].reshape(1, d)
        gpost = ln_mix_post[layer].reshape(1, d)
        if layer % 2 == 0:
            e = layer // 2
            x = _even_layer(x, gpre, ev_w_in[e].astype(BF16), ev_pool_w[e].astype(BF16),
                            ev_pool_scale[e].reshape(1, POOL_WIDTH), ev_sgu_ln_g[e].reshape(1, SGU_WIDTH),
                            ev_sgu_ln_b[e].reshape(1, SGU_WIDTH), ev_sgu_w[e], ev_sgu_b[e].T,
                            ev_w_out[e].astype(BF16), gpost)
        else:
            o = layer // 2
            x = _odd_layer(x, rope, ret_tables, c2st, tri, gpre, od_w_in[o], od_cmp_k_pos[o], od_cmp_k_w1[o],
                           od_cmp_k_w2[o], od_cmp_v_pos[o], od_cmp_v_w1[o], od_cmp_v_w2[o], od_ret_gn_g[o],
                           od_w_out[o], gpost)
        x = _ffn(x.reshape(b * t, d), ln_ffn_pre[layer].reshape(1, d), ln_ffn_post[layer].reshape(1, d),
                 ffn_w_gate[layer].astype(BF16), ffn_w_up[layer].astype(BF16),
                 ffn_w_down[layer].astype(BF16)).reshape(b, t, d)
    return x
```

```python
import functools

import numpy as np
import jax
import jax.numpy as jnp
from jax import lax
from jax.experimental import pallas as pl
from jax.experimental.pallas import tpu as pltpu

F32 = jnp.float32
BF16 = jnp.bfloat16

D_MODEL = 1024
POOL_WINDOWS = (2, 4, 8, 16)
POOL_GROUPS = 4
POOL_WIDTH = D_MODEL // 2
POOL_GDIM = POOL_WIDTH // POOL_GROUPS
POOL_HIST = 16
SGU_GROUPS = 4
SGU_WIDTH = D_MODEL // 2
SGU_GDIM = SGU_WIDTH // SGU_GROUPS
SGU_CHUNK = 128
EVEN_IN = POOL_WIDTH + 2 * SGU_WIDTH
NSA_HEADS = 8
NSA_KV_GROUPS = 2
NSA_HPG = NSA_HEADS // NSA_KV_GROUPS
NSA_HEAD_DIM = 64
NSA_WIDTH = NSA_HEADS * NSA_HEAD_DIM
NSA_GROUP_WIDTH = NSA_HPG * NSA_HEAD_DIM
NSA_KV_WIDTH = NSA_KV_GROUPS * NSA_HEAD_DIM
NSA_BRANCHES = 3
GATE_ROWS = 16
CMP_BLOCK = 32
CMP_STRIDE = 16
CMP_HIDDEN = 128
SEL_BLOCK = 64
SEL_TOPK = 16
SEL_PAD = 64
WINDOW = 512
Q_BLOCK = 256
FORCE_SCORE = 1.0e4
ROPE_THETA = 500000.0
ROPE_DIM = NSA_HEAD_DIM // 4
RET_HEADS = 4
RET_HEAD_DIM = 128
RET_WIDTH = RET_HEADS * RET_HEAD_DIM
RET_CHUNK = 128
RET_THETA = 10000.0
ODD_MIX = NSA_WIDTH + RET_WIDTH
NEG_INF = -1.0e30
LOG2_E = 1.4426950408889634
LANES = 128
SUBLANES = 8

VMEM_LIMIT = 56 * 1024 * 1024
FFN_ROWS = 512
FFN_CHUNKS = 2
EVEN_ROWS = 512
PROJ_ROWS = 512
SEL_KEYS = Q_BLOCK
RANK_CHUNK = 8
RET_ROWS = 512


def _cparams(sem):
    return pltpu.CompilerParams(dimension_semantics=sem, vmem_limit_bytes=VMEM_LIMIT)


def _const_spec(shape):
    nd = len(shape)
    return pl.BlockSpec(shape, lambda *_: (0,) * nd, pipeline_mode=pl.Buffered(1))


def _dot(a, b):
    return jnp.dot(a, b, preferred_element_type=F32)


def _dot_nt(a, b):
    return lax.dot_general(a, b, (((1,), (1,)), ((), ())), preferred_element_type=F32)


def _dot_tn(a, b):
    return lax.dot_general(a, b, (((0,), (0,)), ((), ())), preferred_element_type=F32)


def _rms(x, g, eps=1e-6):
    return x * lax.rsqrt(jnp.mean(x * x, axis=-1, keepdims=True) + eps) * g


def _softmax2_cols(s):
    e = jnp.exp2(s - jnp.max(s, axis=0, keepdims=True))
    return e / jnp.sum(e, axis=0, keepdims=True)


def _eye(n):
    return jnp.where(lax.broadcasted_iota(jnp.int32, (n, n), 0) == lax.broadcasted_iota(jnp.int32, (n, n), 1),
                     1.0, 0.0).astype(BF16)


def _ffn_tile(x, gpre_ref, gpost_ref, wg_ref, wu_ref, wd_ref):
    h = _rms(x, gpre_ref[...]).astype(BF16)
    f_total = wg_ref.shape[1]
    fc = f_total // FFN_CHUNKS
    acc = None
    for c in range(FFN_CHUNKS):
        gate = _dot(h, wg_ref[:, c * fc:(c + 1) * fc])
        up = _dot(h, wu_ref[:, c * fc:(c + 1) * fc])
        act = (gate * jax.nn.sigmoid(gate) * up).astype(BF16)
        part = _dot(act, wd_ref[c * fc:(c + 1) * fc, :])
        acc = part if acc is None else acc + part
    return x + _rms(acc, gpost_ref[...])


def _ffn_specs(d, f):
    return [_const_spec((1, d)), _const_spec((1, d)), _const_spec((d, f)), _const_spec((d, f)), _const_spec((f, d))]


def _even_body(x_ref, gpre_ref, win_ref, poolw_ref, pscale_ref, lng_ref, lnb_ref, sguw_ref, sgub_ref,
               wout_ref, gpost_ref, fpre_ref, fpost_ref, wg_ref, wu_ref, wd_ref, o_ref, hist_ref):
    tt = x_ref.shape[1]
    j = pl.program_id(1)

    @pl.when(j == 0)
    def _():
        hist_ref[0:POOL_HIST, :] = jnp.zeros((POOL_HIST, POOL_WIDTH), F32)

    x = x_ref[0]
    h = _rms(x, gpre_ref[...]).astype(BF16)
    z = _dot(h, win_ref[...])
    a = z[:, :POOL_WIDTH]
    u = z[:, POOL_WIDTH:POOL_WIDTH + SGU_WIDTH]
    v = z[:, POOL_WIDTH + SGU_WIDTH:]

    hist_ref[POOL_HIST:POOL_HIST + tt, :] = a
    t_pos = j * tt + lax.broadcasted_iota(jnp.int32, (tt, 1), 0)
    ya = []
    for gi, w in enumerate(POOL_WINDOWS):
        cols = slice(gi * POOL_GDIM, (gi + 1) * POOL_GDIM)
        win_sum = a[:, cols]
        for s in range(1, w):
            win_sum = win_sum + hist_ref[POOL_HIST - s:POOL_HIST - s + tt, cols]
        cnt = jnp.minimum(t_pos + 1, w).astype(F32)
        diff = (win_sum / cnt - a[:, cols]).astype(BF16)
        ya.append(_dot(diff, poolw_ref[gi]))
    ya = jnp.concatenate(ya, axis=1) * pscale_ref[...]
    hist_ref[0:POOL_HIST, :] = hist_ref[tt:tt + POOL_HIST, :]

    ug = jax.nn.gelu(u)
    vg = jax.nn.gelu(v)
    mu = jnp.mean(vg, axis=-1, keepdims=True)
    var = jnp.mean(jnp.square(vg - mu), axis=-1, keepdims=True)
    vn = ((vg - mu) * lax.rsqrt(var + 1e-5) * lng_ref[...] + lnb_ref[...]).astype(BF16)
    r_i = lax.broadcasted_iota(jnp.int32, (SGU_CHUNK, SGU_CHUNK), 0)
    c_i = lax.broadcasted_iota(jnp.int32, (SGU_CHUNK, SGU_CHUNK), 1)
    yb = []
    for g in range(SGU_GROUPS):
        cols = slice(g * SGU_GDIM, (g + 1) * SGU_GDIM)
        wm = jnp.where(r_i >= c_i, sguw_ref[g], 0.0).astype(BF16)
        bcol = sgub_ref[:, g:g + 1]
        parts = []
        for c in range(tt // SGU_CHUNK):
            rows = slice(c * SGU_CHUNK, (c + 1) * SGU_CHUNK)
            parts.append(ug[rows, cols] * (_dot(wm, vn[rows, cols]) + bcol))
        yb.append(jnp.concatenate(parts, axis=0))
    yb = jnp.concatenate(yb, axis=1)

    m = _dot(ya.astype(BF16), wout_ref[0:POOL_WIDTH, :]) + _dot(yb.astype(BF16), wout_ref[POOL_WIDTH:, :])
    o_ref[0] = _ffn_tile(x + _rms(m, gpost_ref[...]), fpre_ref, fpost_ref, wg_ref, wu_ref, wd_ref)


def _even_layer(x, gpre, win, poolw, pscale, lng, lnb, sguw, sgub_t, wout, gpost, ffn):
    b, t, d = x.shape
    tt = EVEN_ROWS
    row = pl.BlockSpec((1, tt, d), lambda bi, j: (bi, j, 0))
    return pl.pallas_call(
        _even_body,
        grid=(b, t // tt),
        in_specs=[row, _const_spec((1, d)), _const_spec(win.shape), _const_spec(poolw.shape),
                  _const_spec(pscale.shape), _const_spec(lng.shape), _const_spec(lnb.shape),
                  _const_spec(sguw.shape), _const_spec(sgub_t.shape), _const_spec(wout.shape),
                  _const_spec((1, d))] + _ffn_specs(d, ffn[2].shape[1]),
        out_specs=row,
        out_shape=jax.ShapeDtypeStruct((b, t, d), F32),
        scratch_shapes=[pltpu.VMEM((POOL_HIST + tt, POOL_WIDTH), F32)],
        compiler_params=_cparams(("arbitrary", "arbitrary")),
        name="even_layer",
    )(x, gpre, win, poolw, pscale, lng, lnb, sguw, sgub_t, wout, gpost, *ffn)


_C_Q = 0
_C_KC = _C_Q + NSA_WIDTH
_C_VC = _C_KC + NSA_KV_WIDTH
_C_KS = _C_VC + NSA_KV_WIDTH
_C_KW = _C_KS + NSA_KV_WIDTH
_C_RQ = _C_KW + NSA_KV_WIDTH
_C_RK = _C_RQ + RET_WIDTH
_C_RV = _C_RK + RET_WIDTH
_C_RG = _C_RV + RET_WIDTH
ODD_COLS = _C_RG + RET_WIDTH
_R_VS = 0
_R_VW = _R_VS + NSA_KV_WIDTH
_R_GT = _R_VW + NSA_KV_WIDTH
ODD_TROWS = _R_GT + NSA_KV_GROUPS * GATE_ROWS
V_AUG = NSA_HEAD_DIM + 16
VT_ROWS = 2 * NSA_KV_GROUPS * V_AUG


def _rope_nsa(z, c, sa, sb):
    half = ROPE_DIM // 2
    return z * c + pltpu.roll(z, LANES - half, 1) * sa + pltpu.roll(z, half, 1) * sb


def _split3(x):
    a = x.astype(BF16)
    r = x - a.astype(F32)
    b = r.astype(BF16)
    return a, b, (r - b.astype(F32)).astype(BF16)


def _proj_body(x_ref, gpre_ref, w_ref, wt_ref, cs_ref, place_ref, base_ref, cr_ref, sr_ref,
               q_ref, kc_ref, vc_ref, ksa_ref, kw_ref, vt_ref, gt_ref, rq_ref, rk_ref, rv_ref, rg_ref):
    tm = x_ref.shape[1]
    j = pl.program_id(1)
    x = x_ref[0]
    h = _rms(x, gpre_ref[...]).astype(BF16)
    tables = base_ref[...]
    for part in _split3(cs_ref[0]):
        tables = tables + _dot(part, place_ref[...])
    nc, nsa, nsb = tables[:, 0:LANES], tables[:, LANES:2 * LANES], tables[:, 2 * LANES:3 * LANES]
    cr, sr = cr_ref[0], sr_ref[0]
    rc = jnp.concatenate([cr, cr], axis=1)
    rs = jnp.concatenate([-sr, sr], axis=1)

    def cols(start, width):
        return _dot(h, w_ref[:, start:start + width])

    zq = cols(_C_Q, NSA_WIDTH)
    q_scale = NSA_HEAD_DIM ** -0.5 * LOG2_E
    for s in range(NSA_WIDTH // LANES):
        sl = slice(s * LANES, (s + 1) * LANES)
        q_ref[0, :, sl] = (_rope_nsa(zq[:, sl], nc, nsa, nsb) * q_scale).astype(BF16)

    kc_ref[0] = _rope_nsa(cols(_C_KC, LANES), nc, nsa, nsb)
    vc_ref[0] = cols(_C_VC, LANES)

    ks = _rope_nsa(cols(_C_KS, LANES), nc, nsa, nsb)
    kw = _rope_nsa(cols(_C_KW, LANES), nc, nsa, nsb)
    t_pos = j * tm + lax.broadcasted_iota(jnp.int32, (tm, SEL_PAD), 0)
    blk = lax.broadcasted_iota(jnp.int32, (tm, SEL_PAD), 1)
    onehot = jnp.where(t_pos // SEL_BLOCK == blk, 1.0, 0.0).astype(BF16)
    for g in range(NSA_KV_GROUPS):
        sl = slice(g * NSA_HEAD_DIM, (g + 1) * NSA_HEAD_DIM)
        ksa_ref[0, g] = jnp.concatenate([ks[:, sl].astype(BF16), onehot], axis=1)
        kw_ref[0, g] = kw[:, sl].astype(BF16)

    zt = _dot_nt(wt_ref[...], h)
    ones_rows = jnp.where(lax.broadcasted_iota(jnp.int32, (V_AUG - NSA_HEAD_DIM, tm), 0) == 0, 1.0, 0.0).astype(BF16)
    for k in range(2 * NSA_KV_GROUPS):
        vt_ref[0, k * V_AUG:k * V_AUG + NSA_HEAD_DIM] = zt[k * NSA_HEAD_DIM:(k + 1) * NSA_HEAD_DIM].astype(BF16)
        vt_ref[0, k * V_AUG + NSA_HEAD_DIM:(k + 1) * V_AUG] = ones_rows
    gt_ref[0] = jax.nn.sigmoid(zt[_R_GT:])

    k_scale = RET_HEAD_DIM ** -0.5
    zrq = cols(_C_RQ, RET_WIDTH)
    zrk = cols(_C_RK, RET_WIDTH)
    for hh in range(RET_HEADS):
        sl = slice(hh * RET_HEAD_DIM, (hh + 1) * RET_HEAD_DIM)
        zq_h, zk_h = zrq[:, sl], zrk[:, sl]
        rq_ref[0, :, sl] = (zq_h * rc + pltpu.roll(zq_h, RET_HEAD_DIM // 2, 1) * rs).astype(BF16)
        rk_ref[0, :, sl] = ((zk_h * rc + pltpu.roll(zk_h, RET_HEAD_DIM // 2, 1) * rs) * k_scale).astype(BF16)
    rv_ref[0] = cols(_C_RV, RET_WIDTH).astype(BF16)
    rg_ref[0] = cols(_C_RG, RET_WIDTH)


def _odd_proj(x, gpre, w, wt, cs, place, base, cr, sr):
    b, t, d = x.shape
    tm = PROJ_ROWS
    G = NSA_KV_GROUPS

    def row(width):
        return pl.BlockSpec((1, tm, width), lambda bi, j: (bi, j, 0))

    def grp(width):
        return pl.BlockSpec((1, G, tm, width), lambda bi, j: (bi, 0, j, 0))

    def feat(rows):
        return pl.BlockSpec((1, rows, tm), lambda bi, j: (bi, 0, j))

    out_shape = [
        jax.ShapeDtypeStruct((b, t, NSA_WIDTH), BF16),
        jax.ShapeDtypeStruct((b, t, LANES), F32),
        jax.ShapeDtypeStruct((b, t, LANES), F32),
        jax.ShapeDtypeStruct((b, G, t, LANES), BF16),
        jax.ShapeDtypeStruct((b, G, t, NSA_HEAD_DIM), BF16),
        jax.ShapeDtypeStruct((b, VT_ROWS, t), BF16),
        jax.ShapeDtypeStruct((b, G * GATE_ROWS, t), F32),
        jax.ShapeDtypeStruct((b, t, RET_WIDTH), BF16),
        jax.ShapeDtypeStruct((b, t, RET_WIDTH), BF16),
        jax.ShapeDtypeStruct((b, t, RET_WIDTH), BF16),
        jax.ShapeDtypeStruct((b, t, RET_WIDTH), F32),
    ]
    out_specs = [row(NSA_WIDTH), row(LANES), row(LANES), grp(LANES), grp(NSA_HEAD_DIM), feat(VT_ROWS),
                 feat(G * GATE_ROWS), row(RET_WIDTH), row(RET_WIDTH), row(RET_WIDTH), row(RET_WIDTH)]
    return pl.pallas_call(
        _proj_body,
        grid=(b, t // tm),
        in_specs=[row(d), _const_spec((1, d)), _const_spec(w.shape), _const_spec(wt.shape), row(cs.shape[2]),
                  _const_spec(place.shape), _const_spec(base.shape), row(cr.shape[2]), row(sr.shape[2])],
        out_specs=out_specs,
        out_shape=out_shape,
        compiler_params=_cparams(("arbitrary", "arbitrary")),
        name="odd_proj",
    )(x, gpre, w, wt, cs, place, base, cr, sr)


def _compress_body(k_ref, v_ref, kpos_ref, vpos_ref, kw1_ref, vw1_ref, kw2_ref, vw2t_ref, kcm_ref, vct_ref):
    nc = k_ref.shape[1] // CMP_STRIDE
    half = CMP_BLOCK // CMP_STRIDE

    def hidden(x_ref, pos_ref, w1_ref):
        parts = [None] * half
        for off in range(CMP_STRIDE):
            tok = x_ref[0, pl.ds(off, nc, stride=CMP_STRIDE), :]
            for h in range(half):
                l = h * CMP_STRIDE + off
                term = _dot((tok + pos_ref[l:l + 1, :]).astype(BF16), w1_ref[l])
                parts[h] = term if parts[h] is None else parts[h] + term
        pre = parts[0]
        for h in range(1, half):
            pre = pre + pltpu.roll(parts[h], nc - h, 0)
        return jax.nn.gelu(pre).astype(BF16)

    hk = hidden(k_ref, kpos_ref, kw1_ref)
    hv = hidden(v_ref, vpos_ref, vw1_ref)
    for g in range(NSA_KV_GROUPS):
        sl = slice(g * CMP_HIDDEN, (g + 1) * CMP_HIDDEN)
        kcm_ref[0, g] = _dot(hk[:, sl], kw2_ref[...]).astype(BF16)
        vct_ref[0, g] = _dot_nt(vw2t_ref[...], hv[:, sl]).astype(BF16)


def _compress(k, v, kpos, vpos, kw1, vw1, kw2, vw2t):
    b, t, width = k.shape
    nc = t // CMP_STRIDE
    G = NSA_KV_GROUPS
    row = pl.BlockSpec((1, t, width), lambda bi: (bi, 0, 0))
    return pl.pallas_call(
        _compress_body,
        grid=(b,),
        in_specs=[row, row, _const_spec(kpos.shape), _const_spec(vpos.shape), _const_spec(kw1.shape),
                  _const_spec(vw1.shape), _const_spec(kw2.shape), _const_spec(vw2t.shape)],
        out_specs=[pl.BlockSpec((1, G, nc, NSA_HEAD_DIM), lambda bi: (bi, 0, 0, 0)),
                   pl.BlockSpec((1, G, NSA_HEAD_DIM, nc), lambda bi: (bi, 0, 0, 0))],
        out_shape=[jax.ShapeDtypeStruct((b, G, nc, NSA_HEAD_DIM), BF16),
                   jax.ShapeDtypeStruct((b, G, NSA_HEAD_DIM, nc), BF16)],
        compiler_params=_cparams(("arbitrary",)),
        name="compress",
    )(k, v, kpos, vpos, kw1, vw1, kw2, vw2t)


def _nsa_body(q_ref, kcm_ref, vct_ref, c2st_ref, tri_ref, ksa_ref, kw_ref, vt_ref, gt_ref, o_ref,
              score_ref, rank_ref, wbias_ref, sc_ref, sw_ref, pw_ref, sa_ref, pa_ref, lhs_ref, m_ref, alpha_ref, acc_ref, *, top_k):
    G, M, Q, dh = NSA_KV_GROUPS, NSA_HPG, Q_BLOCK, NSA_HEAD_DIM
    cols = M * Q
    nc = kcm_ref.shape[2]
    i = pl.program_id(1)
    t0 = i * Q
    tq = t0 + (lax.broadcasted_iota(jnp.int32, (1, cols), 1) & (Q - 1))
    groups = range(G)

    q_heads, q_rows = [], []
    for g in groups:
        qb = q_ref[0, :, g * NSA_GROUP_WIDTH:(g + 1) * NSA_GROUP_WIDTH]
        q_heads.append([qb[:, m * dh:(m + 1) * dh] for m in range(M)])
        q_rows.append(jnp.concatenate(q_heads[g], axis=0))

    n_idx = lax.broadcasted_iota(jnp.int32, (nc, 1), 0)
    cmp_ok = (n_idx * CMP_STRIDE + (CMP_BLOCK - 1) <= tq) & (n_idx < nc - 1)
    j_idx = lax.broadcasted_iota(jnp.int32, (SEL_PAD, Q), 0)
    cur = (t0 + lax.broadcasted_iota(jnp.int32, (SEL_PAD, Q), 1)) // SEL_BLOCK
    forced = (j_idx == 0) | (j_idx == cur) | (j_idx == cur - 1)
    valid_tok = (tq >= CMP_BLOCK - 1).astype(F32)
    o_cmp = []
    for g in groups:
        sc = sc_ref.at[g]
        sc[...] = jnp.where(cmp_ok, _dot_nt(kcm_ref[0, g], q_rows[g]), NEG_INF)
        sc[...] = jnp.exp2(sc[...] - jnp.max(sc[...], axis=0, keepdims=True))
        norm = valid_tok / jnp.sum(sc[...], axis=0, keepdims=True)
        o_cmp.append(_dot(vct_ref[0, g], sc[...].astype(BF16)) * norm)
        p_grp = sc[:, 0:Q] * norm[:, 0:Q]
        for m in range(1, M):
            p_grp = p_grp + sc[:, m * Q:(m + 1) * Q] * norm[:, m * Q:(m + 1) * Q]
        p_hi = p_grp.astype(BF16)
        p_lo = (p_grp - p_hi.astype(F32)).astype(BF16)
        imp = _dot(c2st_ref[...], p_hi) + _dot(c2st_ref[...], p_lo)
        score_ref[g] = jnp.where(j_idx <= cur, jnp.where(forced, FORCE_SCORE, imp), -1.0)

    def v_aug(branch, g, k0, n):
        r0 = (branch * G + g) * V_AUG
        return vt_ref[0, r0:r0 + V_AUG, pl.ds(k0, n)]

    span = WINDOW + Q
    w0 = pl.multiple_of(jnp.maximum(t0 - WINDOW, 0), Q)
    n_wb = span // Q
    diag_b = jnp.minimum(i, n_wb - 1)
    tri_causal, tri_leaving = tri_ref[0], tri_ref[1]
    for wb in range(n_wb):
        blk_bias = jnp.where(wb == diag_b, tri_causal, jnp.where(wb > diag_b, NEG_INF, 0.0))
        if wb == 0:
            blk_bias = jnp.where(i >= n_wb - 1, tri_leaving, blk_bias)
        wbias_ref[wb * Q:(wb + 1) * Q, :] = blk_bias
    o_win = []
    for g in groups:
        sw = sw_ref.at[g]
        sw[...] = _dot_nt(kw_ref[0, g, pl.ds(w0, span), :], q_rows[g]) + wbias_ref[...]
        pw_ref[g] = jnp.exp2(sw[...] - jnp.max(sw[...], axis=0, keepdims=True)).astype(BF16)
        acc_win = _dot(v_aug(1, g, w0, span), pw_ref[g])
        o_win.append(acc_win[0:dh] / acc_win[dh:dh + 1])

    n_causal = (t0 + Q) // SEL_BLOCK
    rank_ref[...] = jnp.zeros(rank_ref.shape, jnp.int32)
    row_in_tile = lax.broadcasted_iota(jnp.int32, (SUBLANES, LANES), 0)
    for c0 in range(0, SEL_PAD, RANK_CHUNK):
        @pl.when(c0 < n_causal)
        def _(c0=c0):
            for g in groups:
                for l0 in range(0, Q, LANES):
                    ln = slice(l0, l0 + LANES)
                    others = [score_ref[g, k:k + 1, ln] for k in range(c0, c0 + RANK_CHUNK)]
                    for r0 in range(0, SEL_PAD, SUBLANES):
                        mine = score_ref[g, r0:r0 + SUBLANES, ln]
                        count = rank_ref[g, r0:r0 + SUBLANES, ln]
                        for k, other in zip(range(c0, c0 + RANK_CHUNK), others):
                            if r0 > k:
                                beats = other >= mine
                            elif r0 + SUBLANES <= k:
                                beats = other > mine
                            else:
                                beats = (other > mine) | ((other == mine) & (k < r0 + row_in_tile))
                            count = count + beats.astype(jnp.int32)
                        rank_ref[g, r0:r0 + SUBLANES, ln] = count

    eye_q = _eye(Q)
    for g in groups:
        chosen = jnp.where((rank_ref[g] < top_k) & (score_ref[g] >= 0.0), 1.0, 0.0).astype(BF16)
        chosen_t = _dot_nt(eye_q, chosen)
        bias = jnp.where(chosen_t > 0.5, 0.0, NEG_INF).astype(BF16)
        for m in range(M):
            lhs_ref[g, m * Q:(m + 1) * Q, :] = jnp.concatenate([q_heads[g][m], bias], axis=1)
        m_ref[g] = jnp.full((1, cols), NEG_INF, F32)
        acc_ref[g] = jnp.zeros((V_AUG, cols), F32)

    def put_scores(dst, g, kt):
        k0 = pl.multiple_of(kt * SEL_KEYS, SEL_KEYS)
        dst[g] = _dot_nt(ksa_ref[0, g, pl.ds(k0, SEL_KEYS), :], lhs_ref[g])

    def sel_softmax(g, bias=None):
        def scores():
            return sa_ref[g] if bias is None else sa_ref[g] + bias

        m_i = m_ref[g]
        m_new = jnp.maximum(m_i, jnp.max(scores(), axis=0, keepdims=True))
        pa_ref[g] = jnp.exp2(scores() - m_new).astype(BF16)
        alpha_ref[g] = jnp.exp2(m_i - m_new)
        m_ref[g] = m_new

    def sel_accumulate(g, kt):
        k0 = pl.multiple_of(kt * SEL_KEYS, SEL_KEYS)
        acc_ref[g] = alpha_ref[g] * acc_ref[g] + _dot(v_aug(0, g, k0, SEL_KEYS), pa_ref[g])

    n_full = t0 // SEL_KEYS
    lead, lag = 0, G - 1

    @pl.when(n_full >= 1)
    def _():
        put_scores(sa_ref, lead, 0)
        put_scores(sa_ref, lag, 0)
        sel_softmax(lead)

        def sel_step(kt):
            sel_accumulate(lead, kt)
            put_scores(sa_ref, lead, kt + 1)
            sel_softmax(lag)
            sel_accumulate(lag, kt)
            put_scores(sa_ref, lag, kt + 1)
            sel_softmax(lead)

        def sel_two_steps(pair, _):
            sel_step(2 * pair)
            sel_step(2 * pair + 1)
            return 0

        n_steps = n_full - 1
        lax.fori_loop(0, n_steps // 2, sel_two_steps, 0)

        @pl.when(n_steps % 2 == 1)
        def _():
            sel_step(n_steps - 1)

        sel_accumulate(lead, n_full - 1)
        sel_softmax(lag)
        sel_accumulate(lag, n_full - 1)

    o_sel = []
    for g in groups:
        put_scores(sa_ref, g, n_full)
    for g in groups:
        sel_softmax(g, bias=tri_causal)
        sel_accumulate(g, n_full)
        o_sel.append(acc_ref[g, 0:dh, :] / acc_ref[g, dh:dh + 1, :])

    pieces = []
    for g in groups:
        gt = gt_ref[0, g * GATE_ROWS:(g + 1) * GATE_ROWS, :]
        for m in range(M):
            c = slice(m * Q, (m + 1) * Q)
            r = NSA_BRANCHES * m
            mixed = (o_cmp[g][:, c] * gt[r:r + 1] + o_sel[g][:, c] * gt[r + 1:r + 2]
                     + o_win[g][:, c] * gt[r + 2:r + 3])
            pieces.append(_dot_nt(eye_q, mixed.astype(BF16)))
    o_ref[0] = jnp.concatenate(pieces, axis=1).astype(BF16)


def _nsa(q, kcm, vct, c2st, tri, ksa, kw, vt, gt):
    b, t, _ = q.shape
    G = NSA_KV_GROUPS
    nc = kcm.shape[2]
    n_sel = t // SEL_BLOCK
    assert n_sel <= SEL_PAD and t >= WINDOW + Q_BLOCK and t % SEL_KEYS == 0
    top_k = min(SEL_TOPK, n_sel)
    cols, span = NSA_HPG * Q_BLOCK, WINDOW + Q_BLOCK

    def per_batch(shape):
        nd = len(shape)
        return pl.BlockSpec((1,) + shape, lambda bi, i: (bi,) + (0,) * nd)

    token_rows = pl.BlockSpec((1, Q_BLOCK, NSA_WIDTH), lambda bi, i: (bi, i, 0))
    return pl.pallas_call(
        functools.partial(_nsa_body, top_k=top_k),
        grid=(b, t // Q_BLOCK),
        in_specs=[token_rows, per_batch((G, nc, NSA_HEAD_DIM)), per_batch((G, NSA_HEAD_DIM, nc)),
                  _const_spec(c2st.shape), _const_spec(tri.shape), per_batch((G, t, LANES)), per_batch((G, t, NSA_HEAD_DIM)),
                  per_batch((vt.shape[1], t)),
                  pl.BlockSpec((1, G * GATE_ROWS, Q_BLOCK), lambda bi, i: (bi, 0, i))],
        out_specs=token_rows,
        out_shape=jax.ShapeDtypeStruct((b, t, NSA_WIDTH), BF16),
        scratch_shapes=[pltpu.VMEM((G, SEL_PAD, Q_BLOCK), F32),
                        pltpu.VMEM((G, SEL_PAD, Q_BLOCK), jnp.int32),
                        pltpu.VMEM((span, cols), F32),
                        pltpu.VMEM((G, nc, cols), F32),
                        pltpu.VMEM((G, span, cols), F32),
                        pltpu.VMEM((G, span, cols), BF16),
                        pltpu.VMEM((G, SEL_KEYS, cols), F32),
                        pltpu.VMEM((G, SEL_KEYS, cols), BF16),
                        pltpu.VMEM((G, cols, 2 * NSA_HEAD_DIM), BF16),
                        pltpu.VMEM((G, 1, cols), F32),
                        pltpu.VMEM((G, 1, cols), F32),
                        pltpu.VMEM((G, V_AUG, cols), F32)],
        compiler_params=_cparams(("arbitrary", "arbitrary")),
        name="nsa",
    )(q, kcm, vct, c2st, tri, ksa, kw, vt, gt)


def _ret_body(q_ref, k_ref, v_ref, g_ref, decay_ref, zeta_ref, xi_ref, gchunk_ref, gn_ref, o_ref, state_ref):
    C, d = RET_CHUNK, RET_HEAD_DIM

    @pl.when(pl.program_id(1) == 0)
    def _():
        state_ref[...] = jnp.zeros_like(state_ref)

    for c in range(q_ref.shape[1] // C):
        rows = slice(c * C, (c + 1) * C)
        for h in range(RET_HEADS):
            hc = slice(h * d, (h + 1) * d)
            q, k, v = q_ref[0, rows, hc], k_ref[0, rows, hc], v_ref[0, rows, hc]
            s = (_dot_nt(q, k) * decay_ref[h]).astype(BF16)
            state = state_ref[h]
            o = _dot(s, v) + _dot((q.astype(F32) * xi_ref[h]).astype(BF16), state.astype(BF16))
            kz = (k.astype(F32) * zeta_ref[h]).astype(BF16)
            state_ref[h] = state * gchunk_ref[h] + _dot_tn(kz, v)
            mu = jnp.mean(o, axis=-1, keepdims=True)
            var = jnp.mean(jnp.square(o - mu), axis=-1, keepdims=True)
            o = (o - mu) * lax.rsqrt(var + 1e-5) * gn_ref[h]
            gate = g_ref[0, rows, hc]
            o_ref[0, rows, hc] = (gate * jax.nn.sigmoid(gate) * o).astype(BF16)


def _retention(rq, rk, rv, rg, decay, zeta, xi, gchunk, gn):
    b, t, width = rq.shape
    tok = pl.BlockSpec((1, RET_ROWS, width), lambda bi, c: (bi, c, 0))
    return pl.pallas_call(
        _ret_body,
        grid=(b, t // RET_ROWS),
        in_specs=[tok, tok, tok, tok, _const_spec(decay.shape), _const_spec(zeta.shape), _const_spec(xi.shape),
                  _const_spec(gchunk.shape), _const_spec(gn.shape)],
        out_specs=tok,
        out_shape=jax.ShapeDtypeStruct((b, t, width), BF16),
        scratch_shapes=[pltpu.VMEM((RET_HEADS, RET_HEAD_DIM, RET_HEAD_DIM), F32)],
        compiler_params=_cparams(("arbitrary", "arbitrary")),
        name="retention",
    )(rq, rk, rv, rg, decay, zeta, xi, gchunk, gn)


def _oproj_body(x_ref, yc_ref, yd_ref, w_ref, gpost_ref, fpre_ref, fpost_ref, wg_ref, wu_ref, wd_ref, o_ref):
    m = _dot(yc_ref[...], w_ref[0:NSA_WIDTH, :]) + _dot(yd_ref[...], w_ref[NSA_WIDTH:, :])
    o_ref[...] = _ffn_tile(x_ref[...] + _rms(m, gpost_ref[...]), fpre_ref, fpost_ref, wg_ref, wu_ref, wd_ref)


def _odd_out(x2, yc2, yd2, w, gpost, ffn):
    n, d = x2.shape
    tm = FFN_ROWS

    def row(width):
        return pl.BlockSpec((tm, width), lambda i: (i, 0))

    return pl.pallas_call(
        _oproj_body,
        grid=(n // tm,),
        in_specs=[row(d), row(NSA_WIDTH), row(RET_WIDTH), _const_spec(w.shape), _const_spec((1, d))]
        + _ffn_specs(d, ffn[2].shape[1]),
        out_specs=row(d),
        out_shape=jax.ShapeDtypeStruct((n, d), F32),
        compiler_params=_cparams(("arbitrary",)),
        name="odd_out_ffn",
    )(x2, yc2, yd2, w, gpost, *ffn)


def _odd_in_weight(w_in):
    sizes = [NSA_WIDTH] + [NSA_KV_WIDTH] * 6 + [NSA_BRANCHES * NSA_HEADS] + [RET_WIDTH] * 4
    offs = np.concatenate([[0], np.cumsum(sizes)])
    w_in = w_in.astype(BF16)
    q, kc, vc, ks, vs, kw, vw, gt, rq, rk, rv, rg = [w_in[:, offs[n]:offs[n + 1]] for n in range(len(sizes))]
    w = jnp.concatenate([q, kc, vc, ks, kw, rq, rk, rv, rg], axis=1)
    per_group = NSA_BRANCHES * NSA_HPG
    gates = [jnp.pad(gt[:, g * per_group:(g + 1) * per_group], ((0, 0), (0, GATE_ROWS - per_group)))
             for g in range(NSA_KV_GROUPS)]
    wt = jnp.concatenate([vs, vw] + gates, axis=1).T
    return w, wt


def _rope_tables(positions):
    pos = positions.astype(F32)[..., None]
    half = ROPE_DIM // 2
    inv = 1.0 / (ROPE_THETA ** (jnp.arange(0, ROPE_DIM, 2, dtype=F32) / ROPE_DIM))
    ang = pos * inv
    cs = jnp.concatenate([jnp.cos(ang), jnp.sin(ang)], axis=-1)
    place = np.zeros((ROPE_DIM, 3 * LANES), np.float32)
    base = np.zeros((1, 3 * LANES), np.float32)
    for head0 in range(0, LANES, NSA_HEAD_DIM):
        base[0, head0 + ROPE_DIM:head0 + NSA_HEAD_DIM] = 1.0
        for f in range(half):
            place[f, head0 + f] = 1.0
            place[f, head0 + half + f] = 1.0
            place[half + f, LANES + head0 + f] = -1.0
            place[half + f, 2 * LANES + head0 + half + f] = 1.0
    inv_r = 1.0 / (RET_THETA ** (jnp.arange(0, RET_HEAD_DIM, 2, dtype=F32) / RET_HEAD_DIM))
    ang_r = pos * inv_r
    return cs, jnp.asarray(place, dtype=BF16), jnp.asarray(base), jnp.cos(ang_r), jnp.sin(ang_r)


def _retention_tables():
    H, C, d = RET_HEADS, RET_CHUNK, RET_HEAD_DIM
    log_gamma = jnp.log1p(-jnp.exp2(-5.0 - jnp.arange(H, dtype=F32)))
    idx = jnp.arange(C, dtype=F32)
    rel = idx[:, None] - idx[None, :]
    decay = jnp.where(rel >= 0, jnp.exp(jnp.maximum(rel, 0.0)[None] * log_gamma[:, None, None]), 0.0)
    zeta = jnp.exp((C - 1 - idx)[None, :] * log_gamma[:, None])
    xi = jnp.exp((idx + 1.0)[None, :] * log_gamma[:, None])
    gchunk = jnp.exp(C * log_gamma)
    zeta_b = jnp.broadcast_to(zeta[:, :, None], (H, C, d))
    xi_b = jnp.broadcast_to(xi[:, :, None], (H, C, d))
    gchunk_b = jnp.broadcast_to(gchunk[:, None, None], (H, 1, d))
    return decay, zeta_b, xi_b, gchunk_b


def _cmp_to_sel_t(t_len):
    nc = t_len // CMP_STRIDE
    n_cmp = (t_len - CMP_BLOCK) // CMP_STRIDE + 1
    n_sel = t_len // SEL_BLOCK
    c_start = np.arange(nc) * CMP_STRIDE
    s_start = np.arange(SEL_PAD) * SEL_BLOCK
    hit = ((c_start[None, :] < s_start[:, None] + SEL_BLOCK) & (c_start[None, :] + CMP_BLOCK > s_start[:, None])
           & (np.arange(nc)[None, :] < n_cmp) & (np.arange(SEL_PAD)[:, None] < n_sel))
    return jnp.asarray(hit.astype(np.float32), dtype=BF16)


def _triangle_biases():
    r = np.arange(Q_BLOCK)[:, None]
    tok = np.arange(NSA_HPG * Q_BLOCK)[None, :] % Q_BLOCK
    causal = np.where(r <= tok, 0.0, NEG_INF)
    leaving = np.where(r > tok, 0.0, NEG_INF)
    return jnp.asarray(np.stack([causal, leaving]), dtype=F32)


def _compress_weights(pos, w1, w2):
    G, dh = NSA_KV_GROUPS, NSA_HEAD_DIM
    pos_rows = jnp.tile(pos, (1, G))
    w1r = w1.astype(BF16).reshape(CMP_BLOCK, dh, CMP_HIDDEN)
    w1_bd = jnp.einsum('ab,lij->laibj', jnp.eye(G, dtype=BF16), w1r).reshape(CMP_BLOCK, G * dh, G * CMP_HIDDEN)
    return pos_rows, w1_bd, w2.astype(BF16)


def _odd_layer(x, positions_tables, ret_tables, c2st, tri, gpre, w_in, cmp_k_pos, cmp_k_w1, cmp_k_w2,
               cmp_v_pos, cmp_v_w1, cmp_v_w2, gn_g, w_out, gpost, ffn):
    b, t, d = x.shape
    w, wt = _odd_in_weight(w_in)
    (q, kc, vc, ksa, kw, vt, gt, rq, rk, rv, rg) = _odd_proj(x, gpre, w, wt, *positions_tables)
    kpos, kw1, kw2 = _compress_weights(cmp_k_pos, cmp_k_w1, cmp_k_w2)
    vpos, vw1, vw2 = _compress_weights(cmp_v_pos, cmp_v_w1, cmp_v_w2)
    kcm, vct = _compress(kc, vc, kpos, vpos, kw1, vw1, kw2, vw2.T)
    yc = _nsa(q, kcm, vct, c2st, tri, ksa, kw, vt, gt)
    yd = _retention(rq, rk, rv, rg, *ret_tables, gn_g.reshape(RET_HEADS, 1, RET_HEAD_DIM))
    out = _odd_out(x.reshape(b * t, d), yc.reshape(b * t, NSA_WIDTH), yd.reshape(b * t, RET_WIDTH),
                   w_out.astype(BF16), gpost, ffn)
    return out.reshape(b, t, d)


def kernel(x, positions, ln_mix_pre, ln_mix_post, ln_ffn_pre, ln_ffn_post, ffn_w_gate, ffn_w_up, ffn_w_down,
           ev_w_in, ev_pool_w, ev_pool_scale, ev_sgu_ln_g, ev_sgu_ln_b, ev_sgu_w, ev_sgu_b, ev_w_out,
           od_w_in, od_cmp_k_pos, od_cmp_k_w1, od_cmp_k_w2, od_cmp_v_pos, od_cmp_v_w1, od_cmp_v_w2,
           od_ret_gn_g, od_w_out):
    b, t, d = x.shape
    depth = ln_mix_pre.shape[0]
    rope = _rope_tables(positions)
    ret_tables = _retention_tables()
    c2st = _cmp_to_sel_t(t)
    tri = _triangle_biases()
    for layer in range(depth):
        gpre = ln_mix_pre[layer].reshape(1, d)
        gpost = ln_mix_post[layer].reshape(1, d)
        ffn = (ln_ffn_pre[layer].reshape(1, d), ln_ffn_post[layer].reshape(1, d), ffn_w_gate[layer].astype(BF16),
               ffn_w_up[layer].astype(BF16), ffn_w_down[layer].astype(BF16))
        if layer % 2 == 0:
            e = layer // 2
            x = _even_layer(x, gpre, ev_w_in[e].astype(BF16), ev_pool_w[e].astype(BF16),
                            ev_pool_scale[e].reshape(1, POOL_WIDTH), ev_sgu_ln_g[e].reshape(1, SGU_WIDTH),
                            ev_sgu_ln_b[e].reshape(1, SGU_WIDTH), ev_sgu_w[e], ev_sgu_b[e].T,
                            ev_w_out[e].astype(BF16), gpost, ffn)
        else:
            o = layer // 2
            x = _odd_layer(x, rope, ret_tables, c2st, tri, gpre, od_w_in[o], od_cmp_k_pos[o], od_cmp_k_w1[o],
                           od_cmp_k_w2[o], od_cmp_v_pos[o], od_cmp_v_w1[o], od_cmp_v_w2[o], od_ret_gn_g[o],
                           od_w_out[o], gpost, ffn)
    return x
```

```python
import functools

import numpy as np
import jax
import jax.numpy as jnp
from jax import lax
from jax.experimental import pallas as pl
from jax.experimental.pallas import tpu as pltpu

F32 = jnp.float32
BF16 = jnp.bfloat16

D_MODEL = 1024
POOL_WINDOWS = (2, 4, 8, 16)
POOL_GROUPS = 4
POOL_WIDTH = D_MODEL // 2
POOL_GDIM = POOL_WIDTH // POOL_GROUPS
POOL_HIST = 16
SGU_GROUPS = 4
SGU_WIDTH = D_MODEL // 2
SGU_GDIM = SGU_WIDTH // SGU_GROUPS
SGU_CHUNK = 128
EVEN_IN = POOL_WIDTH + 2 * SGU_WIDTH
NSA_HEADS = 8
NSA_KV_GROUPS = 2
NSA_HPG = NSA_HEADS // NSA_KV_GROUPS
NSA_HEAD_DIM = 64
NSA_WIDTH = NSA_HEADS * NSA_HEAD_DIM
NSA_GROUP_WIDTH = NSA_HPG * NSA_HEAD_DIM
NSA_KV_WIDTH = NSA_KV_GROUPS * NSA_HEAD_DIM
NSA_BRANCHES = 3
GATE_ROWS = 16
CMP_BLOCK = 32
CMP_STRIDE = 16
CMP_HIDDEN = 128
SEL_BLOCK = 64
SEL_TOPK = 16
SEL_PAD = 64
WINDOW = 512
Q_BLOCK = 256
FORCE_SCORE = 1.0e4
ROPE_THETA = 500000.0
ROPE_DIM = NSA_HEAD_DIM // 4
RET_HEADS = 4
RET_HEAD_DIM = 128
RET_WIDTH = RET_HEADS * RET_HEAD_DIM
RET_CHUNK = 128
RET_THETA = 10000.0
ODD_MIX = NSA_WIDTH + RET_WIDTH
NEG_INF = -1.0e30
LOG2_E = 1.4426950408889634
LANES = 128
SUBLANES = 8

VMEM_LIMIT = 56 * 1024 * 1024
FFN_ROWS = 512
FFN_CHUNKS = 2
EVEN_ROWS = 512
PROJ_ROWS = 512
SEL_KEYS = Q_BLOCK
RANK_CHUNK = 8
RET_ROWS = 512


def _cparams(sem):
    return pltpu.CompilerParams(dimension_semantics=sem, vmem_limit_bytes=VMEM_LIMIT)


def _const_spec(shape):
    nd = len(shape)
    return pl.BlockSpec(shape, lambda *_: (0,) * nd, pipeline_mode=pl.Buffered(1))


def _dot(a, b):
    return jnp.dot(a, b, preferred_element_type=F32)


def _dot_nt(a, b):
    return lax.dot_general(a, b, (((1,), (1,)), ((), ())), preferred_element_type=F32)


def _dot_tn(a, b):
    return lax.dot_general(a, b, (((0,), (0,)), ((), ())), preferred_element_type=F32)


def _rms(x, g, eps=1e-6):
    return x * lax.rsqrt(jnp.mean(x * x, axis=-1, keepdims=True) + eps) * g


def _softmax2_cols(s):
    e = jnp.exp2(s - jnp.max(s, axis=0, keepdims=True))
    return e / jnp.sum(e, axis=0, keepdims=True)


def _eye(n):
    return jnp.where(lax.broadcasted_iota(jnp.int32, (n, n), 0) == lax.broadcasted_iota(jnp.int32, (n, n), 1),
                     1.0, 0.0).astype(BF16)


def _ffn_tile(x, gpre_ref, gpost_ref, wg_ref, wu_ref, wd_ref):
    h = _rms(x, gpre_ref[...]).astype(BF16)
    f_total = wg_ref.shape[1]
    fc = f_total // FFN_CHUNKS
    acc = None
    for c in range(FFN_CHUNKS):
        gate = _dot(h, wg_ref[:, c * fc:(c + 1) * fc])
        up = _dot(h, wu_ref[:, c * fc:(c + 1) * fc])
        act = (gate * jax.nn.sigmoid(gate) * up).astype(BF16)
        part = _dot(act, wd_ref[c * fc:(c + 1) * fc, :])
        acc = part if acc is None else acc + part
    return x + _rms(acc, gpost_ref[...])


def _ffn_specs(d, f):
    return [_const_spec((1, d)), _const_spec((1, d)), _const_spec((d, f)), _const_spec((d, f)), _const_spec((f, d))]


def _even_body(x_ref, gpre_ref, win_ref, poolw_ref, pscale_ref, lng_ref, lnb_ref, sguw_ref, sgub_ref,
               wout_ref, gpost_ref, fpre_ref, fpost_ref, wg_ref, wu_ref, wd_ref, o_ref, hist_ref):
    tt = x_ref.shape[1]
    j = pl.program_id(1)

    @pl.when(j == 0)
    def _():
        hist_ref[0:POOL_HIST, :] = jnp.zeros((POOL_HIST, POOL_WIDTH), F32)

    x = x_ref[0]
    h = _rms(x, gpre_ref[...]).astype(BF16)
    z = _dot(h, win_ref[...])
    a = z[:, :POOL_WIDTH]
    u = z[:, POOL_WIDTH:POOL_WIDTH + SGU_WIDTH]
    v = z[:, POOL_WIDTH + SGU_WIDTH:]

    hist_ref[POOL_HIST:POOL_HIST + tt, :] = a
    t_pos = j * tt + lax.broadcasted_iota(jnp.int32, (tt, 1), 0)
    ya = []
    for gi, w in enumerate(POOL_WINDOWS):
        cols = slice(gi * POOL_GDIM, (gi + 1) * POOL_GDIM)
        win_sum = a[:, cols]
        for s in range(1, w):
            win_sum = win_sum + hist_ref[POOL_HIST - s:POOL_HIST - s + tt, cols]
        cnt = jnp.minimum(t_pos + 1, w).astype(F32)
        diff = (win_sum / cnt - a[:, cols]).astype(BF16)
        ya.append(_dot(diff, poolw_ref[gi]))
    ya = jnp.concatenate(ya, axis=1) * pscale_ref[...]
    hist_ref[0:POOL_HIST, :] = hist_ref[tt:tt + POOL_HIST, :]

    ug = jax.nn.gelu(u)
    vg = jax.nn.gelu(v)
    mu = jnp.mean(vg, axis=-1, keepdims=True)
    var = jnp.mean(jnp.square(vg - mu), axis=-1, keepdims=True)
    vn = ((vg - mu) * lax.rsqrt(var + 1e-5) * lng_ref[...] + lnb_ref[...]).astype(BF16)
    r_i = lax.broadcasted_iota(jnp.int32, (SGU_CHUNK, SGU_CHUNK), 0)
    c_i = lax.broadcasted_iota(jnp.int32, (SGU_CHUNK, SGU_CHUNK), 1)
    yb = []
    for g in range(SGU_GROUPS):
        cols = slice(g * SGU_GDIM, (g + 1) * SGU_GDIM)
        wm = jnp.where(r_i >= c_i, sguw_ref[g], 0.0).astype(BF16)
        bcol = sgub_ref[:, g:g + 1]
        parts = []
        for c in range(tt // SGU_CHUNK):
            rows = slice(c * SGU_CHUNK, (c + 1) * SGU_CHUNK)
            parts.append(ug[rows, cols] * (_dot(wm, vn[rows, cols]) + bcol))
        yb.append(jnp.concatenate(parts, axis=0))
    yb = jnp.concatenate(yb, axis=1)

    m = _dot(ya.astype(BF16), wout_ref[0:POOL_WIDTH, :]) + _dot(yb.astype(BF16), wout_ref[POOL_WIDTH:, :])
    o_ref[0] = _ffn_tile(x + _rms(m, gpost_ref[...]), fpre_ref, fpost_ref, wg_ref, wu_ref, wd_ref)


def _even_layer(x, gpre, win, poolw, pscale, lng, lnb, sguw, sgub_t, wout, gpost, ffn):
    b, t, d = x.shape
    tt = EVEN_ROWS
    row = pl.BlockSpec((1, tt, d), lambda bi, j: (bi, j, 0))
    return pl.pallas_call(
        _even_body,
        grid=(b, t // tt),
        in_specs=[row, _const_spec((1, d)), _const_spec(win.shape), _const_spec(poolw.shape),
                  _const_spec(pscale.shape), _const_spec(lng.shape), _const_spec(lnb.shape),
                  _const_spec(sguw.shape), _const_spec(sgub_t.shape), _const_spec(wout.shape),
                  _const_spec((1, d))] + _ffn_specs(d, ffn[2].shape[1]),
        out_specs=row,
        out_shape=jax.ShapeDtypeStruct((b, t, d), F32),
        scratch_shapes=[pltpu.VMEM((POOL_HIST + tt, POOL_WIDTH), F32)],
        compiler_params=_cparams(("arbitrary", "arbitrary")),
        name="even_layer",
    )(x, gpre, win, poolw, pscale, lng, lnb, sguw, sgub_t, wout, gpost, *ffn)


_C_Q = 0
_C_KC = _C_Q + NSA_WIDTH
_C_VC = _C_KC + NSA_KV_WIDTH
_C_KS = _C_VC + NSA_KV_WIDTH
_C_KW = _C_KS + NSA_KV_WIDTH
_C_RQ = _C_KW + NSA_KV_WIDTH
_C_RK = _C_RQ + RET_WIDTH
_C_RV = _C_RK + RET_WIDTH
_C_RG = _C_RV + RET_WIDTH
ODD_COLS = _C_RG + RET_WIDTH
_R_VS = 0
_R_VW = _R_VS + NSA_KV_WIDTH
_R_GT = _R_VW + NSA_KV_WIDTH
ODD_TROWS = _R_GT + NSA_KV_GROUPS * GATE_ROWS
V_AUG = NSA_HEAD_DIM + 16
VT_ROWS = 2 * NSA_KV_GROUPS * V_AUG


def _rope_nsa(z, c, sa, sb):
    half = ROPE_DIM // 2
    return z * c + pltpu.roll(z, LANES - half, 1) * sa + pltpu.roll(z, half, 1) * sb


def _split3(x):
    a = x.astype(BF16)
    r = x - a.astype(F32)
    b = r.astype(BF16)
    return a, b, (r - b.astype(F32)).astype(BF16)


def _proj_body(x_ref, gpre_ref, w_ref, wt_ref, cs_ref, place_ref, base_ref, cr_ref, sr_ref,
               q_ref, kc_ref, vc_ref, ksa_ref, kw_ref, vt_ref, gt_ref, rq_ref, rk_ref, rv_ref, rg_ref):
    tm = x_ref.shape[1]
    j = pl.program_id(1)
    x = x_ref[0]
    h = _rms(x, gpre_ref[...]).astype(BF16)
    tables = base_ref[...]
    for part in _split3(cs_ref[0]):
        tables = tables + _dot(part, place_ref[...])
    nc, nsa, nsb = tables[:, 0:LANES], tables[:, LANES:2 * LANES], tables[:, 2 * LANES:3 * LANES]
    cr, sr = cr_ref[0], sr_ref[0]
    rc = jnp.concatenate([cr, cr], axis=1)
    rs = jnp.concatenate([-sr, sr], axis=1)

    def cols(start, width):
        return _dot(h, w_ref[:, start:start + width])

    zq = cols(_C_Q, NSA_WIDTH)
    q_scale = NSA_HEAD_DIM ** -0.5 * LOG2_E
    for s in range(NSA_WIDTH // LANES):
        sl = slice(s * LANES, (s + 1) * LANES)
        q_ref[0, :, sl] = (_rope_nsa(zq[:, sl], nc, nsa, nsb) * q_scale).astype(BF16)

    kc_ref[0] = _rope_nsa(cols(_C_KC, LANES), nc, nsa, nsb)
    vc_ref[0] = cols(_C_VC, LANES)

    ks = _rope_nsa(cols(_C_KS, LANES), nc, nsa, nsb)
    kw = _rope_nsa(cols(_C_KW, LANES), nc, nsa, nsb)
    t_pos = j * tm + lax.broadcasted_iota(jnp.int32, (tm, SEL_PAD), 0)
    blk = lax.broadcasted_iota(jnp.int32, (tm, SEL_PAD), 1)
    onehot = jnp.where(t_pos // SEL_BLOCK == blk, 1.0, 0.0).astype(BF16)
    for g in range(NSA_KV_GROUPS):
        sl = slice(g * NSA_HEAD_DIM, (g + 1) * NSA_HEAD_DIM)
        ksa_ref[0, g] = jnp.concatenate([ks[:, sl].astype(BF16), onehot], axis=1)
        kw_ref[0, g] = kw[:, sl].astype(BF16)

    zt = _dot_nt(wt_ref[...], h)
    ones_rows = jnp.where(lax.broadcasted_iota(jnp.int32, (V_AUG - NSA_HEAD_DIM, tm), 0) == 0, 1.0, 0.0).astype(BF16)
    for k in range(2 * NSA_KV_GROUPS):
        vt_ref[0, k * V_AUG:k * V_AUG + NSA_HEAD_DIM] = zt[k * NSA_HEAD_DIM:(k + 1) * NSA_HEAD_DIM].astype(BF16)
        vt_ref[0, k * V_AUG + NSA_HEAD_DIM:(k + 1) * V_AUG] = ones_rows
    gt_ref[0] = jax.nn.sigmoid(zt[_R_GT:])

    k_scale = RET_HEAD_DIM ** -0.5
    zrq = cols(_C_RQ, RET_WIDTH)
    zrk = cols(_C_RK, RET_WIDTH)
    for hh in range(RET_HEADS):
        sl = slice(hh * RET_HEAD_DIM, (hh + 1) * RET_HEAD_DIM)
        zq_h, zk_h = zrq[:, sl], zrk[:, sl]
        rq_ref[0, :, sl] = (zq_h * rc + pltpu.roll(zq_h, RET_HEAD_DIM // 2, 1) * rs).astype(BF16)
        rk_ref[0, :, sl] = ((zk_h * rc + pltpu.roll(zk_h, RET_HEAD_DIM // 2, 1) * rs) * k_scale).astype(BF16)
    rv_ref[0] = cols(_C_RV, RET_WIDTH).astype(BF16)
    rg_ref[0] = cols(_C_RG, RET_WIDTH)


def _odd_proj(x, gpre, w, wt, cs, place, base, cr, sr):
    b, t, d = x.shape
    tm = PROJ_ROWS
    G = NSA_KV_GROUPS

    def row(width):
        return pl.BlockSpec((1, tm, width), lambda bi, j: (bi, j, 0))

    def grp(width):
        return pl.BlockSpec((1, G, tm, width), lambda bi, j: (bi, 0, j, 0))

    def feat(rows):
        return pl.BlockSpec((1, rows, tm), lambda bi, j: (bi, 0, j))

    out_shape = [
        jax.ShapeDtypeStruct((b, t, NSA_WIDTH), BF16),
        jax.ShapeDtypeStruct((b, t, LANES), F32),
        jax.ShapeDtypeStruct((b, t, LANES), F32),
        jax.ShapeDtypeStruct((b, G, t, LANES), BF16),
        jax.ShapeDtypeStruct((b, G, t, NSA_HEAD_DIM), BF16),
        jax.ShapeDtypeStruct((b, VT_ROWS, t), BF16),
        jax.ShapeDtypeStruct((b, G * GATE_ROWS, t), F32),
        jax.ShapeDtypeStruct((b, t, RET_WIDTH), BF16),
        jax.ShapeDtypeStruct((b, t, RET_WIDTH), BF16),
        jax.ShapeDtypeStruct((b, t, RET_WIDTH), BF16),
        jax.ShapeDtypeStruct((b, t, RET_WIDTH), F32),
    ]
    out_specs = [row(NSA_WIDTH), row(LANES), row(LANES), grp(LANES), grp(NSA_HEAD_DIM), feat(VT_ROWS),
                 feat(G * GATE_ROWS), row(RET_WIDTH), row(RET_WIDTH), row(RET_WIDTH), row(RET_WIDTH)]
    return pl.pallas_call(
        _proj_body,
        grid=(b, t // tm),
        in_specs=[row(d), _const_spec((1, d)), _const_spec(w.shape), _const_spec(wt.shape), row(cs.shape[2]),
                  _const_spec(place.shape), _const_spec(base.shape), row(cr.shape[2]), row(sr.shape[2])],
        out_specs=out_specs,
        out_shape=out_shape,
        compiler_params=_cparams(("arbitrary", "arbitrary")),
        name="odd_proj",
    )(x, gpre, w, wt, cs, place, base, cr, sr)


def _compress_body(k_ref, v_ref, kpos_ref, vpos_ref, kw1_ref, vw1_ref, kw2_ref, vw2t_ref, kcm_ref, vct_ref):
    nc = k_ref.shape[1] // CMP_STRIDE
    half = CMP_BLOCK // CMP_STRIDE

    def hidden(x_ref, pos_ref, w1_ref):
        parts = [None] * half
        for off in range(CMP_STRIDE):
            tok = x_ref[0, pl.ds(off, nc, stride=CMP_STRIDE), :]
            for h in range(half):
                l = h * CMP_STRIDE + off
                term = _dot((tok + pos_ref[l:l + 1, :]).astype(BF16), w1_ref[l])
                parts[h] = term if parts[h] is None else parts[h] + term
        pre = parts[0]
        for h in range(1, half):
            pre = pre + pltpu.roll(parts[h], nc - h, 0)
        return jax.nn.gelu(pre).astype(BF16)

    hk = hidden(k_ref, kpos_ref, kw1_ref)
    hv = hidden(v_ref, vpos_ref, vw1_ref)
    for g in range(NSA_KV_GROUPS):
        sl = slice(g * CMP_HIDDEN, (g + 1) * CMP_HIDDEN)
        kcm_ref[0, g] = _dot(hk[:, sl], kw2_ref[...]).astype(BF16)
        vct_ref[0, g] = _dot_nt(vw2t_ref[...], hv[:, sl]).astype(BF16)


def _compress(k, v, kpos, vpos, kw1, vw1, kw2, vw2t):
    b, t, width = k.shape
    nc = t // CMP_STRIDE
    G = NSA_KV_GROUPS
    row = pl.BlockSpec((1, t, width), lambda bi: (bi, 0, 0))
    return pl.pallas_call(
        _compress_body,
        grid=(b,),
        in_specs=[row, row, _const_spec(kpos.shape), _const_spec(vpos.shape), _const_spec(kw1.shape),
                  _const_spec(vw1.shape), _const_spec(kw2.shape), _const_spec(vw2t.shape)],
        out_specs=[pl.BlockSpec((1, G, nc, NSA_HEAD_DIM), lambda bi: (bi, 0, 0, 0)),
                   pl.BlockSpec((1, G, NSA_HEAD_DIM, nc), lambda bi: (bi, 0, 0, 0))],
        out_shape=[jax.ShapeDtypeStruct((b, G, nc, NSA_HEAD_DIM), BF16),
                   jax.ShapeDtypeStruct((b, G, NSA_HEAD_DIM, nc), BF16)],
        compiler_params=_cparams(("arbitrary",)),
        name="compress",
    )(k, v, kpos, vpos, kw1, vw1, kw2, vw2t)


def _nsa_body(q_ref, kcm_ref, vct_ref, c2st_ref, tri_ref, ksa_ref, kw_ref, vt_ref, gt_ref, o_ref,
              score_ref, rank_ref, wbias_ref, sc_ref, sw_ref, pw_ref, sa_ref, sb_ref, pa_ref, pb_ref, lhs_ref, m_ref, acc_ref, *, top_k):
    G, M, Q, dh = NSA_KV_GROUPS, NSA_HPG, Q_BLOCK, NSA_HEAD_DIM
    cols = M * Q
    nc = kcm_ref.shape[2]
    i = pl.program_id(1)
    t0 = i * Q
    tq = t0 + (lax.broadcasted_iota(jnp.int32, (1, cols), 1) & (Q - 1))
    groups = range(G)

    q_heads, q_rows = [], []
    for g in groups:
        qb = q_ref[0, :, g * NSA_GROUP_WIDTH:(g + 1) * NSA_GROUP_WIDTH]
        q_heads.append([qb[:, m * dh:(m + 1) * dh] for m in range(M)])
        q_rows.append(jnp.concatenate(q_heads[g], axis=0))

    n_idx = lax.broadcasted_iota(jnp.int32, (nc, 1), 0)
    cmp_ok = (n_idx * CMP_STRIDE + (CMP_BLOCK - 1) <= tq) & (n_idx < nc - 1)
    j_idx = lax.broadcasted_iota(jnp.int32, (SEL_PAD, Q), 0)
    cur = (t0 + lax.broadcasted_iota(jnp.int32, (SEL_PAD, Q), 1)) // SEL_BLOCK
    forced = (j_idx == 0) | (j_idx == cur) | (j_idx == cur - 1)
    valid_tok = (tq >= CMP_BLOCK - 1).astype(F32)
    o_cmp = []
    for g in groups:
        sc = sc_ref.at[g]
        sc[...] = jnp.where(cmp_ok, _dot_nt(kcm_ref[0, g], q_rows[g]), NEG_INF)
        sc[...] = jnp.exp2(sc[...] - jnp.max(sc[...], axis=0, keepdims=True))
        norm = valid_tok / jnp.sum(sc[...], axis=0, keepdims=True)
        o_cmp.append(_dot(vct_ref[0, g], sc[...].astype(BF16)) * norm)
        p_grp = sc[:, 0:Q] * norm[:, 0:Q]
        for m in range(1, M):
            p_grp = p_grp + sc[:, m * Q:(m + 1) * Q] * norm[:, m * Q:(m + 1) * Q]
        p_hi = p_grp.astype(BF16)
        p_lo = (p_grp - p_hi.astype(F32)).astype(BF16)
        imp = _dot(c2st_ref[...], p_hi) + _dot(c2st_ref[...], p_lo)
        score_ref[g] = jnp.where(j_idx <= cur, jnp.where(forced, FORCE_SCORE, imp), -1.0)

    def v_aug(branch, g, k0, n):
        r0 = (branch * G + g) * V_AUG
        return vt_ref[0, r0:r0 + V_AUG, pl.ds(k0, n)]

    span = WINDOW + Q
    w0 = pl.multiple_of(jnp.maximum(t0 - WINDOW, 0), Q)
    n_wb = span // Q
    diag_b = jnp.minimum(i, n_wb - 1)
    tri_causal, tri_leaving = tri_ref[0], tri_ref[1]
    for wb in range(n_wb):
        blk_bias = jnp.where(wb == diag_b, tri_causal, jnp.where(wb > diag_b, NEG_INF, 0.0))
        if wb == 0:
            blk_bias = jnp.where(i >= n_wb - 1, tri_leaving, blk_bias)
        wbias_ref[wb * Q:(wb + 1) * Q, :] = blk_bias
    o_win = []
    for g in groups:
        sw = sw_ref.at[g]
        sw[...] = _dot_nt(kw_ref[0, g, pl.ds(w0, span), :], q_rows[g]) + wbias_ref[...]
        pw_ref[g] = jnp.exp2(sw[...] - jnp.max(sw[...], axis=0, keepdims=True)).astype(BF16)
        acc_win = _dot(v_aug(1, g, w0, span), pw_ref[g])
        o_win.append(acc_win[0:dh] / acc_win[dh:dh + 1])

    n_causal = (t0 + Q) // SEL_BLOCK
    rank_ref[...] = jnp.zeros(rank_ref.shape, jnp.int32)
    row_in_tile = lax.broadcasted_iota(jnp.int32, (SUBLANES, LANES), 0)
    for c0 in range(0, SEL_PAD, RANK_CHUNK):
        @pl.when(c0 < n_causal)
        def _(c0=c0):
            for g in groups:
                for l0 in range(0, Q, LANES):
                    ln = slice(l0, l0 + LANES)
                    others = [score_ref[g, k:k + 1, ln] for k in range(c0, c0 + RANK_CHUNK)]
                    for r0 in range(0, SEL_PAD, SUBLANES):
                        mine = score_ref[g, r0:r0 + SUBLANES, ln]
                        count = rank_ref[g, r0:r0 + SUBLANES, ln]
                        for k, other in zip(range(c0, c0 + RANK_CHUNK), others):
                            if r0 > k:
                                beats = other >= mine
                            elif r0 + SUBLANES <= k:
                                beats = other > mine
                            else:
                                beats = (other > mine) | ((other == mine) & (k < r0 + row_in_tile))
                            count = count + beats.astype(jnp.int32)
                        rank_ref[g, r0:r0 + SUBLANES, ln] = count

    eye_q = _eye(Q)
    for g in groups:
        chosen = jnp.where((rank_ref[g] < top_k) & (score_ref[g] >= 0.0), 1.0, 0.0).astype(BF16)
        chosen_t = _dot_nt(eye_q, chosen)
        bias = jnp.where(chosen_t > 0.5, 0.0, NEG_INF).astype(BF16)
        for m in range(M):
            lhs_ref[g, m * Q:(m + 1) * Q, :] = jnp.concatenate([q_heads[g][m], bias], axis=1)
        m_ref[g] = jnp.full((1, cols), NEG_INF, F32)
        acc_ref[g] = jnp.zeros((V_AUG, cols), F32)

    def put_scores(dst, g, kt):
        k0 = pl.multiple_of(kt * SEL_KEYS, SEL_KEYS)
        dst[g] = _dot_nt(ksa_ref[0, g, pl.ds(k0, SEL_KEYS), :], lhs_ref[g])

    def sel_update(src, p_dst, g, kt, bias=None):
        k0 = pl.multiple_of(kt * SEL_KEYS, SEL_KEYS)

        def scores():
            return src[g] if bias is None else src[g] + bias

        m_i = m_ref[g]
        m_new = jnp.maximum(m_i, jnp.max(scores(), axis=0, keepdims=True))
        p_dst[g] = jnp.exp2(scores() - m_new).astype(BF16)
        acc_ref[g] = jnp.exp2(m_i - m_new) * acc_ref[g] + _dot(v_aug(0, g, k0, SEL_KEYS), p_dst[g])
        m_ref[g] = m_new

    n_full = t0 // SEL_KEYS
    odd = n_full & 1
    for g in groups:
        put_scores(sa_ref, g, 0)

    @pl.when(odd == 1)
    def _():
        for g in groups:
            sel_update(sa_ref, pa_ref, g, 0)
            put_scores(sa_ref, g, 1)

    def sel_pair(pair, _):
        ta = odd + 2 * pair
        for g in groups:
            put_scores(sb_ref, g, ta + 1)
        for g in groups:
            sel_update(sa_ref, pa_ref, g, ta)
        for g in groups:
            put_scores(sa_ref, g, ta + 2)
        for g in groups:
            sel_update(sb_ref, pb_ref, g, ta + 1)
        return 0

    lax.fori_loop(0, n_full // 2, sel_pair, 0)
    o_sel = []
    for g in groups:
        sel_update(sa_ref, pa_ref, g, n_full, bias=tri_causal)
        o_sel.append(acc_ref[g, 0:dh, :] / acc_ref[g, dh:dh + 1, :])

    pieces = []
    for g in groups:
        gt = gt_ref[0, g * GATE_ROWS:(g + 1) * GATE_ROWS, :]
        for m in range(M):
            c = slice(m * Q, (m + 1) * Q)
            r = NSA_BRANCHES * m
            mixed = (o_cmp[g][:, c] * gt[r:r + 1] + o_sel[g][:, c] * gt[r + 1:r + 2]
                     + o_win[g][:, c] * gt[r + 2:r + 3])
            pieces.append(_dot_nt(eye_q, mixed.astype(BF16)))
    o_ref[0] = jnp.concatenate(pieces, axis=1).astype(BF16)


def _nsa(q, kcm, vct, c2st, tri, ksa, kw, vt, gt):
    b, t, _ = q.shape
    G = NSA_KV_GROUPS
    nc = kcm.shape[2]
    n_sel = t // SEL_BLOCK
    assert n_sel <= SEL_PAD and t >= WINDOW + Q_BLOCK and t % SEL_KEYS == 0
    top_k = min(SEL_TOPK, n_sel)
    cols, span = NSA_HPG * Q_BLOCK, WINDOW + Q_BLOCK

    def per_batch(shape):
        nd = len(shape)
        return pl.BlockSpec((1,) + shape, lambda bi, i: (bi,) + (0,) * nd)

    token_rows = pl.BlockSpec((1, Q_BLOCK, NSA_WIDTH), lambda bi, i: (bi, i, 0))
    return pl.pallas_call(
        functools.partial(_nsa_body, top_k=top_k),
        grid=(b, t // Q_BLOCK),
        in_specs=[token_rows, per_batch((G, nc, NSA_HEAD_DIM)), per_batch((G, NSA_HEAD_DIM, nc)),
                  _const_spec(c2st.shape), _const_spec(tri.shape), per_batch((G, t, LANES)), per_batch((G, t, NSA_HEAD_DIM)),
                  per_batch((vt.shape[1], t)),
                  pl.BlockSpec((1, G * GATE_ROWS, Q_BLOCK), lambda bi, i: (bi, 0, i))],
        out_specs=token_rows,
        out_shape=jax.ShapeDtypeStruct((b, t, NSA_WIDTH), BF16),
        scratch_shapes=[pltpu.VMEM((G, SEL_PAD, Q_BLOCK), F32),
                        pltpu.VMEM((G, SEL_PAD, Q_BLOCK), jnp.int32),
                        pltpu.VMEM((span, cols), F32),
                        pltpu.VMEM((G, nc, cols), F32),
                        pltpu.VMEM((G, span, cols), F32),
                        pltpu.VMEM((G, span, cols), BF16),
                        pltpu.VMEM((G, SEL_KEYS, cols), F32),
                        pltpu.VMEM((G, SEL_KEYS, cols), F32),
                        pltpu.VMEM((G, SEL_KEYS, cols), BF16),
                        pltpu.VMEM((G, SEL_KEYS, cols), BF16),
                        pltpu.VMEM((G, cols, 2 * NSA_HEAD_DIM), BF16),
                        pltpu.VMEM((G, 1, cols), F32),
                        pltpu.VMEM((G, V_AUG, cols), F32)],
        compiler_params=_cparams(("arbitrary", "arbitrary")),
        name="nsa",
    )(q, kcm, vct, c2st, tri, ksa, kw, vt, gt)


def _ret_body(q_ref, k_ref, v_ref, g_ref, decay_ref, zeta_ref, xi_ref, gchunk_ref, gn_ref, o_ref, state_ref):
    C, d = RET_CHUNK, RET_HEAD_DIM

    @pl.when(pl.program_id(1) == 0)
    def _():
        state_ref[...] = jnp.zeros_like(state_ref)

    n_chunks = q_ref.shape[1] // C
    for h in range(RET_HEADS):
        hc = slice(h * d, (h + 1) * d)
        state = state_ref[h]
        before = []
        for c in range(n_chunks):
            rows = slice(c * C, (c + 1) * C)
            kz = (k_ref[0, rows, hc].astype(F32) * zeta_ref[h]).astype(BF16)
            before.append(state.astype(BF16))
            state = state * gchunk_ref[h] + _dot_tn(kz, v_ref[0, rows, hc])
        state_ref[h] = state
        for c in range(n_chunks):
            rows = slice(c * C, (c + 1) * C)
            q, k, v = q_ref[0, rows, hc], k_ref[0, rows, hc], v_ref[0, rows, hc]
            s = (_dot_nt(q, k) * decay_ref[h]).astype(BF16)
            o = _dot(s, v) + _dot((q.astype(F32) * xi_ref[h]).astype(BF16), before[c])
            mu = jnp.mean(o, axis=-1, keepdims=True)
            var = jnp.mean(jnp.square(o - mu), axis=-1, keepdims=True)
            o = (o - mu) * lax.rsqrt(var + 1e-5) * gn_ref[h]
            gate = g_ref[0, rows, hc]
            o_ref[0, rows, hc] = (gate * jax.nn.sigmoid(gate) * o).astype(BF16)


def _retention(rq, rk, rv, rg, decay, zeta, xi, gchunk, gn):
    b, t, width = rq.shape
    tok = pl.BlockSpec((1, RET_ROWS, width), lambda bi, c: (bi, c, 0))
    return pl.pallas_call(
        _ret_body,
        grid=(b, t // RET_ROWS),
        in_specs=[tok, tok, tok, tok, _const_spec(decay.shape), _const_spec(zeta.shape), _const_spec(xi.shape),
                  _const_spec(gchunk.shape), _const_spec(gn.shape)],
        out_specs=tok,
        out_shape=jax.ShapeDtypeStruct((b, t, width), BF16),
        scratch_shapes=[pltpu.VMEM((RET_HEADS, RET_HEAD_DIM, RET_HEAD_DIM), F32)],
        compiler_params=_cparams(("arbitrary", "arbitrary")),
        name="retention",
    )(rq, rk, rv, rg, decay, zeta, xi, gchunk, gn)


def _oproj_body(x_ref, yc_ref, yd_ref, w_ref, gpost_ref, fpre_ref, fpost_ref, wg_ref, wu_ref, wd_ref, o_ref):
    m = _dot(yc_ref[...], w_ref[0:NSA_WIDTH, :]) + _dot(yd_ref[...], w_ref[NSA_WIDTH:, :])
    o_ref[...] = _ffn_tile(x_ref[...] + _rms(m, gpost_ref[...]), fpre_ref, fpost_ref, wg_ref, wu_ref, wd_ref)


def _odd_out(x2, yc2, yd2, w, gpost, ffn):
    n, d = x2.shape
    tm = FFN_ROWS

    def row(width):
        return pl.BlockSpec((tm, width), lambda i: (i, 0))

    return pl.pallas_call(
        _oproj_body,
        grid=(n // tm,),
        in_specs=[row(d), row(NSA_WIDTH), row(RET_WIDTH), _const_spec(w.shape), _const_spec((1, d))]
        + _ffn_specs(d, ffn[2].shape[1]),
        out_specs=row(d),
        out_shape=jax.ShapeDtypeStruct((n, d), F32),
        compiler_params=_cparams(("arbitrary",)),
        name="odd_out_ffn",
    )(x2, yc2, yd2, w, gpost, *ffn)


def _odd_in_weight(w_in):
    sizes = [NSA_WIDTH] + [NSA_KV_WIDTH] * 6 + [NSA_BRANCHES * NSA_HEADS] + [RET_WIDTH] * 4
    offs = np.concatenate([[0], np.cumsum(sizes)])
    w_in = w_in.astype(BF16)
    q, kc, vc, ks, vs, kw, vw, gt, rq, rk, rv, rg = [w_in[:, offs[n]:offs[n + 1]] for n in range(len(sizes))]
    w = jnp.concatenate([q, kc, vc, ks, kw, rq, rk, rv, rg], axis=1)
    per_group = NSA_BRANCHES * NSA_HPG
    gates = [jnp.pad(gt[:, g * per_group:(g + 1) * per_group], ((0, 0), (0, GATE_ROWS - per_group)))
             for g in range(NSA_KV_GROUPS)]
    wt = jnp.concatenate([vs, vw] + gates, axis=1).T
    return w, wt


def _rope_tables(positions):
    pos = positions.astype(F32)[..., None]
    half = ROPE_DIM // 2
    inv = 1.0 / (ROPE_THETA ** (jnp.arange(0, ROPE_DIM, 2, dtype=F32) / ROPE_DIM))
    ang = pos * inv
    cs = jnp.concatenate([jnp.cos(ang), jnp.sin(ang)], axis=-1)
    place = np.zeros((ROPE_DIM, 3 * LANES), np.float32)
    base = np.zeros((1, 3 * LANES), np.float32)
    for head0 in range(0, LANES, NSA_HEAD_DIM):
        base[0, head0 + ROPE_DIM:head0 + NSA_HEAD_DIM] = 1.0
        for f in range(half):
            place[f, head0 + f] = 1.0
            place[f, head0 + half + f] = 1.0
            place[half + f, LANES + head0 + f] = -1.0
            place[half + f, 2 * LANES + head0 + half + f] = 1.0
    inv_r = 1.0 / (RET_THETA ** (jnp.arange(0, RET_HEAD_DIM, 2, dtype=F32) / RET_HEAD_DIM))
    ang_r = pos * inv_r
    return cs, jnp.asarray(place, dtype=BF16), jnp.asarray(base), jnp.cos(ang_r), jnp.sin(ang_r)


def _retention_tables():
    H, C, d = RET_HEADS, RET_CHUNK, RET_HEAD_DIM
    log_gamma = np.log1p(-np.exp2(-5.0 - np.arange(H, dtype=np.float64)))
    idx = np.arange(C, dtype=np.float64)
    rel = idx[:, None] - idx[None, :]
    decay = np.where(rel >= 0, np.exp(np.maximum(rel, 0.0)[None] * log_gamma[:, None, None]), 0.0)
    zeta = np.exp((C - 1 - idx)[None, :] * log_gamma[:, None])
    xi = np.exp((idx + 1.0)[None, :] * log_gamma[:, None])
    gchunk = np.exp(C * log_gamma)
    zeta_b = np.broadcast_to(zeta[:, :, None], (H, C, d))
    xi_b = np.broadcast_to(xi[:, :, None], (H, C, d))
    gchunk_b = np.broadcast_to(gchunk[:, None, None], (H, 1, d))
    return tuple(jnp.asarray(a, dtype=F32) for a in (decay, zeta_b, xi_b, gchunk_b))


def _cmp_to_sel_t(t_len):
    nc = t_len // CMP_STRIDE
    n_cmp = (t_len - CMP_BLOCK) // CMP_STRIDE + 1
    n_sel = t_len // SEL_BLOCK
    c_start = np.arange(nc) * CMP_STRIDE
    s_start = np.arange(SEL_PAD) * SEL_BLOCK
    hit = ((c_start[None, :] < s_start[:, None] + SEL_BLOCK) & (c_start[None, :] + CMP_BLOCK > s_start[:, None])
           & (np.arange(nc)[None, :] < n_cmp) & (np.arange(SEL_PAD)[:, None] < n_sel))
    return jnp.asarray(hit.astype(np.float32), dtype=BF16)


def _triangle_biases():
    r = np.arange(Q_BLOCK)[:, None]
    tok = np.arange(NSA_HPG * Q_BLOCK)[None, :] % Q_BLOCK
    causal = np.where(r <= tok, 0.0, NEG_INF)
    leaving = np.where(r > tok, 0.0, NEG_INF)
    return jnp.asarray(np.stack([causal, leaving]), dtype=F32)


def _compress_weights(pos, w1, w2):
    G, dh = NSA_KV_GROUPS, NSA_HEAD_DIM
    pos_rows = jnp.tile(pos, (1, G))
    w1r = w1.astype(BF16).reshape(CMP_BLOCK, dh, CMP_HIDDEN)
    w1_bd = jnp.einsum('ab,lij->laibj', jnp.eye(G, dtype=BF16), w1r).reshape(CMP_BLOCK, G * dh, G * CMP_HIDDEN)
    return pos_rows, w1_bd, w2.astype(BF16)


def _odd_layer(x, positions_tables, ret_tables, c2st, tri, gpre, w_in, cmp_k_pos, cmp_k_w1, cmp_k_w2,
               cmp_v_pos, cmp_v_w1, cmp_v_w2, gn_g, w_out, gpost, ffn):
    b, t, d = x.shape
    w, wt = _odd_in_weight(w_in)
    (q, kc, vc, ksa, kw, vt, gt, rq, rk, rv, rg) = _odd_proj(x, gpre, w, wt, *positions_tables)
    kpos, kw1, kw2 = _compress_weights(cmp_k_pos, cmp_k_w1, cmp_k_w2)
    vpos, vw1, vw2 = _compress_weights(cmp_v_pos, cmp_v_w1, cmp_v_w2)
    kcm, vct = _compress(kc, vc, kpos, vpos, kw1, vw1, kw2, vw2.T)
    yc = _nsa(q, kcm, vct, c2st, tri, ksa, kw, vt, gt)
    yd = _retention(rq, rk, rv, rg, *ret_tables, gn_g.reshape(RET_HEADS, 1, RET_HEAD_DIM))
    out = _odd_out(x.reshape(b * t, d), yc.reshape(b * t, NSA_WIDTH), yd.reshape(b * t, RET_WIDTH),
                   w_out.astype(BF16), gpost, ffn)
    return out.reshape(b, t, d)


def kernel(x, positions, ln_mix_pre, ln_mix_post, ln_ffn_pre, ln_ffn_post, ffn_w_gate, ffn_w_up, ffn_w_down,
           ev_w_in, ev_pool_w, ev_pool_scale, ev_sgu_ln_g, ev_sgu_ln_b, ev_sgu_w, ev_sgu_b, ev_w_out,
           od_w_in, od_cmp_k_pos, od_cmp_k_w1, od_cmp_k_w2, od_cmp_v_pos, od_cmp_v_w1, od_cmp_v_w2,
           od_ret_gn_g, od_w_out):
    b, t, d = x.shape
    depth = ln_mix_pre.shape[0]
    rope = _rope_tables(positions)
    ret_tables = _retention_tables()
    c2st = _cmp_to_sel_t(t)
    tri = _triangle_biases()
    for layer in range(depth):
        gpre = ln_mix_pre[layer].reshape(1, d)
        gpost = ln_mix_post[layer].reshape(1, d)
        ffn = (ln_ffn_pre[layer].reshape(1, d), ln_ffn_post[layer].reshape(1, d), ffn_w_gate[layer].astype(BF16),
               ffn_w_up[layer].astype(BF16), ffn_w_down[layer].astype(BF16))
        if layer % 2 == 0:
            e = layer // 2
            x = _even_layer(x, gpre, ev_w_in[e].astype(BF16), ev_pool_w[e].astype(BF16),
                            ev_pool_scale[e].reshape(1, POOL_WIDTH), ev_sgu_ln_g[e].reshape(1, SGU_WIDTH),
                            ev_sgu_ln_b[e].reshape(1, SGU_WIDTH), ev_sgu_w[e], ev_sgu_b[e].T,
                            ev_w_out[e].astype(BF16), gpost, ffn)
        else:
            o = layer // 2
            x = _odd_layer(x, rope, ret_tables, c2st, tri, gpre, od_w_in[o], od_cmp_k_pos[o], od_cmp_k_w1[o],
                           od_cmp_k_w2[o], od_cmp_v_pos[o], od_cmp_v_w1[o], od_cmp_v_w2[o], od_ret_gn_g[o],
                           od_w_out[o], gpost, ffn)
    return x
```

```python
import functools

import numpy as np
import jax
import jax.numpy as jnp
from jax import lax
from jax.experimental import pallas as pl
from jax.experimental.pallas import tpu as pltpu

F32 = jnp.float32
BF16 = jnp.bfloat16

D_MODEL = 1024
POOL_WINDOWS = (2, 4, 8, 16)
POOL_GROUPS = 4
POOL_WIDTH = D_MODEL // 2
POOL_GDIM = POOL_WIDTH // POOL_GROUPS
POOL_HIST = 16
SGU_GROUPS = 4
SGU_WIDTH = D_MODEL // 2
SGU_GDIM = SGU_WIDTH // SGU_GROUPS
SGU_CHUNK = 128
EVEN_IN = POOL_WIDTH + 2 * SGU_WIDTH
NSA_HEADS = 8
NSA_KV_GROUPS = 2
NSA_HPG = NSA_HEADS // NSA_KV_GROUPS
NSA_HEAD_DIM = 64
NSA_WIDTH = NSA_HEADS * NSA_HEAD_DIM
NSA_GROUP_WIDTH = NSA_HPG * NSA_HEAD_DIM
NSA_KV_WIDTH = NSA_KV_GROUPS * NSA_HEAD_DIM
NSA_BRANCHES = 3
GATE_ROWS = 16
CMP_BLOCK = 32
CMP_STRIDE = 16
CMP_HIDDEN = 128
SEL_BLOCK = 64
SEL_TOPK = 16
SEL_PAD = 64
WINDOW = 512
Q_BLOCK = 256
FORCE_SCORE = 1.0e4
ROPE_THETA = 500000.0
ROPE_DIM = NSA_HEAD_DIM // 4
RET_HEADS = 4
RET_HEAD_DIM = 128
RET_WIDTH = RET_HEADS * RET_HEAD_DIM
RET_CHUNK = 128
RET_THETA = 10000.0
ODD_MIX = NSA_WIDTH + RET_WIDTH
NEG_INF = -1.0e30
LOG2_E = 1.4426950408889634
LANES = 128
SUBLANES = 8

VMEM_LIMIT = 56 * 1024 * 1024
FFN_ROWS = 512
FFN_CHUNKS = 11
EVEN_ROWS = 512
PROJ_ROWS = 512
SEL_KEYS = Q_BLOCK
RANK_CHUNK = 8
RET_ROWS = 512


def _cparams(sem):
    return pltpu.CompilerParams(dimension_semantics=sem, vmem_limit_bytes=VMEM_LIMIT)


def _const_spec(shape):
    nd = len(shape)
    return pl.BlockSpec(shape, lambda *_: (0,) * nd, pipeline_mode=pl.Buffered(1))


def _dot(a, b):
    return jnp.dot(a, b, preferred_element_type=F32)


def _dot_nt(a, b):
    return lax.dot_general(a, b, (((1,), (1,)), ((), ())), preferred_element_type=F32)


def _dot_tn(a, b):
    return lax.dot_general(a, b, (((0,), (0,)), ((), ())), preferred_element_type=F32)


def _rms(x, g, eps=1e-6):
    return x * lax.rsqrt(jnp.mean(x * x, axis=-1, keepdims=True) + eps) * g


def _softmax2_cols(s):
    e = jnp.exp2(s - jnp.max(s, axis=0, keepdims=True))
    return e / jnp.sum(e, axis=0, keepdims=True)


def _eye(n):
    return jnp.where(lax.broadcasted_iota(jnp.int32, (n, n), 0) == lax.broadcasted_iota(jnp.int32, (n, n), 1),
                     1.0, 0.0).astype(BF16)


def _ffn_tile(x, gpre_ref, gpost_ref, wg_ref, wu_ref, wd_ref):
    h = _rms(x, gpre_ref[...]).astype(BF16)
    f_total = wg_ref.shape[1]
    fc = f_total // FFN_CHUNKS
    acc = None
    for c in range(FFN_CHUNKS):
        gate = _dot(h, wg_ref[:, c * fc:(c + 1) * fc])
        up = _dot(h, wu_ref[:, c * fc:(c + 1) * fc])
        act = (gate * jax.nn.sigmoid(gate) * up).astype(BF16)
        part = _dot(act, wd_ref[c * fc:(c + 1) * fc, :])
        acc = part if acc is None else acc + part
    return x + _rms(acc, gpost_ref[...])


def _ffn_specs(d, f):
    return [_const_spec((1, d)), _const_spec((1, d)), _const_spec((d, f)), _const_spec((d, f)), _const_spec((f, d))]


def _even_body(x_ref, gpre_ref, win_ref, poolw_ref, pscale_ref, lng_ref, lnb_ref, sguw_ref, sgub_ref,
               wout_ref, gpost_ref, fpre_ref, fpost_ref, wg_ref, wu_ref, wd_ref, o_ref, hist_ref):
    tt = x_ref.shape[1]
    j = pl.program_id(1)

    @pl.when(j == 0)
    def _():
        hist_ref[0:POOL_HIST, :] = jnp.zeros((POOL_HIST, POOL_WIDTH), F32)

    x = x_ref[0]
    h = _rms(x, gpre_ref[...]).astype(BF16)
    z = _dot(h, win_ref[...])
    a = z[:, :POOL_WIDTH]
    u = z[:, POOL_WIDTH:POOL_WIDTH + SGU_WIDTH]
    v = z[:, POOL_WIDTH + SGU_WIDTH:]

    hist_ref[POOL_HIST:POOL_HIST + tt, :] = a
    t_pos = j * tt + lax.broadcasted_iota(jnp.int32, (tt, 1), 0)
    ya = []
    for gi, w in enumerate(POOL_WINDOWS):
        cols = slice(gi * POOL_GDIM, (gi + 1) * POOL_GDIM)
        win_sum = a[:, cols]
        for s in range(1, w):
            win_sum = win_sum + hist_ref[POOL_HIST - s:POOL_HIST - s + tt, cols]
        cnt = jnp.minimum(t_pos + 1, w).astype(F32)
        diff = (win_sum / cnt - a[:, cols]).astype(BF16)
        ya.append(_dot(diff, poolw_ref[gi]))
    ya = jnp.concatenate(ya, axis=1) * pscale_ref[...]
    hist_ref[0:POOL_HIST, :] = hist_ref[tt:tt + POOL_HIST, :]

    ug = jax.nn.gelu(u)
    vg = jax.nn.gelu(v)
    mu = jnp.mean(vg, axis=-1, keepdims=True)
    var = jnp.mean(jnp.square(vg - mu), axis=-1, keepdims=True)
    vn = ((vg - mu) * lax.rsqrt(var + 1e-5) * lng_ref[...] + lnb_ref[...]).astype(BF16)
    r_i = lax.broadcasted_iota(jnp.int32, (SGU_CHUNK, SGU_CHUNK), 0)
    c_i = lax.broadcasted_iota(jnp.int32, (SGU_CHUNK, SGU_CHUNK), 1)
    yb = []
    for g in range(SGU_GROUPS):
        cols = slice(g * SGU_GDIM, (g + 1) * SGU_GDIM)
        wm = jnp.where(r_i >= c_i, sguw_ref[g], 0.0).astype(BF16)
        bcol = sgub_ref[:, g:g + 1]
        parts = []
        for c in range(tt // SGU_CHUNK):
            rows = slice(c * SGU_CHUNK, (c + 1) * SGU_CHUNK)
            parts.append(ug[rows, cols] * (_dot(wm, vn[rows, cols]) + bcol))
        yb.append(jnp.concatenate(parts, axis=0))
    yb = jnp.concatenate(yb, axis=1)

    m = _dot(ya.astype(BF16), wout_ref[0:POOL_WIDTH, :]) + _dot(yb.astype(BF16), wout_ref[POOL_WIDTH:, :])
    o_ref[0] = _ffn_tile(x + _rms(m, gpost_ref[...]), fpre_ref, fpost_ref, wg_ref, wu_ref, wd_ref)


def _even_layer(x, gpre, win, poolw, pscale, lng, lnb, sguw, sgub_t, wout, gpost, ffn):
    b, t, d = x.shape
    tt = EVEN_ROWS
    row = pl.BlockSpec((1, tt, d), lambda bi, j: (bi, j, 0))
    return pl.pallas_call(
        _even_body,
        grid=(b, t // tt),
        in_specs=[row, _const_spec((1, d)), _const_spec(win.shape), _const_spec(poolw.shape),
                  _const_spec(pscale.shape), _const_spec(lng.shape), _const_spec(lnb.shape),
                  _const_spec(sguw.shape), _const_spec(sgub_t.shape), _const_spec(wout.shape),
                  _const_spec((1, d))] + _ffn_specs(d, ffn[2].shape[1]),
        out_specs=row,
        out_shape=jax.ShapeDtypeStruct((b, t, d), F32),
        scratch_shapes=[pltpu.VMEM((POOL_HIST + tt, POOL_WIDTH), F32)],
        compiler_params=_cparams(("arbitrary", "arbitrary")),
        name="even_layer",
    )(x, gpre, win, poolw, pscale, lng, lnb, sguw, sgub_t, wout, gpost, *ffn)


_C_Q = 0
_C_KC = _C_Q + NSA_WIDTH
_C_VC = _C_KC + NSA_KV_WIDTH
_C_KS = _C_VC + NSA_KV_WIDTH
_C_KW = _C_KS + NSA_KV_WIDTH
_C_RQ = _C_KW + NSA_KV_WIDTH
_C_RK = _C_RQ + RET_WIDTH
_C_RV = _C_RK + RET_WIDTH
_C_RG = _C_RV + RET_WIDTH
ODD_COLS = _C_RG + RET_WIDTH
_R_VS = 0
_R_VW = _R_VS + NSA_KV_WIDTH
_R_GT = _R_VW + NSA_KV_WIDTH
ODD_TROWS = _R_GT + NSA_KV_GROUPS * GATE_ROWS
V_AUG = NSA_HEAD_DIM + 16
VT_ROWS = 2 * NSA_KV_GROUPS * V_AUG


def _rope_nsa(z, c, sa, sb):
    half = ROPE_DIM // 2
    return z * c + pltpu.roll(z, LANES - half, 1) * sa + pltpu.roll(z, half, 1) * sb


def _split3(x):
    a = x.astype(BF16)
    r = x - a.astype(F32)
    b = r.astype(BF16)
    return a, b, (r - b.astype(F32)).astype(BF16)


def _proj_body(x_ref, gpre_ref, w_ref, wt_ref, cs_ref, place_ref, base_ref, cr_ref, sr_ref,
               q_ref, kc_ref, vc_ref, ksa_ref, kw_ref, vt_ref, gt_ref, rq_ref, rk_ref, rv_ref, rg_ref):
    tm = x_ref.shape[1]
    j = pl.program_id(1)
    x = x_ref[0]
    h = _rms(x, gpre_ref[...]).astype(BF16)
    tables = base_ref[...]
    for part in _split3(cs_ref[0]):
        tables = tables + _dot(part, place_ref[...])
    nc, nsa, nsb = tables[:, 0:LANES], tables[:, LANES:2 * LANES], tables[:, 2 * LANES:3 * LANES]
    cr, sr = cr_ref[0], sr_ref[0]
    rc = jnp.concatenate([cr, cr], axis=1)
    rs = jnp.concatenate([-sr, sr], axis=1)

    def cols(start, width):
        return _dot(h, w_ref[:, start:start + width])

    zq = cols(_C_Q, NSA_WIDTH)
    q_scale = NSA_HEAD_DIM ** -0.5 * LOG2_E
    for s in range(NSA_WIDTH // LANES):
        sl = slice(s * LANES, (s + 1) * LANES)
        q_ref[0, :, sl] = (_rope_nsa(zq[:, sl], nc, nsa, nsb) * q_scale).astype(BF16)

    zkv = cols(_C_KC, _C_RQ - _C_KC)
    kc_ref[0] = _rope_nsa(zkv[:, _C_KC - _C_KC:_C_VC - _C_KC], nc, nsa, nsb)
    vc_ref[0] = zkv[:, _C_VC - _C_KC:_C_KS - _C_KC]
    ks = _rope_nsa(zkv[:, _C_KS - _C_KC:_C_KW - _C_KC], nc, nsa, nsb)
    kw = _rope_nsa(zkv[:, _C_KW - _C_KC:_C_RQ - _C_KC], nc, nsa, nsb)
    t_pos = j * tm + lax.broadcasted_iota(jnp.int32, (tm, SEL_PAD), 0)
    blk = lax.broadcasted_iota(jnp.int32, (tm, SEL_PAD), 1)
    onehot = jnp.where(t_pos // SEL_BLOCK == blk, 1.0, 0.0).astype(BF16)
    for g in range(NSA_KV_GROUPS):
        sl = slice(g * NSA_HEAD_DIM, (g + 1) * NSA_HEAD_DIM)
        ksa_ref[0, g] = jnp.concatenate([ks[:, sl].astype(BF16), onehot], axis=1)
        kw_ref[0, g] = kw[:, sl].astype(BF16)

    zt = _dot_nt(wt_ref[...], h)
    ones_rows = jnp.where(lax.broadcasted_iota(jnp.int32, (V_AUG - NSA_HEAD_DIM, tm), 0) == 0, 1.0, 0.0).astype(BF16)
    for k in range(2 * NSA_KV_GROUPS):
        vt_ref[0, k * V_AUG:k * V_AUG + NSA_HEAD_DIM] = zt[k * NSA_HEAD_DIM:(k + 1) * NSA_HEAD_DIM].astype(BF16)
        vt_ref[0, k * V_AUG + NSA_HEAD_DIM:(k + 1) * V_AUG] = ones_rows
    gt_ref[0] = jax.nn.sigmoid(zt[_R_GT:])

    k_scale = RET_HEAD_DIM ** -0.5
    zrq = cols(_C_RQ, RET_WIDTH)
    zrk = cols(_C_RK, RET_WIDTH)
    for hh in range(RET_HEADS):
        sl = slice(hh * RET_HEAD_DIM, (hh + 1) * RET_HEAD_DIM)
        zq_h, zk_h = zrq[:, sl], zrk[:, sl]
        rq_ref[0, :, sl] = (zq_h * rc + pltpu.roll(zq_h, RET_HEAD_DIM // 2, 1) * rs).astype(BF16)
        rk_ref[0, :, sl] = ((zk_h * rc + pltpu.roll(zk_h, RET_HEAD_DIM // 2, 1) * rs) * k_scale).astype(BF16)
    rv_ref[0] = cols(_C_RV, RET_WIDTH).astype(BF16)
    rg_ref[0] = cols(_C_RG, RET_WIDTH)


def _odd_proj(x, gpre, w, wt, cs, place, base, cr, sr):
    b, t, d = x.shape
    tm = PROJ_ROWS
    G = NSA_KV_GROUPS

    def row(width):
        return pl.BlockSpec((1, tm, width), lambda bi, j: (bi, j, 0))

    def grp(width):
        return pl.BlockSpec((1, G, tm, width), lambda bi, j: (bi, 0, j, 0))

    def feat(rows):
        return pl.BlockSpec((1, rows, tm), lambda bi, j: (bi, 0, j))

    out_shape = [
        jax.ShapeDtypeStruct((b, t, NSA_WIDTH), BF16),
        jax.ShapeDtypeStruct((b, t, LANES), F32),
        jax.ShapeDtypeStruct((b, t, LANES), F32),
        jax.ShapeDtypeStruct((b, G, t, LANES), BF16),
        jax.ShapeDtypeStruct((b, G, t, NSA_HEAD_DIM), BF16),
        jax.ShapeDtypeStruct((b, VT_ROWS, t), BF16),
        jax.ShapeDtypeStruct((b, G * GATE_ROWS, t), F32),
        jax.ShapeDtypeStruct((b, t, RET_WIDTH), BF16),
        jax.ShapeDtypeStruct((b, t, RET_WIDTH), BF16),
        jax.ShapeDtypeStruct((b, t, RET_WIDTH), BF16),
        jax.ShapeDtypeStruct((b, t, RET_WIDTH), F32),
    ]
    out_specs = [row(NSA_WIDTH), row(LANES), row(LANES), grp(LANES), grp(NSA_HEAD_DIM), feat(VT_ROWS),
                 feat(G * GATE_ROWS), row(RET_WIDTH), row(RET_WIDTH), row(RET_WIDTH), row(RET_WIDTH)]
    return pl.pallas_call(
        _proj_body,
        grid=(b, t // tm),
        in_specs=[row(d), _const_spec((1, d)), _const_spec(w.shape), _const_spec(wt.shape), row(cs.shape[2]),
                  _const_spec(place.shape), _const_spec(base.shape), row(cr.shape[2]), row(sr.shape[2])],
        out_specs=out_specs,
        out_shape=out_shape,
        compiler_params=_cparams(("arbitrary", "arbitrary")),
        name="odd_proj",
    )(x, gpre, w, wt, cs, place, base, cr, sr)


def _compress_body(k_ref, v_ref, kpos_ref, vpos_ref, kw1_ref, vw1_ref, kw2_ref, vw2t_ref, kcm_ref, vct_ref):
    nc = k_ref.shape[1] // CMP_STRIDE
    half = CMP_BLOCK // CMP_STRIDE

    def hidden(x_ref, pos_ref, w1_ref):
        parts = [None] * half
        for off in range(CMP_STRIDE):
            tok = x_ref[0, pl.ds(off, nc, stride=CMP_STRIDE), :]
            for h in range(half):
                l = h * CMP_STRIDE + off
                term = _dot((tok + pos_ref[l:l + 1, :]).astype(BF16), w1_ref[l])
                parts[h] = term if parts[h] is None else parts[h] + term
        pre = parts[0]
        for h in range(1, half):
            pre = pre + pltpu.roll(parts[h], nc - h, 0)
        return jax.nn.gelu(pre).astype(BF16)

    hk = hidden(k_ref, kpos_ref, kw1_ref)
    hv = hidden(v_ref, vpos_ref, vw1_ref)
    for g in range(NSA_KV_GROUPS):
        sl = slice(g * CMP_HIDDEN, (g + 1) * CMP_HIDDEN)
        kcm_ref[0, g] = _dot(hk[:, sl], kw2_ref[...]).astype(BF16)
        vct_ref[0, g] = _dot_nt(vw2t_ref[...], hv[:, sl]).astype(BF16)


def _compress(k, v, kpos, vpos, kw1, vw1, kw2, vw2t):
    b, t, width = k.shape
    nc = t // CMP_STRIDE
    G = NSA_KV_GROUPS
    row = pl.BlockSpec((1, t, width), lambda bi: (bi, 0, 0))
    return pl.pallas_call(
        _compress_body,
        grid=(b,),
        in_specs=[row, row, _const_spec(kpos.shape), _const_spec(vpos.shape), _const_spec(kw1.shape),
                  _const_spec(vw1.shape), _const_spec(kw2.shape), _const_spec(vw2t.shape)],
        out_specs=[pl.BlockSpec((1, G, nc, NSA_HEAD_DIM), lambda bi: (bi, 0, 0, 0)),
                   pl.BlockSpec((1, G, NSA_HEAD_DIM, nc), lambda bi: (bi, 0, 0, 0))],
        out_shape=[jax.ShapeDtypeStruct((b, G, nc, NSA_HEAD_DIM), BF16),
                   jax.ShapeDtypeStruct((b, G, NSA_HEAD_DIM, nc), BF16)],
        compiler_params=_cparams(("arbitrary",)),
        name="compress",
    )(k, v, kpos, vpos, kw1, vw1, kw2, vw2t)


def _nsa_body(q_ref, kcm_ref, vct_ref, c2st_ref, tri_ref, ksa_ref, kw_ref, vt_ref, gt_ref, o_ref,
              score_ref, rank_ref, wbias_ref, sc_ref, sw_ref, pw_ref, sa_ref, sb_ref, pa_ref, pb_ref, lhs_ref, m_ref, acc_ref, *, top_k):
    G, M, Q, dh = NSA_KV_GROUPS, NSA_HPG, Q_BLOCK, NSA_HEAD_DIM
    cols = M * Q
    nc = kcm_ref.shape[2]
    i = pl.program_id(1)
    t0 = i * Q
    tq = t0 + (lax.broadcasted_iota(jnp.int32, (1, cols), 1) & (Q - 1))
    groups = range(G)

    q_heads, q_rows = [], []
    for g in groups:
        qb = q_ref[0, :, g * NSA_GROUP_WIDTH:(g + 1) * NSA_GROUP_WIDTH]
        q_heads.append([qb[:, m * dh:(m + 1) * dh] for m in range(M)])
        q_rows.append(jnp.concatenate(q_heads[g], axis=0))

    n_idx = lax.broadcasted_iota(jnp.int32, (nc, 1), 0)
    cmp_ok = (n_idx * CMP_STRIDE + (CMP_BLOCK - 1) <= tq) & (n_idx < nc - 1)
    j_idx = lax.broadcasted_iota(jnp.int32, (SEL_PAD, Q), 0)
    cur = (t0 + lax.broadcasted_iota(jnp.int32, (SEL_PAD, Q), 1)) // SEL_BLOCK
    forced = (j_idx == 0) | (j_idx == cur) | (j_idx == cur - 1)
    valid_tok = (tq >= CMP_BLOCK - 1).astype(F32)
    o_cmp = []
    for g in groups:
        sc = sc_ref.at[g]
        sc[...] = jnp.where(cmp_ok, _dot_nt(kcm_ref[0, g], q_rows[g]), NEG_INF)
        sc[...] = jnp.exp2(sc[...] - jnp.max(sc[...], axis=0, keepdims=True))
        norm = valid_tok / jnp.sum(sc[...], axis=0, keepdims=True)
        o_cmp.append(_dot(vct_ref[0, g], sc[...].astype(BF16)) * norm)
        p_grp = sc[:, 0:Q] * norm[:, 0:Q]
        for m in range(1, M):
            p_grp = p_grp + sc[:, m * Q:(m + 1) * Q] * norm[:, m * Q:(m + 1) * Q]
        p_hi = p_grp.astype(BF16)
        p_lo = (p_grp - p_hi.astype(F32)).astype(BF16)
        imp = _dot(c2st_ref[...], p_hi) + _dot(c2st_ref[...], p_lo)
        score_ref[g] = jnp.where(j_idx <= cur, jnp.where(forced, FORCE_SCORE, imp), -1.0)

    def v_aug(branch, g, k0, n):
        r0 = (branch * G + g) * V_AUG
        return vt_ref[0, r0:r0 + V_AUG, pl.ds(k0, n)]

    span = WINDOW + Q
    w0 = pl.multiple_of(jnp.maximum(t0 - WINDOW, 0), Q)
    n_wb = span // Q
    diag_b = jnp.minimum(i, n_wb - 1)
    tri_causal, tri_leaving = tri_ref[0], tri_ref[1]
    for wb in range(n_wb):
        blk_bias = jnp.where(wb == diag_b, tri_causal, jnp.where(wb > diag_b, NEG_INF, 0.0))
        if wb == 0:
            blk_bias = jnp.where(i >= n_wb - 1, tri_leaving, blk_bias)
        wbias_ref[wb * Q:(wb + 1) * Q, :] = blk_bias
    for g in groups:
        sw_ref[g] = _dot_nt(kw_ref[0, g, pl.ds(w0, span), :], q_rows[g]) + wbias_ref[...]

    n_causal = (t0 + Q) // SEL_BLOCK
    rank_ref[...] = jnp.zeros(rank_ref.shape, jnp.int32)
    row_in_tile = lax.broadcasted_iota(jnp.int32, (SUBLANES, LANES), 0)
    for c0 in range(0, SEL_PAD, RANK_CHUNK):
        @pl.when(c0 < n_causal)
        def _(c0=c0):
            for g in groups:
                for l0 in range(0, Q, LANES):
                    ln = slice(l0, l0 + LANES)
                    others = [score_ref[g, k:k + 1, ln] for k in range(c0, c0 + RANK_CHUNK)]
                    for r0 in range(0, SEL_PAD, SUBLANES):
                        mine = score_ref[g, r0:r0 + SUBLANES, ln]
                        count = rank_ref[g, r0:r0 + SUBLANES, ln]
                        for k, other in zip(range(c0, c0 + RANK_CHUNK), others):
                            if r0 > k:
                                beats = other >= mine
                            elif r0 + SUBLANES <= k:
                                beats = other > mine
                            else:
                                beats = (other > mine) | ((other == mine) & (k < r0 + row_in_tile))
                            count = count + beats.astype(jnp.int32)
                        rank_ref[g, r0:r0 + SUBLANES, ln] = count

    eye_q = _eye(Q)
    for g in groups:
        chosen = jnp.where((rank_ref[g] < top_k) & (score_ref[g] >= 0.0), 1.0, 0.0).astype(BF16)
        chosen_t = _dot_nt(eye_q, chosen)
        bias = jnp.where(chosen_t > 0.5, 0.0, NEG_INF).astype(BF16)
        for m in range(M):
            lhs_ref[g, m * Q:(m + 1) * Q, :] = jnp.concatenate([q_heads[g][m], bias], axis=1)
        m_ref[g] = jnp.full((1, cols), NEG_INF, F32)
        acc_ref[g] = jnp.zeros((V_AUG, cols), F32)

    def put_scores(dst, g, kt):
        k0 = pl.multiple_of(kt * SEL_KEYS, SEL_KEYS)
        dst[g] = _dot_nt(ksa_ref[0, g, pl.ds(k0, SEL_KEYS), :], lhs_ref[g])

    def sel_update(src, p_dst, g, kt, bias=None):
        k0 = pl.multiple_of(kt * SEL_KEYS, SEL_KEYS)

        def scores():
            return src[g] if bias is None else src[g] + bias

        m_i = m_ref[g]
        m_new = jnp.maximum(m_i, jnp.max(scores(), axis=0, keepdims=True))
        p_dst[g] = jnp.exp2(scores() - m_new).astype(BF16)
        acc_ref[g] = jnp.exp2(m_i - m_new) * acc_ref[g] + _dot(v_aug(0, g, k0, SEL_KEYS), p_dst[g])
        m_ref[g] = m_new

    n_full = t0 // SEL_KEYS
    odd = n_full & 1
    for g in groups:
        put_scores(sa_ref, g, 0)

    o_win = []
    for g in groups:
        sw = sw_ref.at[g]
        pw_ref[g] = jnp.exp2(sw[...] - jnp.max(sw[...], axis=0, keepdims=True)).astype(BF16)
        acc_win = _dot(v_aug(1, g, w0, span), pw_ref[g])
        o_win.append(acc_win[0:dh] / acc_win[dh:dh + 1])

    @pl.when(odd == 1)
    def _():
        for g in groups:
            sel_update(sa_ref, pa_ref, g, 0)
            put_scores(sa_ref, g, 1)

    def sel_pair(pair, _):
        ta = odd + 2 * pair
        for g in groups:
            put_scores(sb_ref, g, ta + 1)
        for g in groups:
            sel_update(sa_ref, pa_ref, g, ta)
        for g in groups:
            put_scores(sa_ref, g, ta + 2)
        for g in groups:
            sel_update(sb_ref, pb_ref, g, ta + 1)
        return 0

    lax.fori_loop(0, n_full // 2, sel_pair, 0)
    o_sel = []
    for g in groups:
        sel_update(sa_ref, pa_ref, g, n_full, bias=tri_causal)
        o_sel.append(acc_ref[g, 0:dh, :] / acc_ref[g, dh:dh + 1, :])

    pieces = []
    for g in groups:
        gt = gt_ref[0, g * GATE_ROWS:(g + 1) * GATE_ROWS, :]
        for m in range(M):
            c = slice(m * Q, (m + 1) * Q)
            r = NSA_BRANCHES * m
            mixed = (o_cmp[g][:, c] * gt[r:r + 1] + o_sel[g][:, c] * gt[r + 1:r + 2]
                     + o_win[g][:, c] * gt[r + 2:r + 3])
            pieces.append(_dot_nt(eye_q, mixed.astype(BF16)))
    o_ref[0] = jnp.concatenate(pieces, axis=1).astype(BF16)


def _nsa(q, kcm, vct, c2st, tri, ksa, kw, vt, gt):
    b, t, _ = q.shape
    G = NSA_KV_GROUPS
    nc = kcm.shape[2]
    n_sel = t // SEL_BLOCK
    assert n_sel <= SEL_PAD and t >= WINDOW + Q_BLOCK and t % SEL_KEYS == 0
    top_k = min(SEL_TOPK, n_sel)
    cols, span = NSA_HPG * Q_BLOCK, WINDOW + Q_BLOCK

    def per_batch(shape):
        nd = len(shape)
        return pl.BlockSpec((1,) + shape, lambda bi, i: (bi,) + (0,) * nd)

    token_rows = pl.BlockSpec((1, Q_BLOCK, NSA_WIDTH), lambda bi, i: (bi, i, 0))
    return pl.pallas_call(
        functools.partial(_nsa_body, top_k=top_k),
        grid=(b, t // Q_BLOCK),
        in_specs=[token_rows, per_batch((G, nc, NSA_HEAD_DIM)), per_batch((G, NSA_HEAD_DIM, nc)),
                  _const_spec(c2st.shape), _const_spec(tri.shape), per_batch((G, t, LANES)), per_batch((G, t, NSA_HEAD_DIM)),
                  per_batch((vt.shape[1], t)),
                  pl.BlockSpec((1, G * GATE_ROWS, Q_BLOCK), lambda bi, i: (bi, 0, i))],
        out_specs=token_rows,
        out_shape=jax.ShapeDtypeStruct((b, t, NSA_WIDTH), BF16),
        scratch_shapes=[pltpu.VMEM((G, SEL_PAD, Q_BLOCK), F32),
                        pltpu.VMEM((G, SEL_PAD, Q_BLOCK), jnp.int32),
                        pltpu.VMEM((span, cols), F32),
                        pltpu.VMEM((G, nc, cols), F32),
                        pltpu.VMEM((G, span, cols), F32),
                        pltpu.VMEM((G, span, cols), BF16),
                        pltpu.VMEM((G, SEL_KEYS, cols), F32),
                        pltpu.VMEM((G, SEL_KEYS, cols), F32),
                        pltpu.VMEM((G, SEL_KEYS, cols), BF16),
                        pltpu.VMEM((G, SEL_KEYS, cols), BF16),
                        pltpu.VMEM((G, cols, 2 * NSA_HEAD_DIM), BF16),
                        pltpu.VMEM((G, 1, cols), F32),
                        pltpu.VMEM((G, V_AUG, cols), F32)],
        compiler_params=_cparams(("arbitrary", "arbitrary")),
        name="nsa",
    )(q, kcm, vct, c2st, tri, ksa, kw, vt, gt)


def _ret_body(q_ref, k_ref, v_ref, g_ref, decay_ref, zeta_ref, xi_ref, gchunk_ref, gn_ref, o_ref, state_ref):
    C, d = RET_CHUNK, RET_HEAD_DIM

    @pl.when(pl.program_id(1) == 0)
    def _():
        state_ref[...] = jnp.zeros_like(state_ref)

    n_chunks = q_ref.shape[1] // C
    for h in range(RET_HEADS):
        hc = slice(h * d, (h + 1) * d)
        state = state_ref[h]
        before = []
        for c in range(n_chunks):
            rows = slice(c * C, (c + 1) * C)
            kz = (k_ref[0, rows, hc].astype(F32) * zeta_ref[h]).astype(BF16)
            before.append(state.astype(BF16))
            state = state * gchunk_ref[h] + _dot_tn(kz, v_ref[0, rows, hc])
        state_ref[h] = state
        for c in range(n_chunks):
            rows = slice(c * C, (c + 1) * C)
            q, k, v = q_ref[0, rows, hc], k_ref[0, rows, hc], v_ref[0, rows, hc]
            s = (_dot_nt(q, k) * decay_ref[h]).astype(BF16)
            o = _dot(s, v) + _dot((q.astype(F32) * xi_ref[h]).astype(BF16), before[c])
            mu = jnp.mean(o, axis=-1, keepdims=True)
            var = jnp.mean(jnp.square(o - mu), axis=-1, keepdims=True)
            o = (o - mu) * lax.rsqrt(var + 1e-5) * gn_ref[h]
            gate = g_ref[0, rows, hc]
            o_ref[0, rows, hc] = (gate * jax.nn.sigmoid(gate) * o).astype(BF16)


def _retention(rq, rk, rv, rg, decay, zeta, xi, gchunk, gn):
    b, t, width = rq.shape
    tok = pl.BlockSpec((1, RET_ROWS, width), lambda bi, c: (bi, c, 0))
    return pl.pallas_call(
        _ret_body,
        grid=(b, t // RET_ROWS),
        in_specs=[tok, tok, tok, tok, _const_spec(decay.shape), _const_spec(zeta.shape), _const_spec(xi.shape),
                  _const_spec(gchunk.shape), _const_spec(gn.shape)],
        out_specs=tok,
        out_shape=jax.ShapeDtypeStruct((b, t, width), BF16),
        scratch_shapes=[pltpu.VMEM((RET_HEADS, RET_HEAD_DIM, RET_HEAD_DIM), F32)],
        compiler_params=_cparams(("arbitrary", "arbitrary")),
        name="retention",
    )(rq, rk, rv, rg, decay, zeta, xi, gchunk, gn)


def _oproj_body(x_ref, yc_ref, yd_ref, w_ref, gpost_ref, fpre_ref, fpost_ref, wg_ref, wu_ref, wd_ref, o_ref):
    m = _dot(yc_ref[...], w_ref[0:NSA_WIDTH, :]) + _dot(yd_ref[...], w_ref[NSA_WIDTH:, :])
    o_ref[...] = _ffn_tile(x_ref[...] + _rms(m, gpost_ref[...]), fpre_ref, fpost_ref, wg_ref, wu_ref, wd_ref)


def _odd_out(x2, yc2, yd2, w, gpost, ffn):
    n, d = x2.shape
    tm = FFN_ROWS

    def row(width):
        return pl.BlockSpec((tm, width), lambda i: (i, 0))

    return pl.pallas_call(
        _oproj_body,
        grid=(n // tm,),
        in_specs=[row(d), row(NSA_WIDTH), row(RET_WIDTH), _const_spec(w.shape), _const_spec((1, d))]
        + _ffn_specs(d, ffn[2].shape[1]),
        out_specs=row(d),
        out_shape=jax.ShapeDtypeStruct((n, d), F32),
        compiler_params=_cparams(("arbitrary",)),
        name="odd_out_ffn",
    )(x2, yc2, yd2, w, gpost, *ffn)


def _odd_in_weight(w_in):
    sizes = [NSA_WIDTH] + [NSA_KV_WIDTH] * 6 + [NSA_BRANCHES * NSA_HEADS] + [RET_WIDTH] * 4
    offs = np.concatenate([[0], np.cumsum(sizes)])
    w_in = w_in.astype(BF16)
    q, kc, vc, ks, vs, kw, vw, gt, rq, rk, rv, rg = [w_in[:, offs[n]:offs[n + 1]] for n in range(len(sizes))]
    w = jnp.concatenate([q, kc, vc, ks, kw, rq, rk, rv, rg], axis=1)
    per_group = NSA_BRANCHES * NSA_HPG
    gates = [jnp.pad(gt[:, g * per_group:(g + 1) * per_group], ((0, 0), (0, GATE_ROWS - per_group)))
             for g in range(NSA_KV_GROUPS)]
    wt = jnp.concatenate([vs, vw] + gates, axis=1).T
    return w, wt


def _rope_tables(positions):
    pos = positions.astype(F32)[..., None]
    half = ROPE_DIM // 2
    inv = 1.0 / (ROPE_THETA ** (jnp.arange(0, ROPE_DIM, 2, dtype=F32) / ROPE_DIM))
    ang = pos * inv
    cs = jnp.concatenate([jnp.cos(ang), jnp.sin(ang)], axis=-1)
    place = np.zeros((ROPE_DIM, 3 * LANES), np.float32)
    base = np.zeros((1, 3 * LANES), np.float32)
    for head0 in range(0, LANES, NSA_HEAD_DIM):
        base[0, head0 + ROPE_DIM:head0 + NSA_HEAD_DIM] = 1.0
        for f in range(half):
            place[f, head0 + f] = 1.0
            place[f, head0 + half + f] = 1.0
            place[half + f, LANES + head0 + f] = -1.0
            place[half + f, 2 * LANES + head0 + half + f] = 1.0
    inv_r = 1.0 / (RET_THETA ** (jnp.arange(0, RET_HEAD_DIM, 2, dtype=F32) / RET_HEAD_DIM))
    ang_r = pos * inv_r
    return cs, jnp.asarray(place, dtype=BF16), jnp.asarray(base), jnp.cos(ang_r), jnp.sin(ang_r)


def _retention_tables():
    H, C, d = RET_HEADS, RET_CHUNK, RET_HEAD_DIM
    log_gamma = np.log1p(-np.exp2(-5.0 - np.arange(H, dtype=np.float64)))
    idx = np.arange(C, dtype=np.float64)
    rel = idx[:, None] - idx[None, :]
    decay = np.where(rel >= 0, np.exp(np.maximum(rel, 0.0)[None] * log_gamma[:, None, None]), 0.0)
    zeta = np.exp((C - 1 - idx)[None, :] * log_gamma[:, None])
    xi = np.exp((idx + 1.0)[None, :] * log_gamma[:, None])
    gchunk = np.exp(C * log_gamma)
    zeta_b = np.broadcast_to(zeta[:, :, None], (H, C, d))
    xi_b = np.broadcast_to(xi[:, :, None], (H, C, d))
    gchunk_b = np.broadcast_to(gchunk[:, None, None], (H, 1, d))
    return tuple(jnp.asarray(a, dtype=F32) for a in (decay, zeta_b, xi_b, gchunk_b))


def _cmp_to_sel_t(t_len):
    nc = t_len // CMP_STRIDE
    n_cmp = (t_len - CMP_BLOCK) // CMP_STRIDE + 1
    n_sel = t_len // SEL_BLOCK
    c_start = np.arange(nc) * CMP_STRIDE
    s_start = np.arange(SEL_PAD) * SEL_BLOCK
    hit = ((c_start[None, :] < s_start[:, None] + SEL_BLOCK) & (c_start[None, :] + CMP_BLOCK > s_start[:, None])
           & (np.arange(nc)[None, :] < n_cmp) & (np.arange(SEL_PAD)[:, None] < n_sel))
    return jnp.asarray(hit.astype(np.float32), dtype=BF16)


def _triangle_biases():
    r = np.arange(Q_BLOCK)[:, None]
    tok = np.arange(NSA_HPG * Q_BLOCK)[None, :] % Q_BLOCK
    causal = np.where(r <= tok, 0.0, NEG_INF)
    leaving = np.where(r > tok, 0.0, NEG_INF)
    return jnp.asarray(np.stack([causal, leaving]), dtype=F32)


def _compress_weights(pos, w1, w2):
    G, dh = NSA_KV_GROUPS, NSA_HEAD_DIM
    pos_rows = jnp.tile(pos, (1, G))
    w1r = w1.astype(BF16).reshape(CMP_BLOCK, dh, CMP_HIDDEN)
    w1_bd = jnp.einsum('ab,lij->laibj', jnp.eye(G, dtype=BF16), w1r).reshape(CMP_BLOCK, G * dh, G * CMP_HIDDEN)
    return pos_rows, w1_bd, w2.astype(BF16)


def _odd_layer(x, positions_tables, ret_tables, c2st, tri, gpre, w_in, cmp_k_pos, cmp_k_w1, cmp_k_w2,
               cmp_v_pos, cmp_v_w1, cmp_v_w2, gn_g, w_out, gpost, ffn):
    b, t, d = x.shape
    w, wt = _odd_in_weight(w_in)
    (q, kc, vc, ksa, kw, vt, gt, rq, rk, rv, rg) = _odd_proj(x, gpre, w, wt, *positions_tables)
    kpos, kw1, kw2 = _compress_weights(cmp_k_pos, cmp_k_w1, cmp_k_w2)
    vpos, vw1, vw2 = _compress_weights(cmp_v_pos, cmp_v_w1, cmp_v_w2)
    kcm, vct = _compress(kc, vc, kpos, vpos, kw1, vw1, kw2, vw2.T)
    yc = _nsa(q, kcm, vct, c2st, tri, ksa, kw, vt, gt)
    yd = _retention(rq, rk, rv, rg, *ret_tables, gn_g.reshape(RET_HEADS, 1, RET_HEAD_DIM))
    out = _odd_out(x.reshape(b * t, d), yc.reshape(b * t, NSA_WIDTH), yd.reshape(b * t, RET_WIDTH),
                   w_out.astype(BF16), gpost, ffn)
    return out.reshape(b, t, d)


def kernel(x, positions, ln_mix_pre, ln_mix_post, ln_ffn_pre, ln_ffn_post, ffn_w_gate, ffn_w_up, ffn_w_down,
           ev_w_in, ev_pool_w, ev_pool_scale, ev_sgu_ln_g, ev_sgu_ln_b, ev_sgu_w, ev_sgu_b, ev_w_out,
           od_w_in, od_cmp_k_pos, od_cmp_k_w1, od_cmp_k_w2, od_cmp_v_pos, od_cmp_v_w1, od_cmp_v_w2,
           od_ret_gn_g, od_w_out):
    b, t, d = x.shape
    depth = ln_mix_pre.shape[0]
    rope = _rope_tables(positions)
    ret_tables = _retention_tables()
    c2st = _cmp_to_sel_t(t)
    tri = _triangle_biases()
    for layer in range(depth):
        gpre = ln_mix_pre[layer].reshape(1, d)
        gpost = ln_mix_post[layer].reshape(1, d)
        ffn = (ln_ffn_pre[layer].reshape(1, d), ln_ffn_post[layer].reshape(1, d), ffn_w_gate[layer].astype(BF16),
               ffn_w_up[layer].astype(BF16), ffn_w_down[layer].astype(BF16))
        if layer % 2 == 0:
            e = layer // 2
            x = _even_layer(x, gpre, ev_w_in[e].astype(BF16), ev_pool_w[e].astype(BF16),
                            ev_pool_scale[e].reshape(1, POOL_WIDTH), ev_sgu_ln_g[e].reshape(1, SGU_WIDTH),
                            ev_sgu_ln_b[e].reshape(1, SGU_WIDTH), ev_sgu_w[e], ev_sgu_b[e].T,
                            ev_w_out[e].astype(BF16), gpost, ffn)
        else:
            o = layer // 2
            x = _odd_layer(x, rope, ret_tables, c2st, tri, gpre, od_w_in[o], od_cmp_k_pos[o], od_cmp_k_w1[o],
                           od_cmp_k_w2[o], od_cmp_v_pos[o], od_cmp_v_w1[o], od_cmp_v_w2[o], od_ret_gn_g[o],
                           od_w_out[o], gpost, ffn)
    return x
```

```python
import functools

import numpy as np
import jax
import jax.numpy as jnp
from jax import lax
from jax.experimental import pallas as pl
from jax.experimental.pallas import tpu as pltpu

F32 = jnp.float32
BF16 = jnp.bfloat16

D_MODEL = 1024
POOL_WINDOWS = (2, 4, 8, 16)
POOL_GROUPS = 4
POOL_WIDTH = D_MODEL // 2
POOL_GDIM = POOL_WIDTH // POOL_GROUPS
POOL_HIST = 16
SGU_GROUPS = 4
SGU_WIDTH = D_MODEL // 2
SGU_GDIM = SGU_WIDTH // SGU_GROUPS
SGU_CHUNK = 128
EVEN_IN = POOL_WIDTH + 2 * SGU_WIDTH
NSA_HEADS = 8
NSA_KV_GROUPS = 2
NSA_HPG = NSA_HEADS // NSA_KV_GROUPS
NSA_HEAD_DIM = 64
NSA_WIDTH = NSA_HEADS * NSA_HEAD_DIM
NSA_GROUP_WIDTH = NSA_HPG * NSA_HEAD_DIM
NSA_KV_WIDTH = NSA_KV_GROUPS * NSA_HEAD_DIM
NSA_BRANCHES = 3
GATE_ROWS = 16
CMP_BLOCK = 32
CMP_STRIDE = 16
CMP_HIDDEN = 128
SEL_BLOCK = 64
SEL_TOPK = 16
SEL_PAD = 64
WINDOW = 512
Q_BLOCK = 256
FORCE_SCORE = 1.0e4
ROPE_THETA = 500000.0
ROPE_DIM = NSA_HEAD_DIM // 4
RET_HEADS = 4
RET_HEAD_DIM = 128
RET_WIDTH = RET_HEADS * RET_HEAD_DIM
RET_CHUNK = 128
RET_THETA = 10000.0
ODD_MIX = NSA_WIDTH + RET_WIDTH
NEG_INF = -1.0e30
LOG2_E = 1.4426950408889634
LANES = 128
SUBLANES = 8

VMEM_LIMIT = 56 * 1024 * 1024
FFN_ROWS = 512
FFN_CHUNKS = 11
EVEN_ROWS = 512
PROJ_ROWS = 512
SEL_KEYS = Q_BLOCK
RANK_CHUNK = 8
RET_ROWS = 512


def _cparams(sem):
    return pltpu.CompilerParams(dimension_semantics=sem, vmem_limit_bytes=VMEM_LIMIT)


def _const_spec(shape):
    nd = len(shape)
    return pl.BlockSpec(shape, lambda *_: (0,) * nd, pipeline_mode=pl.Buffered(1))


def _dot(a, b):
    return jnp.dot(a, b, preferred_element_type=F32)


def _dot_nt(a, b):
    return lax.dot_general(a, b, (((1,), (1,)), ((), ())), preferred_element_type=F32)


def _dot_tn(a, b):
    return lax.dot_general(a, b, (((0,), (0,)), ((), ())), preferred_element_type=F32)


def _rms(x, g, eps=1e-6):
    return x * lax.rsqrt(jnp.mean(x * x, axis=-1, keepdims=True) + eps) * g


def _softmax2_cols(s):
    e = jnp.exp2(s - jnp.max(s, axis=0, keepdims=True))
    return e / jnp.sum(e, axis=0, keepdims=True)


def _eye(n):
    return jnp.where(lax.broadcasted_iota(jnp.int32, (n, n), 0) == lax.broadcasted_iota(jnp.int32, (n, n), 1),
                     1.0, 0.0).astype(BF16)


def _ffn_tile(x, gpre_ref, gpost_ref, wg_ref, wu_ref, wd_ref):
    h = _rms(x, gpre_ref[...]).astype(BF16)
    f_total = wg_ref.shape[1]
    fc = f_total // FFN_CHUNKS
    acc = None
    for c in range(FFN_CHUNKS):
        gate = _dot(h, wg_ref[:, c * fc:(c + 1) * fc])
        up = _dot(h, wu_ref[:, c * fc:(c + 1) * fc])
        act = (gate * jax.nn.sigmoid(gate) * up).astype(BF16)
        part = _dot(act, wd_ref[c * fc:(c + 1) * fc, :])
        acc = part if acc is None else acc + part
    return x + _rms(acc, gpost_ref[...])


def _ffn_specs(d, f):
    return [_const_spec((1, d)), _const_spec((1, d)), _const_spec((d, f)), _const_spec((d, f)), _const_spec((f, d))]


def _even_body(x_ref, gpre_ref, win_ref, poolw_ref, pscale_ref, lng_ref, lnb_ref, sguw_ref, sgub_ref,
               wout_ref, gpost_ref, fpre_ref, fpost_ref, wg_ref, wu_ref, wd_ref, o_ref, hist_ref):
    tt = x_ref.shape[1]
    j = pl.program_id(1)

    @pl.when(j == 0)
    def _():
        hist_ref[0:POOL_HIST, :] = jnp.zeros((POOL_HIST, POOL_WIDTH), F32)

    x = x_ref[0]
    h = _rms(x, gpre_ref[...]).astype(BF16)
    z = _dot(h, win_ref[...])
    a = z[:, :POOL_WIDTH]
    u = z[:, POOL_WIDTH:POOL_WIDTH + SGU_WIDTH]
    v = z[:, POOL_WIDTH + SGU_WIDTH:]

    hist_ref[POOL_HIST:POOL_HIST + tt, :] = a
    t_pos = j * tt + lax.broadcasted_iota(jnp.int32, (tt, 1), 0)
    ya = []
    for gi, w in enumerate(POOL_WINDOWS):
        cols = slice(gi * POOL_GDIM, (gi + 1) * POOL_GDIM)
        win_sum = a[:, cols]
        for s in range(1, w):
            win_sum = win_sum + hist_ref[POOL_HIST - s:POOL_HIST - s + tt, cols]
        cnt = jnp.minimum(t_pos + 1, w).astype(F32)
        diff = (win_sum / cnt - a[:, cols]).astype(BF16)
        ya.append(_dot(diff, poolw_ref[gi]))
    ya = jnp.concatenate(ya, axis=1) * pscale_ref[...]
    hist_ref[0:POOL_HIST, :] = hist_ref[tt:tt + POOL_HIST, :]

    ug = jax.nn.gelu(u)
    vg = jax.nn.gelu(v)
    mu = jnp.mean(vg, axis=-1, keepdims=True)
    var = jnp.mean(jnp.square(vg - mu), axis=-1, keepdims=True)
    vn = ((vg - mu) * lax.rsqrt(var + 1e-5) * lng_ref[...] + lnb_ref[...]).astype(BF16)
    r_i = lax.broadcasted_iota(jnp.int32, (SGU_CHUNK, SGU_CHUNK), 0)
    c_i = lax.broadcasted_iota(jnp.int32, (SGU_CHUNK, SGU_CHUNK), 1)
    yb = []
    for g in range(SGU_GROUPS):
        cols = slice(g * SGU_GDIM, (g + 1) * SGU_GDIM)
        wm = jnp.where(r_i >= c_i, sguw_ref[g], 0.0).astype(BF16)
        bcol = sgub_ref[:, g:g + 1]
        parts = []
        for c in range(tt // SGU_CHUNK):
            rows = slice(c * SGU_CHUNK, (c + 1) * SGU_CHUNK)
            parts.append(ug[rows, cols] * (_dot(wm, vn[rows, cols]) + bcol))
        yb.append(jnp.concatenate(parts, axis=0))
    yb = jnp.concatenate(yb, axis=1)

    m = _dot(ya.astype(BF16), wout_ref[0:POOL_WIDTH, :]) + _dot(yb.astype(BF16), wout_ref[POOL_WIDTH:, :])
    o_ref[0] = _ffn_tile(x + _rms(m, gpost_ref[...]), fpre_ref, fpost_ref, wg_ref, wu_ref, wd_ref)


def _even_layer(x, gpre, win, poolw, pscale, lng, lnb, sguw, sgub_t, wout, gpost, ffn):
    b, t, d = x.shape
    tt = EVEN_ROWS
    row = pl.BlockSpec((1, tt, d), lambda bi, j: (bi, j, 0))
    return pl.pallas_call(
        _even_body,
        grid=(b, t // tt),
        in_specs=[row, _const_spec((1, d)), _const_spec(win.shape), _const_spec(poolw.shape),
                  _const_spec(pscale.shape), _const_spec(lng.shape), _const_spec(lnb.shape),
                  _const_spec(sguw.shape), _const_spec(sgub_t.shape), _const_spec(wout.shape),
                  _const_spec((1, d))] + _ffn_specs(d, ffn[2].shape[1]),
        out_specs=row,
        out_shape=jax.ShapeDtypeStruct((b, t, d), F32),
        scratch_shapes=[pltpu.VMEM((POOL_HIST + tt, POOL_WIDTH), F32)],
        compiler_params=_cparams(("arbitrary", "arbitrary")),
        name="even_layer",
    )(x, gpre, win, poolw, pscale, lng, lnb, sguw, sgub_t, wout, gpost, *ffn)


_C_Q = 0
_C_KC = _C_Q + NSA_WIDTH
_C_VC = _C_KC + NSA_KV_WIDTH
_C_KS = _C_VC + NSA_KV_WIDTH
_C_KW = _C_KS + NSA_KV_WIDTH
_C_RQ = _C_KW + NSA_KV_WIDTH
_C_RK = _C_RQ + RET_WIDTH
_C_RV = _C_RK + RET_WIDTH
_C_RG = _C_RV + RET_WIDTH
ODD_COLS = _C_RG + RET_WIDTH
_R_VS = 0
_R_VW = _R_VS + NSA_KV_WIDTH
_R_GT = _R_VW + NSA_KV_WIDTH
ODD_TROWS = _R_GT + NSA_KV_GROUPS * GATE_ROWS
V_AUG = NSA_HEAD_DIM + 16
VT_ROWS = 2 * NSA_KV_GROUPS * V_AUG


def _rope_nsa(z, c, sa, sb):
    half = ROPE_DIM // 2
    return z * c + pltpu.roll(z, LANES - half, 1) * sa + pltpu.roll(z, half, 1) * sb


def _split3(x):
    a = x.astype(BF16)
    r = x - a.astype(F32)
    b = r.astype(BF16)
    return a, b, (r - b.astype(F32)).astype(BF16)


def _proj_body(x_ref, gpre_ref, w_ref, wr_ref, wt_ref, cs_ref, place_ref, base_ref, cr_ref, sr_ref,
               q_ref, kc_ref, vc_ref, ksa_ref, kw_ref, vt_ref, gt_ref, rq_ref, rk_ref, rv_ref, rg_ref):
    tm = x_ref.shape[1]
    j = pl.program_id(1)
    x = x_ref[0]
    h = _rms(x, gpre_ref[...]).astype(BF16)
    tables = base_ref[...]
    for part in _split3(cs_ref[0]):
        tables = tables + _dot(part, place_ref[...])
    nc, nsa, nsb = tables[:, 0:LANES], tables[:, LANES:2 * LANES], tables[:, 2 * LANES:3 * LANES]
    cr, sr = cr_ref[0], sr_ref[0]
    rc = jnp.concatenate([cr, cr], axis=1)
    rs = jnp.concatenate([-sr, sr], axis=1)

    def cols(start, width):
        if start >= _C_RQ:
            return _dot(h, wr_ref[:, start - _C_RQ:start - _C_RQ + width])
        return _dot(h, w_ref[:, start:start + width])

    zq = cols(_C_Q, NSA_WIDTH)
    q_scale = NSA_HEAD_DIM ** -0.5 * LOG2_E
    for s in range(NSA_WIDTH // LANES):
        sl = slice(s * LANES, (s + 1) * LANES)
        q_ref[0, :, sl] = (_rope_nsa(zq[:, sl], nc, nsa, nsb) * q_scale).astype(BF16)

    zkv = cols(_C_KC, _C_RQ - _C_KC)
    kc_ref[0] = _rope_nsa(zkv[:, _C_KC - _C_KC:_C_VC - _C_KC], nc, nsa, nsb)
    vc_ref[0] = zkv[:, _C_VC - _C_KC:_C_KS - _C_KC]
    ks = _rope_nsa(zkv[:, _C_KS - _C_KC:_C_KW - _C_KC], nc, nsa, nsb)
    kw = _rope_nsa(zkv[:, _C_KW - _C_KC:_C_RQ - _C_KC], nc, nsa, nsb)
    t_pos = j * tm + lax.broadcasted_iota(jnp.int32, (tm, SEL_PAD), 0)
    blk = lax.broadcasted_iota(jnp.int32, (tm, SEL_PAD), 1)
    onehot = jnp.where(t_pos // SEL_BLOCK == blk, 1.0, 0.0).astype(BF16)
    for g in range(NSA_KV_GROUPS):
        sl = slice(g * NSA_HEAD_DIM, (g + 1) * NSA_HEAD_DIM)
        ksa_ref[0, g] = jnp.concatenate([ks[:, sl].astype(BF16), onehot], axis=1)
        kw_ref[0, g] = kw[:, sl].astype(BF16)

    zt = _dot_nt(wt_ref[...], h)
    ones_rows = jnp.where(lax.broadcasted_iota(jnp.int32, (V_AUG - NSA_HEAD_DIM, tm), 0) == 0, 1.0, 0.0).astype(BF16)
    for k in range(2 * NSA_KV_GROUPS):
        vt_ref[0, k * V_AUG:k * V_AUG + NSA_HEAD_DIM] = zt[k * NSA_HEAD_DIM:(k + 1) * NSA_HEAD_DIM].astype(BF16)
        vt_ref[0, k * V_AUG + NSA_HEAD_DIM:(k + 1) * V_AUG] = ones_rows
    gt_ref[0] = jax.nn.sigmoid(zt[_R_GT:])

    k_scale = RET_HEAD_DIM ** -0.5
    zrq = cols(_C_RQ, RET_WIDTH)
    zrk = cols(_C_RK, RET_WIDTH)
    for hh in range(RET_HEADS):
        sl = slice(hh * RET_HEAD_DIM, (hh + 1) * RET_HEAD_DIM)
        zq_h, zk_h = zrq[:, sl], zrk[:, sl]
        rq_ref[0, :, sl] = (zq_h * rc + pltpu.roll(zq_h, RET_HEAD_DIM // 2, 1) * rs).astype(BF16)
        rk_ref[0, :, sl] = ((zk_h * rc + pltpu.roll(zk_h, RET_HEAD_DIM // 2, 1) * rs) * k_scale).astype(BF16)
    rv_ref[0] = cols(_C_RV, RET_WIDTH).astype(BF16)
    rg_ref[0] = cols(_C_RG, RET_WIDTH)


def _odd_proj(x, gpre, w, wr, wt, cs, place, base, cr, sr):
    b, t, d = x.shape
    tm = PROJ_ROWS
    G = NSA_KV_GROUPS

    def row(width):
        return pl.BlockSpec((1, tm, width), lambda bi, j: (bi, j, 0))

    def grp(width):
        return pl.BlockSpec((1, G, tm, width), lambda bi, j: (bi, 0, j, 0))

    def feat(rows):
        return pl.BlockSpec((1, rows, tm), lambda bi, j: (bi, 0, j))

    out_shape = [
        jax.ShapeDtypeStruct((b, t, NSA_WIDTH), BF16),
        jax.ShapeDtypeStruct((b, t, LANES), F32),
        jax.ShapeDtypeStruct((b, t, LANES), F32),
        jax.ShapeDtypeStruct((b, G, t, LANES), BF16),
        jax.ShapeDtypeStruct((b, G, t, NSA_HEAD_DIM), BF16),
        jax.ShapeDtypeStruct((b, VT_ROWS, t), BF16),
        jax.ShapeDtypeStruct((b, G * GATE_ROWS, t), F32),
        jax.ShapeDtypeStruct((b, t, RET_WIDTH), BF16),
        jax.ShapeDtypeStruct((b, t, RET_WIDTH), BF16),
        jax.ShapeDtypeStruct((b, t, RET_WIDTH), BF16),
        jax.ShapeDtypeStruct((b, t, RET_WIDTH), F32),
    ]
    out_specs = [row(NSA_WIDTH), row(LANES), row(LANES), grp(LANES), grp(NSA_HEAD_DIM), feat(VT_ROWS),
                 feat(G * GATE_ROWS), row(RET_WIDTH), row(RET_WIDTH), row(RET_WIDTH), row(RET_WIDTH)]
    return pl.pallas_call(
        _proj_body,
        grid=(b, t // tm),
        in_specs=[row(d), _const_spec((1, d)), _const_spec(w.shape), _const_spec(wr.shape), _const_spec(wt.shape),
                  row(cs.shape[2]),
                  _const_spec(place.shape), _const_spec(base.shape), row(cr.shape[2]), row(sr.shape[2])],
        out_specs=out_specs,
        out_shape=out_shape,
        compiler_params=_cparams(("arbitrary", "arbitrary")),
        name="odd_proj",
    )(x, gpre, w, wr, wt, cs, place, base, cr, sr)


def _compress_body(k_ref, v_ref, kpos_ref, vpos_ref, kw1_ref, vw1_ref, kw2_ref, vw2t_ref, kcm_ref, vct_ref):
    nc = k_ref.shape[1] // CMP_STRIDE
    half = CMP_BLOCK // CMP_STRIDE

    def hidden(x_ref, pos_ref, w1_ref):
        parts = [None] * half
        for off in range(CMP_STRIDE):
            tok = x_ref[0, pl.ds(off, nc, stride=CMP_STRIDE), :]
            for h in range(half):
                l = h * CMP_STRIDE + off
                term = _dot((tok + pos_ref[l:l + 1, :]).astype(BF16), w1_ref[l])
                parts[h] = term if parts[h] is None else parts[h] + term
        pre = parts[0]
        for h in range(1, half):
            pre = pre + pltpu.roll(parts[h], nc - h, 0)
        return jax.nn.gelu(pre).astype(BF16)

    hk = hidden(k_ref, kpos_ref, kw1_ref)
    hv = hidden(v_ref, vpos_ref, vw1_ref)
    for g in range(NSA_KV_GROUPS):
        sl = slice(g * CMP_HIDDEN, (g + 1) * CMP_HIDDEN)
        kcm_ref[0, g] = _dot(hk[:, sl], kw2_ref[...]).astype(BF16)
        vct_ref[0, g] = _dot_nt(vw2t_ref[...], hv[:, sl]).astype(BF16)


def _compress(k, v, kpos, vpos, kw1, vw1, kw2, vw2t):
    b, t, width = k.shape
    nc = t // CMP_STRIDE
    G = NSA_KV_GROUPS
    row = pl.BlockSpec((1, t, width), lambda bi: (bi, 0, 0))
    return pl.pallas_call(
        _compress_body,
        grid=(b,),
        in_specs=[row, row, _const_spec(kpos.shape), _const_spec(vpos.shape), _const_spec(kw1.shape),
                  _const_spec(vw1.shape), _const_spec(kw2.shape), _const_spec(vw2t.shape)],
        out_specs=[pl.BlockSpec((1, G, nc, NSA_HEAD_DIM), lambda bi: (bi, 0, 0, 0)),
                   pl.BlockSpec((1, G, NSA_HEAD_DIM, nc), lambda bi: (bi, 0, 0, 0))],
        out_shape=[jax.ShapeDtypeStruct((b, G, nc, NSA_HEAD_DIM), BF16),
                   jax.ShapeDtypeStruct((b, G, NSA_HEAD_DIM, nc), BF16)],
        compiler_params=_cparams(("arbitrary",)),
        name="compress",
    )(k, v, kpos, vpos, kw1, vw1, kw2, vw2t)


def _nsa_body(q_ref, kcm_ref, vct_ref, c2st_ref, tri_ref, ksa_ref, kw_ref, vt_ref, gt_ref, o_ref,
              score_ref, rank_ref, wbias_ref, sc_ref, sw_ref, pw_ref, sa_ref, sb_ref, pa_ref, pb_ref, lhs_ref, m_ref, mta_ref, mtb_ref, mw_ref, acc_ref, *, top_k):
    G, M, Q, dh = NSA_KV_GROUPS, NSA_HPG, Q_BLOCK, NSA_HEAD_DIM
    cols = M * Q
    nc = kcm_ref.shape[2]
    i = pl.program_id(1)
    t0 = i * Q
    tq = t0 + (lax.broadcasted_iota(jnp.int32, (1, cols), 1) & (Q - 1))
    groups = range(G)

    q_heads, q_rows = [], []
    for g in groups:
        qb = q_ref[0, :, g * NSA_GROUP_WIDTH:(g + 1) * NSA_GROUP_WIDTH]
        q_heads.append([qb[:, m * dh:(m + 1) * dh] for m in range(M)])
        q_rows.append(jnp.concatenate(q_heads[g], axis=0))

    n_idx = lax.broadcasted_iota(jnp.int32, (nc, 1), 0)
    cmp_ok = (n_idx * CMP_STRIDE + (CMP_BLOCK - 1) <= tq) & (n_idx < nc - 1)
    j_idx = lax.broadcasted_iota(jnp.int32, (SEL_PAD, Q), 0)
    cur = (t0 + lax.broadcasted_iota(jnp.int32, (SEL_PAD, Q), 1)) // SEL_BLOCK
    forced = (j_idx == 0) | (j_idx == cur) | (j_idx == cur - 1)
    valid_tok = (tq >= CMP_BLOCK - 1).astype(F32)
    o_cmp = []
    for g in groups:
        sc = sc_ref.at[g]
        sc[...] = jnp.where(cmp_ok, _dot_nt(kcm_ref[0, g], q_rows[g]), NEG_INF)
        sc[...] = jnp.exp2(sc[...] - jnp.max(sc[...], axis=0, keepdims=True))
        norm = valid_tok / jnp.sum(sc[...], axis=0, keepdims=True)
        o_cmp.append(_dot(vct_ref[0, g], sc[...].astype(BF16)) * norm)
        p_grp = sc[:, 0:Q] * norm[:, 0:Q]
        for m in range(1, M):
            p_grp = p_grp + sc[:, m * Q:(m + 1) * Q] * norm[:, m * Q:(m + 1) * Q]
        p_hi = p_grp.astype(BF16)
        p_lo = (p_grp - p_hi.astype(F32)).astype(BF16)
        imp = _dot(c2st_ref[...], p_hi) + _dot(c2st_ref[...], p_lo)
        score_ref[g] = jnp.where(j_idx <= cur, jnp.where(forced, FORCE_SCORE, imp), -1.0)

    def v_aug(branch, g, k0, n):
        r0 = (branch * G + g) * V_AUG
        return vt_ref[0, r0:r0 + V_AUG, pl.ds(k0, n)]

    span = WINDOW + Q
    w0 = pl.multiple_of(jnp.maximum(t0 - WINDOW, 0), Q)
    n_wb = span // Q
    diag_b = jnp.minimum(i, n_wb - 1)
    tri_causal, tri_leaving = tri_ref[0], tri_ref[1]
    for wb in range(n_wb):
        blk_bias = jnp.where(wb == diag_b, tri_causal, jnp.where(wb > diag_b, NEG_INF, 0.0))
        if wb == 0:
            blk_bias = jnp.where(i >= n_wb - 1, tri_leaving, blk_bias)
        wbias_ref[wb * Q:(wb + 1) * Q, :] = blk_bias
    for g in groups:
        s_win = _dot_nt(kw_ref[0, g, pl.ds(w0, span), :], q_rows[g]) + wbias_ref[...]
        sw_ref[g] = s_win
        mw_ref[g] = jnp.max(s_win, axis=0, keepdims=True)

    n_causal = (t0 + Q) // SEL_BLOCK
    rank_ref[...] = jnp.zeros(rank_ref.shape, jnp.int32)
    row_in_tile = lax.broadcasted_iota(jnp.int32, (SUBLANES, LANES), 0)
    for c0 in range(0, SEL_PAD, RANK_CHUNK):
        @pl.when(c0 < n_causal)
        def _(c0=c0):
            for g in groups:
                for l0 in range(0, Q, LANES):
                    ln = slice(l0, l0 + LANES)
                    others = [score_ref[g, k:k + 1, ln] for k in range(c0, c0 + RANK_CHUNK)]
                    for r0 in range(0, SEL_PAD, SUBLANES):
                        mine = score_ref[g, r0:r0 + SUBLANES, ln]
                        count = rank_ref[g, r0:r0 + SUBLANES, ln]
                        for k, other in zip(range(c0, c0 + RANK_CHUNK), others):
                            if r0 > k:
                                beats = other >= mine
                            elif r0 + SUBLANES <= k:
                                beats = other > mine
                            else:
                                beats = (other > mine) | ((other == mine) & (k < r0 + row_in_tile))
                            count = count + beats.astype(jnp.int32)
                        rank_ref[g, r0:r0 + SUBLANES, ln] = count

    eye_q = _eye(Q)
    for g in groups:
        chosen = jnp.where((rank_ref[g] < top_k) & (score_ref[g] >= 0.0), 1.0, 0.0).astype(BF16)
        chosen_t = _dot_nt(eye_q, chosen)
        bias = jnp.where(chosen_t > 0.5, 0.0, NEG_INF).astype(BF16)
        for m in range(M):
            lhs_ref[g, m * Q:(m + 1) * Q, :] = jnp.concatenate([q_heads[g][m], bias], axis=1)
        m_ref[g] = jnp.full((1, cols), NEG_INF, F32)
        acc_ref[g] = jnp.zeros((V_AUG, cols), F32)

    buf_a = (sa_ref, mta_ref, pa_ref)
    buf_b = (sb_ref, mtb_ref, pb_ref)

    def put_scores(buf, g, kt):
        s_dst, max_dst, _ = buf
        k0 = pl.multiple_of(kt * SEL_KEYS, SEL_KEYS)
        s = _dot_nt(ksa_ref[0, g, pl.ds(k0, SEL_KEYS), :], lhs_ref[g])
        s_dst[g] = s
        max_dst[g] = jnp.max(s, axis=0, keepdims=True)

    def sel_update(buf, g, kt, bias=None):
        src, tile_max, p_dst = buf
        k0 = pl.multiple_of(kt * SEL_KEYS, SEL_KEYS)
        m_i = m_ref[g]
        if bias is None:
            m_new = jnp.maximum(m_i, tile_max[g])
            p_dst[g] = jnp.exp2(src[g] - m_new).astype(BF16)
        else:
            m_new = jnp.maximum(m_i, jnp.max(src[g] + bias, axis=0, keepdims=True))
            p_dst[g] = jnp.exp2(src[g] + bias - m_new).astype(BF16)
        acc_ref[g] = jnp.exp2(m_i - m_new) * acc_ref[g] + _dot(v_aug(0, g, k0, SEL_KEYS), p_dst[g])
        m_ref[g] = m_new

    n_full = t0 // SEL_KEYS
    odd = n_full & 1
    for g in groups:
        put_scores(buf_a, g, 0)

    o_win = []
    for g in groups:
        pw_ref[g] = jnp.exp2(sw_ref[g] - mw_ref[g]).astype(BF16)
        acc_win = _dot(v_aug(1, g, w0, span), pw_ref[g])
        o_win.append(acc_win[0:dh] / acc_win[dh:dh + 1])

    @pl.when(odd == 1)
    def _():
        for g in groups:
            sel_update(buf_a, g, 0)
            put_scores(buf_a, g, 1)

    def sel_pair(pair, _):
        ta = odd + 2 * pair
        for g in groups:
            put_scores(buf_b, g, ta + 1)
        for g in groups:
            sel_update(buf_a, g, ta)
        for g in groups:
            put_scores(buf_a, g, ta + 2)
        for g in groups:
            sel_update(buf_b, g, ta + 1)
        return 0

    lax.fori_loop(0, n_full // 2, sel_pair, 0)
    o_sel = []
    for g in groups:
        sel_update(buf_a, g, n_full, bias=tri_causal)
        o_sel.append(acc_ref[g, 0:dh, :] / acc_ref[g, dh:dh + 1, :])

    pieces = []
    for g in groups:
        gt = gt_ref[0, g * GATE_ROWS:(g + 1) * GATE_ROWS, :]
        for m in range(M):
            c = slice(m * Q, (m + 1) * Q)
            r = NSA_BRANCHES * m
            mixed = (o_cmp[g][:, c] * gt[r:r + 1] + o_sel[g][:, c] * gt[r + 1:r + 2]
                     + o_win[g][:, c] * gt[r + 2:r + 3])
            pieces.append(_dot_nt(eye_q, mixed.astype(BF16)))
    o_ref[0] = jnp.concatenate(pieces, axis=1).astype(BF16)


def _nsa(q, kcm, vct, c2st, tri, ksa, kw, vt, gt):
    b, t, _ = q.shape
    G = NSA_KV_GROUPS
    nc = kcm.shape[2]
    n_sel = t // SEL_BLOCK
    assert n_sel <= SEL_PAD and t >= WINDOW + Q_BLOCK and t % SEL_KEYS == 0
    top_k = min(SEL_TOPK, n_sel)
    cols, span = NSA_HPG * Q_BLOCK, WINDOW + Q_BLOCK

    def per_batch(shape):
        nd = len(shape)
        return pl.BlockSpec((1,) + shape, lambda bi, i: (bi,) + (0,) * nd)

    token_rows = pl.BlockSpec((1, Q_BLOCK, NSA_WIDTH), lambda bi, i: (bi, i, 0))
    return pl.pallas_call(
        functools.partial(_nsa_body, top_k=top_k),
        grid=(b, t // Q_BLOCK),
        in_specs=[token_rows, per_batch((G, nc, NSA_HEAD_DIM)), per_batch((G, NSA_HEAD_DIM, nc)),
                  _const_spec(c2st.shape), _const_spec(tri.shape), per_batch((G, t, LANES)), per_batch((G, t, NSA_HEAD_DIM)),
                  per_batch((vt.shape[1], t)),
                  pl.BlockSpec((1, G * GATE_ROWS, Q_BLOCK), lambda bi, i: (bi, 0, i))],
        out_specs=token_rows,
        out_shape=jax.ShapeDtypeStruct((b, t, NSA_WIDTH), BF16),
        scratch_shapes=[pltpu.VMEM((G, SEL_PAD, Q_BLOCK), F32),
                        pltpu.VMEM((G, SEL_PAD, Q_BLOCK), jnp.int32),
                        pltpu.VMEM((span, cols), F32),
                        pltpu.VMEM((G, nc, cols), F32),
                        pltpu.VMEM((G, span, cols), F32),
                        pltpu.VMEM((G, span, cols), BF16),
                        pltpu.VMEM((G, SEL_KEYS, cols), F32),
                        pltpu.VMEM((G, SEL_KEYS, cols), F32),
                        pltpu.VMEM((G, SEL_KEYS, cols), BF16),
                        pltpu.VMEM((G, SEL_KEYS, cols), BF16),
                        pltpu.VMEM((G, cols, 2 * NSA_HEAD_DIM), BF16),
                        pltpu.VMEM((G, 1, cols), F32),
                        pltpu.VMEM((G, 1, cols), F32),
                        pltpu.VMEM((G, 1, cols), F32),
                        pltpu.VMEM((G, 1, cols), F32),
                        pltpu.VMEM((G, V_AUG, cols), F32)],
        compiler_params=_cparams(("arbitrary", "arbitrary")),
        name="nsa",
    )(q, kcm, vct, c2st, tri, ksa, kw, vt, gt)


def _ret_body(q_ref, k_ref, v_ref, g_ref, decay_ref, zeta_ref, xi_ref, gchunk_ref, gn_ref, o_ref, state_ref):
    C, d = RET_CHUNK, RET_HEAD_DIM

    @pl.when(pl.program_id(1) == 0)
    def _():
        state_ref[...] = jnp.zeros_like(state_ref)

    n_chunks = q_ref.shape[1] // C
    for h in range(RET_HEADS):
        hc = slice(h * d, (h + 1) * d)
        state = state_ref[h]
        before = []
        for c in range(n_chunks):
            rows = slice(c * C, (c + 1) * C)
            kz = (k_ref[0, rows, hc].astype(F32) * zeta_ref[h]).astype(BF16)
            before.append(state.astype(BF16))
            state = state * gchunk_ref[h] + _dot_tn(kz, v_ref[0, rows, hc])
        state_ref[h] = state
        for c in range(n_chunks):
            rows = slice(c * C, (c + 1) * C)
            q, k, v = q_ref[0, rows, hc], k_ref[0, rows, hc], v_ref[0, rows, hc]
            s = (_dot_nt(q, k) * decay_ref[h]).astype(BF16)
            o = _dot(s, v) + _dot((q.astype(F32) * xi_ref[h]).astype(BF16), before[c])
            mu = jnp.mean(o, axis=-1, keepdims=True)
            var = jnp.mean(jnp.square(o - mu), axis=-1, keepdims=True)
            o = (o - mu) * lax.rsqrt(var + 1e-5) * gn_ref[h]
            gate = g_ref[0, rows, hc]
            o_ref[0, rows, hc] = (gate * jax.nn.sigmoid(gate) * o).astype(BF16)


def _retention(rq, rk, rv, rg, decay, zeta, xi, gchunk, gn):
    b, t, width = rq.shape
    tok = pl.BlockSpec((1, RET_ROWS, width), lambda bi, c: (bi, c, 0))
    return pl.pallas_call(
        _ret_body,
        grid=(b, t // RET_ROWS),
        in_specs=[tok, tok, tok, tok, _const_spec(decay.shape), _const_spec(zeta.shape), _const_spec(xi.shape),
                  _const_spec(gchunk.shape), _const_spec(gn.shape)],
        out_specs=tok,
        out_shape=jax.ShapeDtypeStruct((b, t, width), BF16),
        scratch_shapes=[pltpu.VMEM((RET_HEADS, RET_HEAD_DIM, RET_HEAD_DIM), F32)],
        compiler_params=_cparams(("arbitrary", "arbitrary")),
        name="retention",
    )(rq, rk, rv, rg, decay, zeta, xi, gchunk, gn)


def _oproj_body(x_ref, yc_ref, yd_ref, w_ref, gpost_ref, fpre_ref, fpost_ref, wg_ref, wu_ref, wd_ref, o_ref):
    m = _dot(yc_ref[...], w_ref[0:NSA_WIDTH, :]) + _dot(yd_ref[...], w_ref[NSA_WIDTH:, :])
    o_ref[...] = _ffn_tile(x_ref[...] + _rms(m, gpost_ref[...]), fpre_ref, fpost_ref, wg_ref, wu_ref, wd_ref)


def _odd_out(x2, yc2, yd2, w, gpost, ffn):
    n, d = x2.shape
    tm = FFN_ROWS

    def row(width):
        return pl.BlockSpec((tm, width), lambda i: (i, 0))

    return pl.pallas_call(
        _oproj_body,
        grid=(n // tm,),
        in_specs=[row(d), row(NSA_WIDTH), row(RET_WIDTH), _const_spec(w.shape), _const_spec((1, d))]
        + _ffn_specs(d, ffn[2].shape[1]),
        out_specs=row(d),
        out_shape=jax.ShapeDtypeStruct((n, d), F32),
        compiler_params=_cparams(("arbitrary",)),
        name="odd_out_ffn",
    )(x2, yc2, yd2, w, gpost, *ffn)


def _odd_in_weight(w_in):
    sizes = [NSA_WIDTH] + [NSA_KV_WIDTH] * 6 + [NSA_BRANCHES * NSA_HEADS] + [RET_WIDTH] * 4
    offs = np.concatenate([[0], np.cumsum(sizes)])
    w_in = w_in.astype(BF16)
    q, kc, vc, ks, vs, kw, vw, gt, rq, rk, rv, rg = [w_in[:, offs[n]:offs[n + 1]] for n in range(len(sizes))]
    w = jnp.concatenate([w_in[:, offs[0]:offs[4]], kw], axis=1)
    wr = w_in[:, offs[8]:offs[12]]
    per_group = NSA_BRANCHES * NSA_HPG
    gates = [jnp.pad(gt[:, g * per_group:(g + 1) * per_group], ((0, 0), (0, GATE_ROWS - per_group)))
             for g in range(NSA_KV_GROUPS)]
    wt = jnp.concatenate([vs, vw] + gates, axis=1).T
    return w, wr, wt


def _rope_tables(positions):
    pos = positions.astype(F32)[..., None]
    half = ROPE_DIM // 2
    inv = 1.0 / (ROPE_THETA ** (jnp.arange(0, ROPE_DIM, 2, dtype=F32) / ROPE_DIM))
    ang = pos * inv
    cs = jnp.concatenate([jnp.cos(ang), jnp.sin(ang)], axis=-1)
    place = np.zeros((ROPE_DIM, 3 * LANES), np.float32)
    base = np.zeros((1, 3 * LANES), np.float32)
    for head0 in range(0, LANES, NSA_HEAD_DIM):
        base[0, head0 + ROPE_DIM:head0 + NSA_HEAD_DIM] = 1.0
        for f in range(half):
            place[f, head0 + f] = 1.0
            place[f, head0 + half + f] = 1.0
            place[half + f, LANES + head0 + f] = -1.0
            place[half + f, 2 * LANES + head0 + half + f] = 1.0
    inv_r = 1.0 / (RET_THETA ** (jnp.arange(0, RET_HEAD_DIM, 2, dtype=F32) / RET_HEAD_DIM))
    ang_r = pos * inv_r
    return cs, jnp.asarray(place, dtype=BF16), jnp.asarray(base), jnp.cos(ang_r), jnp.sin(ang_r)


def _retention_tables():
    H, C, d = RET_HEADS, RET_CHUNK, RET_HEAD_DIM
    log_gamma = np.log1p(-np.exp2(-5.0 - np.arange(H, dtype=np.float64)))
    idx = np.arange(C, dtype=np.float64)
    rel = idx[:, None] - idx[None, :]
    decay = np.where(rel >= 0, np.exp(np.maximum(rel, 0.0)[None] * log_gamma[:, None, None]), 0.0)
    zeta = np.exp((C - 1 - idx)[None, :] * log_gamma[:, None])
    xi = np.exp((idx + 1.0)[None, :] * log_gamma[:, None])
    gchunk = np.exp(C * log_gamma)
    zeta_b = np.broadcast_to(zeta[:, :, None], (H, C, d))
    xi_b = np.broadcast_to(xi[:, :, None], (H, C, d))
    gchunk_b = np.broadcast_to(gchunk[:, None, None], (H, 1, d))
    return tuple(jnp.asarray(a, dtype=F32) for a in (decay, zeta_b, xi_b, gchunk_b))


def _cmp_to_sel_t(t_len):
    nc = t_len // CMP_STRIDE
    n_cmp = (t_len - CMP_BLOCK) // CMP_STRIDE + 1
    n_sel = t_len // SEL_BLOCK
    c_start = np.arange(nc) * CMP_STRIDE
    s_start = np.arange(SEL_PAD) * SEL_BLOCK
    hit = ((c_start[None, :] < s_start[:, None] + SEL_BLOCK) & (c_start[None, :] + CMP_BLOCK > s_start[:, None])
           & (np.arange(nc)[None, :] < n_cmp) & (np.arange(SEL_PAD)[:, None] < n_sel))
    return jnp.asarray(hit.astype(np.float32), dtype=BF16)


def _triangle_biases():
    r = np.arange(Q_BLOCK)[:, None]
    tok = np.arange(NSA_HPG * Q_BLOCK)[None, :] % Q_BLOCK
    causal = np.where(r <= tok, 0.0, NEG_INF)
    leaving = np.where(r > tok, 0.0, NEG_INF)
    return jnp.asarray(np.stack([causal, leaving]), dtype=F32)


def _compress_weights(pos, w1, w2):
    G, dh = NSA_KV_GROUPS, NSA_HEAD_DIM
    pos_rows = jnp.tile(pos, (1, G))
    w1r = w1.astype(BF16).reshape(CMP_BLOCK, dh, CMP_HIDDEN)
    w1_bd = jnp.einsum('ab,lij->laibj', jnp.eye(G, dtype=BF16), w1r).reshape(CMP_BLOCK, G * dh, G * CMP_HIDDEN)
    return pos_rows, w1_bd, w2.astype(BF16)


def _odd_layer(x, positions_tables, ret_tables, c2st, tri, gpre, w_in, cmp_k_pos, cmp_k_w1, cmp_k_w2,
               cmp_v_pos, cmp_v_w1, cmp_v_w2, gn_g, w_out, gpost, ffn):
    b, t, d = x.shape
    w, wr, wt = _odd_in_weight(w_in)
    (q, kc, vc, ksa, kw, vt, gt, rq, rk, rv, rg) = _odd_proj(x, gpre, w, wr, wt, *positions_tables)
    kpos, kw1, kw2 = _compress_weights(cmp_k_pos, cmp_k_w1, cmp_k_w2)
    vpos, vw1, vw2 = _compress_weights(cmp_v_pos, cmp_v_w1, cmp_v_w2)
    kcm, vct = _compress(kc, vc, kpos, vpos, kw1, vw1, kw2, vw2.T)
    yc = _nsa(q, kcm, vct, c2st, tri, ksa, kw, vt, gt)
    yd = _retention(rq, rk, rv, rg, *ret_tables, gn_g.reshape(RET_HEADS, 1, RET_HEAD_DIM))
    out = _odd_out(x.reshape(b * t, d), yc.reshape(b * t, NSA_WIDTH), yd.reshape(b * t, RET_WIDTH),
                   w_out.astype(BF16), gpost, ffn)
    return out.reshape(b, t, d)


def kernel(x, positions, ln_mix_pre, ln_mix_post, ln_ffn_pre, ln_ffn_post, ffn_w_gate, ffn_w_up, ffn_w_down,
           ev_w_in, ev_pool_w, ev_pool_scale, ev_sgu_ln_g, ev_sgu_ln_b, ev_sgu_w, ev_sgu_b, ev_w_out,
           od_w_in, od_cmp_k_pos, od_cmp_k_w1, od_cmp_k_w2, od_cmp_v_pos, od_cmp_v_w1, od_cmp_v_w2,
           od_ret_gn_g, od_w_out):
    b, t, d = x.shape
    depth = ln_mix_pre.shape[0]
    rope = _rope_tables(positions)
    ret_tables = _retention_tables()
    c2st = _cmp_to_sel_t(t)
    tri = _triangle_biases()
    for layer in range(depth):
        gpre = ln_mix_pre[layer].reshape(1, d)
        gpost = ln_mix_post[layer].reshape(1, d)
        ffn = (ln_ffn_pre[layer].reshape(1, d), ln_ffn_post[layer].reshape(1, d), ffn_w_gate[layer].astype(BF16),
               ffn_w_up[layer].astype(BF16), ffn_w_down[layer].astype(BF16))
        if layer % 2 == 0:
            e = layer // 2
            x = _even_layer(x, gpre, ev_w_in[e].astype(BF16), ev_pool_w[e].astype(BF16),
                            ev_pool_scale[e].reshape(1, POOL_WIDTH), ev_sgu_ln_g[e].reshape(1, SGU_WIDTH),
                            ev_sgu_ln_b[e].reshape(1, SGU_WIDTH), ev_sgu_w[e], ev_sgu_b[e].T,
                            ev_w_out[e].astype(BF16), gpost, ffn)
        else:
            o = layer // 2
            x = _odd_layer(x, rope, ret_tables, c2st, tri, gpre, od_w_in[o], od_cmp_k_pos[o], od_cmp_k_w1[o],
                           od_cmp_k_w2[o], od_cmp_v_pos[o], od_cmp_v_w1[o], od_cmp_v_w2[o], od_ret_gn_g[o],
                           od_w_out[o], gpost, ffn)
    return x
```

```python
import functools

import numpy as np
import jax
import jax.numpy as jnp
from jax import lax
from jax.experimental import pallas as pl
from jax.experimental.pallas import tpu as pltpu

F32 = jnp.float32
BF16 = jnp.bfloat16

D_MODEL = 1024
POOL_WINDOWS = (2, 4, 8, 16)
POOL_GROUPS = 4
POOL_WIDTH = D_MODEL // 2
POOL_GDIM = POOL_WIDTH // POOL_GROUPS
POOL_HIST = 16
SGU_GROUPS = 4
SGU_WIDTH = D_MODEL // 2
SGU_GDIM = SGU_WIDTH // SGU_GROUPS
SGU_CHUNK = 128
EVEN_IN = POOL_WIDTH + 2 * SGU_WIDTH
NSA_HEADS = 8
NSA_KV_GROUPS = 2
NSA_HPG = NSA_HEADS // NSA_KV_GROUPS
NSA_HEAD_DIM = 64
NSA_WIDTH = NSA_HEADS * NSA_HEAD_DIM
NSA_GROUP_WIDTH = NSA_HPG * NSA_HEAD_DIM
NSA_KV_WIDTH = NSA_KV_GROUPS * NSA_HEAD_DIM
NSA_BRANCHES = 3
GATE_ROWS = 16
CMP_BLOCK = 32
CMP_STRIDE = 16
CMP_HIDDEN = 128
SEL_BLOCK = 64
SEL_TOPK = 16
SEL_PAD = 64
WINDOW = 512
Q_BLOCK = 256
FORCE_SCORE = 1.0e4
ROPE_THETA = 500000.0
ROPE_DIM = NSA_HEAD_DIM // 4
RET_HEADS = 4
RET_HEAD_DIM = 128
RET_WIDTH = RET_HEADS * RET_HEAD_DIM
RET_CHUNK = 128
RET_THETA = 10000.0
ODD_MIX = NSA_WIDTH + RET_WIDTH
NEG_INF = -1.0e30
LOG2_E = 1.4426950408889634
LANES = 128
SUBLANES = 8

VMEM_LIMIT = 56 * 1024 * 1024
FFN_ROWS = 512
FFN_CHUNKS = 11
EVEN_ROWS = 512
PROJ_ROWS = 512
SEL_KEYS = Q_BLOCK
RANK_CHUNK = 8
RET_ROWS = 512


def _cparams(sem):
    return pltpu.CompilerParams(dimension_semantics=sem, vmem_limit_bytes=VMEM_LIMIT)


def _const_spec(shape):
    nd = len(shape)
    return pl.BlockSpec(shape, lambda *_: (0,) * nd, pipeline_mode=pl.Buffered(1))


def _dot(a, b):
    return jnp.dot(a, b, preferred_element_type=F32)


def _dot_nt(a, b):
    return lax.dot_general(a, b, (((1,), (1,)), ((), ())), preferred_element_type=F32)


def _dot_tn(a, b):
    return lax.dot_general(a, b, (((0,), (0,)), ((), ())), preferred_element_type=F32)


def _rms(x, g, eps=1e-6):
    return x * lax.rsqrt(jnp.mean(x * x, axis=-1, keepdims=True) + eps) * g


def _softmax2_cols(s):
    e = jnp.exp2(s - jnp.max(s, axis=0, keepdims=True))
    return e / jnp.sum(e, axis=0, keepdims=True)


def _eye(n):
    return jnp.where(lax.broadcasted_iota(jnp.int32, (n, n), 0) == lax.broadcasted_iota(jnp.int32, (n, n), 1),
                     1.0, 0.0).astype(BF16)


def _ffn_tile(x, gpre_ref, gpost_ref, wg_ref, wu_ref, wd_ref):
    h = _rms(x, gpre_ref[...]).astype(BF16)
    f_total = wg_ref.shape[1]
    fc = f_total // FFN_CHUNKS
    acc = None
    for c in range(FFN_CHUNKS):
        gate = _dot(h, wg_ref[:, c * fc:(c + 1) * fc])
        up = _dot(h, wu_ref[:, c * fc:(c + 1) * fc])
        act = (gate * jax.nn.sigmoid(gate) * up).astype(BF16)
        part = _dot(act, wd_ref[c * fc:(c + 1) * fc, :])
        acc = part if acc is None else acc + part
    return x + _rms(acc, gpost_ref[...])


def _layer_spec(stacked_shape, layer):
    nd = len(stacked_shape) - 1
    return pl.BlockSpec((None,) + tuple(stacked_shape[1:]), lambda *_: (layer,) + (0,) * nd,
                        pipeline_mode=pl.Buffered(1))


def _weight_spec(w):
    return _layer_spec(w[0].shape, w[1]) if isinstance(w, tuple) else _const_spec(w.shape)


def _weight_array(w):
    return w[0] if isinstance(w, tuple) else w


def _ffn_specs(ffn):
    layer, fpre, fpost, wg, wu, wd = ffn
    return [_const_spec(fpre.shape), _const_spec(fpost.shape), _layer_spec(wg.shape, layer),
            _layer_spec(wu.shape, layer), _layer_spec(wd.shape, layer)]


def _even_body(x_ref, gpre_ref, win_ref, poolw_ref, pscale_ref, lng_ref, lnb_ref, sguw_ref, sgub_ref,
               wout_ref, gpost_ref, fpre_ref, fpost_ref, wg_ref, wu_ref, wd_ref, o_ref, hist_ref):
    tt = x_ref.shape[1]
    j = pl.program_id(1)

    @pl.when(j == 0)
    def _():
        hist_ref[0:POOL_HIST, :] = jnp.zeros((POOL_HIST, POOL_WIDTH), F32)

    x = x_ref[0]
    h = _rms(x, gpre_ref[...]).astype(BF16)
    z = _dot(h, win_ref[...])
    a = z[:, :POOL_WIDTH]
    u = z[:, POOL_WIDTH:POOL_WIDTH + SGU_WIDTH]
    v = z[:, POOL_WIDTH + SGU_WIDTH:]

    hist_ref[POOL_HIST:POOL_HIST + tt, :] = a
    t_pos = j * tt + lax.broadcasted_iota(jnp.int32, (tt, 1), 0)
    ya = []
    for gi, w in enumerate(POOL_WINDOWS):
        cols = slice(gi * POOL_GDIM, (gi + 1) * POOL_GDIM)
        win_sum = a[:, cols]
        for s in range(1, w):
            win_sum = win_sum + hist_ref[POOL_HIST - s:POOL_HIST - s + tt, cols]
        cnt = jnp.minimum(t_pos + 1, w).astype(F32)
        diff = (win_sum / cnt - a[:, cols]).astype(BF16)
        ya.append(_dot(diff, poolw_ref[gi]))
    ya = jnp.concatenate(ya, axis=1) * pscale_ref[...]
    hist_ref[0:POOL_HIST, :] = hist_ref[tt:tt + POOL_HIST, :]

    ug = jax.nn.gelu(u)
    vg = jax.nn.gelu(v)
    mu = jnp.mean(vg, axis=-1, keepdims=True)
    var = jnp.mean(jnp.square(vg - mu), axis=-1, keepdims=True)
    vn = ((vg - mu) * lax.rsqrt(var + 1e-5) * lng_ref[...] + lnb_ref[...]).astype(BF16)
    r_i = lax.broadcasted_iota(jnp.int32, (SGU_CHUNK, SGU_CHUNK), 0)
    c_i = lax.broadcasted_iota(jnp.int32, (SGU_CHUNK, SGU_CHUNK), 1)
    yb = []
    for g in range(SGU_GROUPS):
        cols = slice(g * SGU_GDIM, (g + 1) * SGU_GDIM)
        wm = jnp.where(r_i >= c_i, sguw_ref[g], 0.0).astype(BF16)
        bcol = sgub_ref[:, g:g + 1]
        parts = []
        for c in range(tt // SGU_CHUNK):
            rows = slice(c * SGU_CHUNK, (c + 1) * SGU_CHUNK)
            parts.append(ug[rows, cols] * (_dot(wm, vn[rows, cols]) + bcol))
        yb.append(jnp.concatenate(parts, axis=0))
    yb = jnp.concatenate(yb, axis=1)

    m = _dot(ya.astype(BF16), wout_ref[0:POOL_WIDTH, :]) + _dot(yb.astype(BF16), wout_ref[POOL_WIDTH:, :])
    o_ref[0] = _ffn_tile(x + _rms(m, gpost_ref[...]), fpre_ref, fpost_ref, wg_ref, wu_ref, wd_ref)


def _even_layer(x, gpre, win, poolw, pscale, lng, lnb, sguw, sgub_t, wout, gpost, ffn):
    b, t, d = x.shape
    tt = EVEN_ROWS
    row = pl.BlockSpec((1, tt, d), lambda bi, j: (bi, j, 0))
    return pl.pallas_call(
        _even_body,
        grid=(b, t // tt),
        in_specs=[row, _const_spec((1, d)), _weight_spec(win), _const_spec(poolw.shape),
                  _const_spec(pscale.shape), _const_spec(lng.shape), _const_spec(lnb.shape),
                  _const_spec(sguw.shape), _const_spec(sgub_t.shape), _weight_spec(wout),
                  _const_spec((1, d))] + _ffn_specs(ffn),
        out_specs=row,
        out_shape=jax.ShapeDtypeStruct((b, t, d), F32),
        scratch_shapes=[pltpu.VMEM((POOL_HIST + tt, POOL_WIDTH), F32)],
        compiler_params=_cparams(("arbitrary", "arbitrary")),
        name="even_layer",
    )(x, gpre, _weight_array(win), poolw, pscale, lng, lnb, sguw, sgub_t, _weight_array(wout), gpost, *ffn[1:])


_C_Q = 0
_C_KC = _C_Q + NSA_WIDTH
_C_VC = _C_KC + NSA_KV_WIDTH
_C_KS = _C_VC + NSA_KV_WIDTH
_C_KW = _C_KS + NSA_KV_WIDTH
_C_RQ = _C_KW + NSA_KV_WIDTH
_C_RK = _C_RQ + RET_WIDTH
_C_RV = _C_RK + RET_WIDTH
_C_RG = _C_RV + RET_WIDTH
ODD_COLS = _C_RG + RET_WIDTH
_R_VS = 0
_R_VW = _R_VS + NSA_KV_WIDTH
_R_GT = _R_VW + NSA_KV_WIDTH
ODD_TROWS = _R_GT + NSA_KV_GROUPS * GATE_ROWS
V_AUG = NSA_HEAD_DIM + 16
VT_ROWS = 2 * NSA_KV_GROUPS * V_AUG


def _rope_nsa(z, c, sa, sb):
    half = ROPE_DIM // 2
    return z * c + pltpu.roll(z, LANES - half, 1) * sa + pltpu.roll(z, half, 1) * sb


def _split3(x):
    a = x.astype(BF16)
    r = x - a.astype(F32)
    b = r.astype(BF16)
    return a, b, (r - b.astype(F32)).astype(BF16)


def _proj_body(x_ref, gpre_ref, w_ref, wr_ref, wt_ref, cs_ref, place_ref, base_ref, cr_ref, sr_ref,
               q_ref, kc_ref, vc_ref, ksa_ref, kw_ref, vt_ref, gt_ref, rq_ref, rk_ref, rv_ref, rg_ref):
    tm = x_ref.shape[1]
    j = pl.program_id(1)
    x = x_ref[0]
    h = _rms(x, gpre_ref[...]).astype(BF16)
    tables = base_ref[...]
    for part in _split3(cs_ref[0]):
        tables = tables + _dot(part, place_ref[...])
    nc, nsa, nsb = tables[:, 0:LANES], tables[:, LANES:2 * LANES], tables[:, 2 * LANES:3 * LANES]
    cr, sr = cr_ref[0], sr_ref[0]
    rc = jnp.concatenate([cr, cr], axis=1)
    rs = jnp.concatenate([-sr, sr], axis=1)

    def cols(start, width):
        if start >= _C_RQ:
            return _dot(h, wr_ref[:, start - _C_RQ:start - _C_RQ + width])
        return _dot(h, w_ref[:, start:start + width])

    zq = cols(_C_Q, NSA_WIDTH)
    q_scale = NSA_HEAD_DIM ** -0.5 * LOG2_E
    for s in range(NSA_WIDTH // LANES):
        sl = slice(s * LANES, (s + 1) * LANES)
        q_ref[0, :, sl] = (_rope_nsa(zq[:, sl], nc, nsa, nsb) * q_scale).astype(BF16)

    zkv = cols(_C_KC, _C_RQ - _C_KC)
    kc_ref[0] = _rope_nsa(zkv[:, _C_KC - _C_KC:_C_VC - _C_KC], nc, nsa, nsb)
    vc_ref[0] = zkv[:, _C_VC - _C_KC:_C_KS - _C_KC]
    ks = _rope_nsa(zkv[:, _C_KS - _C_KC:_C_KW - _C_KC], nc, nsa, nsb)
    kw = _rope_nsa(zkv[:, _C_KW - _C_KC:_C_RQ - _C_KC], nc, nsa, nsb)
    t_pos = j * tm + lax.broadcasted_iota(jnp.int32, (tm, SEL_PAD), 0)
    blk = lax.broadcasted_iota(jnp.int32, (tm, SEL_PAD), 1)
    onehot = jnp.where(t_pos // SEL_BLOCK == blk, 1.0, 0.0).astype(BF16)
    for g in range(NSA_KV_GROUPS):
        sl = slice(g * NSA_HEAD_DIM, (g + 1) * NSA_HEAD_DIM)
        ksa_ref[0, g] = jnp.concatenate([ks[:, sl].astype(BF16), onehot], axis=1)
        kw_ref[0, g] = kw[:, sl].astype(BF16)

    zt = _dot_nt(wt_ref[...], h)
    ones_rows = jnp.where(lax.broadcasted_iota(jnp.int32, (V_AUG - NSA_HEAD_DIM, tm), 0) == 0, 1.0, 0.0).astype(BF16)
    for k in range(2 * NSA_KV_GROUPS):
        vt_ref[0, k * V_AUG:k * V_AUG + NSA_HEAD_DIM] = zt[k * NSA_HEAD_DIM:(k + 1) * NSA_HEAD_DIM].astype(BF16)
        vt_ref[0, k * V_AUG + NSA_HEAD_DIM:(k + 1) * V_AUG] = ones_rows
    gt_ref[0] = jax.nn.sigmoid(zt[_R_GT:])

    k_scale = RET_HEAD_DIM ** -0.5
    zrq = cols(_C_RQ, RET_WIDTH)
    zrk = cols(_C_RK, RET_WIDTH)
    for hh in range(RET_HEADS):
        sl = slice(hh * RET_HEAD_DIM, (hh + 1) * RET_HEAD_DIM)
        zq_h, zk_h = zrq[:, sl], zrk[:, sl]
        rq_ref[0, :, sl] = (zq_h * rc + pltpu.roll(zq_h, RET_HEAD_DIM // 2, 1) * rs).astype(BF16)
        rk_ref[0, :, sl] = ((zk_h * rc + pltpu.roll(zk_h, RET_HEAD_DIM // 2, 1) * rs) * k_scale).astype(BF16)
    rv_ref[0] = cols(_C_RV, RET_WIDTH).astype(BF16)
    rg_ref[0] = cols(_C_RG, RET_WIDTH)


def _odd_proj(x, gpre, w, wr, wt, cs, place, base, cr, sr):
    b, t, d = x.shape
    tm = PROJ_ROWS
    G = NSA_KV_GROUPS

    def row(width):
        return pl.BlockSpec((1, tm, width), lambda bi, j: (bi, j, 0))

    def grp(width):
        return pl.BlockSpec((1, G, tm, width), lambda bi, j: (bi, 0, j, 0))

    def feat(rows):
        return pl.BlockSpec((1, rows, tm), lambda bi, j: (bi, 0, j))

    out_shape = [
        jax.ShapeDtypeStruct((b, t, NSA_WIDTH), BF16),
        jax.ShapeDtypeStruct((b, t, LANES), F32),
        jax.ShapeDtypeStruct((b, t, LANES), F32),
        jax.ShapeDtypeStruct((b, G, t, LANES), BF16),
        jax.ShapeDtypeStruct((b, G, t, NSA_HEAD_DIM), BF16),
        jax.ShapeDtypeStruct((b, VT_ROWS, t), BF16),
        jax.ShapeDtypeStruct((b, G * GATE_ROWS, t), F32),
        jax.ShapeDtypeStruct((b, t, RET_WIDTH), BF16),
        jax.ShapeDtypeStruct((b, t, RET_WIDTH), BF16),
        jax.ShapeDtypeStruct((b, t, RET_WIDTH), BF16),
        jax.ShapeDtypeStruct((b, t, RET_WIDTH), F32),
    ]
    out_specs = [row(NSA_WIDTH), row(LANES), row(LANES), grp(LANES), grp(NSA_HEAD_DIM), feat(VT_ROWS),
                 feat(G * GATE_ROWS), row(RET_WIDTH), row(RET_WIDTH), row(RET_WIDTH), row(RET_WIDTH)]
    return pl.pallas_call(
        _proj_body,
        grid=(b, t // tm),
        in_specs=[row(d), _const_spec((1, d)), _const_spec(w.shape), _const_spec(wr.shape), _const_spec(wt.shape),
                  row(cs.shape[2]),
                  _const_spec(place.shape), _const_spec(base.shape), row(cr.shape[2]), row(sr.shape[2])],
        out_specs=out_specs,
        out_shape=out_shape,
        compiler_params=_cparams(("arbitrary", "arbitrary")),
        name="odd_proj",
    )(x, gpre, w, wr, wt, cs, place, base, cr, sr)


def _compress_body(k_ref, v_ref, kpos_ref, vpos_ref, kw1_ref, vw1_ref, kw2_ref, vw2t_ref, kcm_ref, vct_ref):
    nc = k_ref.shape[1] // CMP_STRIDE
    half = CMP_BLOCK // CMP_STRIDE

    def hidden(x_ref, pos_ref, w1_ref):
        parts = [None] * half
        for off in range(CMP_STRIDE):
            tok = x_ref[0, pl.ds(off, nc, stride=CMP_STRIDE), :]
            for h in range(half):
                l = h * CMP_STRIDE + off
                term = _dot((tok + pos_ref[l:l + 1, :]).astype(BF16), w1_ref[l])
                parts[h] = term if parts[h] is None else parts[h] + term
        pre = parts[0]
        for h in range(1, half):
            pre = pre + pltpu.roll(parts[h], nc - h, 0)
        return jax.nn.gelu(pre).astype(BF16)

    hk = hidden(k_ref, kpos_ref, kw1_ref)
    hv = hidden(v_ref, vpos_ref, vw1_ref)
    for g in range(NSA_KV_GROUPS):
        sl = slice(g * CMP_HIDDEN, (g + 1) * CMP_HIDDEN)
        kcm_ref[0, g] = _dot(hk[:, sl], kw2_ref[...]).astype(BF16)
        vct_ref[0, g] = _dot_nt(vw2t_ref[...], hv[:, sl]).astype(BF16)


def _compress(k, v, kpos, vpos, kw1, vw1, kw2, vw2t):
    b, t, width = k.shape
    nc = t // CMP_STRIDE
    G = NSA_KV_GROUPS
    row = pl.BlockSpec((1, t, width), lambda bi: (bi, 0, 0))
    return pl.pallas_call(
        _compress_body,
        grid=(b,),
        in_specs=[row, row, _const_spec(kpos.shape), _const_spec(vpos.shape), _const_spec(kw1.shape),
                  _const_spec(vw1.shape), _const_spec(kw2.shape), _const_spec(vw2t.shape)],
        out_specs=[pl.BlockSpec((1, G, nc, NSA_HEAD_DIM), lambda bi: (bi, 0, 0, 0)),
                   pl.BlockSpec((1, G, NSA_HEAD_DIM, nc), lambda bi: (bi, 0, 0, 0))],
        out_shape=[jax.ShapeDtypeStruct((b, G, nc, NSA_HEAD_DIM), BF16),
                   jax.ShapeDtypeStruct((b, G, NSA_HEAD_DIM, nc), BF16)],
        compiler_params=_cparams(("arbitrary",)),
        name="compress",
    )(k, v, kpos, vpos, kw1, vw1, kw2, vw2t)


def _nsa_body(q_ref, kcm_ref, vct_ref, c2st_ref, tri_ref, ksa_ref, kw_ref, vt_ref, gt_ref, o_ref,
              score_ref, rank_ref, wbias_ref, sc_ref, sw_ref, pw_ref, sa_ref, sb_ref, pa_ref, pb_ref, lhs_ref, m_ref, mta_ref, mtb_ref, mw_ref, acc_ref, *, top_k):
    G, M, Q, dh = NSA_KV_GROUPS, NSA_HPG, Q_BLOCK, NSA_HEAD_DIM
    cols = M * Q
    nc = kcm_ref.shape[2]
    i = pl.program_id(1)
    t0 = i * Q
    tq = t0 + (lax.broadcasted_iota(jnp.int32, (1, cols), 1) & (Q - 1))
    groups = range(G)

    q_heads, q_rows = [], []
    for g in groups:
        qb = q_ref[0, :, g * NSA_GROUP_WIDTH:(g + 1) * NSA_GROUP_WIDTH]
        q_heads.append([qb[:, m * dh:(m + 1) * dh] for m in range(M)])
        q_rows.append(jnp.concatenate(q_heads[g], axis=0))

    n_idx = lax.broadcasted_iota(jnp.int32, (nc, 1), 0)
    cmp_ok = (n_idx * CMP_STRIDE + (CMP_BLOCK - 1) <= tq) & (n_idx < nc - 1)
    j_idx = lax.broadcasted_iota(jnp.int32, (SEL_PAD, Q), 0)
    cur = (t0 + lax.broadcasted_iota(jnp.int32, (SEL_PAD, Q), 1)) // SEL_BLOCK
    forced = (j_idx == 0) | (j_idx == cur) | (j_idx == cur - 1)
    valid_tok = (tq >= CMP_BLOCK - 1).astype(F32)
    o_cmp = []
    for g in groups:
        sc = sc_ref.at[g]
        sc[...] = jnp.where(cmp_ok, _dot_nt(kcm_ref[0, g], q_rows[g]), NEG_INF)
        sc[...] = jnp.exp2(sc[...] - jnp.max(sc[...], axis=0, keepdims=True))
        norm = valid_tok / jnp.sum(sc[...], axis=0, keepdims=True)
        o_cmp.append(_dot(vct_ref[0, g], sc[...].astype(BF16)) * norm)
        p_grp = sc[:, 0:Q] * norm[:, 0:Q]
        for m in range(1, M):
            p_grp = p_grp + sc[:, m * Q:(m + 1) * Q] * norm[:, m * Q:(m + 1) * Q]
        p_hi = p_grp.astype(BF16)
        p_lo = (p_grp - p_hi.astype(F32)).astype(BF16)
        imp = _dot(c2st_ref[...], p_hi) + _dot(c2st_ref[...], p_lo)
        score_ref[g] = jnp.where(j_idx <= cur, jnp.where(forced, FORCE_SCORE, imp), -1.0)

    def v_aug(branch, g, k0, n):
        r0 = (branch * G + g) * V_AUG
        return vt_ref[0, r0:r0 + V_AUG, pl.ds(k0, n)]

    span = WINDOW + Q
    w0 = pl.multiple_of(jnp.maximum(t0 - WINDOW, 0), Q)
    n_wb = span // Q
    diag_b = jnp.minimum(i, n_wb - 1)
    tri_causal, tri_leaving = tri_ref[0], tri_ref[1]
    for wb in range(n_wb):
        blk_bias = jnp.where(wb == diag_b, tri_causal, jnp.where(wb > diag_b, NEG_INF, 0.0))
        if wb == 0:
            blk_bias = jnp.where(i >= n_wb - 1, tri_leaving, blk_bias)
        wbias_ref[wb * Q:(wb + 1) * Q, :] = blk_bias
    for g in groups:
        s_win = _dot_nt(kw_ref[0, g, pl.ds(w0, span), :], q_rows[g]) + wbias_ref[...]
        sw_ref[g] = s_win
        mw_ref[g] = jnp.max(s_win, axis=0, keepdims=True)

    n_causal = (t0 + Q) // SEL_BLOCK
    rank_ref[...] = jnp.zeros(rank_ref.shape, jnp.int32)
    row_in_tile = lax.broadcasted_iota(jnp.int32, (SUBLANES, LANES), 0)
    for c0 in range(0, SEL_PAD, RANK_CHUNK):
        @pl.when(c0 < n_causal)
        def _(c0=c0):
            for g in groups:
                for l0 in range(0, Q, LANES):
                    ln = slice(l0, l0 + LANES)
                    others = [score_ref[g, k:k + 1, ln] for k in range(c0, c0 + RANK_CHUNK)]
                    for r0 in range(0, SEL_PAD, SUBLANES):
                        mine = score_ref[g, r0:r0 + SUBLANES, ln]
                        count = rank_ref[g, r0:r0 + SUBLANES, ln]
                        for k, other in zip(range(c0, c0 + RANK_CHUNK), others):
                            if r0 > k:
                                beats = other >= mine
                            elif r0 + SUBLANES <= k:
                                beats = other > mine
                            else:
                                beats = (other > mine) | ((other == mine) & (k < r0 + row_in_tile))
                            count = count + beats.astype(jnp.int32)
                        rank_ref[g, r0:r0 + SUBLANES, ln] = count

    eye_q = _eye(Q)
    for g in groups:
        chosen = jnp.where((rank_ref[g] < top_k) & (score_ref[g] >= 0.0), 1.0, 0.0).astype(BF16)
        chosen_t = _dot_nt(eye_q, chosen)
        bias = jnp.where(chosen_t > 0.5, 0.0, NEG_INF).astype(BF16)
        for m in range(M):
            lhs_ref[g, m * Q:(m + 1) * Q, :] = jnp.concatenate([q_heads[g][m], bias], axis=1)
        m_ref[g] = jnp.full((1, cols), NEG_INF, F32)
        acc_ref[g] = jnp.zeros((V_AUG, cols), F32)

    buf_a = (sa_ref, mta_ref, pa_ref)
    buf_b = (sb_ref, mtb_ref, pb_ref)

    def put_scores(buf, g, kt):
        s_dst, max_dst, _ = buf
        k0 = pl.multiple_of(kt * SEL_KEYS, SEL_KEYS)
        s = _dot_nt(ksa_ref[0, g, pl.ds(k0, SEL_KEYS), :], lhs_ref[g])
        s_dst[g] = s
        max_dst[g] = jnp.max(s, axis=0, keepdims=True)

    def sel_update(buf, g, kt, bias=None):
        src, tile_max, p_dst = buf
        k0 = pl.multiple_of(kt * SEL_KEYS, SEL_KEYS)
        m_i = m_ref[g]
        if bias is None:
            m_new = jnp.maximum(m_i, tile_max[g])
            p_dst[g] = jnp.exp2(src[g] - m_new).astype(BF16)
        else:
            m_new = jnp.maximum(m_i, jnp.max(src[g] + bias, axis=0, keepdims=True))
            p_dst[g] = jnp.exp2(src[g] + bias - m_new).astype(BF16)
        acc_ref[g] = jnp.exp2(m_i - m_new) * acc_ref[g] + _dot(v_aug(0, g, k0, SEL_KEYS), p_dst[g])
        m_ref[g] = m_new

    n_full = t0 // SEL_KEYS
    odd = n_full & 1
    for g in groups:
        put_scores(buf_a, g, 0)

    o_win = []
    for g in groups:
        pw_ref[g] = jnp.exp2(sw_ref[g] - mw_ref[g]).astype(BF16)
        acc_win = _dot(v_aug(1, g, w0, span), pw_ref[g])
        o_win.append(acc_win[0:dh] / acc_win[dh:dh + 1])

    @pl.when(odd == 1)
    def _():
        for g in groups:
            sel_update(buf_a, g, 0)
            put_scores(buf_a, g, 1)

    def sel_pair(pair, _):
        ta = odd + 2 * pair
        for g in groups:
            put_scores(buf_b, g, ta + 1)
        for g in groups:
            sel_update(buf_a, g, ta)
        for g in groups:
            put_scores(buf_a, g, ta + 2)
        for g in groups:
            sel_update(buf_b, g, ta + 1)
        return 0

    lax.fori_loop(0, n_full // 2, sel_pair, 0)
    o_sel = []
    for g in groups:
        sel_update(buf_a, g, n_full, bias=tri_causal)
        o_sel.append(acc_ref[g, 0:dh, :] / acc_ref[g, dh:dh + 1, :])

    pieces = []
    for g in groups:
        gt = gt_ref[0, g * GATE_ROWS:(g + 1) * GATE_ROWS, :]
        for m in range(M):
            c = slice(m * Q, (m + 1) * Q)
            r = NSA_BRANCHES * m
            mixed = (o_cmp[g][:, c] * gt[r:r + 1] + o_sel[g][:, c] * gt[r + 1:r + 2]
                     + o_win[g][:, c] * gt[r + 2:r + 3])
            pieces.append(_dot_nt(eye_q, mixed.astype(BF16)))
    o_ref[0] = jnp.concatenate(pieces, axis=1).astype(BF16)


def _nsa(q, kcm, vct, c2st, tri, ksa, kw, vt, gt):
    b, t, _ = q.shape
    G = NSA_KV_GROUPS
    nc = kcm.shape[2]
    n_sel = t // SEL_BLOCK
    assert n_sel <= SEL_PAD and t >= WINDOW + Q_BLOCK and t % SEL_KEYS == 0
    top_k = min(SEL_TOPK, n_sel)
    cols, span = NSA_HPG * Q_BLOCK, WINDOW + Q_BLOCK

    def per_batch(shape):
        nd = len(shape)
        return pl.BlockSpec((1,) + shape, lambda bi, i: (bi,) + (0,) * nd)

    token_rows = pl.BlockSpec((1, Q_BLOCK, NSA_WIDTH), lambda bi, i: (bi, i, 0))
    return pl.pallas_call(
        functools.partial(_nsa_body, top_k=top_k),
        grid=(b, t // Q_BLOCK),
        in_specs=[token_rows, per_batch((G, nc, NSA_HEAD_DIM)), per_batch((G, NSA_HEAD_DIM, nc)),
                  _const_spec(c2st.shape), _const_spec(tri.shape), per_batch((G, t, LANES)), per_batch((G, t, NSA_HEAD_DIM)),
                  per_batch((vt.shape[1], t)),
                  pl.BlockSpec((1, G * GATE_ROWS, Q_BLOCK), lambda bi, i: (bi, 0, i))],
        out_specs=token_rows,
        out_shape=jax.ShapeDtypeStruct((b, t, NSA_WIDTH), BF16),
        scratch_shapes=[pltpu.VMEM((G, SEL_PAD, Q_BLOCK), F32),
                        pltpu.VMEM((G, SEL_PAD, Q_BLOCK), jnp.int32),
                        pltpu.VMEM((span, cols), F32),
                        pltpu.VMEM((G, nc, cols), F32),
                        pltpu.VMEM((G, span, cols), F32),
                        pltpu.VMEM((G, span, cols), BF16),
                        pltpu.VMEM((G, SEL_KEYS, cols), F32),
                        pltpu.VMEM((G, SEL_KEYS, cols), F32),
                        pltpu.VMEM((G, SEL_KEYS, cols), BF16),
                        pltpu.VMEM((G, SEL_KEYS, cols), BF16),
                        pltpu.VMEM((G, cols, 2 * NSA_HEAD_DIM), BF16),
                        pltpu.VMEM((G, 1, cols), F32),
                        pltpu.VMEM((G, 1, cols), F32),
                        pltpu.VMEM((G, 1, cols), F32),
                        pltpu.VMEM((G, 1, cols), F32),
                        pltpu.VMEM((G, V_AUG, cols), F32)],
        compiler_params=_cparams(("arbitrary", "arbitrary")),
        name="nsa",
    )(q, kcm, vct, c2st, tri, ksa, kw, vt, gt)


def _ret_body(q_ref, k_ref, v_ref, g_ref, decay_ref, zeta_ref, xi_ref, gchunk_ref, gn_ref, o_ref, state_ref):
    C, d = RET_CHUNK, RET_HEAD_DIM

    @pl.when(pl.program_id(1) == 0)
    def _():
        state_ref[...] = jnp.zeros_like(state_ref)

    n_chunks = q_ref.shape[1] // C
    for h in range(RET_HEADS):
        hc = slice(h * d, (h + 1) * d)
        state = state_ref[h]
        before = []
        for c in range(n_chunks):
            rows = slice(c * C, (c + 1) * C)
            kz = (k_ref[0, rows, hc].astype(F32) * zeta_ref[h]).astype(BF16)
            before.append(state.astype(BF16))
            state = state * gchunk_ref[h] + _dot_tn(kz, v_ref[0, rows, hc])
        state_ref[h] = state
        for c in range(n_chunks):
            rows = slice(c * C, (c + 1) * C)
            q, k, v = q_ref[0, rows, hc], k_ref[0, rows, hc], v_ref[0, rows, hc]
            s = (_dot_nt(q, k) * decay_ref[h]).astype(BF16)
            o = _dot(s, v) + _dot((q.astype(F32) * xi_ref[h]).astype(BF16), before[c])
            mu = jnp.mean(o, axis=-1, keepdims=True)
            var = jnp.mean(jnp.square(o - mu), axis=-1, keepdims=True)
            o = (o - mu) * lax.rsqrt(var + 1e-5) * gn_ref[h]
            gate = g_ref[0, rows, hc]
            o_ref[0, rows, hc] = (gate * jax.nn.sigmoid(gate) * o).astype(BF16)


def _retention(rq, rk, rv, rg, decay, zeta, xi, gchunk, gn):
    b, t, width = rq.shape
    tok = pl.BlockSpec((1, RET_ROWS, width), lambda bi, c: (bi, c, 0))
    return pl.pallas_call(
        _ret_body,
        grid=(b, t // RET_ROWS),
        in_specs=[tok, tok, tok, tok, _const_spec(decay.shape), _const_spec(zeta.shape), _const_spec(xi.shape),
                  _const_spec(gchunk.shape), _const_spec(gn.shape)],
        out_specs=tok,
        out_shape=jax.ShapeDtypeStruct((b, t, width), BF16),
        scratch_shapes=[pltpu.VMEM((RET_HEADS, RET_HEAD_DIM, RET_HEAD_DIM), F32)],
        compiler_params=_cparams(("arbitrary", "arbitrary")),
        name="retention",
    )(rq, rk, rv, rg, decay, zeta, xi, gchunk, gn)


def _oproj_body(x_ref, yc_ref, yd_ref, w_ref, gpost_ref, fpre_ref, fpost_ref, wg_ref, wu_ref, wd_ref, o_ref):
    m = _dot(yc_ref[...], w_ref[0:NSA_WIDTH, :]) + _dot(yd_ref[...], w_ref[NSA_WIDTH:, :])
    o_ref[...] = _ffn_tile(x_ref[...] + _rms(m, gpost_ref[...]), fpre_ref, fpost_ref, wg_ref, wu_ref, wd_ref)


def _odd_out(x2, yc2, yd2, w, gpost, ffn):
    n, d = x2.shape
    tm = FFN_ROWS

    def row(width):
        return pl.BlockSpec((tm, width), lambda i: (i, 0))

    return pl.pallas_call(
        _oproj_body,
        grid=(n // tm,),
        in_specs=[row(d), row(NSA_WIDTH), row(RET_WIDTH), _weight_spec(w), _const_spec((1, d))]
        + _ffn_specs(ffn),
        out_specs=row(d),
        out_shape=jax.ShapeDtypeStruct((n, d), F32),
        compiler_params=_cparams(("arbitrary",)),
        name="odd_out_ffn",
    )(x2, yc2, yd2, _weight_array(w), gpost, *ffn[1:])


def _odd_in_weight(w_in):
    sizes = [NSA_WIDTH] + [NSA_KV_WIDTH] * 6 + [NSA_BRANCHES * NSA_HEADS] + [RET_WIDTH] * 4
    offs = np.concatenate([[0], np.cumsum(sizes)])
    w_in = w_in.astype(BF16)
    q, kc, vc, ks, vs, kw, vw, gt, rq, rk, rv, rg = [w_in[:, offs[n]:offs[n + 1]] for n in range(len(sizes))]
    w = jnp.concatenate([w_in[:, offs[0]:offs[4]], kw], axis=1)
    wr = w_in[:, offs[8]:offs[12]]
    per_group = NSA_BRANCHES * NSA_HPG
    gates = [jnp.pad(gt[:, g * per_group:(g + 1) * per_group], ((0, 0), (0, GATE_ROWS - per_group)))
             for g in range(NSA_KV_GROUPS)]
    wt = jnp.concatenate([vs, vw] + gates, axis=1).T
    return w, wr, wt


def _rope_tables(positions):
    pos = positions.astype(F32)[..., None]
    half = ROPE_DIM // 2
    inv = 1.0 / (ROPE_THETA ** (jnp.arange(0, ROPE_DIM, 2, dtype=F32) / ROPE_DIM))
    ang = pos * inv
    cs = jnp.concatenate([jnp.cos(ang), jnp.sin(ang)], axis=-1)
    place = np.zeros((ROPE_DIM, 3 * LANES), np.float32)
    base = np.zeros((1, 3 * LANES), np.float32)
    for head0 in range(0, LANES, NSA_HEAD_DIM):
        base[0, head0 + ROPE_DIM:head0 + NSA_HEAD_DIM] = 1.0
        for f in range(half):
            place[f, head0 + f] = 1.0
            place[f, head0 + half + f] = 1.0
            place[half + f, LANES + head0 + f] = -1.0
            place[half + f, 2 * LANES + head0 + half + f] = 1.0
    inv_r = 1.0 / (RET_THETA ** (jnp.arange(0, RET_HEAD_DIM, 2, dtype=F32) / RET_HEAD_DIM))
    ang_r = pos * inv_r
    return cs, jnp.asarray(place, dtype=BF16), jnp.asarray(base), jnp.cos(ang_r), jnp.sin(ang_r)


def _retention_tables():
    H, C, d = RET_HEADS, RET_CHUNK, RET_HEAD_DIM
    log_gamma = np.log1p(-np.exp2(-5.0 - np.arange(H, dtype=np.float64)))
    idx = np.arange(C, dtype=np.float64)
    rel = idx[:, None] - idx[None, :]
    decay = np.where(rel >= 0, np.exp(np.maximum(rel, 0.0)[None] * log_gamma[:, None, None]), 0.0)
    zeta = np.exp((C - 1 - idx)[None, :] * log_gamma[:, None])
    xi = np.exp((idx + 1.0)[None, :] * log_gamma[:, None])
    gchunk = np.exp(C * log_gamma)
    zeta_b = np.broadcast_to(zeta[:, :, None], (H, C, d))
    xi_b = np.broadcast_to(xi[:, :, None], (H, C, d))
    gchunk_b = np.broadcast_to(gchunk[:, None, None], (H, 1, d))
    return tuple(jnp.asarray(a, dtype=F32) for a in (decay, zeta_b, xi_b, gchunk_b))


def _cmp_to_sel_t(t_len):
    nc = t_len // CMP_STRIDE
    n_cmp = (t_len - CMP_BLOCK) // CMP_STRIDE + 1
    n_sel = t_len // SEL_BLOCK
    c_start = np.arange(nc) * CMP_STRIDE
    s_start = np.arange(SEL_PAD) * SEL_BLOCK
    hit = ((c_start[None, :] < s_start[:, None] + SEL_BLOCK) & (c_start[None, :] + CMP_BLOCK > s_start[:, None])
           & (np.arange(nc)[None, :] < n_cmp) & (np.arange(SEL_PAD)[:, None] < n_sel))
    return jnp.asarray(hit.astype(np.float32), dtype=BF16)


def _triangle_biases():
    r = np.arange(Q_BLOCK)[:, None]
    tok = np.arange(NSA_HPG * Q_BLOCK)[None, :] % Q_BLOCK
    causal = np.where(r <= tok, 0.0, NEG_INF)
    leaving = np.where(r > tok, 0.0, NEG_INF)
    return jnp.asarray(np.stack([causal, leaving]), dtype=F32)


def _compress_weights(pos, w1, w2):
    G, dh = NSA_KV_GROUPS, NSA_HEAD_DIM
    pos_rows = jnp.tile(pos, (1, G))
    w1r = w1.astype(BF16).reshape(CMP_BLOCK, dh, CMP_HIDDEN)
    w1_bd = jnp.einsum('ab,lij->laibj', jnp.eye(G, dtype=BF16), w1r).reshape(CMP_BLOCK, G * dh, G * CMP_HIDDEN)
    return pos_rows, w1_bd, w2.astype(BF16)


def _odd_layer(x, positions_tables, ret_tables, c2st, tri, gpre, w_in, cmp_k_pos, cmp_k_w1, cmp_k_w2,
               cmp_v_pos, cmp_v_w1, cmp_v_w2, gn_g, w_out, gpost, ffn):
    b, t, d = x.shape
    w, wr, wt = _odd_in_weight(w_in)
    (q, kc, vc, ksa, kw, vt, gt, rq, rk, rv, rg) = _odd_proj(x, gpre, w, wr, wt, *positions_tables)
    kpos, kw1, kw2 = _compress_weights(cmp_k_pos, cmp_k_w1, cmp_k_w2)
    vpos, vw1, vw2 = _compress_weights(cmp_v_pos, cmp_v_w1, cmp_v_w2)
    kcm, vct = _compress(kc, vc, kpos, vpos, kw1, vw1, kw2, vw2.T)
    yc = _nsa(q, kcm, vct, c2st, tri, ksa, kw, vt, gt)
    yd = _retention(rq, rk, rv, rg, *ret_tables, gn_g.reshape(RET_HEADS, 1, RET_HEAD_DIM))
    out = _odd_out(x.reshape(b * t, d), yc.reshape(b * t, NSA_WIDTH), yd.reshape(b * t, RET_WIDTH),
                   w_out, gpost, ffn)
    return out.reshape(b, t, d)


def kernel(x, positions, ln_mix_pre, ln_mix_post, ln_ffn_pre, ln_ffn_post, ffn_w_gate, ffn_w_up, ffn_w_down,
           ev_w_in, ev_pool_w, ev_pool_scale, ev_sgu_ln_g, ev_sgu_ln_b, ev_sgu_w, ev_sgu_b, ev_w_out,
           od_w_in, od_cmp_k_pos, od_cmp_k_w1, od_cmp_k_w2, od_cmp_v_pos, od_cmp_v_w1, od_cmp_v_w2,
           od_ret_gn_g, od_w_out):
    b, t, d = x.shape
    depth = ln_mix_pre.shape[0]
    rope = _rope_tables(positions)
    ret_tables = _retention_tables()
    c2st = _cmp_to_sel_t(t)
    tri = _triangle_biases()
    wg_all, wu_all, wd_all = ffn_w_gate.astype(BF16), ffn_w_up.astype(BF16), ffn_w_down.astype(BF16)
    ev_in_all, ev_out_all, od_out_all = ev_w_in.astype(BF16), ev_w_out.astype(BF16), od_w_out.astype(BF16)
    for layer in range(depth):
        gpre = ln_mix_pre[layer].reshape(1, d)
        gpost = ln_mix_post[layer].reshape(1, d)
        ffn = (layer, ln_ffn_pre[layer].reshape(1, d), ln_ffn_post[layer].reshape(1, d), wg_all, wu_all, wd_all)
        if layer % 2 == 0:
            e = layer // 2
            x = _even_layer(x, gpre, (ev_in_all, e), ev_pool_w[e].astype(BF16),
                            ev_pool_scale[e].reshape(1, POOL_WIDTH), ev_sgu_ln_g[e].reshape(1, SGU_WIDTH),
                            ev_sgu_ln_b[e].reshape(1, SGU_WIDTH), ev_sgu_w[e], ev_sgu_b[e].T,
                            (ev_out_all, e), gpost, ffn)
        else:
            o = layer // 2
            x = _odd_layer(x, rope, ret_tables, c2st, tri, gpre, od_w_in[o], od_cmp_k_pos[o], od_cmp_k_w1[o],
                           od_cmp_k_w2[o], od_cmp_v_pos[o], od_cmp_v_w1[o], od_cmp_v_w2[o], od_ret_gn_g[o],
                           (od_out_all, o), gpost, ffn)
    return x
```

```python
import functools

import numpy as np
import jax
import jax.numpy as jnp
from jax import lax
from jax.experimental import pallas as pl
from jax.experimental.pallas import tpu as pltpu

F32 = jnp.float32
BF16 = jnp.bfloat16

D_MODEL = 1024
POOL_WINDOWS = (2, 4, 8, 16)
POOL_GROUPS = 4
POOL_WIDTH = D_MODEL // 2
POOL_GDIM = POOL_WIDTH // POOL_GROUPS
POOL_HIST = 16
SGU_GROUPS = 4
SGU_WIDTH = D_MODEL // 2
SGU_GDIM = SGU_WIDTH // SGU_GROUPS
SGU_CHUNK = 128
EVEN_IN = POOL_WIDTH + 2 * SGU_WIDTH
NSA_HEADS = 8
NSA_KV_GROUPS = 2
NSA_HPG = NSA_HEADS // NSA_KV_GROUPS
NSA_HEAD_DIM = 64
NSA_WIDTH = NSA_HEADS * NSA_HEAD_DIM
NSA_GROUP_WIDTH = NSA_HPG * NSA_HEAD_DIM
NSA_KV_WIDTH = NSA_KV_GROUPS * NSA_HEAD_DIM
NSA_BRANCHES = 3
GATE_ROWS = 16
CMP_BLOCK = 32
CMP_STRIDE = 16
CMP_HIDDEN = 128
SEL_BLOCK = 64
SEL_TOPK = 16
SEL_PAD = 64
WINDOW = 512
Q_BLOCK = 256
FORCE_SCORE = 1.0e4
ROPE_THETA = 500000.0
ROPE_DIM = NSA_HEAD_DIM // 4
RET_HEADS = 4
RET_HEAD_DIM = 128
RET_WIDTH = RET_HEADS * RET_HEAD_DIM
RET_CHUNK = 128
RET_THETA = 10000.0
ODD_MIX = NSA_WIDTH + RET_WIDTH
NEG_INF = -1.0e30
LOG2_E = 1.4426950408889634
LANES = 128
SUBLANES = 8

VMEM_LIMIT = 56 * 1024 * 1024
FFN_ROWS = 512
FFN_CHUNKS = 11
EVEN_ROWS = 512
PROJ_ROWS = 512
SEL_KEYS = Q_BLOCK
RANK_CHUNK = 8
RET_ROWS = 512


def _cparams(sem):
    return pltpu.CompilerParams(dimension_semantics=sem, vmem_limit_bytes=VMEM_LIMIT)


def _const_spec(shape):
    nd = len(shape)
    return pl.BlockSpec(shape, lambda *_: (0,) * nd, pipeline_mode=pl.Buffered(1))


def _dot(a, b):
    return jnp.dot(a, b, preferred_element_type=F32)


def _dot_nt(a, b):
    return lax.dot_general(a, b, (((1,), (1,)), ((), ())), preferred_element_type=F32)


def _dot_tn(a, b):
    return lax.dot_general(a, b, (((0,), (0,)), ((), ())), preferred_element_type=F32)


def _rms(x, g, eps=1e-6):
    return x * lax.rsqrt(jnp.mean(x * x, axis=-1, keepdims=True) + eps) * g


def _eye(n):
    return jnp.where(lax.broadcasted_iota(jnp.int32, (n, n), 0) == lax.broadcasted_iota(jnp.int32, (n, n), 1),
                     1.0, 0.0).astype(BF16)


def _ffn_tile(x, gpre_ref, gpost_ref, wg_ref, wu_ref, wd_ref):
    h = _rms(x, gpre_ref[...]).astype(BF16)
    f_total = wg_ref.shape[1]
    fc = f_total // FFN_CHUNKS
    acc = None
    for c in range(FFN_CHUNKS):
        gate = _dot(h, wg_ref[:, c * fc:(c + 1) * fc])
        up = _dot(h, wu_ref[:, c * fc:(c + 1) * fc])
        act = (gate * jax.nn.sigmoid(gate) * up).astype(BF16)
        part = _dot(act, wd_ref[c * fc:(c + 1) * fc, :])
        acc = part if acc is None else acc + part
    return x + _rms(acc, gpost_ref[...])


def _layer_spec(stacked_shape, layer):
    nd = len(stacked_shape) - 1
    return pl.BlockSpec((None,) + tuple(stacked_shape[1:]), lambda *_: (layer,) + (0,) * nd,
                        pipeline_mode=pl.Buffered(1))


def _weight_spec(w):
    return _layer_spec(w[0].shape, w[1]) if isinstance(w, tuple) else _const_spec(w.shape)


def _weight_array(w):
    return w[0] if isinstance(w, tuple) else w


def _ffn_specs(ffn):
    layer, fpre, fpost, wg, wu, wd = ffn
    return [_const_spec(fpre.shape), _const_spec(fpost.shape), _layer_spec(wg.shape, layer),
            _layer_spec(wu.shape, layer), _layer_spec(wd.shape, layer)]


def _even_body(x_ref, gpre_ref, win_ref, poolw_ref, pscale_ref, lng_ref, lnb_ref, sguw_ref, sgub_ref,
               wout_ref, gpost_ref, fpre_ref, fpost_ref, wg_ref, wu_ref, wd_ref, o_ref, hist_ref):
    tt = x_ref.shape[1]
    j = pl.program_id(1)

    @pl.when(j == 0)
    def _():
        hist_ref[0:POOL_HIST, :] = jnp.zeros((POOL_HIST, POOL_WIDTH), F32)

    x = x_ref[0]
    h = _rms(x, gpre_ref[...]).astype(BF16)
    z = _dot(h, win_ref[...])
    a = z[:, :POOL_WIDTH]
    u = z[:, POOL_WIDTH:POOL_WIDTH + SGU_WIDTH]
    v = z[:, POOL_WIDTH + SGU_WIDTH:]

    hist_ref[POOL_HIST:POOL_HIST + tt, :] = a
    t_pos = j * tt + lax.broadcasted_iota(jnp.int32, (tt, 1), 0)
    ya = []
    for gi, w in enumerate(POOL_WINDOWS):
        cols = slice(gi * POOL_GDIM, (gi + 1) * POOL_GDIM)
        win_sum = a[:, cols]
        for s in range(1, w):
            win_sum = win_sum + hist_ref[POOL_HIST - s:POOL_HIST - s + tt, cols]
        cnt = jnp.minimum(t_pos + 1, w).astype(F32)
        diff = (win_sum / cnt - a[:, cols]).astype(BF16)
        ya.append(_dot(diff, poolw_ref[gi]))
    ya = jnp.concatenate(ya, axis=1) * pscale_ref[...]
    hist_ref[0:POOL_HIST, :] = hist_ref[tt:tt + POOL_HIST, :]

    ug = jax.nn.gelu(u)
    vg = jax.nn.gelu(v)
    mu = jnp.mean(vg, axis=-1, keepdims=True)
    var = jnp.mean(jnp.square(vg - mu), axis=-1, keepdims=True)
    vn = ((vg - mu) * lax.rsqrt(var + 1e-5) * lng_ref[...] + lnb_ref[...]).astype(BF16)
    r_i = lax.broadcasted_iota(jnp.int32, (SGU_CHUNK, SGU_CHUNK), 0)
    c_i = lax.broadcasted_iota(jnp.int32, (SGU_CHUNK, SGU_CHUNK), 1)
    yb = []
    for g in range(SGU_GROUPS):
        cols = slice(g * SGU_GDIM, (g + 1) * SGU_GDIM)
        wm = jnp.where(r_i >= c_i, sguw_ref[g], 0.0).astype(BF16)
        bcol = sgub_ref[:, g:g + 1]
        parts = []
        for c in range(tt // SGU_CHUNK):
            rows = slice(c * SGU_CHUNK, (c + 1) * SGU_CHUNK)
            parts.append(ug[rows, cols] * (_dot(wm, vn[rows, cols]) + bcol))
        yb.append(jnp.concatenate(parts, axis=0))
    yb = jnp.concatenate(yb, axis=1)

    m = _dot(ya.astype(BF16), wout_ref[0:POOL_WIDTH, :]) + _dot(yb.astype(BF16), wout_ref[POOL_WIDTH:, :])
    o_ref[0] = _ffn_tile(x + _rms(m, gpost_ref[...]), fpre_ref, fpost_ref, wg_ref, wu_ref, wd_ref)


def _even_layer(x, gpre, win, poolw, pscale, lng, lnb, sguw, sgub_t, wout, gpost, ffn):
    b, t, d = x.shape
    tt = EVEN_ROWS
    row = pl.BlockSpec((1, tt, d), lambda bi, j: (bi, j, 0))
    return pl.pallas_call(
        _even_body,
        grid=(b, t // tt),
        in_specs=[row, _const_spec((1, d)), _weight_spec(win), _const_spec(poolw.shape),
                  _const_spec(pscale.shape), _const_spec(lng.shape), _const_spec(lnb.shape),
                  _const_spec(sguw.shape), _const_spec(sgub_t.shape), _weight_spec(wout),
                  _const_spec((1, d))] + _ffn_specs(ffn),
        out_specs=row,
        out_shape=jax.ShapeDtypeStruct((b, t, d), F32),
        scratch_shapes=[pltpu.VMEM((POOL_HIST + tt, POOL_WIDTH), F32)],
        compiler_params=_cparams(("arbitrary", "arbitrary")),
        name="even_layer",
    )(x, gpre, _weight_array(win), poolw, pscale, lng, lnb, sguw, sgub_t, _weight_array(wout), gpost, *ffn[1:])


_C_Q = 0
_C_KC = _C_Q + NSA_WIDTH
_C_VC = _C_KC + NSA_KV_WIDTH
_C_KS = _C_VC + NSA_KV_WIDTH
_C_KW = _C_KS + NSA_KV_WIDTH
_C_RQ = _C_KW + NSA_KV_WIDTH
_C_RK = _C_RQ + RET_WIDTH
_C_RV = _C_RK + RET_WIDTH
_C_RG = _C_RV + RET_WIDTH
ODD_COLS = _C_RG + RET_WIDTH
_R_VS = 0
_R_VW = _R_VS + NSA_KV_WIDTH
_R_GT = _R_VW + NSA_KV_WIDTH
ODD_TROWS = _R_GT + NSA_KV_GROUPS * GATE_ROWS
V_AUG = NSA_HEAD_DIM + 16
VT_ROWS = 2 * NSA_KV_GROUPS * V_AUG


def _rope_nsa(z, c, sa, sb):
    half = ROPE_DIM // 2
    return z * c + pltpu.roll(z, LANES - half, 1) * sa + pltpu.roll(z, half, 1) * sb


def _split3(x):
    a = x.astype(BF16)
    r = x - a.astype(F32)
    b = r.astype(BF16)
    return a, b, (r - b.astype(F32)).astype(BF16)


def _proj_body(x_ref, gpre_ref, w_ref, wr_ref, wt_ref, cs_ref, place_ref, base_ref, cr_ref, sr_ref,
               q_ref, kc_ref, vc_ref, ksa_ref, kw_ref, vt_ref, gt_ref, rq_ref, rk_ref, rv_ref, rg_ref):
    tm = x_ref.shape[1]
    j = pl.program_id(1)
    x = x_ref[0]
    h = _rms(x, gpre_ref[...]).astype(BF16)
    tables = base_ref[...]
    for part in _split3(cs_ref[0]):
        tables = tables + _dot(part, place_ref[...])
    nc, nsa, nsb = tables[:, 0:LANES], tables[:, LANES:2 * LANES], tables[:, 2 * LANES:3 * LANES]
    cr, sr = cr_ref[0], sr_ref[0]
    rc = jnp.concatenate([cr, cr], axis=1)
    rs = jnp.concatenate([-sr, sr], axis=1)

    def cols(start, width):
        if start >= _C_RQ:
            return _dot(h, wr_ref[:, start - _C_RQ:start - _C_RQ + width])
        return _dot(h, w_ref[:, start:start + width])

    zq = cols(_C_Q, NSA_WIDTH)
    q_scale = NSA_HEAD_DIM ** -0.5 * LOG2_E
    for s in range(NSA_WIDTH // LANES):
        sl = slice(s * LANES, (s + 1) * LANES)
        q_ref[0, :, sl] = (_rope_nsa(zq[:, sl], nc, nsa, nsb) * q_scale).astype(BF16)

    zkv = cols(_C_KC, _C_RQ - _C_KC)
    kc_ref[0] = _rope_nsa(zkv[:, _C_KC - _C_KC:_C_VC - _C_KC], nc, nsa, nsb)
    vc_ref[0] = zkv[:, _C_VC - _C_KC:_C_KS - _C_KC]
    ks = _rope_nsa(zkv[:, _C_KS - _C_KC:_C_KW - _C_KC], nc, nsa, nsb)
    kw = _rope_nsa(zkv[:, _C_KW - _C_KC:_C_RQ - _C_KC], nc, nsa, nsb)
    t_pos = j * tm + lax.broadcasted_iota(jnp.int32, (tm, SEL_PAD), 0)
    blk = lax.broadcasted_iota(jnp.int32, (tm, SEL_PAD), 1)
    onehot = jnp.where(t_pos // SEL_BLOCK == blk, 1.0, 0.0).astype(BF16)
    for g in range(NSA_KV_GROUPS):
        sl = slice(g * NSA_HEAD_DIM, (g + 1) * NSA_HEAD_DIM)
        ksa_ref[0, g] = jnp.concatenate([ks[:, sl].astype(BF16), onehot], axis=1)
        kw_ref[0, g] = kw[:, sl].astype(BF16)

    zt = _dot_nt(wt_ref[...], h)
    ones_rows = jnp.where(lax.broadcasted_iota(jnp.int32, (V_AUG - NSA_HEAD_DIM, tm), 0) == 0, 1.0, 0.0).astype(BF16)
    for k in range(2 * NSA_KV_GROUPS):
        vt_ref[0, k * V_AUG:k * V_AUG + NSA_HEAD_DIM] = zt[k * NSA_HEAD_DIM:(k + 1) * NSA_HEAD_DIM].astype(BF16)
        vt_ref[0, k * V_AUG + NSA_HEAD_DIM:(k + 1) * V_AUG] = ones_rows
    gt_ref[0] = jax.nn.sigmoid(zt[_R_GT:])

    k_scale = RET_HEAD_DIM ** -0.5
    zrq = cols(_C_RQ, RET_WIDTH)
    zrk = cols(_C_RK, RET_WIDTH)
    for hh in range(RET_HEADS):
        sl = slice(hh * RET_HEAD_DIM, (hh + 1) * RET_HEAD_DIM)
        zq_h, zk_h = zrq[:, sl], zrk[:, sl]
        rq_ref[0, :, sl] = (zq_h * rc + pltpu.roll(zq_h, RET_HEAD_DIM // 2, 1) * rs).astype(BF16)
        rk_ref[0, :, sl] = ((zk_h * rc + pltpu.roll(zk_h, RET_HEAD_DIM // 2, 1) * rs) * k_scale).astype(BF16)
    rv_ref[0] = cols(_C_RV, RET_WIDTH).astype(BF16)
    rg_ref[0] = cols(_C_RG, RET_WIDTH)


def _odd_proj(x, gpre, w, wr, wt, cs, place, base, cr, sr):
    b, t, d = x.shape
    tm = PROJ_ROWS
    G = NSA_KV_GROUPS

    def row(width):
        return pl.BlockSpec((1, tm, width), lambda bi, j: (bi, j, 0))

    def grp(width):
        return pl.BlockSpec((1, G, tm, width), lambda bi, j: (bi, 0, j, 0))

    def feat(rows):
        return pl.BlockSpec((1, rows, tm), lambda bi, j: (bi, 0, j))

    out_shape = [
        jax.ShapeDtypeStruct((b, t, NSA_WIDTH), BF16),
        jax.ShapeDtypeStruct((b, t, LANES), F32),
        jax.ShapeDtypeStruct((b, t, LANES), F32),
        jax.ShapeDtypeStruct((b, G, t, LANES), BF16),
        jax.ShapeDtypeStruct((b, G, t, NSA_HEAD_DIM), BF16),
        jax.ShapeDtypeStruct((b, VT_ROWS, t), BF16),
        jax.ShapeDtypeStruct((b, G * GATE_ROWS, t), F32),
        jax.ShapeDtypeStruct((b, t, RET_WIDTH), BF16),
        jax.ShapeDtypeStruct((b, t, RET_WIDTH), BF16),
        jax.ShapeDtypeStruct((b, t, RET_WIDTH), BF16),
        jax.ShapeDtypeStruct((b, t, RET_WIDTH), F32),
    ]
    out_specs = [row(NSA_WIDTH), row(LANES), row(LANES), grp(LANES), grp(NSA_HEAD_DIM), feat(VT_ROWS),
                 feat(G * GATE_ROWS), row(RET_WIDTH), row(RET_WIDTH), row(RET_WIDTH), row(RET_WIDTH)]
    return pl.pallas_call(
        _proj_body,
        grid=(b, t // tm),
        in_specs=[row(d), _const_spec((1, d)), _const_spec(w.shape), _const_spec(wr.shape), _const_spec(wt.shape),
                  row(cs.shape[2]),
                  _const_spec(place.shape), _const_spec(base.shape), row(cr.shape[2]), row(sr.shape[2])],
        out_specs=out_specs,
        out_shape=out_shape,
        compiler_params=_cparams(("arbitrary", "arbitrary")),
        name="odd_proj",
    )(x, gpre, w, wr, wt, cs, place, base, cr, sr)


def _compress_body(k_ref, v_ref, kpos_ref, vpos_ref, kw1_ref, vw1_ref, kw2_ref, vw2t_ref, kcm_ref, vct_ref):
    nc = k_ref.shape[1] // CMP_STRIDE
    half = CMP_BLOCK // CMP_STRIDE

    def hidden(x_ref, pos_ref, w1_ref):
        parts = [None] * half
        for off in range(CMP_STRIDE):
            tok = x_ref[0, pl.ds(off, nc, stride=CMP_STRIDE), :]
            for h in range(half):
                l = h * CMP_STRIDE + off
                term = _dot((tok + pos_ref[l:l + 1, :]).astype(BF16), w1_ref[l])
                parts[h] = term if parts[h] is None else parts[h] + term
        pre = parts[0]
        for h in range(1, half):
            pre = pre + pltpu.roll(parts[h], nc - h, 0)
        return jax.nn.gelu(pre).astype(BF16)

    hk = hidden(k_ref, kpos_ref, kw1_ref)
    hv = hidden(v_ref, vpos_ref, vw1_ref)
    for g in range(NSA_KV_GROUPS):
        sl = slice(g * CMP_HIDDEN, (g + 1) * CMP_HIDDEN)
        kcm_ref[0, g] = _dot(hk[:, sl], kw2_ref[...]).astype(BF16)
        vct_ref[0, g] = _dot_nt(vw2t_ref[...], hv[:, sl]).astype(BF16)


def _compress(k, v, kpos, vpos, kw1, vw1, kw2, vw2t):
    b, t, width = k.shape
    nc = t // CMP_STRIDE
    G = NSA_KV_GROUPS
    row = pl.BlockSpec((1, t, width), lambda bi: (bi, 0, 0))
    return pl.pallas_call(
        _compress_body,
        grid=(b,),
        in_specs=[row, row, _const_spec(kpos.shape), _const_spec(vpos.shape), _const_spec(kw1.shape),
                  _const_spec(vw1.shape), _const_spec(kw2.shape), _const_spec(vw2t.shape)],
        out_specs=[pl.BlockSpec((1, G, nc, NSA_HEAD_DIM), lambda bi: (bi, 0, 0, 0)),
                   pl.BlockSpec((1, G, NSA_HEAD_DIM, nc), lambda bi: (bi, 0, 0, 0))],
        out_shape=[jax.ShapeDtypeStruct((b, G, nc, NSA_HEAD_DIM), BF16),
                   jax.ShapeDtypeStruct((b, G, NSA_HEAD_DIM, nc), BF16)],
        compiler_params=_cparams(("arbitrary",)),
        name="compress",
    )(k, v, kpos, vpos, kw1, vw1, kw2, vw2t)


def _nsa_body(q_ref, kcm_ref, vct_ref, c2st_ref, tri_ref, ksa_ref, kw_ref, vt_ref, gt_ref, o_ref,
              score_ref, rank_ref, wbias_ref, sc_ref, sw_ref, pw_ref, sa_ref, sb_ref, pa_ref, pb_ref, lhs_ref,
              m_ref, mta_ref, mtb_ref, mw_ref, acc_ref, *, top_k):
    G, M, Q, dh = NSA_KV_GROUPS, NSA_HPG, Q_BLOCK, NSA_HEAD_DIM
    cols = M * Q
    nc = kcm_ref.shape[2]
    i = pl.program_id(1)
    t0 = i * Q
    tq = t0 + (lax.broadcasted_iota(jnp.int32, (1, cols), 1) & (Q - 1))
    groups = range(G)

    q_heads, q_rows = [], []
    for g in groups:
        qb = q_ref[0, :, g * NSA_GROUP_WIDTH:(g + 1) * NSA_GROUP_WIDTH]
        q_heads.append([qb[:, m * dh:(m + 1) * dh] for m in range(M)])
        q_rows.append(jnp.concatenate(q_heads[g], axis=0))

    n_idx = lax.broadcasted_iota(jnp.int32, (nc, 1), 0)
    cmp_ok = (n_idx * CMP_STRIDE + (CMP_BLOCK - 1) <= tq) & (n_idx < nc - 1)
    j_idx = lax.broadcasted_iota(jnp.int32, (SEL_PAD, Q), 0)
    cur = (t0 + lax.broadcasted_iota(jnp.int32, (SEL_PAD, Q), 1)) // SEL_BLOCK
    forced = (j_idx == 0) | (j_idx == cur) | (j_idx == cur - 1)
    valid_tok = (tq >= CMP_BLOCK - 1).astype(F32)
    o_cmp = []
    for g in groups:
        sc = sc_ref.at[g]
        sc[...] = jnp.where(cmp_ok, _dot_nt(kcm_ref[0, g], q_rows[g]), NEG_INF)
        sc[...] = jnp.exp2(sc[...] - jnp.max(sc[...], axis=0, keepdims=True))
        norm = valid_tok / jnp.sum(sc[...], axis=0, keepdims=True)
        o_cmp.append(_dot(vct_ref[0, g], sc[...].astype(BF16)) * norm)
        p_grp = sc[:, 0:Q] * norm[:, 0:Q]
        for m in range(1, M):
            p_grp = p_grp + sc[:, m * Q:(m + 1) * Q] * norm[:, m * Q:(m + 1) * Q]
        p_hi = p_grp.astype(BF16)
        p_lo = (p_grp - p_hi.astype(F32)).astype(BF16)
        imp = _dot(c2st_ref[...], p_hi) + _dot(c2st_ref[...], p_lo)
        score_ref[g] = jnp.where(j_idx <= cur, jnp.where(forced, FORCE_SCORE, imp), -1.0)

    def v_aug(branch, g, k0, n):
        r0 = (branch * G + g) * V_AUG
        return vt_ref[0, r0:r0 + V_AUG, pl.ds(k0, n)]

    span = WINDOW + Q
    w0 = pl.multiple_of(jnp.maximum(t0 - WINDOW, 0), Q)
    n_wb = span // Q
    diag_b = jnp.minimum(i, n_wb - 1)
    tri_causal, tri_leaving = tri_ref[0], tri_ref[1]
    for wb in range(n_wb):
        blk_bias = jnp.where(wb == diag_b, tri_causal, jnp.where(wb > diag_b, NEG_INF, 0.0))
        if wb == 0:
            blk_bias = jnp.where(i >= n_wb - 1, tri_leaving, blk_bias)
        wbias_ref[wb * Q:(wb + 1) * Q, :] = blk_bias
    for g in groups:
        s_win = _dot_nt(kw_ref[0, g, pl.ds(w0, span), :], q_rows[g]) + wbias_ref[...]
        sw_ref[g] = s_win
        mw_ref[g] = jnp.max(s_win, axis=0, keepdims=True)

    n_causal = (t0 + Q) // SEL_BLOCK
    rank_ref[...] = jnp.zeros(rank_ref.shape, jnp.int32)
    row_in_tile = lax.broadcasted_iota(jnp.int32, (SUBLANES, LANES), 0)
    for c0 in range(0, SEL_PAD, RANK_CHUNK):
        @pl.when(c0 < n_causal)
        def _(c0=c0):
            for g in groups:
                for l0 in range(0, Q, LANES):
                    ln = slice(l0, l0 + LANES)
                    others = [score_ref[g, k:k + 1, ln] for k in range(c0, c0 + RANK_CHUNK)]
                    for r0 in range(0, SEL_PAD, SUBLANES):
                        mine = score_ref[g, r0:r0 + SUBLANES, ln]
                        count = rank_ref[g, r0:r0 + SUBLANES, ln]
                        for k, other in zip(range(c0, c0 + RANK_CHUNK), others):
                            if r0 > k:
                                beats = other >= mine
                            elif r0 + SUBLANES <= k:
                                beats = other > mine
                            else:
                                beats = (other > mine) | ((other == mine) & (k < r0 + row_in_tile))
                            count = count + beats.astype(jnp.int32)
                        rank_ref[g, r0:r0 + SUBLANES, ln] = count

    eye_q = _eye(Q)
    for g in groups:
        chosen = jnp.where((rank_ref[g] < top_k) & (score_ref[g] >= 0.0), 1.0, 0.0).astype(BF16)
        chosen_t = _dot_nt(eye_q, chosen)
        bias = jnp.where(chosen_t > 0.5, 0.0, NEG_INF).astype(BF16)
        for m in range(M):
            lhs_ref[g, m * Q:(m + 1) * Q, :] = jnp.concatenate([q_heads[g][m], bias], axis=1)
        m_ref[g] = jnp.full((1, cols), NEG_INF, F32)
        acc_ref[g] = jnp.zeros((V_AUG, cols), F32)

    buf_a = (sa_ref, mta_ref, pa_ref)
    buf_b = (sb_ref, mtb_ref, pb_ref)

    def put_scores(buf, g, kt):
        s_dst, max_dst, _ = buf
        k0 = pl.multiple_of(kt * SEL_KEYS, SEL_KEYS)
        s = _dot_nt(ksa_ref[0, g, pl.ds(k0, SEL_KEYS), :], lhs_ref[g])
        s_dst[g] = s
        max_dst[g] = jnp.max(s, axis=0, keepdims=True)

    def sel_update(buf, g, kt, bias=None):
        src, tile_max, p_dst = buf
        k0 = pl.multiple_of(kt * SEL_KEYS, SEL_KEYS)
        m_i = m_ref[g]
        if bias is None:
            m_new = jnp.maximum(m_i, tile_max[g])
            p_dst[g] = jnp.exp2(src[g] - m_new).astype(BF16)
        else:
            m_new = jnp.maximum(m_i, jnp.max(src[g] + bias, axis=0, keepdims=True))
            p_dst[g] = jnp.exp2(src[g] + bias - m_new).astype(BF16)
        acc_ref[g] = jnp.exp2(m_i - m_new) * acc_ref[g] + _dot(v_aug(0, g, k0, SEL_KEYS), p_dst[g])
        m_ref[g] = m_new

    n_full = t0 // SEL_KEYS
    odd = n_full & 1
    for g in groups:
        put_scores(buf_a, g, 0)

    o_win = []
    for g in groups:
        pw_ref[g] = jnp.exp2(sw_ref[g] - mw_ref[g]).astype(BF16)
        acc_win = _dot(v_aug(1, g, w0, span), pw_ref[g])
        o_win.append(acc_win[0:dh] / acc_win[dh:dh + 1])

    @pl.when(odd == 1)
    def _():
        for g in groups:
            sel_update(buf_a, g, 0)
            put_scores(buf_a, g, 1)

    def sel_pair(pair, _):
        ta = odd + 2 * pair
        for g in groups:
            put_scores(buf_b, g, ta + 1)
        for g in groups:
            sel_update(buf_a, g, ta)
        for g in groups:
            put_scores(buf_a, g, ta + 2)
        for g in groups:
            sel_update(buf_b, g, ta + 1)
        return 0

    lax.fori_loop(0, n_full // 2, sel_pair, 0)
    o_sel = []
    for g in groups:
        sel_update(buf_a, g, n_full, bias=tri_causal)
        o_sel.append(acc_ref[g, 0:dh, :] / acc_ref[g, dh:dh + 1, :])

    pieces = []
    for g in groups:
        gt = gt_ref[0, g * GATE_ROWS:(g + 1) * GATE_ROWS, :]
        for m in range(M):
            c = slice(m * Q, (m + 1) * Q)
            r = NSA_BRANCHES * m
            mixed = (o_cmp[g][:, c] * gt[r:r + 1] + o_sel[g][:, c] * gt[r + 1:r + 2]
                     + o_win[g][:, c] * gt[r + 2:r + 3])
            pieces.append(_dot_nt(eye_q, mixed.astype(BF16)))
    o_ref[0] = jnp.concatenate(pieces, axis=1).astype(BF16)


def _nsa(q, kcm, vct, c2st, tri, ksa, kw, vt, gt):
    b, t, _ = q.shape
    G = NSA_KV_GROUPS
    nc = kcm.shape[2]
    n_sel = t // SEL_BLOCK
    assert n_sel <= SEL_PAD and t >= WINDOW + Q_BLOCK and t % SEL_KEYS == 0
    top_k = min(SEL_TOPK, n_sel)
    cols, span = NSA_HPG * Q_BLOCK, WINDOW + Q_BLOCK

    def per_batch(shape):
        nd = len(shape)
        return pl.BlockSpec((1,) + shape, lambda bi, i: (bi,) + (0,) * nd)

    token_rows = pl.BlockSpec((1, Q_BLOCK, NSA_WIDTH), lambda bi, i: (bi, i, 0))
    return pl.pallas_call(
        functools.partial(_nsa_body, top_k=top_k),
        grid=(b, t // Q_BLOCK),
        in_specs=[token_rows, per_batch((G, nc, NSA_HEAD_DIM)), per_batch((G, NSA_HEAD_DIM, nc)),
                  _const_spec(c2st.shape), _const_spec(tri.shape), per_batch((G, t, LANES)), per_batch((G, t, NSA_HEAD_DIM)),
                  per_batch((vt.shape[1], t)),
                  pl.BlockSpec((1, G * GATE_ROWS, Q_BLOCK), lambda bi, i: (bi, 0, i))],
        out_specs=token_rows,
        out_shape=jax.ShapeDtypeStruct((b, t, NSA_WIDTH), BF16),
        scratch_shapes=[pltpu.VMEM((G, SEL_PAD, Q_BLOCK), F32),
                        pltpu.VMEM((G, SEL_PAD, Q_BLOCK), jnp.int32),
                        pltpu.VMEM((span, cols), F32),
                        pltpu.VMEM((G, nc, cols), F32),
                        pltpu.VMEM((G, span, cols), F32),
                        pltpu.VMEM((G, span, cols), BF16),
                        pltpu.VMEM((G, SEL_KEYS, cols), F32),
                        pltpu.VMEM((G, SEL_KEYS, cols), F32),
                        pltpu.VMEM((G, SEL_KEYS, cols), BF16),
                        pltpu.VMEM((G, SEL_KEYS, cols), BF16),
                        pltpu.VMEM((G, cols, 2 * NSA_HEAD_DIM), BF16),
                        pltpu.VMEM((G, 1, cols), F32),
                        pltpu.VMEM((G, 1, cols), F32),
                        pltpu.VMEM((G, 1, cols), F32),
                        pltpu.VMEM((G, 1, cols), F32),
                        pltpu.VMEM((G, V_AUG, cols), F32)],
        compiler_params=_cparams(("arbitrary", "arbitrary")),
        name="nsa",
    )(q, kcm, vct, c2st, tri, ksa, kw, vt, gt)


def _ret_body(q_ref, k_ref, v_ref, g_ref, decay_ref, zeta_ref, xi_ref, gchunk_ref, gn_ref, o_ref, state_ref):
    C, d = RET_CHUNK, RET_HEAD_DIM

    @pl.when(pl.program_id(1) == 0)
    def _():
        state_ref[...] = jnp.zeros_like(state_ref)

    n_chunks = q_ref.shape[1] // C
    for h in range(RET_HEADS):
        hc = slice(h * d, (h + 1) * d)
        state = state_ref[h]
        before = []
        for c in range(n_chunks):
            rows = slice(c * C, (c + 1) * C)
            kz = (k_ref[0, rows, hc].astype(F32) * zeta_ref[h]).astype(BF16)
            before.append(state.astype(BF16))
            state = state * gchunk_ref[h] + _dot_tn(kz, v_ref[0, rows, hc])
        state_ref[h] = state
        for c in range(n_chunks):
            rows = slice(c * C, (c + 1) * C)
            q, k, v = q_ref[0, rows, hc], k_ref[0, rows, hc], v_ref[0, rows, hc]
            s = (_dot_nt(q, k) * decay_ref[h]).astype(BF16)
            o = _dot(s, v) + _dot((q.astype(F32) * xi_ref[h]).astype(BF16), before[c])
            mu = jnp.mean(o, axis=-1, keepdims=True)
            var = jnp.mean(jnp.square(o - mu), axis=-1, keepdims=True)
            o = (o - mu) * lax.rsqrt(var + 1e-5) * gn_ref[h]
            gate = g_ref[0, rows, hc]
            o_ref[0, rows, hc] = (gate * jax.nn.sigmoid(gate) * o).astype(BF16)


def _retention(rq, rk, rv, rg, decay, zeta, xi, gchunk, gn):
    b, t, width = rq.shape
    tok = pl.BlockSpec((1, RET_ROWS, width), lambda bi, c: (bi, c, 0))
    return pl.pallas_call(
        _ret_body,
        grid=(b, t // RET_ROWS),
        in_specs=[tok, tok, tok, tok, _const_spec(decay.shape), _const_spec(zeta.shape), _const_spec(xi.shape),
                  _const_spec(gchunk.shape), _const_spec(gn.shape)],
        out_specs=tok,
        out_shape=jax.ShapeDtypeStruct((b, t, width), BF16),
        scratch_shapes=[pltpu.VMEM((RET_HEADS, RET_HEAD_DIM, RET_HEAD_DIM), F32)],
        compiler_params=_cparams(("arbitrary", "arbitrary")),
        name="retention",
    )(rq, rk, rv, rg, decay, zeta, xi, gchunk, gn)


def _oproj_body(x_ref, yc_ref, yd_ref, w_ref, gpost_ref, fpre_ref, fpost_ref, wg_ref, wu_ref, wd_ref, o_ref):
    m = _dot(yc_ref[...], w_ref[0:NSA_WIDTH, :]) + _dot(yd_ref[...], w_ref[NSA_WIDTH:, :])
    o_ref[...] = _ffn_tile(x_ref[...] + _rms(m, gpost_ref[...]), fpre_ref, fpost_ref, wg_ref, wu_ref, wd_ref)


def _odd_out(x2, yc2, yd2, w, gpost, ffn):
    n, d = x2.shape
    tm = FFN_ROWS

    def row(width):
        return pl.BlockSpec((tm, width), lambda i: (i, 0))

    return pl.pallas_call(
        _oproj_body,
        grid=(n // tm,),
        in_specs=[row(d), row(NSA_WIDTH), row(RET_WIDTH), _weight_spec(w), _const_spec((1, d))]
        + _ffn_specs(ffn),
        out_specs=row(d),
        out_shape=jax.ShapeDtypeStruct((n, d), F32),
        compiler_params=_cparams(("arbitrary",)),
        name="odd_out_ffn",
    )(x2, yc2, yd2, _weight_array(w), gpost, *ffn[1:])


def _odd_in_weight(w_in):
    sizes = [NSA_WIDTH] + [NSA_KV_WIDTH] * 6 + [NSA_BRANCHES * NSA_HEADS] + [RET_WIDTH] * 4
    offs = np.concatenate([[0], np.cumsum(sizes)])
    w_in = w_in.astype(BF16)
    q, kc, vc, ks, vs, kw, vw, gt, rq, rk, rv, rg = [w_in[:, offs[n]:offs[n + 1]] for n in range(len(sizes))]
    w = jnp.concatenate([w_in[:, offs[0]:offs[4]], kw], axis=1)
    wr = w_in[:, offs[8]:offs[12]]
    per_group = NSA_BRANCHES * NSA_HPG
    gates = [jnp.pad(gt[:, g * per_group:(g + 1) * per_group], ((0, 0), (0, GATE_ROWS - per_group)))
             for g in range(NSA_KV_GROUPS)]
    wt = jnp.concatenate([vs, vw] + gates, axis=1).T
    return w, wr, wt


def _rope_tables(positions):
    pos = positions.astype(F32)[..., None]
    half = ROPE_DIM // 2
    inv = 1.0 / (ROPE_THETA ** (jnp.arange(0, ROPE_DIM, 2, dtype=F32) / ROPE_DIM))
    ang = pos * inv
    cs = jnp.concatenate([jnp.cos(ang), jnp.sin(ang)], axis=-1)
    place = np.zeros((ROPE_DIM, 3 * LANES), np.float32)
    base = np.zeros((1, 3 * LANES), np.float32)
    for head0 in range(0, LANES, NSA_HEAD_DIM):
        base[0, head0 + ROPE_DIM:head0 + NSA_HEAD_DIM] = 1.0
        for f in range(half):
            place[f, head0 + f] = 1.0
            place[f, head0 + half + f] = 1.0
            place[half + f, LANES + head0 + f] = -1.0
            place[half + f, 2 * LANES + head0 + half + f] = 1.0
    inv_r = 1.0 / (RET_THETA ** (jnp.arange(0, RET_HEAD_DIM, 2, dtype=F32) / RET_HEAD_DIM))
    ang_r = pos * inv_r
    return cs, jnp.asarray(place, dtype=BF16), jnp.asarray(base), jnp.cos(ang_r), jnp.sin(ang_r)


def _retention_tables():
    H, C, d = RET_HEADS, RET_CHUNK, RET_HEAD_DIM
    log_gamma = np.log1p(-np.exp2(-5.0 - np.arange(H, dtype=np.float64)))
    idx = np.arange(C, dtype=np.float64)
    rel = idx[:, None] - idx[None, :]
    decay = np.where(rel >= 0, np.exp(np.maximum(rel, 0.0)[None] * log_gamma[:, None, None]), 0.0)
    zeta = np.exp((C - 1 - idx)[None, :] * log_gamma[:, None])
    xi = np.exp((idx + 1.0)[None, :] * log_gamma[:, None])
    gchunk = np.exp(C * log_gamma)
    zeta_b = np.broadcast_to(zeta[:, :, None], (H, C, d))
    xi_b = np.broadcast_to(xi[:, :, None], (H, C, d))
    gchunk_b = np.broadcast_to(gchunk[:, None, None], (H, 1, d))
    return tuple(jnp.asarray(a, dtype=F32) for a in (decay, zeta_b, xi_b, gchunk_b))


def _cmp_to_sel_t(t_len):
    nc = t_len // CMP_STRIDE
    n_cmp = (t_len - CMP_BLOCK) // CMP_STRIDE + 1
    n_sel = t_len // SEL_BLOCK
    c_start = np.arange(nc) * CMP_STRIDE
    s_start = np.arange(SEL_PAD) * SEL_BLOCK
    hit = ((c_start[None, :] < s_start[:, None] + SEL_BLOCK) & (c_start[None, :] + CMP_BLOCK > s_start[:, None])
           & (np.arange(nc)[None, :] < n_cmp) & (np.arange(SEL_PAD)[:, None] < n_sel))
    return jnp.asarray(hit.astype(np.float32), dtype=BF16)


def _triangle_biases():
    r = np.arange(Q_BLOCK)[:, None]
    tok = np.arange(NSA_HPG * Q_BLOCK)[None, :] % Q_BLOCK
    causal = np.where(r <= tok, 0.0, NEG_INF)
    leaving = np.where(r > tok, 0.0, NEG_INF)
    return jnp.asarray(np.stack([causal, leaving]), dtype=F32)


def _compress_weights(pos, w1, w2):
    G, dh = NSA_KV_GROUPS, NSA_HEAD_DIM
    pos_rows = jnp.tile(pos, (1, G))
    w1r = w1.astype(BF16).reshape(CMP_BLOCK, dh, CMP_HIDDEN)
    w1_bd = jnp.einsum('ab,lij->laibj', jnp.eye(G, dtype=BF16), w1r).reshape(CMP_BLOCK, G * dh, G * CMP_HIDDEN)
    return pos_rows, w1_bd, w2.astype(BF16)


def _odd_layer(x, positions_tables, ret_tables, c2st, tri, gpre, w_in, cmp_k_pos, cmp_k_w1, cmp_k_w2,
               cmp_v_pos, cmp_v_w1, cmp_v_w2, gn_g, w_out, gpost, ffn):
    b, t, d = x.shape
    w, wr, wt = _odd_in_weight(w_in)
    (q, kc, vc, ksa, kw, vt, gt, rq, rk, rv, rg) = _odd_proj(x, gpre, w, wr, wt, *positions_tables)
    kpos, kw1, kw2 = _compress_weights(cmp_k_pos, cmp_k_w1, cmp_k_w2)
    vpos, vw1, vw2 = _compress_weights(cmp_v_pos, cmp_v_w1, cmp_v_w2)
    kcm, vct = _compress(kc, vc, kpos, vpos, kw1, vw1, kw2, vw2.T)
    yc = _nsa(q, kcm, vct, c2st, tri, ksa, kw, vt, gt)
    yd = _retention(rq, rk, rv, rg, *ret_tables, gn_g.reshape(RET_HEADS, 1, RET_HEAD_DIM))
    out = _odd_out(x.reshape(b * t, d), yc.reshape(b * t, NSA_WIDTH), yd.reshape(b * t, RET_WIDTH),
                   w_out, gpost, ffn)
    return out.reshape(b, t, d)


def kernel(x, positions, ln_mix_pre, ln_mix_post, ln_ffn_pre, ln_ffn_post, ffn_w_gate, ffn_w_up, ffn_w_down,
           ev_w_in, ev_pool_w, ev_pool_scale, ev_sgu_ln_g, ev_sgu_ln_b, ev_sgu_w, ev_sgu_b, ev_w_out,
           od_w_in, od_cmp_k_pos, od_cmp_k_w1, od_cmp_k_w2, od_cmp_v_pos, od_cmp_v_w1, od_cmp_v_w2,
           od_ret_gn_g, od_w_out):
    b, t, d = x.shape
    depth = ln_mix_pre.shape[0]
    rope = _rope_tables(positions)
    ret_tables = _retention_tables()
    c2st = _cmp_to_sel_t(t)
    tri = _triangle_biases()
    wg_all, wu_all, wd_all = ffn_w_gate.astype(BF16), ffn_w_up.astype(BF16), ffn_w_down.astype(BF16)
    ev_in_all, ev_out_all, od_out_all = ev_w_in.astype(BF16), ev_w_out.astype(BF16), od_w_out.astype(BF16)
    for layer in range(depth):
        gpre = ln_mix_pre[layer].reshape(1, d)
        gpost = ln_mix_post[layer].reshape(1, d)
        ffn = (layer, ln_ffn_pre[layer].reshape(1, d), ln_ffn_post[layer].reshape(1, d), wg_all, wu_all, wd_all)
        if layer % 2 == 0:
            e = layer // 2
            x = _even_layer(x, gpre, (ev_in_all, e), ev_pool_w[e].astype(BF16),
                            ev_pool_scale[e].reshape(1, POOL_WIDTH), ev_sgu_ln_g[e].reshape(1, SGU_WIDTH),
                            ev_sgu_ln_b[e].reshape(1, SGU_WIDTH), ev_sgu_w[e], ev_sgu_b[e].T,
                            (ev_out_all, e), gpost, ffn)
        else:
            o = layer // 2
            x = _odd_layer(x, rope, ret_tables, c2st, tri, gpre, od_w_in[o], od_cmp_k_pos[o], od_cmp_k_w1[o],
                           od_cmp_k_w2[o], od_cmp_v_pos[o], od_cmp_v_w1[o], od_cmp_v_w2[o], od_ret_gn_g[o],
                           (od_out_all, o), gpost, ffn)
    return x
```

```python
import functools

import numpy as np
import jax
import jax.numpy as jnp
from jax import lax
from jax.experimental import pallas as pl
from jax.experimental.pallas import tpu as pltpu

F32 = jnp.float32
BF16 = jnp.bfloat16

D_MODEL = 1024
POOL_WINDOWS = (2, 4, 8, 16)
POOL_GROUPS = 4
POOL_WIDTH = D_MODEL // 2
POOL_GDIM = POOL_WIDTH // POOL_GROUPS
POOL_HIST = 16
SGU_GROUPS = 4
SGU_WIDTH = D_MODEL // 2
SGU_GDIM = SGU_WIDTH // SGU_GROUPS
SGU_CHUNK = 128
EVEN_IN = POOL_WIDTH + 2 * SGU_WIDTH
NSA_HEADS = 8
NSA_KV_GROUPS = 2
NSA_HPG = NSA_HEADS // NSA_KV_GROUPS
NSA_HEAD_DIM = 64
NSA_WIDTH = NSA_HEADS * NSA_HEAD_DIM
NSA_GROUP_WIDTH = NSA_HPG * NSA_HEAD_DIM
NSA_KV_WIDTH = NSA_KV_GROUPS * NSA_HEAD_DIM
NSA_BRANCHES = 3
GATE_ROWS = 16
CMP_BLOCK = 32
CMP_STRIDE = 16
CMP_HIDDEN = 128
SEL_BLOCK = 64
SEL_TOPK = 16
SEL_PAD = 64
WINDOW = 512
Q_BLOCK = 256
FORCE_SCORE = 1.0e4
ROPE_THETA = 500000.0
ROPE_DIM = NSA_HEAD_DIM // 4
RET_HEADS = 4
RET_HEAD_DIM = 128
RET_WIDTH = RET_HEADS * RET_HEAD_DIM
RET_CHUNK = 128
RET_THETA = 10000.0
ODD_MIX = NSA_WIDTH + RET_WIDTH
NEG_INF = -1.0e30
LOG2_E = 1.4426950408889634
LANES = 128
SUBLANES = 8

VMEM_LIMIT = 56 * 1024 * 1024
FFN_ROWS = 512
FFN_CHUNKS = 11
EVEN_ROWS = 512
PROJ_ROWS = 512
SEL_KEYS = Q_BLOCK
RANK_CHUNK = 8
RET_ROWS = 512


def _cparams(sem):
    return pltpu.CompilerParams(dimension_semantics=sem, vmem_limit_bytes=VMEM_LIMIT)


def _const_spec(shape):
    nd = len(shape)
    return pl.BlockSpec(shape, lambda *_: (0,) * nd, pipeline_mode=pl.Buffered(1))


def _dot(a, b):
    return jnp.dot(a, b, preferred_element_type=F32)


def _dot_nt(a, b):
    return lax.dot_general(a, b, (((1,), (1,)), ((), ())), preferred_element_type=F32)


def _dot_tn(a, b):
    return lax.dot_general(a, b, (((0,), (0,)), ((), ())), preferred_element_type=F32)


def _rms(x, g, eps=1e-6):
    return x * lax.rsqrt(jnp.mean(x * x, axis=-1, keepdims=True) + eps) * g


def _eye(n):
    return jnp.where(lax.broadcasted_iota(jnp.int32, (n, n), 0) == lax.broadcasted_iota(jnp.int32, (n, n), 1),
                     1.0, 0.0).astype(BF16)


def _ffn_tile(x, gpre_ref, gpost_ref, wg_ref, wu_ref, wd_ref):
    h = _rms(x, gpre_ref[...]).astype(BF16)
    f_total = wg_ref.shape[1]
    fc = f_total // FFN_CHUNKS
    acc = None
    for c in range(FFN_CHUNKS):
        gate = _dot(h, wg_ref[:, c * fc:(c + 1) * fc])
        up = _dot(h, wu_ref[:, c * fc:(c + 1) * fc])
        act = (gate * jax.nn.sigmoid(gate) * up).astype(BF16)
        part = _dot(act, wd_ref[c * fc:(c + 1) * fc, :])
        acc = part if acc is None else acc + part
    return x + _rms(acc, gpost_ref[...])


def _layer_spec(stacked_shape, layer):
    nd = len(stacked_shape) - 1
    return pl.BlockSpec((None,) + tuple(stacked_shape[1:]), lambda *_: (layer,) + (0,) * nd,
                        pipeline_mode=pl.Buffered(1))


def _weight_spec(w):
    return _layer_spec(w[0].shape, w[1]) if isinstance(w, tuple) else _const_spec(w.shape)


def _weight_array(w):
    return w[0] if isinstance(w, tuple) else w


def _ffn_specs(ffn):
    layer, fpre, fpost, wg, wu, wd = ffn
    return [_const_spec(fpre.shape), _const_spec(fpost.shape), _layer_spec(wg.shape, layer),
            _layer_spec(wu.shape, layer), _layer_spec(wd.shape, layer)]


def _even_body(x_ref, gpre_ref, win_ref, poolw_ref, pscale_ref, lng_ref, lnb_ref, sguw_ref, sgub_ref,
               wout_ref, gpost_ref, fpre_ref, fpost_ref, wg_ref, wu_ref, wd_ref, o_ref, hist_ref):
    tt = x_ref.shape[1]
    j = pl.program_id(1)

    @pl.when(j == 0)
    def _():
        hist_ref[0:POOL_HIST, :] = jnp.zeros((POOL_HIST, POOL_WIDTH), F32)

    x = x_ref[0]
    h = _rms(x, gpre_ref[...]).astype(BF16)
    z = _dot(h, win_ref[...])
    a = z[:, :POOL_WIDTH]
    u = z[:, POOL_WIDTH:POOL_WIDTH + SGU_WIDTH]
    v = z[:, POOL_WIDTH + SGU_WIDTH:]

    hist_ref[POOL_HIST:POOL_HIST + tt, :] = a
    t_pos = j * tt + lax.broadcasted_iota(jnp.int32, (tt, 1), 0)
    ya = []
    for gi, w in enumerate(POOL_WINDOWS):
        cols = slice(gi * POOL_GDIM, (gi + 1) * POOL_GDIM)
        win_sum = a[:, cols]
        for s in range(1, w):
            win_sum = win_sum + hist_ref[POOL_HIST - s:POOL_HIST - s + tt, cols]
        cnt = jnp.minimum(t_pos + 1, w).astype(F32)
        diff = (win_sum / cnt - a[:, cols]).astype(BF16)
        ya.append(_dot(diff, poolw_ref[gi]))
    ya = jnp.concatenate(ya, axis=1) * pscale_ref[...]
    hist_ref[0:POOL_HIST, :] = hist_ref[tt:tt + POOL_HIST, :]

    ug = jax.nn.gelu(u)
    vg = jax.nn.gelu(v)
    mu = jnp.mean(vg, axis=-1, keepdims=True)
    var = jnp.mean(jnp.square(vg - mu), axis=-1, keepdims=True)
    vn = ((vg - mu) * lax.rsqrt(var + 1e-5) * lng_ref[...] + lnb_ref[...]).astype(BF16)
    r_i = lax.broadcasted_iota(jnp.int32, (SGU_CHUNK, SGU_CHUNK), 0)
    c_i = lax.broadcasted_iota(jnp.int32, (SGU_CHUNK, SGU_CHUNK), 1)
    yb = []
    for g in range(SGU_GROUPS):
        cols = slice(g * SGU_GDIM, (g + 1) * SGU_GDIM)
        wm = jnp.where(r_i >= c_i, sguw_ref[g], 0.0).astype(BF16)
        bcol = sgub_ref[:, g:g + 1]
        parts = []
        for c in range(tt // SGU_CHUNK):
            rows = slice(c * SGU_CHUNK, (c + 1) * SGU_CHUNK)
            parts.append(ug[rows, cols] * (_dot(wm, vn[rows, cols]) + bcol))
        yb.append(jnp.concatenate(parts, axis=0))
    yb = jnp.concatenate(yb, axis=1)

    m = _dot(ya.astype(BF16), wout_ref[0:POOL_WIDTH, :]) + _dot(yb.astype(BF16), wout_ref[POOL_WIDTH:, :])
    o_ref[0] = _ffn_tile(x + _rms(m, gpost_ref[...]), fpre_ref, fpost_ref, wg_ref, wu_ref, wd_ref)


def _even_layer(x, gpre, win, poolw, pscale, lng, lnb, sguw, sgub_t, wout, gpost, ffn):
    b, t, d = x.shape
    tt = EVEN_ROWS
    row = pl.BlockSpec((1, tt, d), lambda bi, j: (bi, j, 0))
    return pl.pallas_call(
        _even_body,
        grid=(b, t // tt),
        in_specs=[row, _const_spec((1, d)), _weight_spec(win), _const_spec(poolw.shape),
                  _const_spec(pscale.shape), _const_spec(lng.shape), _const_spec(lnb.shape),
                  _const_spec(sguw.shape), _const_spec(sgub_t.shape), _weight_spec(wout),
                  _const_spec((1, d))] + _ffn_specs(ffn),
        out_specs=row,
        out_shape=jax.ShapeDtypeStruct((b, t, d), F32),
        scratch_shapes=[pltpu.VMEM((POOL_HIST + tt, POOL_WIDTH), F32)],
        compiler_params=_cparams(("arbitrary", "arbitrary")),
        name="even_layer",
    )(x, gpre, _weight_array(win), poolw, pscale, lng, lnb, sguw, sgub_t, _weight_array(wout), gpost, *ffn[1:])


_C_Q = 0
_C_KC = _C_Q + NSA_WIDTH
_C_VC = _C_KC + NSA_KV_WIDTH
_C_KS = _C_VC + NSA_KV_WIDTH
_C_KW = _C_KS + NSA_KV_WIDTH
_C_RQ = _C_KW + NSA_KV_WIDTH
_C_RK = _C_RQ + RET_WIDTH
_C_RV = _C_RK + RET_WIDTH
_C_RG = _C_RV + RET_WIDTH
ODD_COLS = _C_RG + RET_WIDTH
_R_VS = 0
_R_VW = _R_VS + NSA_KV_WIDTH
_R_GT = _R_VW + NSA_KV_WIDTH
ODD_TROWS = _R_GT + NSA_KV_GROUPS * GATE_ROWS
V_AUG = NSA_HEAD_DIM + 16
VT_ROWS = 2 * NSA_KV_GROUPS * V_AUG


def _rope_nsa(z, c, sa, sb):
    half = ROPE_DIM // 2
    return z * c + pltpu.roll(z, LANES - half, 1) * sa + pltpu.roll(z, half, 1) * sb


def _split3(x):
    a = x.astype(BF16)
    r = x - a.astype(F32)
    b = r.astype(BF16)
    return a, b, (r - b.astype(F32)).astype(BF16)


def _proj_body(x_ref, gpre_ref, w_ref, wr_ref, wt_ref, cs_ref, place_ref, base_ref, cr_ref, sr_ref,
               q_ref, kc_ref, vc_ref, ksa_ref, kw_ref, vt_ref, gt_ref, rq_ref, rk_ref, rv_ref, rg_ref):
    tm = x_ref.shape[1]
    j = pl.program_id(1)
    x = x_ref[0]
    h = _rms(x, gpre_ref[...]).astype(BF16)
    tables = base_ref[...]
    for part in _split3(cs_ref[0]):
        tables = tables + _dot(part, place_ref[...])
    nc, nsa, nsb = tables[:, 0:LANES], tables[:, LANES:2 * LANES], tables[:, 2 * LANES:3 * LANES]
    cr, sr = cr_ref[0], sr_ref[0]
    rc = jnp.concatenate([cr, cr], axis=1)
    rs = jnp.concatenate([-sr, sr], axis=1)

    def cols(start, width):
        if start >= _C_RQ:
            return _dot(h, wr_ref[:, start - _C_RQ:start - _C_RQ + width])
        return _dot(h, w_ref[:, start:start + width])

    zq = cols(_C_Q, NSA_WIDTH)
    q_scale = NSA_HEAD_DIM ** -0.5 * LOG2_E
    for s in range(NSA_WIDTH // LANES):
        sl = slice(s * LANES, (s + 1) * LANES)
        q_ref[0, :, sl] = (_rope_nsa(zq[:, sl], nc, nsa, nsb) * q_scale).astype(BF16)

    zkv = cols(_C_KC, _C_RQ - _C_KC)
    kc_ref[0] = _rope_nsa(zkv[:, _C_KC - _C_KC:_C_VC - _C_KC], nc, nsa, nsb)
    vc_ref[0] = zkv[:, _C_VC - _C_KC:_C_KS - _C_KC]
    ks = _rope_nsa(zkv[:, _C_KS - _C_KC:_C_KW - _C_KC], nc, nsa, nsb)
    kw = _rope_nsa(zkv[:, _C_KW - _C_KC:_C_RQ - _C_KC], nc, nsa, nsb)
    t_pos = j * tm + lax.broadcasted_iota(jnp.int32, (tm, SEL_PAD), 0)
    blk = lax.broadcasted_iota(jnp.int32, (tm, SEL_PAD), 1)
    onehot = jnp.where(t_pos // SEL_BLOCK == blk, 1.0, 0.0).astype(BF16)
    for g in range(NSA_KV_GROUPS):
        sl = slice(g * NSA_HEAD_DIM, (g + 1) * NSA_HEAD_DIM)
        ksa_ref[0, g] = jnp.concatenate([ks[:, sl].astype(BF16), onehot], axis=1)
        kw_ref[0, g] = kw[:, sl].astype(BF16)

    zt = _dot_nt(wt_ref[...], h)
    ones_rows = jnp.where(lax.broadcasted_iota(jnp.int32, (V_AUG - NSA_HEAD_DIM, tm), 0) == 0, 1.0, 0.0).astype(BF16)
    for k in range(2 * NSA_KV_GROUPS):
        vt_ref[0, k * V_AUG:k * V_AUG + NSA_HEAD_DIM] = zt[k * NSA_HEAD_DIM:(k + 1) * NSA_HEAD_DIM].astype(BF16)
        vt_ref[0, k * V_AUG + NSA_HEAD_DIM:(k + 1) * V_AUG] = ones_rows
    gt_ref[0] = jax.nn.sigmoid(zt[_R_GT:])

    k_scale = RET_HEAD_DIM ** -0.5
    zrq = cols(_C_RQ, RET_WIDTH)
    zrk = cols(_C_RK, RET_WIDTH)
    for hh in range(RET_HEADS):
        sl = slice(hh * RET_HEAD_DIM, (hh + 1) * RET_HEAD_DIM)
        zq_h, zk_h = zrq[:, sl], zrk[:, sl]
        rq_ref[0, :, sl] = (zq_h * rc + pltpu.roll(zq_h, RET_HEAD_DIM // 2, 1) * rs).astype(BF16)
        rk_ref[0, :, sl] = ((zk_h * rc + pltpu.roll(zk_h, RET_HEAD_DIM // 2, 1) * rs) * k_scale).astype(BF16)
    rv_ref[0] = cols(_C_RV, RET_WIDTH).astype(BF16)
    rg_ref[0] = cols(_C_RG, RET_WIDTH)


def _odd_proj(x, gpre, w, wr, wt, cs, place, base, cr, sr):
    b, t, d = x.shape
    tm = PROJ_ROWS
    G = NSA_KV_GROUPS

    def row(width):
        return pl.BlockSpec((1, tm, width), lambda bi, j: (bi, j, 0))

    def grp(width):
        return pl.BlockSpec((1, G, tm, width), lambda bi, j: (bi, 0, j, 0))

    def feat(rows):
        return pl.BlockSpec((1, rows, tm), lambda bi, j: (bi, 0, j))

    out_shape = [
        jax.ShapeDtypeStruct((b, t, NSA_WIDTH), BF16),
        jax.ShapeDtypeStruct((b, t, LANES), F32),
        jax.ShapeDtypeStruct((b, t, LANES), F32),
        jax.ShapeDtypeStruct((b, G, t, LANES), BF16),
        jax.ShapeDtypeStruct((b, G, t, NSA_HEAD_DIM), BF16),
        jax.ShapeDtypeStruct((b, VT_ROWS, t), BF16),
        jax.ShapeDtypeStruct((b, G * GATE_ROWS, t), F32),
        jax.ShapeDtypeStruct((b, t, RET_WIDTH), BF16),
        jax.ShapeDtypeStruct((b, t, RET_WIDTH), BF16),
        jax.ShapeDtypeStruct((b, t, RET_WIDTH), BF16),
        jax.ShapeDtypeStruct((b, t, RET_WIDTH), F32),
    ]
    out_specs = [row(NSA_WIDTH), row(LANES), row(LANES), grp(LANES), grp(NSA_HEAD_DIM), feat(VT_ROWS),
                 feat(G * GATE_ROWS), row(RET_WIDTH), row(RET_WIDTH), row(RET_WIDTH), row(RET_WIDTH)]
    return pl.pallas_call(
        _proj_body,
        grid=(b, t // tm),
        in_specs=[row(d), _const_spec((1, d)), _const_spec(w.shape), _const_spec(wr.shape), _const_spec(wt.shape),
                  row(cs.shape[2]),
                  _const_spec(place.shape), _const_spec(base.shape), row(cr.shape[2]), row(sr.shape[2])],
        out_specs=out_specs,
        out_shape=out_shape,
        compiler_params=_cparams(("arbitrary", "arbitrary")),
        name="odd_proj",
    )(x, gpre, w, wr, wt, cs, place, base, cr, sr)


def _compress_body(k_ref, v_ref, kpos_ref, vpos_ref, kw1_ref, vw1_ref, kw2_ref, vw2t_ref, kcm_ref, vct_ref):
    nc = k_ref.shape[1] // CMP_STRIDE
    half = CMP_BLOCK // CMP_STRIDE

    def hidden(x_ref, pos_ref, w1_ref):
        parts = [None] * half
        for off in range(CMP_STRIDE):
            tok = x_ref[0, pl.ds(off, nc, stride=CMP_STRIDE), :]
            for h in range(half):
                l = h * CMP_STRIDE + off
                term = _dot((tok + pos_ref[l:l + 1, :]).astype(BF16), w1_ref[l])
                parts[h] = term if parts[h] is None else parts[h] + term
        pre = parts[0]
        for h in range(1, half):
            pre = pre + pltpu.roll(parts[h], nc - h, 0)
        return jax.nn.gelu(pre).astype(BF16)

    hk = hidden(k_ref, kpos_ref, kw1_ref)
    hv = hidden(v_ref, vpos_ref, vw1_ref)
    for g in range(NSA_KV_GROUPS):
        sl = slice(g * CMP_HIDDEN, (g + 1) * CMP_HIDDEN)
        kcm_ref[0, g] = _dot(hk[:, sl], kw2_ref[...]).astype(BF16)
        vct_ref[0, g] = _dot_nt(vw2t_ref[...], hv[:, sl]).astype(BF16)


def _compress(k, v, kpos, vpos, kw1, vw1, kw2, vw2t):
    b, t, width = k.shape
    nc = t // CMP_STRIDE
    G = NSA_KV_GROUPS
    row = pl.BlockSpec((1, t, width), lambda bi: (bi, 0, 0))
    return pl.pallas_call(
        _compress_body,
        grid=(b,),
        in_specs=[row, row, _const_spec(kpos.shape), _const_spec(vpos.shape), _const_spec(kw1.shape),
                  _const_spec(vw1.shape), _const_spec(kw2.shape), _const_spec(vw2t.shape)],
        out_specs=[pl.BlockSpec((1, G, nc, NSA_HEAD_DIM), lambda bi: (bi, 0, 0, 0)),
                   pl.BlockSpec((1, G, NSA_HEAD_DIM, nc), lambda bi: (bi, 0, 0, 0))],
        out_shape=[jax.ShapeDtypeStruct((b, G, nc, NSA_HEAD_DIM), BF16),
                   jax.ShapeDtypeStruct((b, G, NSA_HEAD_DIM, nc), BF16)],
        compiler_params=_cparams(("arbitrary",)),
        name="compress",
    )(k, v, kpos, vpos, kw1, vw1, kw2, vw2t)


def _nsa_body(q_ref, kcm_ref, vct_ref, c2st_ref, tri_ref, ksa_ref, kw_ref, vt_ref, gt_ref, o_ref,
              score_ref, rank_ref, wbias_ref, sc_ref, sw_ref, pw_ref, sa_ref, sb_ref, pa_ref, pb_ref, lhs_ref,
              m_ref, mta_ref, mtb_ref, mw_ref, acc_ref, *, top_k):
    G, M, Q, dh = NSA_KV_GROUPS, NSA_HPG, Q_BLOCK, NSA_HEAD_DIM
    cols = M * Q
    nc = kcm_ref.shape[2]
    i = pl.program_id(1)
    t0 = i * Q
    tq = t0 + (lax.broadcasted_iota(jnp.int32, (1, cols), 1) & (Q - 1))
    groups = range(G)

    q_heads, q_rows = [], []
    for g in groups:
        qb = q_ref[0, :, g * NSA_GROUP_WIDTH:(g + 1) * NSA_GROUP_WIDTH]
        q_heads.append([qb[:, m * dh:(m + 1) * dh] for m in range(M)])
        q_rows.append(jnp.concatenate(q_heads[g], axis=0))

    n_idx = lax.broadcasted_iota(jnp.int32, (nc, 1), 0)
    cmp_ok = (n_idx * CMP_STRIDE + (CMP_BLOCK - 1) <= tq) & (n_idx < nc - 1)
    j_idx = lax.broadcasted_iota(jnp.int32, (SEL_PAD, Q), 0)
    cur = (t0 + lax.broadcasted_iota(jnp.int32, (SEL_PAD, Q), 1)) // SEL_BLOCK
    forced = (j_idx == 0) | (j_idx == cur) | (j_idx == cur - 1)
    valid_tok = (tq >= CMP_BLOCK - 1).astype(F32)
    o_cmp = []
    for g in groups:
        sc = sc_ref.at[g]
        sc[...] = jnp.where(cmp_ok, _dot_nt(kcm_ref[0, g], q_rows[g]), NEG_INF)
        sc[...] = jnp.exp2(sc[...] - jnp.max(sc[...], axis=0, keepdims=True))
        norm = valid_tok / jnp.sum(sc[...], axis=0, keepdims=True)
        o_cmp.append(_dot(vct_ref[0, g], sc[...].astype(BF16)) * norm)
        p_grp = sc[:, 0:Q] * norm[:, 0:Q]
        for m in range(1, M):
            p_grp = p_grp + sc[:, m * Q:(m + 1) * Q] * norm[:, m * Q:(m + 1) * Q]
        p_hi = p_grp.astype(BF16)
        p_lo = (p_grp - p_hi.astype(F32)).astype(BF16)
        imp = _dot(c2st_ref[...], p_hi) + _dot(c2st_ref[...], p_lo)
        score_ref[g] = jnp.where(j_idx <= cur, jnp.where(forced, FORCE_SCORE, imp), -1.0)

    def v_aug(branch, g, k0, n):
        r0 = (branch * G + g) * V_AUG
        return vt_ref[0, r0:r0 + V_AUG, pl.ds(k0, n)]

    span = WINDOW + Q
    w0 = pl.multiple_of(jnp.maximum(t0 - WINDOW, 0), Q)
    n_wb = span // Q
    diag_b = jnp.minimum(i, n_wb - 1)
    tri_causal, tri_leaving = tri_ref[0], tri_ref[1]
    for wb in range(n_wb):
        blk_bias = jnp.where(wb == diag_b, tri_causal, jnp.where(wb > diag_b, NEG_INF, 0.0))
        if wb == 0:
            blk_bias = jnp.where(i >= n_wb - 1, tri_leaving, blk_bias)
        wbias_ref[wb * Q:(wb + 1) * Q, :] = blk_bias
    for g in groups:
        s_win = _dot_nt(kw_ref[0, g, pl.ds(w0, span), :], q_rows[g]) + wbias_ref[...]
        sw_ref[g] = s_win
        mw_ref[g] = jnp.max(s_win, axis=0, keepdims=True)

    n_causal = (t0 + Q) // SEL_BLOCK
    rank_ref[...] = jnp.zeros(rank_ref.shape, jnp.int32)
    row_in_tile = lax.broadcasted_iota(jnp.int32, (SUBLANES, LANES), 0)
    for c0 in range(0, SEL_PAD, RANK_CHUNK):
        @pl.when(c0 < n_causal)
        def _(c0=c0):
            for g in groups:
                for l0 in range(0, Q, LANES):
                    ln = slice(l0, l0 + LANES)
                    others = [score_ref[g, k:k + 1, ln] for k in range(c0, c0 + RANK_CHUNK)]
                    for r0 in range(0, SEL_PAD, SUBLANES):
                        mine = score_ref[g, r0:r0 + SUBLANES, ln]
                        count = rank_ref[g, r0:r0 + SUBLANES, ln]
                        for k, other in zip(range(c0, c0 + RANK_CHUNK), others):
                            if r0 > k:
                                beats = other >= mine
                            elif r0 + SUBLANES <= k:
                                beats = other > mine
                            else:
                                beats = (other > mine) | ((other == mine) & (k < r0 + row_in_tile))
                            count = count + beats.astype(jnp.int32)
                        rank_ref[g, r0:r0 + SUBLANES, ln] = count

    eye_q = _eye(Q)
    for g in groups:
        chosen = jnp.where((rank_ref[g] < top_k) & (score_ref[g] >= 0.0), 1.0, 0.0).astype(BF16)
        chosen_t = _dot_nt(eye_q, chosen)
        bias = jnp.where(chosen_t > 0.5, 0.0, NEG_INF).astype(BF16)
        for m in range(M):
            lhs_ref[g, m * Q:(m + 1) * Q, :] = jnp.concatenate([q_heads[g][m], bias], axis=1)
        m_ref[g] = jnp.full((1, cols), NEG_INF, F32)
        acc_ref[g] = jnp.zeros((V_AUG, cols), F32)

    buf_a = (sa_ref, mta_ref, pa_ref)
    buf_b = (sb_ref, mtb_ref, pb_ref)

    def put_scores(buf, g, kt):
        s_dst, max_dst, _ = buf
        k0 = pl.multiple_of(kt * SEL_KEYS, SEL_KEYS)
        s = _dot_nt(ksa_ref[0, g, pl.ds(k0, SEL_KEYS), :], lhs_ref[g])
        s_dst[g] = s
        max_dst[g] = jnp.max(s, axis=0, keepdims=True)

    def sel_update(buf, g, kt, bias=None):
        src, tile_max, p_dst = buf
        k0 = pl.multiple_of(kt * SEL_KEYS, SEL_KEYS)
        m_i = m_ref[g]
        if bias is None:
            m_new = jnp.maximum(m_i, tile_max[g])
            p_dst[g] = jnp.exp2(src[g] - m_new).astype(BF16)
        else:
            m_new = jnp.maximum(m_i, jnp.max(src[g] + bias, axis=0, keepdims=True))
            p_dst[g] = jnp.exp2(src[g] + bias - m_new).astype(BF16)
        acc_ref[g] = jnp.exp2(m_i - m_new) * acc_ref[g] + _dot(v_aug(0, g, k0, SEL_KEYS), p_dst[g])
        m_ref[g] = m_new

    n_full = t0 // SEL_KEYS
    odd = n_full & 1
    for g in groups:
        put_scores(buf_a, g, 0)

    o_win = []
    for g in groups:
        pw_ref[g] = jnp.exp2(sw_ref[g] - mw_ref[g]).astype(BF16)
        acc_win = _dot(v_aug(1, g, w0, span), pw_ref[g])
        o_win.append(acc_win[0:dh] / acc_win[dh:dh + 1])

    @pl.when(odd == 1)
    def _():
        for g in groups:
            sel_update(buf_a, g, 0)
            put_scores(buf_a, g, 1)

    def sel_pair(pair, _):
        ta = odd + 2 * pair
        for g in groups:
            put_scores(buf_b, g, ta + 1)
        for g in groups:
            sel_update(buf_a, g, ta)
        for g in groups:
            put_scores(buf_a, g, ta + 2)
        for g in groups:
            sel_update(buf_b, g, ta + 1)
        return 0

    lax.fori_loop(0, n_full // 2, sel_pair, 0)
    o_sel = []
    for g in groups:
        sel_update(buf_a, g, n_full, bias=tri_causal)
        o_sel.append(acc_ref[g, 0:dh, :] / acc_ref[g, dh:dh + 1, :])

    for g in groups:
        gt = gt_ref[0, g * GATE_ROWS:(g + 1) * GATE_ROWS, :]
        heads = []
        for m in range(M):
            c = slice(m * Q, (m + 1) * Q)
            r = NSA_BRANCHES * m
            mixed = (o_cmp[g][:, c] * gt[r:r + 1] + o_sel[g][:, c] * gt[r + 1:r + 2]
                     + o_win[g][:, c] * gt[r + 2:r + 3])
            heads.append(mixed.astype(BF16))
        o_ref[0, :, g * NSA_GROUP_WIDTH:(g + 1) * NSA_GROUP_WIDTH] = _dot_nt(
            eye_q, jnp.concatenate(heads, axis=0)).astype(BF16)


def _nsa(q, kcm, vct, c2st, tri, ksa, kw, vt, gt):
    b, t, _ = q.shape
    G = NSA_KV_GROUPS
    nc = kcm.shape[2]
    n_sel = t // SEL_BLOCK
    assert n_sel <= SEL_PAD and t >= WINDOW + Q_BLOCK and t % SEL_KEYS == 0
    top_k = min(SEL_TOPK, n_sel)
    cols, span = NSA_HPG * Q_BLOCK, WINDOW + Q_BLOCK

    def per_batch(shape):
        nd = len(shape)
        return pl.BlockSpec((1,) + shape, lambda bi, i: (bi,) + (0,) * nd)

    token_rows = pl.BlockSpec((1, Q_BLOCK, NSA_WIDTH), lambda bi, i: (bi, i, 0))
    return pl.pallas_call(
        functools.partial(_nsa_body, top_k=top_k),
        grid=(b, t // Q_BLOCK),
        in_specs=[token_rows, per_batch((G, nc, NSA_HEAD_DIM)), per_batch((G, NSA_HEAD_DIM, nc)),
                  _const_spec(c2st.shape), _const_spec(tri.shape), per_batch((G, t, LANES)), per_batch((G, t, NSA_HEAD_DIM)),
                  per_batch((vt.shape[1], t)),
                  pl.BlockSpec((1, G * GATE_ROWS, Q_BLOCK), lambda bi, i: (bi, 0, i))],
        out_specs=token_rows,
        out_shape=jax.ShapeDtypeStruct((b, t, NSA_WIDTH), BF16),
        scratch_shapes=[pltpu.VMEM((G, SEL_PAD, Q_BLOCK), F32),
                        pltpu.VMEM((G, SEL_PAD, Q_BLOCK), jnp.int32),
                        pltpu.VMEM((span, cols), F32),
                        pltpu.VMEM((G, nc, cols), F32),
                        pltpu.VMEM((G, span, cols), F32),
                        pltpu.VMEM((G, span, cols), BF16),
                        pltpu.VMEM((G, SEL_KEYS, cols), F32),
                        pltpu.VMEM((G, SEL_KEYS, cols), F32),
                        pltpu.VMEM((G, SEL_KEYS, cols), BF16),
                        pltpu.VMEM((G, SEL_KEYS, cols), BF16),
                        pltpu.VMEM((G, cols, 2 * NSA_HEAD_DIM), BF16),
                        pltpu.VMEM((G, 1, cols), F32),
                        pltpu.VMEM((G, 1, cols), F32),
                        pltpu.VMEM((G, 1, cols), F32),
                        pltpu.VMEM((G, 1, cols), F32),
                        pltpu.VMEM((G, V_AUG, cols), F32)],
        compiler_params=_cparams(("arbitrary", "arbitrary")),
        name="nsa",
    )(q, kcm, vct, c2st, tri, ksa, kw, vt, gt)


def _ret_body(q_ref, k_ref, v_ref, g_ref, decay_ref, zeta_ref, xi_ref, gchunk_ref, gn_ref, o_ref, state_ref):
    C, d = RET_CHUNK, RET_HEAD_DIM

    @pl.when(pl.program_id(1) == 0)
    def _():
        state_ref[...] = jnp.zeros_like(state_ref)

    n_chunks = q_ref.shape[1] // C
    for h in range(RET_HEADS):
        hc = slice(h * d, (h + 1) * d)
        state = state_ref[h]
        before = []
        for c in range(n_chunks):
            rows = slice(c * C, (c + 1) * C)
            kz = (k_ref[0, rows, hc].astype(F32) * zeta_ref[h]).astype(BF16)
            before.append(state.astype(BF16))
            state = state * gchunk_ref[h] + _dot_tn(kz, v_ref[0, rows, hc])
        state_ref[h] = state
        for c in range(n_chunks):
            rows = slice(c * C, (c + 1) * C)
            q, k, v = q_ref[0, rows, hc], k_ref[0, rows, hc], v_ref[0, rows, hc]
            s = (_dot_nt(q, k) * decay_ref[h]).astype(BF16)
            o = _dot(s, v) + _dot((q.astype(F32) * xi_ref[h]).astype(BF16), before[c])
            mu = jnp.mean(o, axis=-1, keepdims=True)
            var = jnp.mean(jnp.square(o - mu), axis=-1, keepdims=True)
            o = (o - mu) * lax.rsqrt(var + 1e-5) * gn_ref[h]
            gate = g_ref[0, rows, hc]
            o_ref[0, rows, hc] = (gate * jax.nn.sigmoid(gate) * o).astype(BF16)


def _retention(rq, rk, rv, rg, decay, zeta, xi, gchunk, gn):
    b, t, width = rq.shape
    tok = pl.BlockSpec((1, RET_ROWS, width), lambda bi, c: (bi, c, 0))
    return pl.pallas_call(
        _ret_body,
        grid=(b, t // RET_ROWS),
        in_specs=[tok, tok, tok, tok, _const_spec(decay.shape), _const_spec(zeta.shape), _const_spec(xi.shape),
                  _const_spec(gchunk.shape), _const_spec(gn.shape)],
        out_specs=tok,
        out_shape=jax.ShapeDtypeStruct((b, t, width), BF16),
        scratch_shapes=[pltpu.VMEM((RET_HEADS, RET_HEAD_DIM, RET_HEAD_DIM), F32)],
        compiler_params=_cparams(("arbitrary", "arbitrary")),
        name="retention",
    )(rq, rk, rv, rg, decay, zeta, xi, gchunk, gn)


def _oproj_body(x_ref, yc_ref, yd_ref, w_ref, gpost_ref, fpre_ref, fpost_ref, wg_ref, wu_ref, wd_ref, o_ref):
    m = _dot(yc_ref[...], w_ref[0:NSA_WIDTH, :]) + _dot(yd_ref[...], w_ref[NSA_WIDTH:, :])
    o_ref[...] = _ffn_tile(x_ref[...] + _rms(m, gpost_ref[...]), fpre_ref, fpost_ref, wg_ref, wu_ref, wd_ref)


def _odd_out(x2, yc2, yd2, w, gpost, ffn):
    n, d = x2.shape
    tm = FFN_ROWS

    def row(width):
        return pl.BlockSpec((tm, width), lambda i: (i, 0))

    return pl.pallas_call(
        _oproj_body,
        grid=(n // tm,),
        in_specs=[row(d), row(NSA_WIDTH), row(RET_WIDTH), _weight_spec(w), _const_spec((1, d))]
        + _ffn_specs(ffn),
        out_specs=row(d),
        out_shape=jax.ShapeDtypeStruct((n, d), F32),
        compiler_params=_cparams(("arbitrary",)),
        name="odd_out_ffn",
    )(x2, yc2, yd2, _weight_array(w), gpost, *ffn[1:])


def _odd_in_weight(w_in):
    sizes = [NSA_WIDTH] + [NSA_KV_WIDTH] * 6 + [NSA_BRANCHES * NSA_HEADS] + [RET_WIDTH] * 4
    offs = np.concatenate([[0], np.cumsum(sizes)])
    w_in = w_in.astype(BF16)
    q, kc, vc, ks, vs, kw, vw, gt, rq, rk, rv, rg = [w_in[:, offs[n]:offs[n + 1]] for n in range(len(sizes))]
    w = jnp.concatenate([w_in[:, offs[0]:offs[4]], kw], axis=1)
    wr = w_in[:, offs[8]:offs[12]]
    per_group = NSA_BRANCHES * NSA_HPG
    gates = [jnp.pad(gt[:, g * per_group:(g + 1) * per_group], ((0, 0), (0, GATE_ROWS - per_group)))
             for g in range(NSA_KV_GROUPS)]
    wt = jnp.concatenate([vs, vw] + gates, axis=1).T
    return w, wr, wt


def _rope_tables(positions):
    pos = positions.astype(F32)[..., None]
    half = ROPE_DIM // 2
    inv = 1.0 / (ROPE_THETA ** (jnp.arange(0, ROPE_DIM, 2, dtype=F32) / ROPE_DIM))
    ang = pos * inv
    cs = jnp.concatenate([jnp.cos(ang), jnp.sin(ang)], axis=-1)
    place = np.zeros((ROPE_DIM, 3 * LANES), np.float32)
    base = np.zeros((1, 3 * LANES), np.float32)
    for head0 in range(0, LANES, NSA_HEAD_DIM):
        base[0, head0 + ROPE_DIM:head0 + NSA_HEAD_DIM] = 1.0
        for f in range(half):
            place[f, head0 + f] = 1.0
            place[f, head0 + half + f] = 1.0
            place[half + f, LANES + head0 + f] = -1.0
            place[half + f, 2 * LANES + head0 + half + f] = 1.0
    inv_r = 1.0 / (RET_THETA ** (jnp.arange(0, RET_HEAD_DIM, 2, dtype=F32) / RET_HEAD_DIM))
    ang_r = pos * inv_r
    return cs, jnp.asarray(place, dtype=BF16), jnp.asarray(base), jnp.cos(ang_r), jnp.sin(ang_r)


def _retention_tables():
    H, C, d = RET_HEADS, RET_CHUNK, RET_HEAD_DIM
    log_gamma = np.log1p(-np.exp2(-5.0 - np.arange(H, dtype=np.float64)))
    idx = np.arange(C, dtype=np.float64)
    rel = idx[:, None] - idx[None, :]
    decay = np.where(rel >= 0, np.exp(np.maximum(rel, 0.0)[None] * log_gamma[:, None, None]), 0.0)
    zeta = np.exp((C - 1 - idx)[None, :] * log_gamma[:, None])
    xi = np.exp((idx + 1.0)[None, :] * log_gamma[:, None])
    gchunk = np.exp(C * log_gamma)
    zeta_b = np.broadcast_to(zeta[:, :, None], (H, C, d))
    xi_b = np.broadcast_to(xi[:, :, None], (H, C, d))
    gchunk_b = np.broadcast_to(gchunk[:, None, None], (H, 1, d))
    return tuple(jnp.asarray(a, dtype=F32) for a in (decay, zeta_b, xi_b, gchunk_b))


def _cmp_to_sel_t(t_len):
    nc = t_len // CMP_STRIDE
    n_cmp = (t_len - CMP_BLOCK) // CMP_STRIDE + 1
    n_sel = t_len // SEL_BLOCK
    c_start = np.arange(nc) * CMP_STRIDE
    s_start = np.arange(SEL_PAD) * SEL_BLOCK
    hit = ((c_start[None, :] < s_start[:, None] + SEL_BLOCK) & (c_start[None, :] + CMP_BLOCK > s_start[:, None])
           & (np.arange(nc)[None, :] < n_cmp) & (np.arange(SEL_PAD)[:, None] < n_sel))
    return jnp.asarray(hit.astype(np.float32), dtype=BF16)


def _triangle_biases():
    r = np.arange(Q_BLOCK)[:, None]
    tok = np.arange(NSA_HPG * Q_BLOCK)[None, :] % Q_BLOCK
    causal = np.where(r <= tok, 0.0, NEG_INF)
    leaving = np.where(r > tok, 0.0, NEG_INF)
    return jnp.asarray(np.stack([causal, leaving]), dtype=F32)


def _compress_weights(pos, w1, w2):
    G, dh = NSA_KV_GROUPS, NSA_HEAD_DIM
    pos_rows = jnp.tile(pos, (1, G))
    w1r = w1.astype(BF16).reshape(CMP_BLOCK, dh, CMP_HIDDEN)
    w1_bd = jnp.einsum('ab,lij->laibj', jnp.eye(G, dtype=BF16), w1r).reshape(CMP_BLOCK, G * dh, G * CMP_HIDDEN)
    return pos_rows, w1_bd, w2.astype(BF16)


def _odd_layer(x, positions_tables, ret_tables, c2st, tri, gpre, w_in, cmp_k_pos, cmp_k_w1, cmp_k_w2,
               cmp_v_pos, cmp_v_w1, cmp_v_w2, gn_g, w_out, gpost, ffn):
    b, t, d = x.shape
    w, wr, wt = _odd_in_weight(w_in)
    (q, kc, vc, ksa, kw, vt, gt, rq, rk, rv, rg) = _odd_proj(x, gpre, w, wr, wt, *positions_tables)
    kpos, kw1, kw2 = _compress_weights(cmp_k_pos, cmp_k_w1, cmp_k_w2)
    vpos, vw1, vw2 = _compress_weights(cmp_v_pos, cmp_v_w1, cmp_v_w2)
    kcm, vct = _compress(kc, vc, kpos, vpos, kw1, vw1, kw2, vw2.T)
    yc = _nsa(q, kcm, vct, c2st, tri, ksa, kw, vt, gt)
    yd = _retention(rq, rk, rv, rg, *ret_tables, gn_g.reshape(RET_HEADS, 1, RET_HEAD_DIM))
    out = _odd_out(x.reshape(b * t, d), yc.reshape(b * t, NSA_WIDTH), yd.reshape(b * t, RET_WIDTH),
                   w_out, gpost, ffn)
    return out.reshape(b, t, d)


def kernel(x, positions, ln_mix_pre, ln_mix_post, ln_ffn_pre, ln_ffn_post, ffn_w_gate, ffn_w_up, ffn_w_down,
           ev_w_in, ev_pool_w, ev_pool_scale, ev_sgu_ln_g, ev_sgu_ln_b, ev_sgu_w, ev_sgu_b, ev_w_out,
           od_w_in, od_cmp_k_pos, od_cmp_k_w1, od_cmp_k_w2, od_cmp_v_pos, od_cmp_v_w1, od_cmp_v_w2,
           od_ret_gn_g, od_w_out):
    b, t, d = x.shape
    depth = ln_mix_pre.shape[0]
    rope = _rope_tables(positions)
    ret_tables = _retention_tables()
    c2st = _cmp_to_sel_t(t)
    tri = _triangle_biases()
    wg_all, wu_all, wd_all = ffn_w_gate.astype(BF16), ffn_w_up.astype(BF16), ffn_w_down.astype(BF16)
    ev_in_all, ev_out_all, od_out_all = ev_w_in.astype(BF16), ev_w_out.astype(BF16), od_w_out.astype(BF16)
    for layer in range(depth):
        gpre = ln_mix_pre[layer].reshape(1, d)
        gpost = ln_mix_post[layer].reshape(1, d)
        ffn = (layer, ln_ffn_pre[layer].reshape(1, d), ln_ffn_post[layer].reshape(1, d), wg_all, wu_all, wd_all)
        if layer % 2 == 0:
            e = layer // 2
            x = _even_layer(x, gpre, (ev_in_all, e), ev_pool_w[e].astype(BF16),
                            ev_pool_scale[e].reshape(1, POOL_WIDTH), ev_sgu_ln_g[e].reshape(1, SGU_WIDTH),
                            ev_sgu_ln_b[e].reshape(1, SGU_WIDTH), ev_sgu_w[e], ev_sgu_b[e].T,
                            (ev_out_all, e), gpost, ffn)
        else:
            o = layer // 2
            x = _odd_layer(x, rope, ret_tables, c2st, tri, gpre, od_w_in[o], od_cmp_k_pos[o], od_cmp_k_w1[o],
                           od_cmp_k_w2[o], od_cmp_v_pos[o], od_cmp_v_w1[o], od_cmp_v_w2[o], od_ret_gn_g[o],
                           (od_out_all, o), gpost, ffn)
    return x
```

```python
import functools

import numpy as np
import jax
import jax.numpy as jnp
from jax import lax
from jax.experimental import pallas as pl
from jax.experimental.pallas import tpu as pltpu

F32 = jnp.float32
BF16 = jnp.bfloat16

D_MODEL = 1024
POOL_WINDOWS = (2, 4, 8, 16)
POOL_GROUPS = 4
POOL_WIDTH = D_MODEL // 2
POOL_GDIM = POOL_WIDTH // POOL_GROUPS
POOL_HIST = 16
SGU_GROUPS = 4
SGU_WIDTH = D_MODEL // 2
SGU_GDIM = SGU_WIDTH // SGU_GROUPS
SGU_CHUNK = 128
EVEN_IN = POOL_WIDTH + 2 * SGU_WIDTH
NSA_HEADS = 8
NSA_KV_GROUPS = 2
NSA_HPG = NSA_HEADS // NSA_KV_GROUPS
NSA_HEAD_DIM = 64
NSA_WIDTH = NSA_HEADS * NSA_HEAD_DIM
NSA_GROUP_WIDTH = NSA_HPG * NSA_HEAD_DIM
NSA_KV_WIDTH = NSA_KV_GROUPS * NSA_HEAD_DIM
NSA_BRANCHES = 3
GATE_ROWS = 16
CMP_BLOCK = 32
CMP_STRIDE = 16
CMP_HIDDEN = 128
SEL_BLOCK = 64
SEL_TOPK = 16
SEL_PAD = 64
WINDOW = 512
Q_BLOCK = 256
FORCE_SCORE = 1.0e4
ROPE_THETA = 500000.0
ROPE_DIM = NSA_HEAD_DIM // 4
RET_HEADS = 4
RET_HEAD_DIM = 128
RET_WIDTH = RET_HEADS * RET_HEAD_DIM
RET_CHUNK = 128
RET_THETA = 10000.0
ODD_MIX = NSA_WIDTH + RET_WIDTH
NEG_INF = -1.0e30
LOG2_E = 1.4426950408889634
LANES = 128
SUBLANES = 8

VMEM_LIMIT = 56 * 1024 * 1024
FFN_ROWS = 512
FFN_CHUNKS = 11
EVEN_ROWS = 512
PROJ_ROWS = 512
SEL_KEYS = Q_BLOCK
RANK_CHUNK = 8
RET_ROWS = 512


def _cparams(sem):
    return pltpu.CompilerParams(dimension_semantics=sem, vmem_limit_bytes=VMEM_LIMIT)


def _const_spec(shape):
    nd = len(shape)
    return pl.BlockSpec(shape, lambda *_: (0,) * nd, pipeline_mode=pl.Buffered(1))


def _dot(a, b):
    return jnp.dot(a, b, preferred_element_type=F32)


def _dot_nt(a, b):
    return lax.dot_general(a, b, (((1,), (1,)), ((), ())), preferred_element_type=F32)


def _dot_tn(a, b):
    return lax.dot_general(a, b, (((0,), (0,)), ((), ())), preferred_element_type=F32)


def _rms(x, g, eps=1e-6):
    return x * lax.rsqrt(jnp.mean(x * x, axis=-1, keepdims=True) + eps) * g


def _eye(n):
    return jnp.where(lax.broadcasted_iota(jnp.int32, (n, n), 0) == lax.broadcasted_iota(jnp.int32, (n, n), 1),
                     1.0, 0.0).astype(BF16)


def _ffn_tile(x, gpre_ref, gpost_ref, wg_ref, wu_ref, wd_ref):
    h = _rms(x, gpre_ref[...]).astype(BF16)
    f_total = wg_ref.shape[1]
    fc = f_total // FFN_CHUNKS
    acc = None
    for c in range(FFN_CHUNKS):
        gate = _dot(h, wg_ref[:, c * fc:(c + 1) * fc])
        up = _dot(h, wu_ref[:, c * fc:(c + 1) * fc])
        act = (gate * jax.nn.sigmoid(gate) * up).astype(BF16)
        part = _dot(act, wd_ref[c * fc:(c + 1) * fc, :])
        acc = part if acc is None else acc + part
    return x + _rms(acc, gpost_ref[...])


def _layer_spec(stacked_shape, layer):
    nd = len(stacked_shape) - 1
    return pl.BlockSpec((None,) + tuple(stacked_shape[1:]), lambda *_: (layer,) + (0,) * nd,
                        pipeline_mode=pl.Buffered(1))


def _weight_spec(w):
    return _layer_spec(w[0].shape, w[1]) if isinstance(w, tuple) else _const_spec(w.shape)


def _weight_array(w):
    return w[0] if isinstance(w, tuple) else w


def _ffn_specs(ffn):
    layer, fpre, fpost, wg, wu, wd = ffn
    return [_const_spec(fpre.shape), _const_spec(fpost.shape), _layer_spec(wg.shape, layer),
            _layer_spec(wu.shape, layer), _layer_spec(wd.shape, layer)]


def _even_body(x_ref, gpre_ref, win_ref, poolw_ref, pscale_ref, lng_ref, lnb_ref, sguw_ref, sgub_ref,
               wout_ref, gpost_ref, fpre_ref, fpost_ref, wg_ref, wu_ref, wd_ref, o_ref, hist_ref):
    tt = x_ref.shape[1]
    j = pl.program_id(1)

    @pl.when(j == 0)
    def _():
        hist_ref[0:POOL_HIST, :] = jnp.zeros((POOL_HIST, POOL_WIDTH), F32)

    x = x_ref[0]
    h = _rms(x, gpre_ref[...]).astype(BF16)
    z = _dot(h, win_ref[...])
    a = z[:, :POOL_WIDTH]
    u = z[:, POOL_WIDTH:POOL_WIDTH + SGU_WIDTH]
    v = z[:, POOL_WIDTH + SGU_WIDTH:]

    hist_ref[POOL_HIST:POOL_HIST + tt, :] = a
    t_pos = j * tt + lax.broadcasted_iota(jnp.int32, (tt, 1), 0)
    ya = []
    for gi, w in enumerate(POOL_WINDOWS):
        cols = slice(gi * POOL_GDIM, (gi + 1) * POOL_GDIM)
        win_sum = a[:, cols]
        for s in range(1, w):
            win_sum = win_sum + hist_ref[POOL_HIST - s:POOL_HIST - s + tt, cols]
        cnt = jnp.minimum(t_pos + 1, w).astype(F32)
        diff = (win_sum / cnt - a[:, cols]).astype(BF16)
        ya.append(_dot(diff, poolw_ref[gi]))
    ya = jnp.concatenate(ya, axis=1) * pscale_ref[...]
    hist_ref[0:POOL_HIST, :] = hist_ref[tt:tt + POOL_HIST, :]

    ug = jax.nn.gelu(u)
    vg = jax.nn.gelu(v)
    mu = jnp.mean(vg, axis=-1, keepdims=True)
    var = jnp.mean(jnp.square(vg - mu), axis=-1, keepdims=True)
    vn = ((vg - mu) * lax.rsqrt(var + 1e-5) * lng_ref[...] + lnb_ref[...]).astype(BF16)
    r_i = lax.broadcasted_iota(jnp.int32, (SGU_CHUNK, SGU_CHUNK), 0)
    c_i = lax.broadcasted_iota(jnp.int32, (SGU_CHUNK, SGU_CHUNK), 1)
    yb = []
    for g in range(SGU_GROUPS):
        cols = slice(g * SGU_GDIM, (g + 1) * SGU_GDIM)
        wm = jnp.where(r_i >= c_i, sguw_ref[g], 0.0).astype(BF16)
        bcol = sgub_ref[:, g:g + 1]
        parts = []
        for c in range(tt // SGU_CHUNK):
            rows = slice(c * SGU_CHUNK, (c + 1) * SGU_CHUNK)
            parts.append(ug[rows, cols] * (_dot(wm, vn[rows, cols]) + bcol))
        yb.append(jnp.concatenate(parts, axis=0))
    yb = jnp.concatenate(yb, axis=1)

    m = _dot(ya.astype(BF16), wout_ref[0:POOL_WIDTH, :]) + _dot(yb.astype(BF16), wout_ref[POOL_WIDTH:, :])
    o_ref[0] = _ffn_tile(x + _rms(m, gpost_ref[...]), fpre_ref, fpost_ref, wg_ref, wu_ref, wd_ref)


def _even_layer(x, gpre, win, poolw, pscale, lng, lnb, sguw, sgub_t, wout, gpost, ffn):
    b, t, d = x.shape
    tt = EVEN_ROWS
    row = pl.BlockSpec((1, tt, d), lambda bi, j: (bi, j, 0))
    return pl.pallas_call(
        _even_body,
        grid=(b, t // tt),
        in_specs=[row, _const_spec((1, d)), _weight_spec(win), _const_spec(poolw.shape),
                  _const_spec(pscale.shape), _const_spec(lng.shape), _const_spec(lnb.shape),
                  _const_spec(sguw.shape), _const_spec(sgub_t.shape), _weight_spec(wout),
                  _const_spec((1, d))] + _ffn_specs(ffn),
        out_specs=row,
        out_shape=jax.ShapeDtypeStruct((b, t, d), F32),
        scratch_shapes=[pltpu.VMEM((POOL_HIST + tt, POOL_WIDTH), F32)],
        compiler_params=_cparams(("arbitrary", "arbitrary")),
        name="even_layer",
    )(x, gpre, _weight_array(win), poolw, pscale, lng, lnb, sguw, sgub_t, _weight_array(wout), gpost, *ffn[1:])


_C_Q = 0
_C_KC = _C_Q + NSA_WIDTH
_C_VC = _C_KC + NSA_KV_WIDTH
_C_KS = _C_VC + NSA_KV_WIDTH
_C_KW = _C_KS + NSA_KV_WIDTH
_C_RQ = _C_KW + NSA_KV_WIDTH
_C_RK = _C_RQ + RET_WIDTH
_C_RV = _C_RK + RET_WIDTH
_C_RG = _C_RV + RET_WIDTH
ODD_COLS = _C_RG + RET_WIDTH
_R_VS = 0
_R_VW = _R_VS + NSA_KV_WIDTH
_R_GT = _R_VW + NSA_KV_WIDTH
ODD_TROWS = _R_GT + NSA_KV_GROUPS * GATE_ROWS
V_AUG = NSA_HEAD_DIM + 16
VT_ROWS = 2 * NSA_KV_GROUPS * V_AUG


def _rope_nsa(z, c, sa, sb):
    half = ROPE_DIM // 2
    return z * c + pltpu.roll(z, LANES - half, 1) * sa + pltpu.roll(z, half, 1) * sb


def _split3(x):
    a = x.astype(BF16)
    r = x - a.astype(F32)
    b = r.astype(BF16)
    return a, b, (r - b.astype(F32)).astype(BF16)


def _proj_body(x_ref, gpre_ref, w_ref, wr_ref, wt_ref, cs_ref, place_ref, base_ref, cr_ref, sr_ref,
               q_ref, kc_ref, vc_ref, ksa_ref, kw_ref, vt_ref, gt_ref, rq_ref, rk_ref, rv_ref, rg_ref):
    tm = x_ref.shape[1]
    j = pl.program_id(1)
    x = x_ref[0]
    h = _rms(x, gpre_ref[...]).astype(BF16)
    tables = base_ref[...]
    for part in _split3(cs_ref[0]):
        tables = tables + _dot(part, place_ref[...])
    nc, nsa, nsb = tables[:, 0:LANES], tables[:, LANES:2 * LANES], tables[:, 2 * LANES:3 * LANES]
    cr, sr = cr_ref[0], sr_ref[0]
    rc = jnp.concatenate([cr, cr], axis=1)
    rs = jnp.concatenate([-sr, sr], axis=1)

    def cols(start, width):
        if start >= _C_RQ:
            return _dot(h, wr_ref[:, start - _C_RQ:start - _C_RQ + width])
        return _dot(h, w_ref[:, start:start + width])

    zq = cols(_C_Q, NSA_WIDTH)
    q_scale = NSA_HEAD_DIM ** -0.5 * LOG2_E
    for s in range(NSA_WIDTH // LANES):
        sl = slice(s * LANES, (s + 1) * LANES)
        q_ref[0, :, sl] = (_rope_nsa(zq[:, sl], nc, nsa, nsb) * q_scale).astype(BF16)

    zkv = cols(_C_KC, _C_RQ - _C_KC)
    kc_ref[0] = _rope_nsa(zkv[:, _C_KC - _C_KC:_C_VC - _C_KC], nc, nsa, nsb)
    vc_ref[0] = zkv[:, _C_VC - _C_KC:_C_KS - _C_KC]
    ks = _rope_nsa(zkv[:, _C_KS - _C_KC:_C_KW - _C_KC], nc, nsa, nsb)
    kw = _rope_nsa(zkv[:, _C_KW - _C_KC:_C_RQ - _C_KC], nc, nsa, nsb)
    t_pos = j * tm + lax.broadcasted_iota(jnp.int32, (tm, SEL_PAD), 0)
    blk = lax.broadcasted_iota(jnp.int32, (tm, SEL_PAD), 1)
    onehot = jnp.where(t_pos // SEL_BLOCK == blk, 1.0, 0.0).astype(BF16)
    for g in range(NSA_KV_GROUPS):
        sl = slice(g * NSA_HEAD_DIM, (g + 1) * NSA_HEAD_DIM)
        ksa_ref[0, g] = jnp.concatenate([ks[:, sl].astype(BF16), onehot], axis=1)
        kw_ref[0, g] = kw[:, sl].astype(BF16)

    zt = _dot_nt(wt_ref[...], h)
    ones_rows = jnp.where(lax.broadcasted_iota(jnp.int32, (V_AUG - NSA_HEAD_DIM, tm), 0) == 0, 1.0, 0.0).astype(BF16)
    for k in range(2 * NSA_KV_GROUPS):
        vt_ref[0, k * V_AUG:k * V_AUG + NSA_HEAD_DIM] = zt[k * NSA_HEAD_DIM:(k + 1) * NSA_HEAD_DIM].astype(BF16)
        vt_ref[0, k * V_AUG + NSA_HEAD_DIM:(k + 1) * V_AUG] = ones_rows
    gt_ref[0] = jax.nn.sigmoid(zt[_R_GT:])

    k_scale = RET_HEAD_DIM ** -0.5
    zrq = cols(_C_RQ, RET_WIDTH)
    zrk = cols(_C_RK, RET_WIDTH)
    for hh in range(RET_HEADS):
        sl = slice(hh * RET_HEAD_DIM, (hh + 1) * RET_HEAD_DIM)
        zq_h, zk_h = zrq[:, sl], zrk[:, sl]
        rq_ref[0, :, sl] = (zq_h * rc + pltpu.roll(zq_h, RET_HEAD_DIM // 2, 1) * rs).astype(BF16)
        rk_ref[0, :, sl] = ((zk_h * rc + pltpu.roll(zk_h, RET_HEAD_DIM // 2, 1) * rs) * k_scale).astype(BF16)
    rv_ref[0] = cols(_C_RV, RET_WIDTH).astype(BF16)
    rg_ref[0] = cols(_C_RG, RET_WIDTH)


def _odd_proj(x, gpre, w, wr, wt, cs, place, base, cr, sr):
    b, t, d = x.shape
    tm = PROJ_ROWS
    G = NSA_KV_GROUPS

    def row(width):
        return pl.BlockSpec((1, tm, width), lambda bi, j: (bi, j, 0))

    def grp(width):
        return pl.BlockSpec((1, G, tm, width), lambda bi, j: (bi, 0, j, 0))

    def feat(rows):
        return pl.BlockSpec((1, rows, tm), lambda bi, j: (bi, 0, j))

    out_shape = [
        jax.ShapeDtypeStruct((b, t, NSA_WIDTH), BF16),
        jax.ShapeDtypeStruct((b, t, LANES), F32),
        jax.ShapeDtypeStruct((b, t, LANES), F32),
        jax.ShapeDtypeStruct((b, G, t, LANES), BF16),
        jax.ShapeDtypeStruct((b, G, t, NSA_HEAD_DIM), BF16),
        jax.ShapeDtypeStruct((b, VT_ROWS, t), BF16),
        jax.ShapeDtypeStruct((b, G * GATE_ROWS, t), F32),
        jax.ShapeDtypeStruct((b, t, RET_WIDTH), BF16),
        jax.ShapeDtypeStruct((b, t, RET_WIDTH), BF16),
        jax.ShapeDtypeStruct((b, t, RET_WIDTH), BF16),
        jax.ShapeDtypeStruct((b, t, RET_WIDTH), F32),
    ]
    out_specs = [row(NSA_WIDTH), row(LANES), row(LANES), grp(LANES), grp(NSA_HEAD_DIM), feat(VT_ROWS),
                 feat(G * GATE_ROWS), row(RET_WIDTH), row(RET_WIDTH), row(RET_WIDTH), row(RET_WIDTH)]
    return pl.pallas_call(
        _proj_body,
        grid=(b, t // tm),
        in_specs=[row(d), _const_spec((1, d)), _const_spec(w.shape), _const_spec(wr.shape), _const_spec(wt.shape),
                  row(cs.shape[2]),
                  _const_spec(place.shape), _const_spec(base.shape), row(cr.shape[2]), row(sr.shape[2])],
        out_specs=out_specs,
        out_shape=out_shape,
        compiler_params=_cparams(("arbitrary", "arbitrary")),
        name="odd_proj",
    )(x, gpre, w, wr, wt, cs, place, base, cr, sr)


def _compress_body(k_ref, v_ref, kpos_ref, vpos_ref, kw1_ref, vw1_ref, kw2_ref, vw2t_ref, kcm_ref, vct_ref):
    nc = k_ref.shape[1] // CMP_STRIDE
    half = CMP_BLOCK // CMP_STRIDE

    def hidden(x_ref, pos_ref, w1_ref):
        parts = [None] * half
        for off in range(CMP_STRIDE):
            tok = x_ref[0, pl.ds(off, nc, stride=CMP_STRIDE), :]
            for h in range(half):
                l = h * CMP_STRIDE + off
                term = _dot((tok + pos_ref[l:l + 1, :]).astype(BF16), w1_ref[l])
                parts[h] = term if parts[h] is None else parts[h] + term
        pre = parts[0]
        for h in range(1, half):
            pre = pre + pltpu.roll(parts[h], nc - h, 0)
        return jax.nn.gelu(pre).astype(BF16)

    hk = hidden(k_ref, kpos_ref, kw1_ref)
    hv = hidden(v_ref, vpos_ref, vw1_ref)
    for g in range(NSA_KV_GROUPS):
        sl = slice(g * CMP_HIDDEN, (g + 1) * CMP_HIDDEN)
        kcm_ref[0, g] = _dot(hk[:, sl], kw2_ref[...]).astype(BF16)
        vct_ref[0, g] = _dot_nt(vw2t_ref[...], hv[:, sl]).astype(BF16)


def _compress(k, v, kpos, vpos, kw1, vw1, kw2, vw2t):
    b, t, width = k.shape
    nc = t // CMP_STRIDE
    G = NSA_KV_GROUPS
    row = pl.BlockSpec((1, t, width), lambda bi: (bi, 0, 0))
    return pl.pallas_call(
        _compress_body,
        grid=(b,),
        in_specs=[row, row, _const_spec(kpos.shape), _const_spec(vpos.shape), _const_spec(kw1.shape),
                  _const_spec(vw1.shape), _const_spec(kw2.shape), _const_spec(vw2t.shape)],
        out_specs=[pl.BlockSpec((1, G, nc, NSA_HEAD_DIM), lambda bi: (bi, 0, 0, 0)),
                   pl.BlockSpec((1, G, NSA_HEAD_DIM, nc), lambda bi: (bi, 0, 0, 0))],
        out_shape=[jax.ShapeDtypeStruct((b, G, nc, NSA_HEAD_DIM), BF16),
                   jax.ShapeDtypeStruct((b, G, NSA_HEAD_DIM, nc), BF16)],
        compiler_params=_cparams(("arbitrary",)),
        name="compress",
    )(k, v, kpos, vpos, kw1, vw1, kw2, vw2t)


def _nsa_body(q_ref, kcm_ref, vct_ref, c2st_ref, wmask_ref, ksa_ref, kw_ref, vt_ref, gt_ref, o_ref,
              score_ref, rank_ref, sc_ref, sw_ref, pw_ref, sa_ref, sb_ref, pa_ref, pb_ref, lhs_ref,
              m_ref, mta_ref, mtb_ref, mw_ref, acc_ref, *, top_k):
    G, M, Q, dh = NSA_KV_GROUPS, NSA_HPG, Q_BLOCK, NSA_HEAD_DIM
    cols = M * Q
    nc = kcm_ref.shape[2]
    i = pl.program_id(1)
    t0 = i * Q
    tq = t0 + (lax.broadcasted_iota(jnp.int32, (1, cols), 1) & (Q - 1))
    groups = range(G)

    q_heads, q_rows = [], []
    for g in groups:
        qb = q_ref[0, :, g * NSA_GROUP_WIDTH:(g + 1) * NSA_GROUP_WIDTH]
        q_heads.append([qb[:, m * dh:(m + 1) * dh] for m in range(M)])
        q_rows.append(jnp.concatenate(q_heads[g], axis=0))

    n_idx = lax.broadcasted_iota(jnp.int32, (nc, 1), 0)
    cmp_ok = (n_idx * CMP_STRIDE + (CMP_BLOCK - 1) <= tq) & (n_idx < nc - 1)
    j_idx = lax.broadcasted_iota(jnp.int32, (SEL_PAD, Q), 0)
    cur = (t0 + lax.broadcasted_iota(jnp.int32, (SEL_PAD, Q), 1)) // SEL_BLOCK
    forced = (j_idx == 0) | (j_idx == cur) | (j_idx == cur - 1)
    valid_tok = (tq >= CMP_BLOCK - 1).astype(F32)
    o_cmp = []
    for g in groups:
        sc = sc_ref.at[g]
        sc[...] = jnp.where(cmp_ok, _dot_nt(kcm_ref[0, g], q_rows[g]), NEG_INF)
        sc[...] = jnp.exp2(sc[...] - jnp.max(sc[...], axis=0, keepdims=True))
        norm = valid_tok / jnp.sum(sc[...], axis=0, keepdims=True)
        o_cmp.append(_dot(vct_ref[0, g], sc[...].astype(BF16)) * norm)
        p_grp = sc[:, 0:Q] * norm[:, 0:Q]
        for m in range(1, M):
            p_grp = p_grp + sc[:, m * Q:(m + 1) * Q] * norm[:, m * Q:(m + 1) * Q]
        p_hi = p_grp.astype(BF16)
        p_lo = (p_grp - p_hi.astype(F32)).astype(BF16)
        imp = _dot(c2st_ref[...], p_hi) + _dot(c2st_ref[...], p_lo)
        score_ref[g] = jnp.where(j_idx <= cur, jnp.where(forced, FORCE_SCORE, imp), -1.0)

    def v_aug(branch, g, k0, n):
        r0 = (branch * G + g) * V_AUG
        return vt_ref[0, r0:r0 + V_AUG, pl.ds(k0, n)]

    span = WINDOW + Q
    w0 = pl.multiple_of(jnp.maximum(t0 - WINDOW, 0), Q)
    n_wb = span // Q
    diag_b = jnp.minimum(i, n_wb - 1)
    tri_causal = wmask_ref[n_wb - 1, (n_wb - 1) * Q:n_wb * Q, :]
    for g in groups:
        s_win = _dot_nt(kw_ref[0, g, pl.ds(w0, span), :], q_rows[g]) + wmask_ref[diag_b]
        sw_ref[g] = s_win
        mw_ref[g] = jnp.max(s_win, axis=0, keepdims=True)

    n_causal = (t0 + Q) // SEL_BLOCK
    rank_ref[...] = jnp.zeros(rank_ref.shape, jnp.int32)
    row_in_tile = lax.broadcasted_iota(jnp.int32, (SUBLANES, LANES), 0)
    for c0 in range(0, SEL_PAD, RANK_CHUNK):
        @pl.when(c0 < n_causal)
        def _(c0=c0):
            for g in groups:
                for l0 in range(0, Q, LANES):
                    ln = slice(l0, l0 + LANES)
                    others = [score_ref[g, k:k + 1, ln] for k in range(c0, c0 + RANK_CHUNK)]
                    for r0 in range(0, SEL_PAD, SUBLANES):
                        mine = score_ref[g, r0:r0 + SUBLANES, ln]
                        count = rank_ref[g, r0:r0 + SUBLANES, ln]
                        for k, other in zip(range(c0, c0 + RANK_CHUNK), others):
                            if r0 > k:
                                beats = other >= mine
                            elif r0 + SUBLANES <= k:
                                beats = other > mine
                            else:
                                beats = (other > mine) | ((other == mine) & (k < r0 + row_in_tile))
                            count = count + beats.astype(jnp.int32)
                        rank_ref[g, r0:r0 + SUBLANES, ln] = count

    eye_q = _eye(Q)
    for g in groups:
        chosen = jnp.where((rank_ref[g] < top_k) & (score_ref[g] >= 0.0), 1.0, 0.0).astype(BF16)
        chosen_t = _dot_nt(eye_q, chosen)
        bias = jnp.where(chosen_t > 0.5, 0.0, NEG_INF).astype(BF16)
        for m in range(M):
            lhs_ref[g, m * Q:(m + 1) * Q, :] = jnp.concatenate([q_heads[g][m], bias], axis=1)
        m_ref[g] = jnp.full((1, cols), NEG_INF, F32)
        acc_ref[g] = jnp.zeros((V_AUG, cols), F32)

    buf_a = (sa_ref, mta_ref, pa_ref)
    buf_b = (sb_ref, mtb_ref, pb_ref)

    def put_scores(buf, g, kt):
        s_dst, max_dst, _ = buf
        k0 = pl.multiple_of(kt * SEL_KEYS, SEL_KEYS)
        s = _dot_nt(ksa_ref[0, g, pl.ds(k0, SEL_KEYS), :], lhs_ref[g])
        s_dst[g] = s
        max_dst[g] = jnp.max(s, axis=0, keepdims=True)

    def sel_update(buf, g, kt, bias=None):
        src, tile_max, p_dst = buf
        k0 = pl.multiple_of(kt * SEL_KEYS, SEL_KEYS)
        m_i = m_ref[g]
        if bias is None:
            m_new = jnp.maximum(m_i, tile_max[g])
            p_dst[g] = jnp.exp2(src[g] - m_new).astype(BF16)
        else:
            m_new = jnp.maximum(m_i, jnp.max(src[g] + bias, axis=0, keepdims=True))
            p_dst[g] = jnp.exp2(src[g] + bias - m_new).astype(BF16)
        acc_ref[g] = jnp.exp2(m_i - m_new) * acc_ref[g] + _dot(v_aug(0, g, k0, SEL_KEYS), p_dst[g])
        m_ref[g] = m_new

    n_full = t0 // SEL_KEYS
    odd = n_full & 1
    for g in groups:
        put_scores(buf_a, g, 0)

    o_win = []
    for g in groups:
        pw_ref[g] = jnp.exp2(sw_ref[g] - mw_ref[g]).astype(BF16)
        acc_win = _dot(v_aug(1, g, w0, span), pw_ref[g])
        o_win.append(acc_win[0:dh] / acc_win[dh:dh + 1])

    @pl.when(odd == 1)
    def _():
        for g in groups:
            sel_update(buf_a, g, 0)
            put_scores(buf_a, g, 1)

    def sel_pair(pair, _):
        ta = odd + 2 * pair
        for g in groups:
            put_scores(buf_b, g, ta + 1)
        for g in groups:
            sel_update(buf_a, g, ta)
        for g in groups:
            put_scores(buf_a, g, ta + 2)
        for g in groups:
            sel_update(buf_b, g, ta + 1)
        return 0

    lax.fori_loop(0, n_full // 2, sel_pair, 0)
    o_sel = []
    for g in groups:
        sel_update(buf_a, g, n_full, bias=tri_causal)
        o_sel.append(acc_ref[g, 0:dh, :] / acc_ref[g, dh:dh + 1, :])

    for g in groups:
        gt = gt_ref[0, g * GATE_ROWS:(g + 1) * GATE_ROWS, :]
        heads = []
        for m in range(M):
            c = slice(m * Q, (m + 1) * Q)
            r = NSA_BRANCHES * m
            mixed = (o_cmp[g][:, c] * gt[r:r + 1] + o_sel[g][:, c] * gt[r + 1:r + 2]
                     + o_win[g][:, c] * gt[r + 2:r + 3])
            heads.append(mixed.astype(BF16))
        o_ref[0, :, g * NSA_GROUP_WIDTH:(g + 1) * NSA_GROUP_WIDTH] = _dot_nt(
            eye_q, jnp.concatenate(heads, axis=0)).astype(BF16)


def _nsa(q, kcm, vct, c2st, wmask, ksa, kw, vt, gt):
    b, t, _ = q.shape
    G = NSA_KV_GROUPS
    nc = kcm.shape[2]
    n_sel = t // SEL_BLOCK
    assert n_sel <= SEL_PAD and t >= WINDOW + Q_BLOCK and t % SEL_KEYS == 0
    top_k = min(SEL_TOPK, n_sel)
    cols, span = NSA_HPG * Q_BLOCK, WINDOW + Q_BLOCK

    def per_batch(shape):
        nd = len(shape)
        return pl.BlockSpec((1,) + shape, lambda bi, i: (bi,) + (0,) * nd)

    token_rows = pl.BlockSpec((1, Q_BLOCK, NSA_WIDTH), lambda bi, i: (bi, i, 0))
    return pl.pallas_call(
        functools.partial(_nsa_body, top_k=top_k),
        grid=(b, t // Q_BLOCK),
        in_specs=[token_rows, per_batch((G, nc, NSA_HEAD_DIM)), per_batch((G, NSA_HEAD_DIM, nc)),
                  _const_spec(c2st.shape), _const_spec(wmask.shape), per_batch((G, t, LANES)), per_batch((G, t, NSA_HEAD_DIM)),
                  per_batch((vt.shape[1], t)),
                  pl.BlockSpec((1, G * GATE_ROWS, Q_BLOCK), lambda bi, i: (bi, 0, i))],
        out_specs=token_rows,
        out_shape=jax.ShapeDtypeStruct((b, t, NSA_WIDTH), BF16),
        scratch_shapes=[pltpu.VMEM((G, SEL_PAD, Q_BLOCK), F32),
                        pltpu.VMEM((G, SEL_PAD, Q_BLOCK), jnp.int32),
                        pltpu.VMEM((G, nc, cols), F32),
                        pltpu.VMEM((G, span, cols), F32),
                        pltpu.VMEM((G, span, cols), BF16),
                        pltpu.VMEM((G, SEL_KEYS, cols), F32),
                        pltpu.VMEM((G, SEL_KEYS, cols), F32),
                        pltpu.VMEM((G, SEL_KEYS, cols), BF16),
                        pltpu.VMEM((G, SEL_KEYS, cols), BF16),
                        pltpu.VMEM((G, cols, 2 * NSA_HEAD_DIM), BF16),
                        pltpu.VMEM((G, 1, cols), F32),
                        pltpu.VMEM((G, 1, cols), F32),
                        pltpu.VMEM((G, 1, cols), F32),
                        pltpu.VMEM((G, 1, cols), F32),
                        pltpu.VMEM((G, V_AUG, cols), F32)],
        compiler_params=_cparams(("arbitrary", "arbitrary")),
        name="nsa",
    )(q, kcm, vct, c2st, wmask, ksa, kw, vt, gt)


def _ret_body(q_ref, k_ref, v_ref, g_ref, decay_ref, zeta_ref, xi_ref, gchunk_ref, gn_ref, o_ref, state_ref):
    C, d = RET_CHUNK, RET_HEAD_DIM

    @pl.when(pl.program_id(1) == 0)
    def _():
        state_ref[...] = jnp.zeros_like(state_ref)

    n_chunks = q_ref.shape[1] // C
    for h in range(RET_HEADS):
        hc = slice(h * d, (h + 1) * d)
        state = state_ref[h]
        before = []
        for c in range(n_chunks):
            rows = slice(c * C, (c + 1) * C)
            kz = (k_ref[0, rows, hc].astype(F32) * zeta_ref[h]).astype(BF16)
            before.append(state.astype(BF16))
            state = state * gchunk_ref[h] + _dot_tn(kz, v_ref[0, rows, hc])
        state_ref[h] = state
        for c in range(n_chunks):
            rows = slice(c * C, (c + 1) * C)
            q, k, v = q_ref[0, rows, hc], k_ref[0, rows, hc], v_ref[0, rows, hc]
            s = (_dot_nt(q, k) * decay_ref[h]).astype(BF16)
            o = _dot(s, v) + _dot((q.astype(F32) * xi_ref[h]).astype(BF16), before[c])
            mu = jnp.mean(o, axis=-1, keepdims=True)
            var = jnp.mean(jnp.square(o - mu), axis=-1, keepdims=True)
            o = (o - mu) * lax.rsqrt(var + 1e-5) * gn_ref[h]
            gate = g_ref[0, rows, hc]
            o_ref[0, rows, hc] = (gate * jax.nn.sigmoid(gate) * o).astype(BF16)


def _retention(rq, rk, rv, rg, decay, zeta, xi, gchunk, gn):
    b, t, width = rq.shape
    tok = pl.BlockSpec((1, RET_ROWS, width), lambda bi, c: (bi, c, 0))
    return pl.pallas_call(
        _ret_body,
        grid=(b, t // RET_ROWS),
        in_specs=[tok, tok, tok, tok, _const_spec(decay.shape), _const_spec(zeta.shape), _const_spec(xi.shape),
                  _const_spec(gchunk.shape), _const_spec(gn.shape)],
        out_specs=tok,
        out_shape=jax.ShapeDtypeStruct((b, t, width), BF16),
        scratch_shapes=[pltpu.VMEM((RET_HEADS, RET_HEAD_DIM, RET_HEAD_DIM), F32)],
        compiler_params=_cparams(("arbitrary", "arbitrary")),
        name="retention",
    )(rq, rk, rv, rg, decay, zeta, xi, gchunk, gn)


def _oproj_body(x_ref, yc_ref, yd_ref, w_ref, gpost_ref, fpre_ref, fpost_ref, wg_ref, wu_ref, wd_ref, o_ref):
    m = _dot(yc_ref[...], w_ref[0:NSA_WIDTH, :]) + _dot(yd_ref[...], w_ref[NSA_WIDTH:, :])
    o_ref[...] = _ffn_tile(x_ref[...] + _rms(m, gpost_ref[...]), fpre_ref, fpost_ref, wg_ref, wu_ref, wd_ref)


def _odd_out(x2, yc2, yd2, w, gpost, ffn):
    n, d = x2.shape
    tm = FFN_ROWS

    def row(width):
        return pl.BlockSpec((tm, width), lambda i: (i, 0))

    return pl.pallas_call(
        _oproj_body,
        grid=(n // tm,),
        in_specs=[row(d), row(NSA_WIDTH), row(RET_WIDTH), _weight_spec(w), _const_spec((1, d))]
        + _ffn_specs(ffn),
        out_specs=row(d),
        out_shape=jax.ShapeDtypeStruct((n, d), F32),
        compiler_params=_cparams(("arbitrary",)),
        name="odd_out_ffn",
    )(x2, yc2, yd2, _weight_array(w), gpost, *ffn[1:])


def _odd_in_weight(w_in):
    sizes = [NSA_WIDTH] + [NSA_KV_WIDTH] * 6 + [NSA_BRANCHES * NSA_HEADS] + [RET_WIDTH] * 4
    offs = np.concatenate([[0], np.cumsum(sizes)])
    w_in = w_in.astype(BF16)
    q, kc, vc, ks, vs, kw, vw, gt, rq, rk, rv, rg = [w_in[:, offs[n]:offs[n + 1]] for n in range(len(sizes))]
    w = jnp.concatenate([w_in[:, offs[0]:offs[4]], kw], axis=1)
    wr = w_in[:, offs[8]:offs[12]]
    per_group = NSA_BRANCHES * NSA_HPG
    gates = [jnp.pad(gt[:, g * per_group:(g + 1) * per_group], ((0, 0), (0, GATE_ROWS - per_group)))
             for g in range(NSA_KV_GROUPS)]
    wt = jnp.concatenate([vs, vw] + gates, axis=1).T
    return w, wr, wt


def _rope_tables(positions):
    pos = positions.astype(F32)[..., None]
    half = ROPE_DIM // 2
    inv = 1.0 / (ROPE_THETA ** (jnp.arange(0, ROPE_DIM, 2, dtype=F32) / ROPE_DIM))
    ang = pos * inv
    cs = jnp.concatenate([jnp.cos(ang), jnp.sin(ang)], axis=-1)
    place = np.zeros((ROPE_DIM, 3 * LANES), np.float32)
    base = np.zeros((1, 3 * LANES), np.float32)
    for head0 in range(0, LANES, NSA_HEAD_DIM):
        base[0, head0 + ROPE_DIM:head0 + NSA_HEAD_DIM] = 1.0
        for f in range(half):
            place[f, head0 + f] = 1.0
            place[f, head0 + half + f] = 1.0
            place[half + f, LANES + head0 + f] = -1.0
            place[half + f, 2 * LANES + head0 + half + f] = 1.0
    inv_r = 1.0 / (RET_THETA ** (jnp.arange(0, RET_HEAD_DIM, 2, dtype=F32) / RET_HEAD_DIM))
    ang_r = pos * inv_r
    return cs, jnp.asarray(place, dtype=BF16), jnp.asarray(base), jnp.cos(ang_r), jnp.sin(ang_r)


def _retention_tables():
    H, C, d = RET_HEADS, RET_CHUNK, RET_HEAD_DIM
    log_gamma = np.log1p(-np.exp2(-5.0 - np.arange(H, dtype=np.float64)))
    idx = np.arange(C, dtype=np.float64)
    rel = idx[:, None] - idx[None, :]
    decay = np.where(rel >= 0, np.exp(np.maximum(rel, 0.0)[None] * log_gamma[:, None, None]), 0.0)
    zeta = np.exp((C - 1 - idx)[None, :] * log_gamma[:, None])
    xi = np.exp((idx + 1.0)[None, :] * log_gamma[:, None])
    gchunk = np.exp(C * log_gamma)
    zeta_b = np.broadcast_to(zeta[:, :, None], (H, C, d))
    xi_b = np.broadcast_to(xi[:, :, None], (H, C, d))
    gchunk_b = np.broadcast_to(gchunk[:, None, None], (H, 1, d))
    return tuple(jnp.asarray(a, dtype=F32) for a in (decay, zeta_b, xi_b, gchunk_b))


def _cmp_to_sel_t(t_len):
    nc = t_len // CMP_STRIDE
    n_cmp = (t_len - CMP_BLOCK) // CMP_STRIDE + 1
    n_sel = t_len // SEL_BLOCK
    c_start = np.arange(nc) * CMP_STRIDE
    s_start = np.arange(SEL_PAD) * SEL_BLOCK
    hit = ((c_start[None, :] < s_start[:, None] + SEL_BLOCK) & (c_start[None, :] + CMP_BLOCK > s_start[:, None])
           & (np.arange(nc)[None, :] < n_cmp) & (np.arange(SEL_PAD)[:, None] < n_sel))
    return jnp.asarray(hit.astype(np.float32), dtype=BF16)


def _window_masks():
    Q = Q_BLOCK
    n_wb = (WINDOW + Q) // Q
    r = np.arange(Q)[:, None]
    tok = np.arange(NSA_HPG * Q)[None, :] % Q
    causal = np.where(r <= tok, 0.0, NEG_INF)
    leaving = np.where(r > tok, 0.0, NEG_INF)
    masks = np.zeros((n_wb, n_wb * Q, NSA_HPG * Q), np.float32)
    for d in range(n_wb):
        masks[d, d * Q:(d + 1) * Q] = causal
        masks[d, (d + 1) * Q:] = NEG_INF
    masks[n_wb - 1, 0:Q] = leaving
    return jnp.asarray(masks)


def _compress_weights(pos, w1, w2):
    G, dh = NSA_KV_GROUPS, NSA_HEAD_DIM
    pos_rows = jnp.tile(pos, (1, G))
    w1r = w1.astype(BF16).reshape(CMP_BLOCK, dh, CMP_HIDDEN)
    w1_bd = jnp.einsum('ab,lij->laibj', jnp.eye(G, dtype=BF16), w1r).reshape(CMP_BLOCK, G * dh, G * CMP_HIDDEN)
    return pos_rows, w1_bd, w2.astype(BF16)


def _odd_layer(x, positions_tables, ret_tables, c2st, wmask, gpre, w_in, cmp_k_pos, cmp_k_w1, cmp_k_w2,
               cmp_v_pos, cmp_v_w1, cmp_v_w2, gn_g, w_out, gpost, ffn):
    b, t, d = x.shape
    w, wr, wt = _odd_in_weight(w_in)
    (q, kc, vc, ksa, kw, vt, gt, rq, rk, rv, rg) = _odd_proj(x, gpre, w, wr, wt, *positions_tables)
    kpos, kw1, kw2 = _compress_weights(cmp_k_pos, cmp_k_w1, cmp_k_w2)
    vpos, vw1, vw2 = _compress_weights(cmp_v_pos, cmp_v_w1, cmp_v_w2)
    kcm, vct = _compress(kc, vc, kpos, vpos, kw1, vw1, kw2, vw2.T)
    yc = _nsa(q, kcm, vct, c2st, wmask, ksa, kw, vt, gt)
    yd = _retention(rq, rk, rv, rg, *ret_tables, gn_g.reshape(RET_HEADS, 1, RET_HEAD_DIM))
    out = _odd_out(x.reshape(b * t, d), yc.reshape(b * t, NSA_WIDTH), yd.reshape(b * t, RET_WIDTH),
                   w_out, gpost, ffn)
    return out.reshape(b, t, d)


def kernel(x, positions, ln_mix_pre, ln_mix_post, ln_ffn_pre, ln_ffn_post, ffn_w_gate, ffn_w_up, ffn_w_down,
           ev_w_in, ev_pool_w, ev_pool_scale, ev_sgu_ln_g, ev_sgu_ln_b, ev_sgu_w, ev_sgu_b, ev_w_out,
           od_w_in, od_cmp_k_pos, od_cmp_k_w1, od_cmp_k_w2, od_cmp_v_pos, od_cmp_v_w1, od_cmp_v_w2,
           od_ret_gn_g, od_w_out):
    b, t, d = x.shape
    depth = ln_mix_pre.shape[0]
    rope = _rope_tables(positions)
    ret_tables = _retention_tables()
    c2st = _cmp_to_sel_t(t)
    wmask = _window_masks()
    wg_all, wu_all, wd_all = ffn_w_gate.astype(BF16), ffn_w_up.astype(BF16), ffn_w_down.astype(BF16)
    ev_in_all, ev_out_all, od_out_all = ev_w_in.astype(BF16), ev_w_out.astype(BF16), od_w_out.astype(BF16)
    for layer in range(depth):
        gpre = ln_mix_pre[layer].reshape(1, d)
        gpost = ln_mix_post[layer].reshape(1, d)
        ffn = (layer, ln_ffn_pre[layer].reshape(1, d), ln_ffn_post[layer].reshape(1, d), wg_all, wu_all, wd_all)
        if layer % 2 == 0:
            e = layer // 2
            x = _even_layer(x, gpre, (ev_in_all, e), ev_pool_w[e].astype(BF16),
                            ev_pool_scale[e].reshape(1, POOL_WIDTH), ev_sgu_ln_g[e].reshape(1, SGU_WIDTH),
                            ev_sgu_ln_b[e].reshape(1, SGU_WIDTH), ev_sgu_w[e], ev_sgu_b[e].T,
                            (ev_out_all, e), gpost, ffn)
        else:
            o = layer // 2
            x = _odd_layer(x, rope, ret_tables, c2st, wmask, gpre, od_w_in[o], od_cmp_k_pos[o], od_cmp_k_w1[o],
                           od_cmp_k_w2[o], od_cmp_v_pos[o], od_cmp_v_w1[o], od_cmp_v_w2[o], od_ret_gn_g[o],
                           (od_out_all, o), gpost, ffn)
    return x
```

```python
import functools

import numpy as np
import jax
import jax.numpy as jnp
from jax import lax
from jax.experimental import pallas as pl
from jax.experimental.pallas import tpu as pltpu

F32 = jnp.float32
BF16 = jnp.bfloat16

D_MODEL = 1024
POOL_WINDOWS = (2, 4, 8, 16)
POOL_GROUPS = 4
POOL_WIDTH = D_MODEL // 2
POOL_GDIM = POOL_WIDTH // POOL_GROUPS
POOL_HIST = 16
SGU_GROUPS = 4
SGU_WIDTH = D_MODEL // 2
SGU_GDIM = SGU_WIDTH // SGU_GROUPS
SGU_CHUNK = 128
EVEN_IN = POOL_WIDTH + 2 * SGU_WIDTH
NSA_HEADS = 8
NSA_KV_GROUPS = 2
NSA_HPG = NSA_HEADS // NSA_KV_GROUPS
NSA_HEAD_DIM = 64
NSA_WIDTH = NSA_HEADS * NSA_HEAD_DIM
NSA_GROUP_WIDTH = NSA_HPG * NSA_HEAD_DIM
NSA_KV_WIDTH = NSA_KV_GROUPS * NSA_HEAD_DIM
NSA_BRANCHES = 3
GATE_ROWS = 16
CMP_BLOCK = 32
CMP_STRIDE = 16
CMP_HIDDEN = 128
SEL_BLOCK = 64
SEL_TOPK = 16
SEL_PAD = 64
WINDOW = 512
Q_BLOCK = 256
FORCE_SCORE = 1.0e4
ROPE_THETA = 500000.0
ROPE_DIM = NSA_HEAD_DIM // 4
RET_HEADS = 4
RET_HEAD_DIM = 128
RET_WIDTH = RET_HEADS * RET_HEAD_DIM
RET_CHUNK = 128
RET_THETA = 10000.0
ODD_MIX = NSA_WIDTH + RET_WIDTH
NEG_INF = -1.0e30
LOG2_E = 1.4426950408889634
LANES = 128
SUBLANES = 8
BF16_SUBLANES = 2 * SUBLANES
MXU_COLS = 256

VMEM_LIMIT = 56 * 1024 * 1024
FFN_ROWS = 512
EVEN_ROWS = 512
PROJ_ROWS = 512
SEL_KEYS = Q_BLOCK
RANK_CHUNK = 8
RET_ROWS = 512


def _cparams(sem):
    return pltpu.CompilerParams(dimension_semantics=sem, vmem_limit_bytes=VMEM_LIMIT)


def _const_spec(shape):
    nd = len(shape)
    return pl.BlockSpec(shape, lambda *_: (0,) * nd, pipeline_mode=pl.Buffered(1))


def _dot(a, b):
    return jnp.dot(a, b, preferred_element_type=F32)


def _dot_nt(a, b):
    return lax.dot_general(a, b, (((1,), (1,)), ((), ())), preferred_element_type=F32)


def _dot_tn(a, b):
    return lax.dot_general(a, b, (((0,), (0,)), ((), ())), preferred_element_type=F32)


def _rms(x, g, eps=1e-6):
    return x * lax.rsqrt(jnp.mean(x * x, axis=-1, keepdims=True) + eps) * g


def _eye(n):
    return jnp.where(lax.broadcasted_iota(jnp.int32, (n, n), 0) == lax.broadcasted_iota(jnp.int32, (n, n), 1),
                     1.0, 0.0).astype(BF16)


def _ffn_tile(x, gpre_ref, gpost_ref, wg_ref, wu_ref, wd_ref):
    h = _rms(x, gpre_ref[...]).astype(BF16)
    f_total = wg_ref.shape[1]
    fc = MXU_COLS
    assert f_total % fc == 0
    acc = None
    for c in range(f_total // fc):
        gate = _dot(h, wg_ref[:, c * fc:(c + 1) * fc])
        up = _dot(h, wu_ref[:, c * fc:(c + 1) * fc])
        act = (gate * jax.nn.sigmoid(gate) * up).astype(BF16)
        part = _dot(act, wd_ref[c * fc:(c + 1) * fc, :])
        acc = part if acc is None else acc + part
    return x + _rms(acc, gpost_ref[...])


def _layer_spec(stacked_shape, layer):
    nd = len(stacked_shape) - 1
    return pl.BlockSpec((None,) + tuple(stacked_shape[1:]), lambda *_: (layer,) + (0,) * nd,
                        pipeline_mode=pl.Buffered(1))


def _weight_spec(w):
    return _layer_spec(w[0].shape, w[1]) if isinstance(w, tuple) else _const_spec(w.shape)


def _weight_array(w):
    return w[0] if isinstance(w, tuple) else w


def _ffn_specs(ffn):
    layer, fpre, fpost, wg, wu, wd = ffn
    return [_const_spec(fpre.shape), _const_spec(fpost.shape), _layer_spec(wg.shape, layer),
            _layer_spec(wu.shape, layer), _layer_spec(wd.shape, layer)]


def _even_body(x_ref, gpre_ref, win_ref, poolw_ref, pscale_ref, lng_ref, lnb_ref, sguw_ref, sgub_ref,
               wout_ref, gpost_ref, fpre_ref, fpost_ref, wg_ref, wu_ref, wd_ref, o_ref, hist_ref):
    tt = x_ref.shape[1]
    j = pl.program_id(1)

    @pl.when(j == 0)
    def _():
        hist_ref[0:POOL_HIST, :] = jnp.zeros((POOL_HIST, POOL_WIDTH), F32)

    x = x_ref[0]
    h = _rms(x, gpre_ref[...]).astype(BF16)
    z = _dot(h, win_ref[...])
    a = z[:, :POOL_WIDTH]
    u = z[:, POOL_WIDTH:POOL_WIDTH + SGU_WIDTH]
    v = z[:, POOL_WIDTH + SGU_WIDTH:]

    hist_ref[POOL_HIST:POOL_HIST + tt, :] = a
    t_pos = j * tt + lax.broadcasted_iota(jnp.int32, (tt, 1), 0)
    ya = []
    for gi, w in enumerate(POOL_WINDOWS):
        cols = slice(gi * POOL_GDIM, (gi + 1) * POOL_GDIM)
        win_sum = a[:, cols]
        for s in range(1, w):
            win_sum = win_sum + hist_ref[POOL_HIST - s:POOL_HIST - s + tt, cols]
        cnt = jnp.minimum(t_pos + 1, w).astype(F32)
        diff = (win_sum / cnt - a[:, cols]).astype(BF16)
        ya.append(_dot(diff, poolw_ref[gi]))
    ya = jnp.concatenate(ya, axis=1) * pscale_ref[...]
    hist_ref[0:POOL_HIST, :] = hist_ref[tt:tt + POOL_HIST, :]

    ug = jax.nn.gelu(u)
    vg = jax.nn.gelu(v)
    mu = jnp.mean(vg, axis=-1, keepdims=True)
    var = jnp.mean(jnp.square(vg - mu), axis=-1, keepdims=True)
    vn = ((vg - mu) * lax.rsqrt(var + 1e-5) * lng_ref[...] + lnb_ref[...]).astype(BF16)
    r_i = lax.broadcasted_iota(jnp.int32, (SGU_CHUNK, SGU_CHUNK), 0)
    c_i = lax.broadcasted_iota(jnp.int32, (SGU_CHUNK, SGU_CHUNK), 1)
    yb = []
    for g in range(SGU_GROUPS):
        cols = slice(g * SGU_GDIM, (g + 1) * SGU_GDIM)
        wm = jnp.where(r_i >= c_i, sguw_ref[g], 0.0).astype(BF16)
        bcol = sgub_ref[:, g:g + 1]
        parts = []
        for c in range(tt // SGU_CHUNK):
            rows = slice(c * SGU_CHUNK, (c + 1) * SGU_CHUNK)
            parts.append(ug[rows, cols] * (_dot(wm, vn[rows, cols]) + bcol))
        yb.append(jnp.concatenate(parts, axis=0))
    yb = jnp.concatenate(yb, axis=1)

    m = _dot(ya.astype(BF16), wout_ref[0:POOL_WIDTH, :]) + _dot(yb.astype(BF16), wout_ref[POOL_WIDTH:, :])
    o_ref[0] = _ffn_tile(x + _rms(m, gpost_ref[...]), fpre_ref, fpost_ref, wg_ref, wu_ref, wd_ref)


def _even_layer(x, gpre, win, poolw, pscale, lng, lnb, sguw, sgub_t, wout, gpost, ffn):
    b, t, d = x.shape
    tt = EVEN_ROWS
    row = pl.BlockSpec((1, tt, d), lambda bi, j: (bi, j, 0))
    return pl.pallas_call(
        _even_body,
        grid=(b, t // tt),
        in_specs=[row, _const_spec((1, d)), _weight_spec(win), _const_spec(poolw.shape),
                  _const_spec(pscale.shape), _const_spec(lng.shape), _const_spec(lnb.shape),
                  _const_spec(sguw.shape), _const_spec(sgub_t.shape), _weight_spec(wout),
                  _const_spec((1, d))] + _ffn_specs(ffn),
        out_specs=row,
        out_shape=jax.ShapeDtypeStruct((b, t, d), F32),
        scratch_shapes=[pltpu.VMEM((POOL_HIST + tt, POOL_WIDTH), F32)],
        compiler_params=_cparams(("arbitrary", "arbitrary")),
        name="even_layer",
    )(x, gpre, _weight_array(win), poolw, pscale, lng, lnb, sguw, sgub_t, _weight_array(wout), gpost, *ffn[1:])


_C_Q = 0
_C_KC = _C_Q + NSA_WIDTH
_C_VC = _C_KC + NSA_KV_WIDTH
_C_KS = _C_VC + NSA_KV_WIDTH
_C_KW = _C_KS + NSA_KV_WIDTH
_C_RQ = _C_KW + NSA_KV_WIDTH
_C_RK = _C_RQ + RET_WIDTH
_C_RV = _C_RK + RET_WIDTH
_C_RG = _C_RV + RET_WIDTH
ODD_COLS = _C_RG + RET_WIDTH
_R_VS = 0
_R_VW = _R_VS + NSA_KV_WIDTH
_R_GT = _R_VW + NSA_KV_WIDTH
ODD_TROWS = _R_GT + NSA_KV_GROUPS * GATE_ROWS
V_AUG = NSA_HEAD_DIM + BF16_SUBLANES
VT_ROWS = 2 * NSA_KV_GROUPS * V_AUG


def _rope_nsa(z, c, sa, sb):
    half = ROPE_DIM // 2
    return z * c + pltpu.roll(z, LANES - half, 1) * sa + pltpu.roll(z, half, 1) * sb


def _split3(x):
    a = x.astype(BF16)
    r = x - a.astype(F32)
    b = r.astype(BF16)
    return a, b, (r - b.astype(F32)).astype(BF16)


def _proj_body(x_ref, gpre_ref, w_ref, wr_ref, wt_ref, cs_ref, place_ref, base_ref, cr_ref, sr_ref,
               q_ref, kc_ref, vc_ref, ksa_ref, kw_ref, vt_ref, gt_ref, rq_ref, rk_ref, rv_ref, rg_ref):
    tm = x_ref.shape[1]
    j = pl.program_id(1)
    x = x_ref[0]
    h = _rms(x, gpre_ref[...]).astype(BF16)
    tables = base_ref[...]
    for part in _split3(cs_ref[0]):
        tables = tables + _dot(part, place_ref[...])
    nc, nsa, nsb = tables[:, 0:LANES], tables[:, LANES:2 * LANES], tables[:, 2 * LANES:3 * LANES]
    cr, sr = cr_ref[0], sr_ref[0]
    rc = jnp.concatenate([cr, cr], axis=1)
    rs = jnp.concatenate([-sr, sr], axis=1)

    def cols(start, width):
        if start >= _C_RQ:
            return _dot(h, wr_ref[:, start - _C_RQ:start - _C_RQ + width])
        return _dot(h, w_ref[:, start:start + width])

    zq = cols(_C_Q, NSA_WIDTH)
    q_scale = NSA_HEAD_DIM ** -0.5 * LOG2_E
    for s in range(NSA_WIDTH // LANES):
        sl = slice(s * LANES, (s + 1) * LANES)
        q_ref[0, :, sl] = (_rope_nsa(zq[:, sl], nc, nsa, nsb) * q_scale).astype(BF16)

    zkv = cols(_C_KC, _C_RQ - _C_KC)
    kc_ref[0] = _rope_nsa(zkv[:, _C_KC - _C_KC:_C_VC - _C_KC], nc, nsa, nsb)
    vc_ref[0] = zkv[:, _C_VC - _C_KC:_C_KS - _C_KC]
    ks = _rope_nsa(zkv[:, _C_KS - _C_KC:_C_KW - _C_KC], nc, nsa, nsb)
    kw = _rope_nsa(zkv[:, _C_KW - _C_KC:_C_RQ - _C_KC], nc, nsa, nsb)
    t_pos = j * tm + lax.broadcasted_iota(jnp.int32, (tm, SEL_PAD), 0)
    blk = lax.broadcasted_iota(jnp.int32, (tm, SEL_PAD), 1)
    onehot = jnp.where(t_pos // SEL_BLOCK == blk, 1.0, 0.0).astype(BF16)
    for g in range(NSA_KV_GROUPS):
        sl = slice(g * NSA_HEAD_DIM, (g + 1) * NSA_HEAD_DIM)
        ksa_ref[0, g] = jnp.concatenate([ks[:, sl].astype(BF16), onehot], axis=1)
        kw_ref[0, g] = kw[:, sl].astype(BF16)

    zt = _dot_nt(wt_ref[...], h)
    ones_rows = jnp.where(lax.broadcasted_iota(jnp.int32, (V_AUG - NSA_HEAD_DIM, tm), 0) == 0, 1.0, 0.0).astype(BF16)
    for k in range(2 * NSA_KV_GROUPS):
        vt_ref[0, k * V_AUG:k * V_AUG + NSA_HEAD_DIM] = zt[k * NSA_HEAD_DIM:(k + 1) * NSA_HEAD_DIM].astype(BF16)
        vt_ref[0, k * V_AUG + NSA_HEAD_DIM:(k + 1) * V_AUG] = ones_rows
    gt_ref[0] = jax.nn.sigmoid(zt[_R_GT:])

    k_scale = RET_HEAD_DIM ** -0.5
    zrq = cols(_C_RQ, RET_WIDTH)
    zrk = cols(_C_RK, RET_WIDTH)
    for hh in range(RET_HEADS):
        sl = slice(hh * RET_HEAD_DIM, (hh + 1) * RET_HEAD_DIM)
        zq_h, zk_h = zrq[:, sl], zrk[:, sl]
        rq_ref[0, :, sl] = (zq_h * rc + pltpu.roll(zq_h, RET_HEAD_DIM // 2, 1) * rs).astype(BF16)
        rk_ref[0, :, sl] = ((zk_h * rc + pltpu.roll(zk_h, RET_HEAD_DIM // 2, 1) * rs) * k_scale).astype(BF16)
    rv_ref[0] = cols(_C_RV, RET_WIDTH).astype(BF16)
    rg_ref[0] = cols(_C_RG, RET_WIDTH)


def _odd_proj(x, gpre, w, wr, wt, cs, place, base, cr, sr):
    b, t, d = x.shape
    tm = PROJ_ROWS
    G = NSA_KV_GROUPS

    def row(width):
        return pl.BlockSpec((1, tm, width), lambda bi, j: (bi, j, 0))

    def grp(width):
        return pl.BlockSpec((1, G, tm, width), lambda bi, j: (bi, 0, j, 0))

    def feat(rows):
        return pl.BlockSpec((1, rows, tm), lambda bi, j: (bi, 0, j))

    out_shape = [
        jax.ShapeDtypeStruct((b, t, NSA_WIDTH), BF16),
        jax.ShapeDtypeStruct((b, t, LANES), F32),
        jax.ShapeDtypeStruct((b, t, LANES), F32),
        jax.ShapeDtypeStruct((b, G, t, LANES), BF16),
        jax.ShapeDtypeStruct((b, G, t, NSA_HEAD_DIM), BF16),
        jax.ShapeDtypeStruct((b, VT_ROWS, t), BF16),
        jax.ShapeDtypeStruct((b, G * GATE_ROWS, t), F32),
        jax.ShapeDtypeStruct((b, t, RET_WIDTH), BF16),
        jax.ShapeDtypeStruct((b, t, RET_WIDTH), BF16),
        jax.ShapeDtypeStruct((b, t, RET_WIDTH), BF16),
        jax.ShapeDtypeStruct((b, t, RET_WIDTH), F32),
    ]
    out_specs = [row(NSA_WIDTH), row(LANES), row(LANES), grp(LANES), grp(NSA_HEAD_DIM), feat(VT_ROWS),
                 feat(G * GATE_ROWS), row(RET_WIDTH), row(RET_WIDTH), row(RET_WIDTH), row(RET_WIDTH)]
    return pl.pallas_call(
        _proj_body,
        grid=(b, t // tm),
        in_specs=[row(d), _const_spec((1, d)), _const_spec(w.shape), _const_spec(wr.shape), _const_spec(wt.shape),
                  row(cs.shape[2]),
                  _const_spec(place.shape), _const_spec(base.shape), row(cr.shape[2]), row(sr.shape[2])],
        out_specs=out_specs,
        out_shape=out_shape,
        compiler_params=_cparams(("arbitrary", "arbitrary")),
        name="odd_proj",
    )(x, gpre, w, wr, wt, cs, place, base, cr, sr)


def _compress_body(k_ref, v_ref, kpos_ref, vpos_ref, kw1_ref, vw1_ref, kw2_ref, vw2t_ref, kcm_ref, vct_ref):
    nc = k_ref.shape[1] // CMP_STRIDE
    half = CMP_BLOCK // CMP_STRIDE

    def hidden(x_ref, pos_ref, w1_ref):
        parts = [None] * half
        for off in range(CMP_STRIDE):
            tok = x_ref[0, pl.ds(off, nc, stride=CMP_STRIDE), :]
            for h in range(half):
                l = h * CMP_STRIDE + off
                term = _dot((tok + pos_ref[l:l + 1, :]).astype(BF16), w1_ref[l])
                parts[h] = term if parts[h] is None else parts[h] + term
        pre = parts[0]
        for h in range(1, half):
            pre = pre + pltpu.roll(parts[h], nc - h, 0)
        return jax.nn.gelu(pre).astype(BF16)

    hk = hidden(k_ref, kpos_ref, kw1_ref)
    hv = hidden(v_ref, vpos_ref, vw1_ref)
    for g in range(NSA_KV_GROUPS):
        sl = slice(g * CMP_HIDDEN, (g + 1) * CMP_HIDDEN)
        kcm_ref[0, g] = _dot(hk[:, sl], kw2_ref[...]).astype(BF16)
        vct_ref[0, g] = _dot_nt(vw2t_ref[...], hv[:, sl]).astype(BF16)


def _compress(k, v, kpos, vpos, kw1, vw1, kw2, vw2t):
    b, t, width = k.shape
    nc = t // CMP_STRIDE
    G = NSA_KV_GROUPS
    row = pl.BlockSpec((1, t, width), lambda bi: (bi, 0, 0))
    return pl.pallas_call(
        _compress_body,
        grid=(b,),
        in_specs=[row, row, _const_spec(kpos.shape), _const_spec(vpos.shape), _const_spec(kw1.shape),
                  _const_spec(vw1.shape), _const_spec(kw2.shape), _const_spec(vw2t.shape)],
        out_specs=[pl.BlockSpec((1, G, nc, NSA_HEAD_DIM), lambda bi: (bi, 0, 0, 0)),
                   pl.BlockSpec((1, G, NSA_HEAD_DIM, nc), lambda bi: (bi, 0, 0, 0))],
        out_shape=[jax.ShapeDtypeStruct((b, G, nc, NSA_HEAD_DIM), BF16),
                   jax.ShapeDtypeStruct((b, G, NSA_HEAD_DIM, nc), BF16)],
        compiler_params=_cparams(("arbitrary",)),
        name="compress",
    )(k, v, kpos, vpos, kw1, vw1, kw2, vw2t)


def _nsa_body(q_ref, kcm_ref, vct_ref, c2st_ref, wmask_ref, ksa_ref, kw_ref, vt_ref, gt_ref, o_ref,
              score_ref, rank_ref, sc_ref, sw_ref, pw_ref, sa_ref, sb_ref, pa_ref, pb_ref, lhs_ref,
              m_ref, mta_ref, mtb_ref, mw_ref, acc_ref, *, top_k):
    G, M, Q, dh = NSA_KV_GROUPS, NSA_HPG, Q_BLOCK, NSA_HEAD_DIM
    cols = M * Q
    nc = kcm_ref.shape[2]
    i = pl.program_id(1)
    t0 = i * Q
    tq = t0 + (lax.broadcasted_iota(jnp.int32, (1, cols), 1) & (Q - 1))
    groups = range(G)

    q_heads, q_rows = [], []
    for g in groups:
        qb = q_ref[0, :, g * NSA_GROUP_WIDTH:(g + 1) * NSA_GROUP_WIDTH]
        q_heads.append([qb[:, m * dh:(m + 1) * dh] for m in range(M)])
        q_rows.append(jnp.concatenate(q_heads[g], axis=0))

    n_idx = lax.broadcasted_iota(jnp.int32, (nc, 1), 0)
    cmp_ok = (n_idx * CMP_STRIDE + (CMP_BLOCK - 1) <= tq) & (n_idx < nc - 1)
    j_idx = lax.broadcasted_iota(jnp.int32, (SEL_PAD, Q), 0)
    cur = (t0 + lax.broadcasted_iota(jnp.int32, (SEL_PAD, Q), 1)) // SEL_BLOCK
    forced = (j_idx == 0) | (j_idx == cur) | (j_idx == cur - 1)
    valid_tok = (tq >= CMP_BLOCK - 1).astype(F32)
    o_cmp = []
    for g in groups:
        sc = sc_ref.at[g]
        sc[...] = jnp.where(cmp_ok, _dot_nt(kcm_ref[0, g], q_rows[g]), NEG_INF)
        sc[...] = jnp.exp2(sc[...] - jnp.max(sc[...], axis=0, keepdims=True))
        norm = valid_tok / jnp.sum(sc[...], axis=0, keepdims=True)
        o_cmp.append(_dot(vct_ref[0, g], sc[...].astype(BF16)) * norm)
        p_grp = sc[:, 0:Q] * norm[:, 0:Q]
        for m in range(1, M):
            p_grp = p_grp + sc[:, m * Q:(m + 1) * Q] * norm[:, m * Q:(m + 1) * Q]
        p_hi = p_grp.astype(BF16)
        p_lo = (p_grp - p_hi.astype(F32)).astype(BF16)
        imp = _dot(c2st_ref[...], p_hi) + _dot(c2st_ref[...], p_lo)
        score_ref[g] = jnp.where(j_idx <= cur, jnp.where(forced, FORCE_SCORE, imp), -1.0)

    def v_aug(branch, g, k0, n):
        r0 = (branch * G + g) * V_AUG
        return vt_ref[0, r0:r0 + V_AUG, pl.ds(k0, n)]

    span = WINDOW + Q
    w0 = pl.multiple_of(jnp.maximum(t0 - WINDOW, 0), Q)
    n_wb = span // Q
    diag_b = jnp.minimum(i, n_wb - 1)
    tri_causal = wmask_ref[n_wb - 1, (n_wb - 1) * Q:n_wb * Q, :]
    for g in groups:
        s_win = _dot_nt(kw_ref[0, g, pl.ds(w0, span), :], q_rows[g]) + wmask_ref[diag_b]
        sw_ref[g] = s_win
        mw_ref[g] = jnp.max(s_win, axis=0, keepdims=True)

    n_causal = (t0 + Q) // SEL_BLOCK
    rank_ref[...] = jnp.zeros(rank_ref.shape, jnp.int32)
    row_in_tile = lax.broadcasted_iota(jnp.int32, (SUBLANES, LANES), 0)
    for c0 in range(0, SEL_PAD, RANK_CHUNK):
        @pl.when(c0 < n_causal)
        def _(c0=c0):
            for g in groups:
                for l0 in range(0, Q, LANES):
                    ln = slice(l0, l0 + LANES)
                    others = [score_ref[g, k:k + 1, ln] for k in range(c0, c0 + RANK_CHUNK)]
                    for r0 in range(0, SEL_PAD, SUBLANES):
                        mine = score_ref[g, r0:r0 + SUBLANES, ln]
                        count = rank_ref[g, r0:r0 + SUBLANES, ln]
                        for k, other in zip(range(c0, c0 + RANK_CHUNK), others):
                            if r0 > k:
                                beats = other >= mine
                            elif r0 + SUBLANES <= k:
                                beats = other > mine
                            else:
                                beats = (other > mine) | ((other == mine) & (k < r0 + row_in_tile))
                            count = count + beats.astype(jnp.int32)
                        rank_ref[g, r0:r0 + SUBLANES, ln] = count

    eye_q = _eye(Q)
    for g in groups:
        chosen = jnp.where((rank_ref[g] < top_k) & (score_ref[g] >= 0.0), 1.0, 0.0).astype(BF16)
        chosen_t = _dot_nt(eye_q, chosen)
        bias = jnp.where(chosen_t > 0.5, 0.0, NEG_INF).astype(BF16)
        for m in range(M):
            lhs_ref[g, m * Q:(m + 1) * Q, :] = jnp.concatenate([q_heads[g][m], bias], axis=1)
        m_ref[g] = jnp.full((1, cols), NEG_INF, F32)
        acc_ref[g] = jnp.zeros((V_AUG, cols), F32)

    buf_a = (sa_ref, mta_ref, pa_ref)
    buf_b = (sb_ref, mtb_ref, pb_ref)

    def put_scores(buf, g, kt):
        s_dst, max_dst, _ = buf
        k0 = pl.multiple_of(kt * SEL_KEYS, SEL_KEYS)
        s = _dot_nt(ksa_ref[0, g, pl.ds(k0, SEL_KEYS), :], lhs_ref[g])
        s_dst[g] = s
        max_dst[g] = jnp.max(s, axis=0, keepdims=True)

    def sel_update(buf, g, kt, bias=None):
        src, tile_max, p_dst = buf
        k0 = pl.multiple_of(kt * SEL_KEYS, SEL_KEYS)
        m_i = m_ref[g]
        if bias is None:
            m_new = jnp.maximum(m_i, tile_max[g])
            p_dst[g] = jnp.exp2(src[g] - m_new).astype(BF16)
        else:
            m_new = jnp.maximum(m_i, jnp.max(src[g] + bias, axis=0, keepdims=True))
            p_dst[g] = jnp.exp2(src[g] + bias - m_new).astype(BF16)
        acc_ref[g] = jnp.exp2(m_i - m_new) * acc_ref[g] + _dot(v_aug(0, g, k0, SEL_KEYS), p_dst[g])
        m_ref[g] = m_new

    n_full = t0 // SEL_KEYS
    odd = n_full & 1
    for g in groups:
        put_scores(buf_a, g, 0)

    o_win = []
    for g in groups:
        pw_ref[g] = jnp.exp2(sw_ref[g] - mw_ref[g]).astype(BF16)
        acc_win = _dot(v_aug(1, g, w0, span), pw_ref[g])
        o_win.append(acc_win[0:dh] / acc_win[dh:dh + 1])

    @pl.when(odd == 1)
    def _():
        for g in groups:
            sel_update(buf_a, g, 0)
            put_scores(buf_a, g, 1)

    def sel_pair(pair, _):
        ta = odd + 2 * pair
        for g in groups:
            put_scores(buf_b, g, ta + 1)
        for g in groups:
            sel_update(buf_a, g, ta)
        for g in groups:
            put_scores(buf_a, g, ta + 2)
        for g in groups:
            sel_update(buf_b, g, ta + 1)
        return 0

    lax.fori_loop(0, n_full // 2, sel_pair, 0)
    o_sel = []
    for g in groups:
        sel_update(buf_a, g, n_full, bias=tri_causal)
        o_sel.append(acc_ref[g, 0:dh, :] / acc_ref[g, dh:dh + 1, :])

    for g in groups:
        gt = gt_ref[0, g * GATE_ROWS:(g + 1) * GATE_ROWS, :]
        heads = []
        for m in range(M):
            c = slice(m * Q, (m + 1) * Q)
            r = NSA_BRANCHES * m
            mixed = (o_cmp[g][:, c] * gt[r:r + 1] + o_sel[g][:, c] * gt[r + 1:r + 2]
                     + o_win[g][:, c] * gt[r + 2:r + 3])
            heads.append(mixed.astype(BF16))
        o_ref[0, :, g * NSA_GROUP_WIDTH:(g + 1) * NSA_GROUP_WIDTH] = _dot_nt(
            eye_q, jnp.concatenate(heads, axis=0)).astype(BF16)


def _nsa(q, kcm, vct, c2st, wmask, ksa, kw, vt, gt):
    b, t, _ = q.shape
    G = NSA_KV_GROUPS
    nc = kcm.shape[2]
    n_sel = t // SEL_BLOCK
    assert n_sel <= SEL_PAD and t >= WINDOW + Q_BLOCK and t % SEL_KEYS == 0
    top_k = min(SEL_TOPK, n_sel)
    cols, span = NSA_HPG * Q_BLOCK, WINDOW + Q_BLOCK

    def per_batch(shape):
        nd = len(shape)
        return pl.BlockSpec((1,) + shape, lambda bi, i: (bi,) + (0,) * nd)

    token_rows = pl.BlockSpec((1, Q_BLOCK, NSA_WIDTH), lambda bi, i: (bi, i, 0))
    return pl.pallas_call(
        functools.partial(_nsa_body, top_k=top_k),
        grid=(b, t // Q_BLOCK),
        in_specs=[token_rows, per_batch((G, nc, NSA_HEAD_DIM)), per_batch((G, NSA_HEAD_DIM, nc)),
                  _const_spec(c2st.shape), _const_spec(wmask.shape), per_batch((G, t, LANES)), per_batch((G, t, NSA_HEAD_DIM)),
                  per_batch((vt.shape[1], t)),
                  pl.BlockSpec((1, G * GATE_ROWS, Q_BLOCK), lambda bi, i: (bi, 0, i))],
        out_specs=token_rows,
        out_shape=jax.ShapeDtypeStruct((b, t, NSA_WIDTH), BF16),
        scratch_shapes=[pltpu.VMEM((G, SEL_PAD, Q_BLOCK), F32),
                        pltpu.VMEM((G, SEL_PAD, Q_BLOCK), jnp.int32),
                        pltpu.VMEM((G, nc, cols), F32),
                        pltpu.VMEM((G, span, cols), F32),
                        pltpu.VMEM((G, span, cols), BF16),
                        pltpu.VMEM((G, SEL_KEYS, cols), F32),
                        pltpu.VMEM((G, SEL_KEYS, cols), F32),
                        pltpu.VMEM((G, SEL_KEYS, cols), BF16),
                        pltpu.VMEM((G, SEL_KEYS, cols), BF16),
                        pltpu.VMEM((G, cols, 2 * NSA_HEAD_DIM), BF16),
                        pltpu.VMEM((G, 1, cols), F32),
                        pltpu.VMEM((G, 1, cols), F32),
                        pltpu.VMEM((G, 1, cols), F32),
                        pltpu.VMEM((G, 1, cols), F32),
                        pltpu.VMEM((G, V_AUG, cols), F32)],
        compiler_params=_cparams(("arbitrary", "arbitrary")),
        name="nsa",
    )(q, kcm, vct, c2st, wmask, ksa, kw, vt, gt)


def _ret_body(q_ref, k_ref, v_ref, g_ref, decay_ref, zeta_ref, xi_ref, gchunk_ref, gn_ref, o_ref, state_ref):
    C, d = RET_CHUNK, RET_HEAD_DIM

    @pl.when(pl.program_id(1) == 0)
    def _():
        state_ref[...] = jnp.zeros_like(state_ref)

    n_chunks = q_ref.shape[1] // C
    for h in range(RET_HEADS):
        hc = slice(h * d, (h + 1) * d)
        state = state_ref[h]
        before = []
        for c in range(n_chunks):
            rows = slice(c * C, (c + 1) * C)
            kz = (k_ref[0, rows, hc].astype(F32) * zeta_ref[h]).astype(BF16)
            before.append(state.astype(BF16))
            state = state * gchunk_ref[h] + _dot_tn(kz, v_ref[0, rows, hc])
        state_ref[h] = state
        for c in range(n_chunks):
            rows = slice(c * C, (c + 1) * C)
            q, k, v = q_ref[0, rows, hc], k_ref[0, rows, hc], v_ref[0, rows, hc]
            s = (_dot_nt(q, k) * decay_ref[h]).astype(BF16)
            o = _dot(s, v) + _dot((q.astype(F32) * xi_ref[h]).astype(BF16), before[c])
            mu = jnp.mean(o, axis=-1, keepdims=True)
            var = jnp.mean(jnp.square(o - mu), axis=-1, keepdims=True)
            o = (o - mu) * lax.rsqrt(var + 1e-5) * gn_ref[h]
            gate = g_ref[0, rows, hc]
            o_ref[0, rows, hc] = (gate * jax.nn.sigmoid(gate) * o).astype(BF16)


def _retention(rq, rk, rv, rg, decay, zeta, xi, gchunk, gn):
    b, t, width = rq.shape
    tok = pl.BlockSpec((1, RET_ROWS, width), lambda bi, c: (bi, c, 0))
    return pl.pallas_call(
        _ret_body,
        grid=(b, t // RET_ROWS),
        in_specs=[tok, tok, tok, tok, _const_spec(decay.shape), _const_spec(zeta.shape), _const_spec(xi.shape),
                  _const_spec(gchunk.shape), _const_spec(gn.shape)],
        out_specs=tok,
        out_shape=jax.ShapeDtypeStruct((b, t, width), BF16),
        scratch_shapes=[pltpu.VMEM((RET_HEADS, RET_HEAD_DIM, RET_HEAD_DIM), F32)],
        compiler_params=_cparams(("arbitrary", "arbitrary")),
        name="retention",
    )(rq, rk, rv, rg, decay, zeta, xi, gchunk, gn)


def _oproj_body(x_ref, yc_ref, yd_ref, w_ref, gpost_ref, fpre_ref, fpost_ref, wg_ref, wu_ref, wd_ref, o_ref):
    m = _dot(yc_ref[...], w_ref[0:NSA_WIDTH, :]) + _dot(yd_ref[...], w_ref[NSA_WIDTH:, :])
    o_ref[...] = _ffn_tile(x_ref[...] + _rms(m, gpost_ref[...]), fpre_ref, fpost_ref, wg_ref, wu_ref, wd_ref)


def _odd_out(x2, yc2, yd2, w, gpost, ffn):
    n, d = x2.shape
    tm = FFN_ROWS

    def row(width):
        return pl.BlockSpec((tm, width), lambda i: (i, 0))

    return pl.pallas_call(
        _oproj_body,
        grid=(n // tm,),
        in_specs=[row(d), row(NSA_WIDTH), row(RET_WIDTH), _weight_spec(w), _const_spec((1, d))]
        + _ffn_specs(ffn),
        out_specs=row(d),
        out_shape=jax.ShapeDtypeStruct((n, d), F32),
        compiler_params=_cparams(("arbitrary",)),
        name="odd_out_ffn",
    )(x2, yc2, yd2, _weight_array(w), gpost, *ffn[1:])


def _odd_in_weight(w_in):
    sizes = [NSA_WIDTH] + [NSA_KV_WIDTH] * 6 + [NSA_BRANCHES * NSA_HEADS] + [RET_WIDTH] * 4
    offs = np.concatenate([[0], np.cumsum(sizes)])
    w_in = w_in.astype(BF16)
    q, kc, vc, ks, vs, kw, vw, gt, rq, rk, rv, rg = [w_in[:, offs[n]:offs[n + 1]] for n in range(len(sizes))]
    w = jnp.concatenate([w_in[:, offs[0]:offs[4]], kw], axis=1)
    wr = w_in[:, offs[8]:offs[12]]
    per_group = NSA_BRANCHES * NSA_HPG
    gates = [jnp.pad(gt[:, g * per_group:(g + 1) * per_group], ((0, 0), (0, GATE_ROWS - per_group)))
             for g in range(NSA_KV_GROUPS)]
    wt = jnp.concatenate([vs, vw] + gates, axis=1).T
    return w, wr, wt


def _rope_tables(positions):
    pos = positions.astype(F32)[..., None]
    half = ROPE_DIM // 2
    inv = 1.0 / (ROPE_THETA ** (jnp.arange(0, ROPE_DIM, 2, dtype=F32) / ROPE_DIM))
    ang = pos * inv
    cs = jnp.concatenate([jnp.cos(ang), jnp.sin(ang)], axis=-1)
    place = np.zeros((ROPE_DIM, 3 * LANES), np.float32)
    base = np.zeros((1, 3 * LANES), np.float32)
    for head0 in range(0, LANES, NSA_HEAD_DIM):
        base[0, head0 + ROPE_DIM:head0 + NSA_HEAD_DIM] = 1.0
        for f in range(half):
            place[f, head0 + f] = 1.0
            place[f, head0 + half + f] = 1.0
            place[half + f, LANES + head0 + f] = -1.0
            place[half + f, 2 * LANES + head0 + half + f] = 1.0
    inv_r = 1.0 / (RET_THETA ** (jnp.arange(0, RET_HEAD_DIM, 2, dtype=F32) / RET_HEAD_DIM))
    ang_r = pos * inv_r
    return cs, jnp.asarray(place, dtype=BF16), jnp.asarray(base), jnp.cos(ang_r), jnp.sin(ang_r)


def _retention_tables():
    H, C, d = RET_HEADS, RET_CHUNK, RET_HEAD_DIM
    log_gamma = np.log1p(-np.exp2(-5.0 - np.arange(H, dtype=np.float64)))
    idx = np.arange(C, dtype=np.float64)
    rel = idx[:, None] - idx[None, :]
    decay = np.where(rel >= 0, np.exp(np.maximum(rel, 0.0)[None] * log_gamma[:, None, None]), 0.0)
    zeta = np.exp((C - 1 - idx)[None, :] * log_gamma[:, None])
    xi = np.exp((idx + 1.0)[None, :] * log_gamma[:, None])
    gchunk = np.exp(C * log_gamma)
    zeta_b = np.broadcast_to(zeta[:, :, None], (H, C, d))
    xi_b = np.broadcast_to(xi[:, :, None], (H, C, d))
    gchunk_b = np.broadcast_to(gchunk[:, None, None], (H, 1, d))
    return tuple(jnp.asarray(a, dtype=F32) for a in (decay, zeta_b, xi_b, gchunk_b))


def _cmp_to_sel_t(t_len):
    nc = t_len // CMP_STRIDE
    n_cmp = (t_len - CMP_BLOCK) // CMP_STRIDE + 1
    n_sel = t_len // SEL_BLOCK
    c_start = np.arange(nc) * CMP_STRIDE
    s_start = np.arange(SEL_PAD) * SEL_BLOCK
    hit = ((c_start[None, :] < s_start[:, None] + SEL_BLOCK) & (c_start[None, :] + CMP_BLOCK > s_start[:, None])
           & (np.arange(nc)[None, :] < n_cmp) & (np.arange(SEL_PAD)[:, None] < n_sel))
    return jnp.asarray(hit.astype(np.float32), dtype=BF16)


def _window_masks():
    Q = Q_BLOCK
    n_wb = (WINDOW + Q) // Q
    r = np.arange(Q)[:, None]
    tok = np.arange(NSA_HPG * Q)[None, :] % Q
    causal = np.where(r <= tok, 0.0, NEG_INF)
    leaving = np.where(r > tok, 0.0, NEG_INF)
    masks = np.zeros((n_wb, n_wb * Q, NSA_HPG * Q), np.float32)
    for d in range(n_wb):
        masks[d, d * Q:(d + 1) * Q] = causal
        masks[d, (d + 1) * Q:] = NEG_INF
    masks[n_wb - 1, 0:Q] = leaving
    return jnp.asarray(masks)


def _compress_weights(pos, w1, w2):
    G, dh = NSA_KV_GROUPS, NSA_HEAD_DIM
    pos_rows = jnp.tile(pos, (1, G))
    w1r = w1.astype(BF16).reshape(CMP_BLOCK, dh, CMP_HIDDEN)
    w1_bd = jnp.einsum('ab,lij->laibj', jnp.eye(G, dtype=BF16), w1r).reshape(CMP_BLOCK, G * dh, G * CMP_HIDDEN)
    return pos_rows, w1_bd, w2.astype(BF16)


def _odd_layer(x, positions_tables, ret_tables, c2st, wmask, gpre, w_in, cmp_k_pos, cmp_k_w1, cmp_k_w2,
               cmp_v_pos, cmp_v_w1, cmp_v_w2, gn_g, w_out, gpost, ffn):
    b, t, d = x.shape
    w, wr, wt = _odd_in_weight(w_in)
    (q, kc, vc, ksa, kw, vt, gt, rq, rk, rv, rg) = _odd_proj(x, gpre, w, wr, wt, *positions_tables)
    kpos, kw1, kw2 = _compress_weights(cmp_k_pos, cmp_k_w1, cmp_k_w2)
    vpos, vw1, vw2 = _compress_weights(cmp_v_pos, cmp_v_w1, cmp_v_w2)
    kcm, vct = _compress(kc, vc, kpos, vpos, kw1, vw1, kw2, vw2.T)
    yc = _nsa(q, kcm, vct, c2st, wmask, ksa, kw, vt, gt)
    yd = _retention(rq, rk, rv, rg, *ret_tables, gn_g.reshape(RET_HEADS, 1, RET_HEAD_DIM))
    out = _odd_out(x.reshape(b * t, d), yc.reshape(b * t, NSA_WIDTH), yd.reshape(b * t, RET_WIDTH),
                   w_out, gpost, ffn)
    return out.reshape(b, t, d)


def kernel(x, positions, ln_mix_pre, ln_mix_post, ln_ffn_pre, ln_ffn_post, ffn_w_gate, ffn_w_up, ffn_w_down,
           ev_w_in, ev_pool_w, ev_pool_scale, ev_sgu_ln_g, ev_sgu_ln_b, ev_sgu_w, ev_sgu_b, ev_w_out,
           od_w_in, od_cmp_k_pos, od_cmp_k_w1, od_cmp_k_w2, od_cmp_v_pos, od_cmp_v_w1, od_cmp_v_w2,
           od_ret_gn_g, od_w_out):
    b, t, d = x.shape
    depth = ln_mix_pre.shape[0]
    rope = _rope_tables(positions)
    ret_tables = _retention_tables()
    c2st = _cmp_to_sel_t(t)
    wmask = _window_masks()
    wg_all, wu_all, wd_all = ffn_w_gate.astype(BF16), ffn_w_up.astype(BF16), ffn_w_down.astype(BF16)
    ev_in_all, ev_out_all, od_out_all = ev_w_in.astype(BF16), ev_w_out.astype(BF16), od_w_out.astype(BF16)
    for layer in range(depth):
        gpre = ln_mix_pre[layer].reshape(1, d)
        gpost = ln_mix_post[layer].reshape(1, d)
        ffn = (layer, ln_ffn_pre[layer].reshape(1, d), ln_ffn_post[layer].reshape(1, d), wg_all, wu_all, wd_all)
        if layer % 2 == 0:
            e = layer // 2
            x = _even_layer(x, gpre, (ev_in_all, e), ev_pool_w[e].astype(BF16),
                            ev_pool_scale[e].reshape(1, POOL_WIDTH), ev_sgu_ln_g[e].reshape(1, SGU_WIDTH),
                            ev_sgu_ln_b[e].reshape(1, SGU_WIDTH), ev_sgu_w[e], ev_sgu_b[e].T,
                            (ev_out_all, e), gpost, ffn)
        else:
            o = layer // 2
            x = _odd_layer(x, rope, ret_tables, c2st, wmask, gpre, od_w_in[o], od_cmp_k_pos[o], od_cmp_k_w1[o],
                           od_cmp_k_w2[o], od_cmp_v_pos[o], od_cmp_v_w1[o], od_cmp_v_w2[o], od_ret_gn_g[o],
                           (od_out_all, o), gpost, ffn)
    return x
```

```python
import functools

import numpy as np
import jax
import jax.numpy as jnp
from jax import lax
from jax.experimental import pallas as pl
from jax.experimental.pallas import tpu as pltpu

F32 = jnp.float32
BF16 = jnp.bfloat16

D_MODEL = 1024
POOL_WINDOWS = (2, 4, 8, 16)
POOL_GROUPS = 4
POOL_WIDTH = D_MODEL // 2
POOL_GDIM = POOL_WIDTH // POOL_GROUPS
POOL_HIST = 16
SGU_GROUPS = 4
SGU_WIDTH = D_MODEL // 2
SGU_GDIM = SGU_WIDTH // SGU_GROUPS
SGU_CHUNK = 128
EVEN_IN = POOL_WIDTH + 2 * SGU_WIDTH
NSA_HEADS = 8
NSA_KV_GROUPS = 2
NSA_HPG = NSA_HEADS // NSA_KV_GROUPS
NSA_HEAD_DIM = 64
NSA_WIDTH = NSA_HEADS * NSA_HEAD_DIM
NSA_GROUP_WIDTH = NSA_HPG * NSA_HEAD_DIM
NSA_KV_WIDTH = NSA_KV_GROUPS * NSA_HEAD_DIM
NSA_BRANCHES = 3
GATE_ROWS = 16
CMP_BLOCK = 32
CMP_STRIDE = 16
CMP_HIDDEN = 128
SEL_BLOCK = 64
SEL_TOPK = 16
SEL_PAD = 64
WINDOW = 512
Q_BLOCK = 256
FORCE_SCORE = 1.0e4
ROPE_THETA = 500000.0
ROPE_DIM = NSA_HEAD_DIM // 4
RET_HEADS = 4
RET_HEAD_DIM = 128
RET_WIDTH = RET_HEADS * RET_HEAD_DIM
RET_CHUNK = 128
RET_THETA = 10000.0
ODD_MIX = NSA_WIDTH + RET_WIDTH
NEG_INF = -1.0e30
LOG2_E = 1.4426950408889634
LANES = 128
SUBLANES = 8
BF16_SUBLANES = 2 * SUBLANES
MXU_COLS = 256

VMEM_LIMIT = 56 * 1024 * 1024
FFN_ROWS = 512
EVEN_ROWS = 512
PROJ_ROWS = 512
SEL_KEYS = Q_BLOCK
RANK_CHUNK = 8
RET_ROWS = 512


def _cparams(sem):
    return pltpu.CompilerParams(dimension_semantics=sem, vmem_limit_bytes=VMEM_LIMIT)


def _const_spec(shape):
    nd = len(shape)
    return pl.BlockSpec(shape, lambda *_: (0,) * nd, pipeline_mode=pl.Buffered(1))


def _dot(a, b):
    return jnp.dot(a, b, preferred_element_type=F32)


def _dot_nt(a, b):
    return lax.dot_general(a, b, (((1,), (1,)), ((), ())), preferred_element_type=F32)


def _dot_tn(a, b):
    return lax.dot_general(a, b, (((0,), (0,)), ((), ())), preferred_element_type=F32)


def _rms(x, g, eps=1e-6):
    return x * lax.rsqrt(jnp.mean(x * x, axis=-1, keepdims=True) + eps) * g


def _eye(n):
    return jnp.where(lax.broadcasted_iota(jnp.int32, (n, n), 0) == lax.broadcasted_iota(jnp.int32, (n, n), 1),
                     1.0, 0.0).astype(BF16)


def _ffn_tile(x, gpre_ref, gpost_ref, wg_ref, wu_ref, wd_ref):
    h = _rms(x, gpre_ref[...]).astype(BF16)
    f_total = wg_ref.shape[1]
    fc = MXU_COLS
    assert f_total % fc == 0
    acc = None
    for c in range(f_total // fc):
        gate = _dot(h, wg_ref[:, c * fc:(c + 1) * fc])
        up = _dot(h, wu_ref[:, c * fc:(c + 1) * fc])
        act = (gate * jax.nn.sigmoid(gate) * up).astype(BF16)
        part = _dot(act, wd_ref[c * fc:(c + 1) * fc, :])
        acc = part if acc is None else acc + part
    return x + _rms(acc, gpost_ref[...])


def _layer_spec(stacked_shape, layer):
    nd = len(stacked_shape) - 1
    return pl.BlockSpec((None,) + tuple(stacked_shape[1:]), lambda *_: (layer,) + (0,) * nd,
                        pipeline_mode=pl.Buffered(1))


def _weight_spec(w):
    return _layer_spec(w[0].shape, w[1]) if isinstance(w, tuple) else _const_spec(w.shape)


def _weight_array(w):
    return w[0] if isinstance(w, tuple) else w


def _ffn_specs(ffn):
    layer, fpre, fpost, wg, wu, wd = ffn
    return [_const_spec(fpre.shape), _const_spec(fpost.shape), _layer_spec(wg.shape, layer),
            _layer_spec(wu.shape, layer), _layer_spec(wd.shape, layer)]


def _even_body(x_ref, gpre_ref, win_ref, poolw_ref, pscale_ref, lng_ref, lnb_ref, sguw_ref, sgub_ref,
               wout_ref, gpost_ref, fpre_ref, fpost_ref, wg_ref, wu_ref, wd_ref, o_ref, *levels):
    tt = x_ref.shape[1]
    j = pl.program_id(1)

    @pl.when(j == 0)
    def _():
        for level in levels:
            level[0:POOL_HIST, :] = jnp.zeros((POOL_HIST, level.shape[1]), F32)

    x = x_ref[0]
    h = _rms(x, gpre_ref[...]).astype(BF16)
    z = _dot(h, win_ref[...])
    a = z[:, :POOL_WIDTH]
    u = z[:, POOL_WIDTH:POOL_WIDTH + SGU_WIDTH]
    v = z[:, POOL_WIDTH + SGU_WIDTH:]

    assert POOL_WINDOWS == tuple(2 ** (k + 1) for k in range(POOL_GROUPS))
    t_pos = j * tt + lax.broadcasted_iota(jnp.int32, (tt, 1), 0)
    ya = []
    cur = a
    for gi, w in enumerate(POOL_WINDOWS):
        level = levels[gi]
        level[POOL_HIST:POOL_HIST + tt, :] = cur
        shift = w // 2
        sums = cur + level[POOL_HIST - shift:POOL_HIST - shift + tt, :]
        level[0:POOL_HIST, :] = level[tt:tt + POOL_HIST, :]
        cols = slice(gi * POOL_GDIM, (gi + 1) * POOL_GDIM)
        cnt = jnp.minimum(t_pos + 1, w).astype(F32)
        diff = (sums[:, 0:POOL_GDIM] / cnt - a[:, cols]).astype(BF16)
        ya.append(_dot(diff, poolw_ref[gi]))
        if gi + 1 < POOL_GROUPS:
            cur = sums[:, POOL_GDIM:]
    ya = jnp.concatenate(ya, axis=1) * pscale_ref[...]

    ug = jax.nn.gelu(u)
    vg = jax.nn.gelu(v)
    mu = jnp.mean(vg, axis=-1, keepdims=True)
    var = jnp.mean(jnp.square(vg - mu), axis=-1, keepdims=True)
    vn = ((vg - mu) * lax.rsqrt(var + 1e-5) * lng_ref[...] + lnb_ref[...]).astype(BF16)
    r_i = lax.broadcasted_iota(jnp.int32, (SGU_CHUNK, SGU_CHUNK), 0)
    c_i = lax.broadcasted_iota(jnp.int32, (SGU_CHUNK, SGU_CHUNK), 1)
    yb = []
    for g in range(SGU_GROUPS):
        cols = slice(g * SGU_GDIM, (g + 1) * SGU_GDIM)
        wm = jnp.where(r_i >= c_i, sguw_ref[g], 0.0).astype(BF16)
        bcol = sgub_ref[:, g:g + 1]
        parts = []
        for c in range(tt // SGU_CHUNK):
            rows = slice(c * SGU_CHUNK, (c + 1) * SGU_CHUNK)
            parts.append(ug[rows, cols] * (_dot(wm, vn[rows, cols]) + bcol))
        yb.append(jnp.concatenate(parts, axis=0))
    yb = jnp.concatenate(yb, axis=1)

    m = _dot(ya.astype(BF16), wout_ref[0:POOL_WIDTH, :]) + _dot(yb.astype(BF16), wout_ref[POOL_WIDTH:, :])
    o_ref[0] = _ffn_tile(x + _rms(m, gpost_ref[...]), fpre_ref, fpost_ref, wg_ref, wu_ref, wd_ref)


def _even_layer(x, gpre, win, poolw, pscale, lng, lnb, sguw, sgub_t, wout, gpost, ffn):
    b, t, d = x.shape
    tt = EVEN_ROWS
    row = pl.BlockSpec((1, tt, d), lambda bi, j: (bi, j, 0))
    return pl.pallas_call(
        _even_body,
        grid=(b, t // tt),
        in_specs=[row, _const_spec((1, d)), _weight_spec(win), _const_spec(poolw.shape),
                  _const_spec(pscale.shape), _const_spec(lng.shape), _const_spec(lnb.shape),
                  _const_spec(sguw.shape), _const_spec(sgub_t.shape), _weight_spec(wout),
                  _const_spec((1, d))] + _ffn_specs(ffn),
        out_specs=row,
        out_shape=jax.ShapeDtypeStruct((b, t, d), F32),
        scratch_shapes=[pltpu.VMEM((POOL_HIST + tt, POOL_WIDTH - k * POOL_GDIM), F32) for k in range(POOL_GROUPS)],
        compiler_params=_cparams(("arbitrary", "arbitrary")),
        name="even_layer",
    )(x, gpre, _weight_array(win), poolw, pscale, lng, lnb, sguw, sgub_t, _weight_array(wout), gpost, *ffn[1:])


_C_Q = 0
_C_KC = _C_Q + NSA_WIDTH
_C_VC = _C_KC + NSA_KV_WIDTH
_C_KS = _C_VC + NSA_KV_WIDTH
_C_KW = _C_KS + NSA_KV_WIDTH
_C_RQ = _C_KW + NSA_KV_WIDTH
_C_RK = _C_RQ + RET_WIDTH
_C_RV = _C_RK + RET_WIDTH
_C_RG = _C_RV + RET_WIDTH
ODD_COLS = _C_RG + RET_WIDTH
_R_VS = 0
_R_VW = _R_VS + NSA_KV_WIDTH
_R_GT = _R_VW + NSA_KV_WIDTH
ODD_TROWS = _R_GT + NSA_KV_GROUPS * GATE_ROWS
V_AUG = NSA_HEAD_DIM + BF16_SUBLANES
VT_ROWS = 2 * NSA_KV_GROUPS * V_AUG


def _rope_nsa(z, c, sa, sb):
    half = ROPE_DIM // 2
    return z * c + pltpu.roll(z, LANES - half, 1) * sa + pltpu.roll(z, half, 1) * sb


def _split3(x):
    a = x.astype(BF16)
    r = x - a.astype(F32)
    b = r.astype(BF16)
    return a, b, (r - b.astype(F32)).astype(BF16)


def _proj_body(x_ref, gpre_ref, w_ref, wr_ref, wt_ref, cs_ref, place_ref, base_ref, cr_ref, sr_ref,
               q_ref, kc_ref, vc_ref, ksa_ref, kw_ref, vt_ref, gt_ref, rq_ref, rk_ref, rv_ref, rg_ref):
    tm = x_ref.shape[1]
    j = pl.program_id(1)
    x = x_ref[0]
    h = _rms(x, gpre_ref[...]).astype(BF16)
    tables = base_ref[...]
    for part in _split3(cs_ref[0]):
        tables = tables + _dot(part, place_ref[...])
    nc, nsa, nsb = tables[:, 0:LANES], tables[:, LANES:2 * LANES], tables[:, 2 * LANES:3 * LANES]
    cr, sr = cr_ref[0], sr_ref[0]
    rc = jnp.concatenate([cr, cr], axis=1)
    rs = jnp.concatenate([-sr, sr], axis=1)

    def cols(start, width):
        if start >= _C_RQ:
            return _dot(h, wr_ref[:, start - _C_RQ:start - _C_RQ + width])
        return _dot(h, w_ref[:, start:start + width])

    zq = cols(_C_Q, NSA_WIDTH)
    q_scale = NSA_HEAD_DIM ** -0.5 * LOG2_E
    for s in range(NSA_WIDTH // LANES):
        sl = slice(s * LANES, (s + 1) * LANES)
        q_ref[0, :, sl] = (_rope_nsa(zq[:, sl], nc, nsa, nsb) * q_scale).astype(BF16)

    zkv = cols(_C_KC, _C_RQ - _C_KC)
    kc_ref[0] = _rope_nsa(zkv[:, _C_KC - _C_KC:_C_VC - _C_KC], nc, nsa, nsb)
    vc_ref[0] = zkv[:, _C_VC - _C_KC:_C_KS - _C_KC]
    ks = _rope_nsa(zkv[:, _C_KS - _C_KC:_C_KW - _C_KC], nc, nsa, nsb)
    kw = _rope_nsa(zkv[:, _C_KW - _C_KC:_C_RQ - _C_KC], nc, nsa, nsb)
    t_pos = j * tm + lax.broadcasted_iota(jnp.int32, (tm, SEL_PAD), 0)
    blk = lax.broadcasted_iota(jnp.int32, (tm, SEL_PAD), 1)
    onehot = jnp.where(t_pos // SEL_BLOCK == blk, 1.0, 0.0).astype(BF16)
    for g in range(NSA_KV_GROUPS):
        sl = slice(g * NSA_HEAD_DIM, (g + 1) * NSA_HEAD_DIM)
        ksa_ref[0, g] = jnp.concatenate([ks[:, sl].astype(BF16), onehot], axis=1)
        kw_ref[0, g] = kw[:, sl].astype(BF16)

    zt = _dot_nt(wt_ref[...], h)
    ones_rows = jnp.where(lax.broadcasted_iota(jnp.int32, (V_AUG - NSA_HEAD_DIM, tm), 0) == 0, 1.0, 0.0).astype(BF16)
    for k in range(2 * NSA_KV_GROUPS):
        vt_ref[0, k * V_AUG:k * V_AUG + NSA_HEAD_DIM] = zt[k * NSA_HEAD_DIM:(k + 1) * NSA_HEAD_DIM].astype(BF16)
        vt_ref[0, k * V_AUG + NSA_HEAD_DIM:(k + 1) * V_AUG] = ones_rows
    gt_ref[0] = jax.nn.sigmoid(zt[_R_GT:])

    k_scale = RET_HEAD_DIM ** -0.5
    zrq = cols(_C_RQ, RET_WIDTH)
    zrk = cols(_C_RK, RET_WIDTH)
    for hh in range(RET_HEADS):
        sl = slice(hh * RET_HEAD_DIM, (hh + 1) * RET_HEAD_DIM)
        zq_h, zk_h = zrq[:, sl], zrk[:, sl]
        rq_ref[0, :, sl] = (zq_h * rc + pltpu.roll(zq_h, RET_HEAD_DIM // 2, 1) * rs).astype(BF16)
        rk_ref[0, :, sl] = ((zk_h * rc + pltpu.roll(zk_h, RET_HEAD_DIM // 2, 1) * rs) * k_scale).astype(BF16)
    rv_ref[0] = cols(_C_RV, RET_WIDTH).astype(BF16)
    rg_ref[0] = cols(_C_RG, RET_WIDTH)


def _odd_proj(x, gpre, w, wr, wt, cs, place, base, cr, sr):
    b, t, d = x.shape
    tm = PROJ_ROWS
    G = NSA_KV_GROUPS

    def row(width):
        return pl.BlockSpec((1, tm, width), lambda bi, j: (bi, j, 0))

    def grp(width):
        return pl.BlockSpec((1, G, tm, width), lambda bi, j: (bi, 0, j, 0))

    def feat(rows):
        return pl.BlockSpec((1, rows, tm), lambda bi, j: (bi, 0, j))

    out_shape = [
        jax.ShapeDtypeStruct((b, t, NSA_WIDTH), BF16),
        jax.ShapeDtypeStruct((b, t, LANES), F32),
        jax.ShapeDtypeStruct((b, t, LANES), F32),
        jax.ShapeDtypeStruct((b, G, t, LANES), BF16),
        jax.ShapeDtypeStruct((b, G, t, NSA_HEAD_DIM), BF16),
        jax.ShapeDtypeStruct((b, VT_ROWS, t), BF16),
        jax.ShapeDtypeStruct((b, G * GATE_ROWS, t), F32),
        jax.ShapeDtypeStruct((b, t, RET_WIDTH), BF16),
        jax.ShapeDtypeStruct((b, t, RET_WIDTH), BF16),
        jax.ShapeDtypeStruct((b, t, RET_WIDTH), BF16),
        jax.ShapeDtypeStruct((b, t, RET_WIDTH), F32),
    ]
    out_specs = [row(NSA_WIDTH), row(LANES), row(LANES), grp(LANES), grp(NSA_HEAD_DIM), feat(VT_ROWS),
                 feat(G * GATE_ROWS), row(RET_WIDTH), row(RET_WIDTH), row(RET_WIDTH), row(RET_WIDTH)]
    return pl.pallas_call(
        _proj_body,
        grid=(b, t // tm),
        in_specs=[row(d), _const_spec((1, d)), _const_spec(w.shape), _const_spec(wr.shape), _const_spec(wt.shape),
                  row(cs.shape[2]),
                  _const_spec(place.shape), _const_spec(base.shape), row(cr.shape[2]), row(sr.shape[2])],
        out_specs=out_specs,
        out_shape=out_shape,
        compiler_params=_cparams(("arbitrary", "arbitrary")),
        name="odd_proj",
    )(x, gpre, w, wr, wt, cs, place, base, cr, sr)


def _compress_body(k_ref, v_ref, kpos_ref, vpos_ref, kw1_ref, vw1_ref, kw2_ref, vw2t_ref, kcm_ref, vct_ref):
    nc = k_ref.shape[1] // CMP_STRIDE
    half = CMP_BLOCK // CMP_STRIDE

    def hidden(x_ref, pos_ref, w1_ref):
        parts = [None] * half
        for off in range(CMP_STRIDE):
            tok = x_ref[0, pl.ds(off, nc, stride=CMP_STRIDE), :]
            for h in range(half):
                l = h * CMP_STRIDE + off
                term = _dot((tok + pos_ref[l:l + 1, :]).astype(BF16), w1_ref[l])
                parts[h] = term if parts[h] is None else parts[h] + term
        pre = parts[0]
        for h in range(1, half):
            pre = pre + pltpu.roll(parts[h], nc - h, 0)
        return jax.nn.gelu(pre).astype(BF16)

    hk = hidden(k_ref, kpos_ref, kw1_ref)
    hv = hidden(v_ref, vpos_ref, vw1_ref)
    for g in range(NSA_KV_GROUPS):
        sl = slice(g * CMP_HIDDEN, (g + 1) * CMP_HIDDEN)
        kcm_ref[0, g] = _dot(hk[:, sl], kw2_ref[...]).astype(BF16)
        vct_ref[0, g] = _dot_nt(vw2t_ref[...], hv[:, sl]).astype(BF16)


def _compress(k, v, kpos, vpos, kw1, vw1, kw2, vw2t):
    b, t, width = k.shape
    nc = t // CMP_STRIDE
    G = NSA_KV_GROUPS
    row = pl.BlockSpec((1, t, width), lambda bi: (bi, 0, 0))
    return pl.pallas_call(
        _compress_body,
        grid=(b,),
        in_specs=[row, row, _const_spec(kpos.shape), _const_spec(vpos.shape), _const_spec(kw1.shape),
                  _const_spec(vw1.shape), _const_spec(kw2.shape), _const_spec(vw2t.shape)],
        out_specs=[pl.BlockSpec((1, G, nc, NSA_HEAD_DIM), lambda bi: (bi, 0, 0, 0)),
                   pl.BlockSpec((1, G, NSA_HEAD_DIM, nc), lambda bi: (bi, 0, 0, 0))],
        out_shape=[jax.ShapeDtypeStruct((b, G, nc, NSA_HEAD_DIM), BF16),
                   jax.ShapeDtypeStruct((b, G, NSA_HEAD_DIM, nc), BF16)],
        compiler_params=_cparams(("arbitrary",)),
        name="compress",
    )(k, v, kpos, vpos, kw1, vw1, kw2, vw2t)


def _nsa_body(q_ref, kcm_ref, vct_ref, c2st_ref, wmask_ref, ksa_ref, kw_ref, vt_ref, gt_ref, o_ref,
              score_ref, rank_ref, sc_ref, sw_ref, pw_ref, sa_ref, sb_ref, pa_ref, pb_ref, lhs_ref,
              m_ref, mta_ref, mtb_ref, mw_ref, acc_ref, *, top_k):
    G, M, Q, dh = NSA_KV_GROUPS, NSA_HPG, Q_BLOCK, NSA_HEAD_DIM
    cols = M * Q
    nc = kcm_ref.shape[2]
    i = pl.program_id(1)
    t0 = i * Q
    tq = t0 + (lax.broadcasted_iota(jnp.int32, (1, cols), 1) & (Q - 1))
    groups = range(G)

    q_heads, q_rows = [], []
    for g in groups:
        qb = q_ref[0, :, g * NSA_GROUP_WIDTH:(g + 1) * NSA_GROUP_WIDTH]
        q_heads.append([qb[:, m * dh:(m + 1) * dh] for m in range(M)])
        q_rows.append(jnp.concatenate(q_heads[g], axis=0))

    n_idx = lax.broadcasted_iota(jnp.int32, (nc, 1), 0)
    cmp_ok = (n_idx * CMP_STRIDE + (CMP_BLOCK - 1) <= tq) & (n_idx < nc - 1)
    j_idx = lax.broadcasted_iota(jnp.int32, (SEL_PAD, Q), 0)
    cur = (t0 + lax.broadcasted_iota(jnp.int32, (SEL_PAD, Q), 1)) // SEL_BLOCK
    forced = (j_idx == 0) | (j_idx == cur) | (j_idx == cur - 1)
    valid_tok = (tq >= CMP_BLOCK - 1).astype(F32)
    o_cmp = []
    for g in groups:
        sc = sc_ref.at[g]
        sc[...] = jnp.where(cmp_ok, _dot_nt(kcm_ref[0, g], q_rows[g]), NEG_INF)
        sc[...] = jnp.exp2(sc[...] - jnp.max(sc[...], axis=0, keepdims=True))
        norm = valid_tok / jnp.sum(sc[...], axis=0, keepdims=True)
        o_cmp.append(_dot(vct_ref[0, g], sc[...].astype(BF16)) * norm)
        p_grp = sc[:, 0:Q] * norm[:, 0:Q]
        for m in range(1, M):
            p_grp = p_grp + sc[:, m * Q:(m + 1) * Q] * norm[:, m * Q:(m + 1) * Q]
        p_hi = p_grp.astype(BF16)
        p_lo = (p_grp - p_hi.astype(F32)).astype(BF16)
        imp = _dot(c2st_ref[...], p_hi) + _dot(c2st_ref[...], p_lo)
        score_ref[g] = jnp.where(j_idx <= cur, jnp.where(forced, FORCE_SCORE, imp), -1.0)

    def v_aug(branch, g, k0, n):
        r0 = (branch * G + g) * V_AUG
        return vt_ref[0, r0:r0 + V_AUG, pl.ds(k0, n)]

    span = WINDOW + Q
    w0 = pl.multiple_of(jnp.maximum(t0 - WINDOW, 0), Q)
    n_wb = span // Q
    diag_b = jnp.minimum(i, n_wb - 1)
    tri_causal = wmask_ref[n_wb - 1, (n_wb - 1) * Q:n_wb * Q, :]
    for g in groups:
        s_win = _dot_nt(kw_ref[0, g, pl.ds(w0, span), :], q_rows[g]) + wmask_ref[diag_b]
        sw_ref[g] = s_win
        mw_ref[g] = jnp.max(s_win, axis=0, keepdims=True)

    n_causal = (t0 + Q) // SEL_BLOCK
    rank_ref[...] = jnp.zeros(rank_ref.shape, jnp.int32)
    row_in_tile = lax.broadcasted_iota(jnp.int32, (SUBLANES, LANES), 0)
    for c0 in range(0, SEL_PAD, RANK_CHUNK):
        @pl.when(c0 < n_causal)
        def _(c0=c0):
            for g in groups:
                for l0 in range(0, Q, LANES):
                    ln = slice(l0, l0 + LANES)
                    others = [score_ref[g, k:k + 1, ln] for k in range(c0, c0 + RANK_CHUNK)]
                    for r0 in range(0, SEL_PAD, SUBLANES):
                        mine = score_ref[g, r0:r0 + SUBLANES, ln]
                        count = rank_ref[g, r0:r0 + SUBLANES, ln]
                        for k, other in zip(range(c0, c0 + RANK_CHUNK), others):
                            if r0 > k:
                                beats = other >= mine
                            elif r0 + SUBLANES <= k:
                                beats = other > mine
                            else:
                                beats = (other > mine) | ((other == mine) & (k < r0 + row_in_tile))
                            count = count + beats.astype(jnp.int32)
                        rank_ref[g, r0:r0 + SUBLANES, ln] = count

    eye_q = _eye(Q)
    for g in groups:
        chosen = jnp.where((rank_ref[g] < top_k) & (score_ref[g] >= 0.0), 1.0, 0.0).astype(BF16)
        chosen_t = _dot_nt(eye_q, chosen)
        bias = jnp.where(chosen_t > 0.5, 0.0, NEG_INF).astype(BF16)
        for m in range(M):
            lhs_ref[g, m * Q:(m + 1) * Q, :] = jnp.concatenate([q_heads[g][m], bias], axis=1)
        m_ref[g] = jnp.full((1, cols), NEG_INF, F32)
        acc_ref[g] = jnp.zeros((V_AUG, cols), F32)

    buf_a = (sa_ref, mta_ref, pa_ref)
    buf_b = (sb_ref, mtb_ref, pb_ref)

    def put_scores(buf, g, kt):
        s_dst, max_dst, _ = buf
        k0 = pl.multiple_of(kt * SEL_KEYS, SEL_KEYS)
        s = _dot_nt(ksa_ref[0, g, pl.ds(k0, SEL_KEYS), :], lhs_ref[g])
        s_dst[g] = s
        max_dst[g] = jnp.max(s, axis=0, keepdims=True)

    def sel_update(buf, g, kt, bias=None):
        src, tile_max, p_dst = buf
        k0 = pl.multiple_of(kt * SEL_KEYS, SEL_KEYS)
        m_i = m_ref[g]
        if bias is None:
            m_new = jnp.maximum(m_i, tile_max[g])
            p_dst[g] = jnp.exp2(src[g] - m_new).astype(BF16)
        else:
            m_new = jnp.maximum(m_i, jnp.max(src[g] + bias, axis=0, keepdims=True))
            p_dst[g] = jnp.exp2(src[g] + bias - m_new).astype(BF16)
        acc_ref[g] = jnp.exp2(m_i - m_new) * acc_ref[g] + _dot(v_aug(0, g, k0, SEL_KEYS), p_dst[g])
        m_ref[g] = m_new

    n_full = t0 // SEL_KEYS
    odd = n_full & 1
    for g in groups:
        put_scores(buf_a, g, 0)

    o_win = []
    for g in groups:
        pw_ref[g] = jnp.exp2(sw_ref[g] - mw_ref[g]).astype(BF16)
        acc_win = _dot(v_aug(1, g, w0, span), pw_ref[g])
        o_win.append(acc_win[0:dh] / acc_win[dh:dh + 1])

    @pl.when(odd == 1)
    def _():
        for g in groups:
            sel_update(buf_a, g, 0)
            put_scores(buf_a, g, 1)

    def sel_pair(pair, _):
        ta = odd + 2 * pair
        for g in groups:
            put_scores(buf_b, g, ta + 1)
        for g in groups:
            sel_update(buf_a, g, ta)
        for g in groups:
            put_scores(buf_a, g, ta + 2)
        for g in groups:
            sel_update(buf_b, g, ta + 1)
        return 0

    lax.fori_loop(0, n_full // 2, sel_pair, 0)
    o_sel = []
    for g in groups:
        sel_update(buf_a, g, n_full, bias=tri_causal)
        o_sel.append(acc_ref[g, 0:dh, :] / acc_ref[g, dh:dh + 1, :])

    for g in groups:
        gt = gt_ref[0, g * GATE_ROWS:(g + 1) * GATE_ROWS, :]
        heads = []
        for m in range(M):
            c = slice(m * Q, (m + 1) * Q)
            r = NSA_BRANCHES * m
            mixed = (o_cmp[g][:, c] * gt[r:r + 1] + o_sel[g][:, c] * gt[r + 1:r + 2]
                     + o_win[g][:, c] * gt[r + 2:r + 3])
            heads.append(mixed.astype(BF16))
        o_ref[0, :, g * NSA_GROUP_WIDTH:(g + 1) * NSA_GROUP_WIDTH] = _dot_nt(
            eye_q, jnp.concatenate(heads, axis=0)).astype(BF16)


def _nsa(q, kcm, vct, c2st, wmask, ksa, kw, vt, gt):
    b, t, _ = q.shape
    G = NSA_KV_GROUPS
    nc = kcm.shape[2]
    n_sel = t // SEL_BLOCK
    assert n_sel <= SEL_PAD and t >= WINDOW + Q_BLOCK and t % SEL_KEYS == 0
    top_k = min(SEL_TOPK, n_sel)
    cols, span = NSA_HPG * Q_BLOCK, WINDOW + Q_BLOCK

    def per_batch(shape):
        nd = len(shape)
        return pl.BlockSpec((1,) + shape, lambda bi, i: (bi,) + (0,) * nd)

    token_rows = pl.BlockSpec((1, Q_BLOCK, NSA_WIDTH), lambda bi, i: (bi, i, 0))
    return pl.pallas_call(
        functools.partial(_nsa_body, top_k=top_k),
        grid=(b, t // Q_BLOCK),
        in_specs=[token_rows, per_batch((G, nc, NSA_HEAD_DIM)), per_batch((G, NSA_HEAD_DIM, nc)),
                  _const_spec(c2st.shape), _const_spec(wmask.shape), per_batch((G, t, LANES)), per_batch((G, t, NSA_HEAD_DIM)),
                  per_batch((vt.shape[1], t)),
                  pl.BlockSpec((1, G * GATE_ROWS, Q_BLOCK), lambda bi, i: (bi, 0, i))],
        out_specs=token_rows,
        out_shape=jax.ShapeDtypeStruct((b, t, NSA_WIDTH), BF16),
        scratch_shapes=[pltpu.VMEM((G, SEL_PAD, Q_BLOCK), F32),
                        pltpu.VMEM((G, SEL_PAD, Q_BLOCK), jnp.int32),
                        pltpu.VMEM((G, nc, cols), F32),
                        pltpu.VMEM((G, span, cols), F32),
                        pltpu.VMEM((G, span, cols), BF16),
                        pltpu.VMEM((G, SEL_KEYS, cols), F32),
                        pltpu.VMEM((G, SEL_KEYS, cols), F32),
                        pltpu.VMEM((G, SEL_KEYS, cols), BF16),
                        pltpu.VMEM((G, SEL_KEYS, cols), BF16),
                        pltpu.VMEM((G, cols, 2 * NSA_HEAD_DIM), BF16),
                        pltpu.VMEM((G, 1, cols), F32),
                        pltpu.VMEM((G, 1, cols), F32),
                        pltpu.VMEM((G, 1, cols), F32),
                        pltpu.VMEM((G, 1, cols), F32),
                        pltpu.VMEM((G, V_AUG, cols), F32)],
        compiler_params=_cparams(("arbitrary", "arbitrary")),
        name="nsa",
    )(q, kcm, vct, c2st, wmask, ksa, kw, vt, gt)


def _ret_body(q_ref, k_ref, v_ref, g_ref, decay_ref, zeta_ref, xi_ref, gchunk_ref, gn_ref, o_ref, state_ref):
    C, d = RET_CHUNK, RET_HEAD_DIM

    @pl.when(pl.program_id(1) == 0)
    def _():
        state_ref[...] = jnp.zeros_like(state_ref)

    n_chunks = q_ref.shape[1] // C
    for h in range(RET_HEADS):
        hc = slice(h * d, (h + 1) * d)
        state = state_ref[h]
        before = []
        for c in range(n_chunks):
            rows = slice(c * C, (c + 1) * C)
            kz = (k_ref[0, rows, hc].astype(F32) * zeta_ref[h]).astype(BF16)
            before.append(state.astype(BF16))
            state = state * gchunk_ref[h] + _dot_tn(kz, v_ref[0, rows, hc])
        state_ref[h] = state
        for c in range(n_chunks):
            rows = slice(c * C, (c + 1) * C)
            q, k, v = q_ref[0, rows, hc], k_ref[0, rows, hc], v_ref[0, rows, hc]
            s = (_dot_nt(q, k) * decay_ref[h]).astype(BF16)
            o = _dot(s, v) + _dot((q.astype(F32) * xi_ref[h]).astype(BF16), before[c])
            mu = jnp.mean(o, axis=-1, keepdims=True)
            var = jnp.mean(jnp.square(o - mu), axis=-1, keepdims=True)
            o = (o - mu) * lax.rsqrt(var + 1e-5) * gn_ref[h]
            gate = g_ref[0, rows, hc]
            o_ref[0, rows, hc] = (gate * jax.nn.sigmoid(gate) * o).astype(BF16)


def _retention(rq, rk, rv, rg, decay, zeta, xi, gchunk, gn):
    b, t, width = rq.shape
    tok = pl.BlockSpec((1, RET_ROWS, width), lambda bi, c: (bi, c, 0))
    return pl.pallas_call(
        _ret_body,
        grid=(b, t // RET_ROWS),
        in_specs=[tok, tok, tok, tok, _const_spec(decay.shape), _const_spec(zeta.shape), _const_spec(xi.shape),
                  _const_spec(gchunk.shape), _const_spec(gn.shape)],
        out_specs=tok,
        out_shape=jax.ShapeDtypeStruct((b, t, width), BF16),
        scratch_shapes=[pltpu.VMEM((RET_HEADS, RET_HEAD_DIM, RET_HEAD_DIM), F32)],
        compiler_params=_cparams(("arbitrary", "arbitrary")),
        name="retention",
    )(rq, rk, rv, rg, decay, zeta, xi, gchunk, gn)


def _oproj_body(x_ref, yc_ref, yd_ref, w_ref, gpost_ref, fpre_ref, fpost_ref, wg_ref, wu_ref, wd_ref, o_ref):
    m = _dot(yc_ref[...], w_ref[0:NSA_WIDTH, :]) + _dot(yd_ref[...], w_ref[NSA_WIDTH:, :])
    o_ref[...] = _ffn_tile(x_ref[...] + _rms(m, gpost_ref[...]), fpre_ref, fpost_ref, wg_ref, wu_ref, wd_ref)


def _odd_out(x2, yc2, yd2, w, gpost, ffn):
    n, d = x2.shape
    tm = FFN_ROWS

    def row(width):
        return pl.BlockSpec((tm, width), lambda i: (i, 0))

    return pl.pallas_call(
        _oproj_body,
        grid=(n // tm,),
        in_specs=[row(d), row(NSA_WIDTH), row(RET_WIDTH), _weight_spec(w), _const_spec((1, d))]
        + _ffn_specs(ffn),
        out_specs=row(d),
        out_shape=jax.ShapeDtypeStruct((n, d), F32),
        compiler_params=_cparams(("arbitrary",)),
        name="odd_out_ffn",
    )(x2, yc2, yd2, _weight_array(w), gpost, *ffn[1:])


def _odd_in_weight(w_in):
    sizes = [NSA_WIDTH] + [NSA_KV_WIDTH] * 6 + [NSA_BRANCHES * NSA_HEADS] + [RET_WIDTH] * 4
    offs = np.concatenate([[0], np.cumsum(sizes)])
    w_in = w_in.astype(BF16)
    q, kc, vc, ks, vs, kw, vw, gt, rq, rk, rv, rg = [w_in[:, offs[n]:offs[n + 1]] for n in range(len(sizes))]
    w = jnp.concatenate([w_in[:, offs[0]:offs[4]], kw], axis=1)
    wr = w_in[:, offs[8]:offs[12]]
    per_group = NSA_BRANCHES * NSA_HPG
    gates = [jnp.pad(gt[:, g * per_group:(g + 1) * per_group], ((0, 0), (0, GATE_ROWS - per_group)))
             for g in range(NSA_KV_GROUPS)]
    wt = jnp.concatenate([vs, vw] + gates, axis=1).T
    return w, wr, wt


def _rope_tables(positions):
    pos = positions.astype(F32)[..., None]
    half = ROPE_DIM // 2
    inv = 1.0 / (ROPE_THETA ** (jnp.arange(0, ROPE_DIM, 2, dtype=F32) / ROPE_DIM))
    ang = pos * inv
    cs = jnp.concatenate([jnp.cos(ang), jnp.sin(ang)], axis=-1)
    place = np.zeros((ROPE_DIM, 3 * LANES), np.float32)
    base = np.zeros((1, 3 * LANES), np.float32)
    for head0 in range(0, LANES, NSA_HEAD_DIM):
        base[0, head0 + ROPE_DIM:head0 + NSA_HEAD_DIM] = 1.0
        for f in range(half):
            place[f, head0 + f] = 1.0
            place[f, head0 + half + f] = 1.0
            place[half + f, LANES + head0 + f] = -1.0
            place[half + f, 2 * LANES + head0 + half + f] = 1.0
    inv_r = 1.0 / (RET_THETA ** (jnp.arange(0, RET_HEAD_DIM, 2, dtype=F32) / RET_HEAD_DIM))
    ang_r = pos * inv_r
    return cs, jnp.asarray(place, dtype=BF16), jnp.asarray(base), jnp.cos(ang_r), jnp.sin(ang_r)


def _retention_tables():
    H, C, d = RET_HEADS, RET_CHUNK, RET_HEAD_DIM
    log_gamma = np.log1p(-np.exp2(-5.0 - np.arange(H, dtype=np.float64)))
    idx = np.arange(C, dtype=np.float64)
    rel = idx[:, None] - idx[None, :]
    decay = np.where(rel >= 0, np.exp(np.maximum(rel, 0.0)[None] * log_gamma[:, None, None]), 0.0)
    zeta = np.exp((C - 1 - idx)[None, :] * log_gamma[:, None])
    xi = np.exp((idx + 1.0)[None, :] * log_gamma[:, None])
    gchunk = np.exp(C * log_gamma)
    zeta_b = np.broadcast_to(zeta[:, :, None], (H, C, d))
    xi_b = np.broadcast_to(xi[:, :, None], (H, C, d))
    gchunk_b = np.broadcast_to(gchunk[:, None, None], (H, 1, d))
    return tuple(jnp.asarray(a, dtype=F32) for a in (decay, zeta_b, xi_b, gchunk_b))


def _cmp_to_sel_t(t_len):
    nc = t_len // CMP_STRIDE
    n_cmp = (t_len - CMP_BLOCK) // CMP_STRIDE + 1
    n_sel = t_len // SEL_BLOCK
    c_start = np.arange(nc) * CMP_STRIDE
    s_start = np.arange(SEL_PAD) * SEL_BLOCK
    hit = ((c_start[None, :] < s_start[:, None] + SEL_BLOCK) & (c_start[None, :] + CMP_BLOCK > s_start[:, None])
           & (np.arange(nc)[None, :] < n_cmp) & (np.arange(SEL_PAD)[:, None] < n_sel))
    return jnp.asarray(hit.astype(np.float32), dtype=BF16)


def _window_masks():
    Q = Q_BLOCK
    n_wb = (WINDOW + Q) // Q
    r = np.arange(Q)[:, None]
    tok = np.arange(NSA_HPG * Q)[None, :] % Q
    causal = np.where(r <= tok, 0.0, NEG_INF)
    leaving = np.where(r > tok, 0.0, NEG_INF)
    masks = np.zeros((n_wb, n_wb * Q, NSA_HPG * Q), np.float32)
    for d in range(n_wb):
        masks[d, d * Q:(d + 1) * Q] = causal
        masks[d, (d + 1) * Q:] = NEG_INF
    masks[n_wb - 1, 0:Q] = leaving
    return jnp.asarray(masks)


def _compress_weights(pos, w1, w2):
    G, dh = NSA_KV_GROUPS, NSA_HEAD_DIM
    pos_rows = jnp.tile(pos, (1, G))
    w1r = w1.astype(BF16).reshape(CMP_BLOCK, dh, CMP_HIDDEN)
    w1_bd = jnp.einsum('ab,lij->laibj', jnp.eye(G, dtype=BF16), w1r).reshape(CMP_BLOCK, G * dh, G * CMP_HIDDEN)
    return pos_rows, w1_bd, w2.astype(BF16)


def _odd_layer(x, positions_tables, ret_tables, c2st, wmask, gpre, w_in, cmp_k_pos, cmp_k_w1, cmp_k_w2,
               cmp_v_pos, cmp_v_w1, cmp_v_w2, gn_g, w_out, gpost, ffn):
    b, t, d = x.shape
    w, wr, wt = _odd_in_weight(w_in)
    (q, kc, vc, ksa, kw, vt, gt, rq, rk, rv, rg) = _odd_proj(x, gpre, w, wr, wt, *positions_tables)
    kpos, kw1, kw2 = _compress_weights(cmp_k_pos, cmp_k_w1, cmp_k_w2)
    vpos, vw1, vw2 = _compress_weights(cmp_v_pos, cmp_v_w1, cmp_v_w2)
    kcm, vct = _compress(kc, vc, kpos, vpos, kw1, vw1, kw2, vw2.T)
    yc = _nsa(q, kcm, vct, c2st, wmask, ksa, kw, vt, gt)
    yd = _retention(rq, rk, rv, rg, *ret_tables, gn_g.reshape(RET_HEADS, 1, RET_HEAD_DIM))
    out = _odd_out(x.reshape(b * t, d), yc.reshape(b * t, NSA_WIDTH), yd.reshape(b * t, RET_WIDTH),
                   w_out, gpost, ffn)
    return out.reshape(b, t, d)


def kernel(x, positions, ln_mix_pre, ln_mix_post, ln_ffn_pre, ln_ffn_post, ffn_w_gate, ffn_w_up, ffn_w_down,
           ev_w_in, ev_pool_w, ev_pool_scale, ev_sgu_ln_g, ev_sgu_ln_b, ev_sgu_w, ev_sgu_b, ev_w_out,
           od_w_in, od_cmp_k_pos, od_cmp_k_w1, od_cmp_k_w2, od_cmp_v_pos, od_cmp_v_w1, od_cmp_v_w2,
           od_ret_gn_g, od_w_out):
    b, t, d = x.shape
    depth = ln_mix_pre.shape[0]
    rope = _rope_tables(positions)
    ret_tables = _retention_tables()
    c2st = _cmp_to_sel_t(t)
    wmask = _window_masks()
    wg_all, wu_all, wd_all = ffn_w_gate.astype(BF16), ffn_w_up.astype(BF16), ffn_w_down.astype(BF16)
    ev_in_all, ev_out_all, od_out_all = ev_w_in.astype(BF16), ev_w_out.astype(BF16), od_w_out.astype(BF16)
    for layer in range(depth):
        gpre = ln_mix_pre[layer].reshape(1, d)
        gpost = ln_mix_post[layer].reshape(1, d)
        ffn = (layer, ln_ffn_pre[layer].reshape(1, d), ln_ffn_post[layer].reshape(1, d), wg_all, wu_all, wd_all)
        if layer % 2 == 0:
            e = layer // 2
            x = _even_layer(x, gpre, (ev_in_all, e), ev_pool_w[e].astype(BF16),
                            ev_pool_scale[e].reshape(1, POOL_WIDTH), ev_sgu_ln_g[e].reshape(1, SGU_WIDTH),
                            ev_sgu_ln_b[e].reshape(1, SGU_WIDTH), ev_sgu_w[e], ev_sgu_b[e].T,
                            (ev_out_all, e), gpost, ffn)
        else:
            o = layer // 2
            x = _odd_layer(x, rope, ret_tables, c2st, wmask, gpre, od_w_in[o], od_cmp_k_pos[o], od_cmp_k_w1[o],
                           od_cmp_k_w2[o], od_cmp_v_pos[o], od_cmp_v_w1[o], od_cmp_v_w2[o], od_ret_gn_g[o],
                           (od_out_all, o), gpost, ffn)
    return x
```

```python
import functools

import numpy as np
import jax
import jax.numpy as jnp
from jax import lax
from jax.experimental import pallas as pl
from jax.experimental.pallas import tpu as pltpu

F32 = jnp.float32
BF16 = jnp.bfloat16

D_MODEL = 1024
POOL_WINDOWS = (2, 4, 8, 16)
POOL_GROUPS = 4
POOL_WIDTH = D_MODEL // 2
POOL_GDIM = POOL_WIDTH // POOL_GROUPS
POOL_HIST = 16
SGU_GROUPS = 4
SGU_WIDTH = D_MODEL // 2
SGU_GDIM = SGU_WIDTH // SGU_GROUPS
SGU_CHUNK = 128
EVEN_IN = POOL_WIDTH + 2 * SGU_WIDTH
NSA_HEADS = 8
NSA_KV_GROUPS = 2
NSA_HPG = NSA_HEADS // NSA_KV_GROUPS
NSA_HEAD_DIM = 64
NSA_WIDTH = NSA_HEADS * NSA_HEAD_DIM
NSA_GROUP_WIDTH = NSA_HPG * NSA_HEAD_DIM
NSA_KV_WIDTH = NSA_KV_GROUPS * NSA_HEAD_DIM
NSA_BRANCHES = 3
GATE_ROWS = 16
CMP_BLOCK = 32
CMP_STRIDE = 16
CMP_HIDDEN = 128
SEL_BLOCK = 64
SEL_TOPK = 16
SEL_PAD = 64
WINDOW = 512
Q_BLOCK = 256
FORCE_SCORE = 1.0e4
ROPE_THETA = 500000.0
ROPE_DIM = NSA_HEAD_DIM // 4
RET_HEADS = 4
RET_HEAD_DIM = 128
RET_WIDTH = RET_HEADS * RET_HEAD_DIM
RET_CHUNK = 128
RET_THETA = 10000.0
ODD_MIX = NSA_WIDTH + RET_WIDTH
NEG_INF = -1.0e30
LOG2_E = 1.4426950408889634
LANES = 128
SUBLANES = 8
BF16_SUBLANES = 2 * SUBLANES
MXU_COLS = 256

VMEM_LIMIT = 56 * 1024 * 1024
FFN_ROWS = 512
EVEN_ROWS = 512
PROJ_ROWS = 512
SEL_KEYS = Q_BLOCK
RANK_CHUNK = 8
RET_ROWS = 512


def _cparams(sem):
    return pltpu.CompilerParams(dimension_semantics=sem, vmem_limit_bytes=VMEM_LIMIT)


def _const_spec(shape):
    nd = len(shape)
    return pl.BlockSpec(shape, lambda *_: (0,) * nd, pipeline_mode=pl.Buffered(1))


def _dot(a, b):
    return jnp.dot(a, b, preferred_element_type=F32)


def _dot_nt(a, b):
    return lax.dot_general(a, b, (((1,), (1,)), ((), ())), preferred_element_type=F32)


def _dot_tn(a, b):
    return lax.dot_general(a, b, (((0,), (0,)), ((), ())), preferred_element_type=F32)


def _rms(x, g, eps=1e-6):
    return x * lax.rsqrt(jnp.mean(x * x, axis=-1, keepdims=True) + eps) * g


def _eye(n):
    return jnp.where(lax.broadcasted_iota(jnp.int32, (n, n), 0) == lax.broadcasted_iota(jnp.int32, (n, n), 1),
                     1.0, 0.0).astype(BF16)


def _ffn_tile(x, gpre_ref, gpost_ref, wg_ref, wu_ref, wd_ref):
    h = _rms(x, gpre_ref[...]).astype(BF16)
    f_total = wg_ref.shape[1]
    fc = MXU_COLS
    assert f_total % fc == 0
    acc = None
    for c in range(f_total // fc):
        gate = _dot(h, wg_ref[:, c * fc:(c + 1) * fc])
        up = _dot(h, wu_ref[:, c * fc:(c + 1) * fc])
        act = (gate * jax.nn.sigmoid(gate) * up).astype(BF16)
        part = _dot(act, wd_ref[c * fc:(c + 1) * fc, :])
        acc = part if acc is None else acc + part
    return x + _rms(acc, gpost_ref[...])


def _layer_spec(stacked_shape, layer):
    nd = len(stacked_shape) - 1
    return pl.BlockSpec((None,) + tuple(stacked_shape[1:]), lambda *_: (layer,) + (0,) * nd,
                        pipeline_mode=pl.Buffered(1))


def _weight_spec(w):
    return _layer_spec(w[0].shape, w[1]) if isinstance(w, tuple) else _const_spec(w.shape)


def _weight_array(w):
    return w[0] if isinstance(w, tuple) else w


def _ffn_specs(ffn):
    layer, fpre, fpost, wg, wu, wd = ffn
    return [_const_spec(fpre.shape), _const_spec(fpost.shape), _layer_spec(wg.shape, layer),
            _layer_spec(wu.shape, layer), _layer_spec(wd.shape, layer)]


def _even_body(x_ref, gpre_ref, win_ref, poolw_ref, pscale_ref, lng_ref, lnb_ref, sguw_ref, sgub_ref,
               wout_ref, gpost_ref, fpre_ref, fpost_ref, wg_ref, wu_ref, wd_ref, o_ref, *levels):
    tt = x_ref.shape[1]
    j = pl.program_id(1)

    @pl.when(j == 0)
    def _():
        for level in levels:
            level[0:POOL_HIST, :] = jnp.zeros((POOL_HIST, level.shape[1]), F32)

    x = x_ref[0]
    h = _rms(x, gpre_ref[...]).astype(BF16)
    z = _dot(h, win_ref[...])
    a = z[:, :POOL_WIDTH]
    u = z[:, POOL_WIDTH:POOL_WIDTH + SGU_WIDTH]
    v = z[:, POOL_WIDTH + SGU_WIDTH:]

    assert POOL_WINDOWS == tuple(2 ** (k + 1) for k in range(POOL_GROUPS))
    t_pos = j * tt + lax.broadcasted_iota(jnp.int32, (tt, 1), 0)
    ya = []
    cur = a
    for gi, w in enumerate(POOL_WINDOWS):
        level = levels[gi]
        level[POOL_HIST:POOL_HIST + tt, :] = cur
        shift = w // 2
        sums = cur + level[POOL_HIST - shift:POOL_HIST - shift + tt, :]
        level[0:POOL_HIST, :] = level[tt:tt + POOL_HIST, :]
        cols = slice(gi * POOL_GDIM, (gi + 1) * POOL_GDIM)
        cnt = jnp.minimum(t_pos + 1, w).astype(F32)
        diff = (sums[:, 0:POOL_GDIM] / cnt - a[:, cols]).astype(BF16)
        ya.append(_dot(diff, poolw_ref[gi]))
        if gi + 1 < POOL_GROUPS:
            cur = sums[:, POOL_GDIM:]
    ya = jnp.concatenate(ya, axis=1) * pscale_ref[...]

    ug = jax.nn.gelu(u)
    vg = jax.nn.gelu(v)
    mu = jnp.mean(vg, axis=-1, keepdims=True)
    var = jnp.mean(jnp.square(vg - mu), axis=-1, keepdims=True)
    vn = ((vg - mu) * lax.rsqrt(var + 1e-5) * lng_ref[...] + lnb_ref[...]).astype(BF16)
    r_i = lax.broadcasted_iota(jnp.int32, (SGU_CHUNK, SGU_CHUNK), 0)
    c_i = lax.broadcasted_iota(jnp.int32, (SGU_CHUNK, SGU_CHUNK), 1)
    yb = []
    for g in range(SGU_GROUPS):
        cols = slice(g * SGU_GDIM, (g + 1) * SGU_GDIM)
        wm = jnp.where(r_i >= c_i, sguw_ref[g], 0.0).astype(BF16)
        bcol = sgub_ref[:, g:g + 1]
        parts = []
        for c in range(tt // SGU_CHUNK):
            rows = slice(c * SGU_CHUNK, (c + 1) * SGU_CHUNK)
            parts.append(ug[rows, cols] * (_dot(wm, vn[rows, cols]) + bcol))
        yb.append(jnp.concatenate(parts, axis=0))
    yb = jnp.concatenate(yb, axis=1)

    m = _dot(ya.astype(BF16), wout_ref[0:POOL_WIDTH, :]) + _dot(yb.astype(BF16), wout_ref[POOL_WIDTH:, :])
    o_ref[0] = _ffn_tile(x + _rms(m, gpost_ref[...]), fpre_ref, fpost_ref, wg_ref, wu_ref, wd_ref)


def _even_layer(x, gpre, win, poolw, pscale, lng, lnb, sguw, sgub_t, wout, gpost, ffn):
    b, t, d = x.shape
    tt = EVEN_ROWS
    row = pl.BlockSpec((1, tt, d), lambda bi, j: (bi, j, 0))
    return pl.pallas_call(
        _even_body,
        grid=(b, t // tt),
        in_specs=[row, _const_spec((1, d)), _weight_spec(win), _const_spec(poolw.shape),
                  _const_spec(pscale.shape), _const_spec(lng.shape), _const_spec(lnb.shape),
                  _const_spec(sguw.shape), _const_spec(sgub_t.shape), _weight_spec(wout),
                  _const_spec((1, d))] + _ffn_specs(ffn),
        out_specs=row,
        out_shape=jax.ShapeDtypeStruct((b, t, d), F32),
        scratch_shapes=[pltpu.VMEM((POOL_HIST + tt, POOL_WIDTH - k * POOL_GDIM), F32) for k in range(POOL_GROUPS)],
        compiler_params=_cparams(("arbitrary", "arbitrary")),
        name="even_layer",
    )(x, gpre, _weight_array(win), poolw, pscale, lng, lnb, sguw, sgub_t, _weight_array(wout), gpost, *ffn[1:])


_C_Q = 0
_C_KC = _C_Q + NSA_WIDTH
_C_VC = _C_KC + NSA_KV_WIDTH
_C_KS = _C_VC + NSA_KV_WIDTH
_C_KW = _C_KS + NSA_KV_WIDTH
_C_RQ = _C_KW + NSA_KV_WIDTH
_C_RK = _C_RQ + RET_WIDTH
_C_RV = _C_RK + RET_WIDTH
_C_RG = _C_RV + RET_WIDTH
ODD_COLS = _C_RG + RET_WIDTH
_R_VS = 0
_R_VW = _R_VS + NSA_KV_WIDTH
_R_GT = _R_VW + NSA_KV_WIDTH
ODD_TROWS = _R_GT + NSA_KV_GROUPS * GATE_ROWS
V_AUG = NSA_HEAD_DIM + BF16_SUBLANES
VT_ROWS = 2 * NSA_KV_GROUPS * V_AUG


def _rope_nsa(z, c, sa, sb):
    half = ROPE_DIM // 2
    return z * c + pltpu.roll(z, LANES - half, 1) * sa + pltpu.roll(z, half, 1) * sb


def _split3(x):
    a = x.astype(BF16)
    r = x - a.astype(F32)
    b = r.astype(BF16)
    return a, b, (r - b.astype(F32)).astype(BF16)


def _proj_body(x_ref, gpre_ref, w_ref, wr_ref, wt_ref, cs_ref, place_ref, base_ref, cr_ref, sr_ref,
               q_ref, kc_ref, vc_ref, ksa_ref, kw_ref, vt_ref, gt_ref, rq_ref, rk_ref, rv_ref, rg_ref):
    tm = x_ref.shape[1]
    j = pl.program_id(1)
    x = x_ref[0]
    h = _rms(x, gpre_ref[...]).astype(BF16)
    tables = base_ref[...]
    for part in _split3(cs_ref[0]):
        tables = tables + _dot(part, place_ref[...])
    nc, nsa, nsb = tables[:, 0:LANES], tables[:, LANES:2 * LANES], tables[:, 2 * LANES:3 * LANES]
    cr, sr = cr_ref[0], sr_ref[0]
    rc = jnp.concatenate([cr, cr], axis=1)
    rs = jnp.concatenate([-sr, sr], axis=1)

    def cols(start, width):
        if start >= _C_RQ:
            return _dot(h, wr_ref[:, start - _C_RQ:start - _C_RQ + width])
        return _dot(h, w_ref[:, start:start + width])

    zq = cols(_C_Q, NSA_WIDTH)
    q_scale = NSA_HEAD_DIM ** -0.5 * LOG2_E
    for s in range(NSA_WIDTH // LANES):
        sl = slice(s * LANES, (s + 1) * LANES)
        q_ref[0, :, sl] = (_rope_nsa(zq[:, sl], nc, nsa, nsb) * q_scale).astype(BF16)

    zkv = cols(_C_KC, _C_RQ - _C_KC)
    kc_ref[0] = _rope_nsa(zkv[:, _C_KC - _C_KC:_C_VC - _C_KC], nc, nsa, nsb)
    vc_ref[0] = zkv[:, _C_VC - _C_KC:_C_KS - _C_KC]
    ks = _rope_nsa(zkv[:, _C_KS - _C_KC:_C_KW - _C_KC], nc, nsa, nsb)
    kw = _rope_nsa(zkv[:, _C_KW - _C_KC:_C_RQ - _C_KC], nc, nsa, nsb)
    t_pos = j * tm + lax.broadcasted_iota(jnp.int32, (tm, SEL_PAD), 0)
    blk = lax.broadcasted_iota(jnp.int32, (tm, SEL_PAD), 1)
    onehot = jnp.where(t_pos // SEL_BLOCK == blk, 1.0, 0.0).astype(BF16)
    for g in range(NSA_KV_GROUPS):
        sl = slice(g * NSA_HEAD_DIM, (g + 1) * NSA_HEAD_DIM)
        ksa_ref[0, g] = jnp.concatenate([ks[:, sl].astype(BF16), onehot], axis=1)
        kw_ref[0, g] = kw[:, sl].astype(BF16)

    zt = _dot_nt(wt_ref[...], h)
    ones_rows = jnp.where(lax.broadcasted_iota(jnp.int32, (V_AUG - NSA_HEAD_DIM, tm), 0) == 0, 1.0, 0.0).astype(BF16)
    for k in range(2 * NSA_KV_GROUPS):
        vt_ref[0, k * V_AUG:k * V_AUG + NSA_HEAD_DIM] = zt[k * NSA_HEAD_DIM:(k + 1) * NSA_HEAD_DIM].astype(BF16)
        vt_ref[0, k * V_AUG + NSA_HEAD_DIM:(k + 1) * V_AUG] = ones_rows
    gt_ref[0] = jax.nn.sigmoid(zt[_R_GT:])

    k_scale = RET_HEAD_DIM ** -0.5
    zrq = cols(_C_RQ, RET_WIDTH)
    zrk = cols(_C_RK, RET_WIDTH)
    for hh in range(RET_HEADS):
        sl = slice(hh * RET_HEAD_DIM, (hh + 1) * RET_HEAD_DIM)
        zq_h, zk_h = zrq[:, sl], zrk[:, sl]
        rq_ref[0, :, sl] = (zq_h * rc + pltpu.roll(zq_h, RET_HEAD_DIM // 2, 1) * rs).astype(BF16)
        rk_ref[0, :, sl] = ((zk_h * rc + pltpu.roll(zk_h, RET_HEAD_DIM // 2, 1) * rs) * k_scale).astype(BF16)
    rv_ref[0] = cols(_C_RV, RET_WIDTH).astype(BF16)
    rg_ref[0] = cols(_C_RG, RET_WIDTH)


def _odd_proj(x, gpre, w, wr, wt, cs, place, base, cr, sr):
    b, t, d = x.shape
    tm = PROJ_ROWS
    G = NSA_KV_GROUPS

    def row(width):
        return pl.BlockSpec((1, tm, width), lambda bi, j: (bi, j, 0))

    def grp(width):
        return pl.BlockSpec((1, G, tm, width), lambda bi, j: (bi, 0, j, 0))

    def feat(rows):
        return pl.BlockSpec((1, rows, tm), lambda bi, j: (bi, 0, j))

    out_shape = [
        jax.ShapeDtypeStruct((b, t, NSA_WIDTH), BF16),
        jax.ShapeDtypeStruct((b, t, LANES), F32),
        jax.ShapeDtypeStruct((b, t, LANES), F32),
        jax.ShapeDtypeStruct((b, G, t, LANES), BF16),
        jax.ShapeDtypeStruct((b, G, t, NSA_HEAD_DIM), BF16),
        jax.ShapeDtypeStruct((b, VT_ROWS, t), BF16),
        jax.ShapeDtypeStruct((b, G * GATE_ROWS, t), F32),
        jax.ShapeDtypeStruct((b, t, RET_WIDTH), BF16),
        jax.ShapeDtypeStruct((b, t, RET_WIDTH), BF16),
        jax.ShapeDtypeStruct((b, t, RET_WIDTH), BF16),
        jax.ShapeDtypeStruct((b, t, RET_WIDTH), F32),
    ]
    out_specs = [row(NSA_WIDTH), row(LANES), row(LANES), grp(LANES), grp(NSA_HEAD_DIM), feat(VT_ROWS),
                 feat(G * GATE_ROWS), row(RET_WIDTH), row(RET_WIDTH), row(RET_WIDTH), row(RET_WIDTH)]
    return pl.pallas_call(
        _proj_body,
        grid=(b, t // tm),
        in_specs=[row(d), _const_spec((1, d)), _const_spec(w.shape), _const_spec(wr.shape), _const_spec(wt.shape),
                  row(cs.shape[2]),
                  _const_spec(place.shape), _const_spec(base.shape), row(cr.shape[2]), row(sr.shape[2])],
        out_specs=out_specs,
        out_shape=out_shape,
        compiler_params=_cparams(("arbitrary", "arbitrary")),
        name="odd_proj",
    )(x, gpre, w, wr, wt, cs, place, base, cr, sr)


def _compress_body(k_ref, v_ref, kpos_ref, vpos_ref, kw1_ref, vw1_ref, kw2_ref, vw2t_ref, kcm_ref, vct_ref):
    nc = k_ref.shape[1] // CMP_STRIDE
    half = CMP_BLOCK // CMP_STRIDE

    def hidden(x_ref, pos_ref, w1_ref):
        parts = [None] * half
        for off in range(CMP_STRIDE):
            tok = x_ref[0, pl.ds(off, nc, stride=CMP_STRIDE), :]
            for h in range(half):
                l = h * CMP_STRIDE + off
                term = _dot((tok + pos_ref[l:l + 1, :]).astype(BF16), w1_ref[l])
                parts[h] = term if parts[h] is None else parts[h] + term
        pre = parts[0]
        for h in range(1, half):
            pre = pre + pltpu.roll(parts[h], nc - h, 0)
        return jax.nn.gelu(pre).astype(BF16)

    hk = hidden(k_ref, kpos_ref, kw1_ref)
    hv = hidden(v_ref, vpos_ref, vw1_ref)
    for g in range(NSA_KV_GROUPS):
        sl = slice(g * CMP_HIDDEN, (g + 1) * CMP_HIDDEN)
        kcm_ref[0, g] = _dot(hk[:, sl], kw2_ref[...]).astype(BF16)
        vct_ref[0, g] = _dot_nt(vw2t_ref[...], hv[:, sl]).astype(BF16)


def _compress(k, v, kpos, vpos, kw1, vw1, kw2, vw2t):
    b, t, width = k.shape
    nc = t // CMP_STRIDE
    G = NSA_KV_GROUPS
    row = pl.BlockSpec((1, t, width), lambda bi: (bi, 0, 0))
    return pl.pallas_call(
        _compress_body,
        grid=(b,),
        in_specs=[row, row, _const_spec(kpos.shape), _const_spec(vpos.shape), _const_spec(kw1.shape),
                  _const_spec(vw1.shape), _const_spec(kw2.shape), _const_spec(vw2t.shape)],
        out_specs=[pl.BlockSpec((1, G, nc, NSA_HEAD_DIM), lambda bi: (bi, 0, 0, 0)),
                   pl.BlockSpec((1, G, NSA_HEAD_DIM, nc), lambda bi: (bi, 0, 0, 0))],
        out_shape=[jax.ShapeDtypeStruct((b, G, nc, NSA_HEAD_DIM), BF16),
                   jax.ShapeDtypeStruct((b, G, NSA_HEAD_DIM, nc), BF16)],
        compiler_params=_cparams(("arbitrary",)),
        name="compress",
    )(k, v, kpos, vpos, kw1, vw1, kw2, vw2t)


def _nsa_body(q_ref, kcm_ref, vct_ref, c2st_ref, wmask_ref, ksa_ref, kw_ref, vt_ref, gt_ref, o_ref,
              score_ref, rank_ref, sc_ref, sw_ref, pw_ref, sa_ref, sb_ref, pa_ref, pb_ref, lhs_ref,
              m_ref, mta_ref, mtb_ref, mw_ref, acc_ref, *, top_k):
    G, M, Q, dh = NSA_KV_GROUPS, NSA_HPG, Q_BLOCK, NSA_HEAD_DIM
    cols = M * Q
    nc = kcm_ref.shape[2]
    i = pl.program_id(1)
    t0 = i * Q
    tq = t0 + (lax.broadcasted_iota(jnp.int32, (1, cols), 1) & (Q - 1))
    groups = range(G)

    q_heads, q_rows = [], []
    for g in groups:
        qb = q_ref[0, :, g * NSA_GROUP_WIDTH:(g + 1) * NSA_GROUP_WIDTH]
        q_heads.append([qb[:, m * dh:(m + 1) * dh] for m in range(M)])
        q_rows.append(jnp.concatenate(q_heads[g], axis=0))

    n_idx = lax.broadcasted_iota(jnp.int32, (nc, 1), 0)
    cmp_ok = (n_idx * CMP_STRIDE + (CMP_BLOCK - 1) <= tq) & (n_idx < nc - 1)
    j_idx = lax.broadcasted_iota(jnp.int32, (SEL_PAD, Q), 0)
    cur = (t0 + lax.broadcasted_iota(jnp.int32, (SEL_PAD, Q), 1)) // SEL_BLOCK
    forced = (j_idx == 0) | (j_idx == cur) | (j_idx == cur - 1)
    valid_tok = (tq >= CMP_BLOCK - 1).astype(F32)
    o_cmp = []
    for g in groups:
        sc = sc_ref.at[g]
        sc[...] = jnp.where(cmp_ok, _dot_nt(kcm_ref[0, g], q_rows[g]), NEG_INF)
        sc[...] = jnp.exp2(sc[...] - jnp.max(sc[...], axis=0, keepdims=True))
        norm = valid_tok / jnp.sum(sc[...], axis=0, keepdims=True)
        o_cmp.append(_dot(vct_ref[0, g], sc[...].astype(BF16)) * norm)
        p_grp = sc[:, 0:Q] * norm[:, 0:Q]
        for m in range(1, M):
            p_grp = p_grp + sc[:, m * Q:(m + 1) * Q] * norm[:, m * Q:(m + 1) * Q]
        p_hi = p_grp.astype(BF16)
        p_lo = (p_grp - p_hi.astype(F32)).astype(BF16)
        imp = _dot(c2st_ref[...], p_hi) + _dot(c2st_ref[...], p_lo)
        score_ref[g] = jnp.where(j_idx <= cur, jnp.where(forced, FORCE_SCORE, imp), -1.0)

    def v_aug(branch, g, k0, n):
        r0 = (branch * G + g) * V_AUG
        return vt_ref[0, r0:r0 + V_AUG, pl.ds(k0, n)]

    span = WINDOW + Q
    w0 = pl.multiple_of(jnp.maximum(t0 - WINDOW, 0), Q)
    n_wb = span // Q
    diag_b = jnp.minimum(i, n_wb - 1)
    tri_causal = wmask_ref[n_wb - 1, (n_wb - 1) * Q:n_wb * Q, :]
    for g in groups:
        s_win = _dot_nt(kw_ref[0, g, pl.ds(w0, span), :], q_rows[g]) + wmask_ref[diag_b]
        sw_ref[g] = s_win
        mw_ref[g] = jnp.max(s_win, axis=0, keepdims=True)

    n_causal = (t0 + Q) // SEL_BLOCK
    rank_ref[...] = jnp.zeros(rank_ref.shape, jnp.int32)
    row_in_tile = lax.broadcasted_iota(jnp.int32, (SUBLANES, LANES), 0)

    def rank_rows(c0, row_lo, row_hi):
        for g in groups:
            for l0 in range(0, Q, LANES):
                ln = slice(l0, l0 + LANES)
                others = [score_ref[g, k:k + 1, ln] for k in range(c0, c0 + RANK_CHUNK)]
                for r0 in range(row_lo, row_hi, SUBLANES):
                    mine = score_ref[g, r0:r0 + SUBLANES, ln]
                    count = rank_ref[g, r0:r0 + SUBLANES, ln]
                    for k, other in zip(range(c0, c0 + RANK_CHUNK), others):
                        if r0 > k:
                            beats = other >= mine
                        elif r0 + SUBLANES <= k:
                            beats = other > mine
                        else:
                            beats = (other > mine) | ((other == mine) & (k < r0 + row_in_tile))
                        count = count + beats.astype(jnp.int32)
                    rank_ref[g, r0:r0 + SUBLANES, ln] = count

    for c0 in range(0, SEL_PAD, RANK_CHUNK):
        @pl.when(c0 < n_causal)
        def _(c0=c0):
            rank_rows(c0, 0, SEL_PAD // 2)

        @pl.when(n_causal > max(c0, SEL_PAD // 2))
        def _(c0=c0):
            rank_rows(c0, SEL_PAD // 2, SEL_PAD)

    eye_q = _eye(Q)
    for g in groups:
        chosen = jnp.where((rank_ref[g] < top_k) & (score_ref[g] >= 0.0), 1.0, 0.0).astype(BF16)
        chosen_t = _dot_nt(eye_q, chosen)
        bias = jnp.where(chosen_t > 0.5, 0.0, NEG_INF).astype(BF16)
        for m in range(M):
            lhs_ref[g, m * Q:(m + 1) * Q, :] = jnp.concatenate([q_heads[g][m], bias], axis=1)
        m_ref[g] = jnp.full((1, cols), NEG_INF, F32)
        acc_ref[g] = jnp.zeros((V_AUG, cols), F32)

    buf_a = (sa_ref, mta_ref, pa_ref)
    buf_b = (sb_ref, mtb_ref, pb_ref)

    def put_scores(buf, g, kt):
        s_dst, max_dst, _ = buf
        k0 = pl.multiple_of(kt * SEL_KEYS, SEL_KEYS)
        s = _dot_nt(ksa_ref[0, g, pl.ds(k0, SEL_KEYS), :], lhs_ref[g])
        s_dst[g] = s
        max_dst[g] = jnp.max(s, axis=0, keepdims=True)

    def sel_update(buf, g, kt, bias=None):
        src, tile_max, p_dst = buf
        k0 = pl.multiple_of(kt * SEL_KEYS, SEL_KEYS)
        m_i = m_ref[g]
        if bias is None:
            m_new = jnp.maximum(m_i, tile_max[g])
            p_dst[g] = jnp.exp2(src[g] - m_new).astype(BF16)
        else:
            m_new = jnp.maximum(m_i, jnp.max(src[g] + bias, axis=0, keepdims=True))
            p_dst[g] = jnp.exp2(src[g] + bias - m_new).astype(BF16)
        acc_ref[g] = jnp.exp2(m_i - m_new) * acc_ref[g] + _dot(v_aug(0, g, k0, SEL_KEYS), p_dst[g])
        m_ref[g] = m_new

    n_full = t0 // SEL_KEYS
    odd = n_full & 1
    for g in groups:
        put_scores(buf_a, g, 0)

    o_win = []
    for g in groups:
        pw_ref[g] = jnp.exp2(sw_ref[g] - mw_ref[g]).astype(BF16)
        acc_win = _dot(v_aug(1, g, w0, span), pw_ref[g])
        o_win.append(acc_win[0:dh] / acc_win[dh:dh + 1])

    @pl.when(odd == 1)
    def _():
        for g in groups:
            sel_update(buf_a, g, 0)
            put_scores(buf_a, g, 1)

    def sel_pair(pair, _):
        ta = odd + 2 * pair
        for g in groups:
            put_scores(buf_b, g, ta + 1)
        for g in groups:
            sel_update(buf_a, g, ta)
        for g in groups:
            put_scores(buf_a, g, ta + 2)
        for g in groups:
            sel_update(buf_b, g, ta + 1)
        return 0

    lax.fori_loop(0, n_full // 2, sel_pair, 0)
    o_sel = []
    for g in groups:
        sel_update(buf_a, g, n_full, bias=tri_causal)
        o_sel.append(acc_ref[g, 0:dh, :] / acc_ref[g, dh:dh + 1, :])

    for g in groups:
        gt = gt_ref[0, g * GATE_ROWS:(g + 1) * GATE_ROWS, :]
        heads = []
        for m in range(M):
            c = slice(m * Q, (m + 1) * Q)
            r = NSA_BRANCHES * m
            mixed = (o_cmp[g][:, c] * gt[r:r + 1] + o_sel[g][:, c] * gt[r + 1:r + 2]
                     + o_win[g][:, c] * gt[r + 2:r + 3])
            heads.append(mixed.astype(BF16))
        o_ref[0, :, g * NSA_GROUP_WIDTH:(g + 1) * NSA_GROUP_WIDTH] = _dot_nt(
            eye_q, jnp.concatenate(heads, axis=0)).astype(BF16)


def _nsa(q, kcm, vct, c2st, wmask, ksa, kw, vt, gt):
    b, t, _ = q.shape
    G = NSA_KV_GROUPS
    nc = kcm.shape[2]
    n_sel = t // SEL_BLOCK
    assert n_sel <= SEL_PAD and t >= WINDOW + Q_BLOCK and t % SEL_KEYS == 0
    top_k = min(SEL_TOPK, n_sel)
    cols, span = NSA_HPG * Q_BLOCK, WINDOW + Q_BLOCK

    def per_batch(shape):
        nd = len(shape)
        return pl.BlockSpec((1,) + shape, lambda bi, i: (bi,) + (0,) * nd)

    token_rows = pl.BlockSpec((1, Q_BLOCK, NSA_WIDTH), lambda bi, i: (bi, i, 0))
    return pl.pallas_call(
        functools.partial(_nsa_body, top_k=top_k),
        grid=(b, t // Q_BLOCK),
        in_specs=[token_rows, per_batch((G, nc, NSA_HEAD_DIM)), per_batch((G, NSA_HEAD_DIM, nc)),
                  _const_spec(c2st.shape), _const_spec(wmask.shape), per_batch((G, t, LANES)), per_batch((G, t, NSA_HEAD_DIM)),
                  per_batch((vt.shape[1], t)),
                  pl.BlockSpec((1, G * GATE_ROWS, Q_BLOCK), lambda bi, i: (bi, 0, i))],
        out_specs=token_rows,
        out_shape=jax.ShapeDtypeStruct((b, t, NSA_WIDTH), BF16),
        scratch_shapes=[pltpu.VMEM((G, SEL_PAD, Q_BLOCK), F32),
                        pltpu.VMEM((G, SEL_PAD, Q_BLOCK), jnp.int32),
                        pltpu.VMEM((G, nc, cols), F32),
                        pltpu.VMEM((G, span, cols), F32),
                        pltpu.VMEM((G, span, cols), BF16),
                        pltpu.VMEM((G, SEL_KEYS, cols), F32),
                        pltpu.VMEM((G, SEL_KEYS, cols), F32),
                        pltpu.VMEM((G, SEL_KEYS, cols), BF16),
                        pltpu.VMEM((G, SEL_KEYS, cols), BF16),
                        pltpu.VMEM((G, cols, 2 * NSA_HEAD_DIM), BF16),
                        pltpu.VMEM((G, 1, cols), F32),
                        pltpu.VMEM((G, 1, cols), F32),
                        pltpu.VMEM((G, 1, cols), F32),
                        pltpu.VMEM((G, 1, cols), F32),
                        pltpu.VMEM((G, V_AUG, cols), F32)],
        compiler_params=_cparams(("arbitrary", "arbitrary")),
        name="nsa",
    )(q, kcm, vct, c2st, wmask, ksa, kw, vt, gt)


def _ret_body(q_ref, k_ref, v_ref, g_ref, decay_ref, zeta_ref, xi_ref, gchunk_ref, gn_ref, o_ref, state_ref):
    C, d = RET_CHUNK, RET_HEAD_DIM

    @pl.when(pl.program_id(1) == 0)
    def _():
        state_ref[...] = jnp.zeros_like(state_ref)

    n_chunks = q_ref.shape[1] // C
    for h in range(RET_HEADS):
        hc = slice(h * d, (h + 1) * d)
        state = state_ref[h]
        before = []
        for c in range(n_chunks):
            rows = slice(c * C, (c + 1) * C)
            kz = (k_ref[0, rows, hc].astype(F32) * zeta_ref[h]).astype(BF16)
            before.append(state.astype(BF16))
            state = state * gchunk_ref[h] + _dot_tn(kz, v_ref[0, rows, hc])
        state_ref[h] = state
        for c in range(n_chunks):
            rows = slice(c * C, (c + 1) * C)
            q, k, v = q_ref[0, rows, hc], k_ref[0, rows, hc], v_ref[0, rows, hc]
            s = (_dot_nt(q, k) * decay_ref[h]).astype(BF16)
            o = _dot(s, v) + _dot((q.astype(F32) * xi_ref[h]).astype(BF16), before[c])
            mu = jnp.mean(o, axis=-1, keepdims=True)
            var = jnp.mean(jnp.square(o - mu), axis=-1, keepdims=True)
            o = (o - mu) * lax.rsqrt(var + 1e-5) * gn_ref[h]
            gate = g_ref[0, rows, hc]
            o_ref[0, rows, hc] = (gate * jax.nn.sigmoid(gate) * o).astype(BF16)


def _retention(rq, rk, rv, rg, decay, zeta, xi, gchunk, gn):
    b, t, width = rq.shape
    tok = pl.BlockSpec((1, RET_ROWS, width), lambda bi, c: (bi, c, 0))
    return pl.pallas_call(
        _ret_body,
        grid=(b, t // RET_ROWS),
        in_specs=[tok, tok, tok, tok, _const_spec(decay.shape), _const_spec(zeta.shape), _const_spec(xi.shape),
                  _const_spec(gchunk.shape), _const_spec(gn.shape)],
        out_specs=tok,
        out_shape=jax.ShapeDtypeStruct((b, t, width), BF16),
        scratch_shapes=[pltpu.VMEM((RET_HEADS, RET_HEAD_DIM, RET_HEAD_DIM), F32)],
        compiler_params=_cparams(("arbitrary", "arbitrary")),
        name="retention",
    )(rq, rk, rv, rg, decay, zeta, xi, gchunk, gn)


def _oproj_body(x_ref, yc_ref, yd_ref, w_ref, gpost_ref, fpre_ref, fpost_ref, wg_ref, wu_ref, wd_ref, o_ref):
    m = _dot(yc_ref[...], w_ref[0:NSA_WIDTH, :]) + _dot(yd_ref[...], w_ref[NSA_WIDTH:, :])
    o_ref[...] = _ffn_tile(x_ref[...] + _rms(m, gpost_ref[...]), fpre_ref, fpost_ref, wg_ref, wu_ref, wd_ref)


def _odd_out(x2, yc2, yd2, w, gpost, ffn):
    n, d = x2.shape
    tm = FFN_ROWS

    def row(width):
        return pl.BlockSpec((tm, width), lambda i: (i, 0))

    return pl.pallas_call(
        _oproj_body,
        grid=(n // tm,),
        in_specs=[row(d), row(NSA_WIDTH), row(RET_WIDTH), _weight_spec(w), _const_spec((1, d))]
        + _ffn_specs(ffn),
        out_specs=row(d),
        out_shape=jax.ShapeDtypeStruct((n, d), F32),
        compiler_params=_cparams(("arbitrary",)),
        name="odd_out_ffn",
    )(x2, yc2, yd2, _weight_array(w), gpost, *ffn[1:])


def _odd_in_weight(w_in):
    sizes = [NSA_WIDTH] + [NSA_KV_WIDTH] * 6 + [NSA_BRANCHES * NSA_HEADS] + [RET_WIDTH] * 4
    offs = np.concatenate([[0], np.cumsum(sizes)])
    w_in = w_in.astype(BF16)
    q, kc, vc, ks, vs, kw, vw, gt, rq, rk, rv, rg = [w_in[:, offs[n]:offs[n + 1]] for n in range(len(sizes))]
    w = jnp.concatenate([w_in[:, offs[0]:offs[4]], kw], axis=1)
    wr = w_in[:, offs[8]:offs[12]]
    per_group = NSA_BRANCHES * NSA_HPG
    gates = [jnp.pad(gt[:, g * per_group:(g + 1) * per_group], ((0, 0), (0, GATE_ROWS - per_group)))
             for g in range(NSA_KV_GROUPS)]
    wt = jnp.concatenate([vs, vw] + gates, axis=1).T
    return w, wr, wt


def _rope_tables(positions):
    pos = positions.astype(F32)[..., None]
    half = ROPE_DIM // 2
    inv = 1.0 / (ROPE_THETA ** (jnp.arange(0, ROPE_DIM, 2, dtype=F32) / ROPE_DIM))
    ang = pos * inv
    cs = jnp.concatenate([jnp.cos(ang), jnp.sin(ang)], axis=-1)
    place = np.zeros((ROPE_DIM, 3 * LANES), np.float32)
    base = np.zeros((1, 3 * LANES), np.float32)
    for head0 in range(0, LANES, NSA_HEAD_DIM):
        base[0, head0 + ROPE_DIM:head0 + NSA_HEAD_DIM] = 1.0
        for f in range(half):
            place[f, head0 + f] = 1.0
            place[f, head0 + half + f] = 1.0
            place[half + f, LANES + head0 + f] = -1.0
            place[half + f, 2 * LANES + head0 + half + f] = 1.0
    inv_r = 1.0 / (RET_THETA ** (jnp.arange(0, RET_HEAD_DIM, 2, dtype=F32) / RET_HEAD_DIM))
    ang_r = pos * inv_r
    return cs, jnp.asarray(place, dtype=BF16), jnp.asarray(base), jnp.cos(ang_r), jnp.sin(ang_r)


def _retention_tables():
    H, C, d = RET_HEADS, RET_CHUNK, RET_HEAD_DIM
    log_gamma = np.log1p(-np.exp2(-5.0 - np.arange(H, dtype=np.float64)))
    idx = np.arange(C, dtype=np.float64)
    rel = idx[:, None] - idx[None, :]
    decay = np.where(rel >= 0, np.exp(np.maximum(rel, 0.0)[None] * log_gamma[:, None, None]), 0.0)
    zeta = np.exp((C - 1 - idx)[None, :] * log_gamma[:, None])
    xi = np.exp((idx + 1.0)[None, :] * log_gamma[:, None])
    gchunk = np.exp(C * log_gamma)
    zeta_b = np.broadcast_to(zeta[:, :, None], (H, C, d))
    xi_b = np.broadcast_to(xi[:, :, None], (H, C, d))
    gchunk_b = np.broadcast_to(gchunk[:, None, None], (H, 1, d))
    return tuple(jnp.asarray(a, dtype=F32) for a in (decay, zeta_b, xi_b, gchunk_b))


def _cmp_to_sel_t(t_len):
    nc = t_len // CMP_STRIDE
    n_cmp = (t_len - CMP_BLOCK) // CMP_STRIDE + 1
    n_sel = t_len // SEL_BLOCK
    c_start = np.arange(nc) * CMP_STRIDE
    s_start = np.arange(SEL_PAD) * SEL_BLOCK
    hit = ((c_start[None, :] < s_start[:, None] + SEL_BLOCK) & (c_start[None, :] + CMP_BLOCK > s_start[:, None])
           & (np.arange(nc)[None, :] < n_cmp) & (np.arange(SEL_PAD)[:, None] < n_sel))
    return jnp.asarray(hit.astype(np.float32), dtype=BF16)


def _window_masks():
    Q = Q_BLOCK
    n_wb = (WINDOW + Q) // Q
    r = np.arange(Q)[:, None]
    tok = np.arange(NSA_HPG * Q)[None, :] % Q
    causal = np.where(r <= tok, 0.0, NEG_INF)
    leaving = np.where(r > tok, 0.0, NEG_INF)
    masks = np.zeros((n_wb, n_wb * Q, NSA_HPG * Q), np.float32)
    for d in range(n_wb):
        masks[d, d * Q:(d + 1) * Q] = causal
        masks[d, (d + 1) * Q:] = NEG_INF
    masks[n_wb - 1, 0:Q] = leaving
    return jnp.asarray(masks)


def _compress_weights(pos, w1, w2):
    G, dh = NSA_KV_GROUPS, NSA_HEAD_DIM
    pos_rows = jnp.tile(pos, (1, G))
    w1r = w1.astype(BF16).reshape(CMP_BLOCK, dh, CMP_HIDDEN)
    w1_bd = jnp.einsum('ab,lij->laibj', jnp.eye(G, dtype=BF16), w1r).reshape(CMP_BLOCK, G * dh, G * CMP_HIDDEN)
    return pos_rows, w1_bd, w2.astype(BF16)


def _odd_layer(x, positions_tables, ret_tables, c2st, wmask, gpre, w_in, cmp_k_pos, cmp_k_w1, cmp_k_w2,
               cmp_v_pos, cmp_v_w1, cmp_v_w2, gn_g, w_out, gpost, ffn):
    b, t, d = x.shape
    w, wr, wt = _odd_in_weight(w_in)
    (q, kc, vc, ksa, kw, vt, gt, rq, rk, rv, rg) = _odd_proj(x, gpre, w, wr, wt, *positions_tables)
    kpos, kw1, kw2 = _compress_weights(cmp_k_pos, cmp_k_w1, cmp_k_w2)
    vpos, vw1, vw2 = _compress_weights(cmp_v_pos, cmp_v_w1, cmp_v_w2)
    kcm, vct = _compress(kc, vc, kpos, vpos, kw1, vw1, kw2, vw2.T)
    yc = _nsa(q, kcm, vct, c2st, wmask, ksa, kw, vt, gt)
    yd = _retention(rq, rk, rv, rg, *ret_tables, gn_g.reshape(RET_HEADS, 1, RET_HEAD_DIM))
    out = _odd_out(x.reshape(b * t, d), yc.reshape(b * t, NSA_WIDTH), yd.reshape(b * t, RET_WIDTH),
                   w_out, gpost, ffn)
    return out.reshape(b, t, d)


def kernel(x, positions, ln_mix_pre, ln_mix_post, ln_ffn_pre, ln_ffn_post, ffn_w_gate, ffn_w_up, ffn_w_down,
           ev_w_in, ev_pool_w, ev_pool_scale, ev_sgu_ln_g, ev_sgu_ln_b, ev_sgu_w, ev_sgu_b, ev_w_out,
           od_w_in, od_cmp_k_pos, od_cmp_k_w1, od_cmp_k_w2, od_cmp_v_pos, od_cmp_v_w1, od_cmp_v_w2,
           od_ret_gn_g, od_w_out):
    b, t, d = x.shape
    depth = ln_mix_pre.shape[0]
    rope = _rope_tables(positions)
    ret_tables = _retention_tables()
    c2st = _cmp_to_sel_t(t)
    wmask = _window_masks()
    wg_all, wu_all, wd_all = ffn_w_gate.astype(BF16), ffn_w_up.astype(BF16), ffn_w_down.astype(BF16)
    ev_in_all, ev_out_all, od_out_all = ev_w_in.astype(BF16), ev_w_out.astype(BF16), od_w_out.astype(BF16)
    for layer in range(depth):
        gpre = ln_mix_pre[layer].reshape(1, d)
        gpost = ln_mix_post[layer].reshape(1, d)
        ffn = (layer, ln_ffn_pre[layer].reshape(1, d), ln_ffn_post[layer].reshape(1, d), wg_all, wu_all, wd_all)
        if layer % 2 == 0:
            e = layer // 2
            x = _even_layer(x, gpre, (ev_in_all, e), ev_pool_w[e].astype(BF16),
                            ev_pool_scale[e].reshape(1, POOL_WIDTH), ev_sgu_ln_g[e].reshape(1, SGU_WIDTH),
                            ev_sgu_ln_b[e].reshape(1, SGU_WIDTH), ev_sgu_w[e], ev_sgu_b[e].T,
                            (ev_out_all, e), gpost, ffn)
        else:
            o = layer // 2
            x = _odd_layer(x, rope, ret_tables, c2st, wmask, gpre, od_w_in[o], od_cmp_k_pos[o], od_cmp_k_w1[o],
                           od_cmp_k_w2[o], od_cmp_v_pos[o], od_cmp_v_w1[o], od_cmp_v_w2[o], od_ret_gn_g[o],
                           (od_out_all, o), gpost, ffn)
    return x
```

```python
import functools

import numpy as np
import jax
import jax.numpy as jnp
from jax import lax
from jax.experimental import pallas as pl
from jax.experimental.pallas import tpu as pltpu

F32 = jnp.float32
BF16 = jnp.bfloat16

D_MODEL = 1024
POOL_WINDOWS = (2, 4, 8, 16)
POOL_GROUPS = 4
POOL_WIDTH = D_MODEL // 2
POOL_GDIM = POOL_WIDTH // POOL_GROUPS
POOL_HIST = 16
SGU_GROUPS = 4
SGU_WIDTH = D_MODEL // 2
SGU_GDIM = SGU_WIDTH // SGU_GROUPS
SGU_CHUNK = 128
EVEN_IN = POOL_WIDTH + 2 * SGU_WIDTH
NSA_HEADS = 8
NSA_KV_GROUPS = 2
NSA_HPG = NSA_HEADS // NSA_KV_GROUPS
NSA_HEAD_DIM = 64
NSA_WIDTH = NSA_HEADS * NSA_HEAD_DIM
NSA_GROUP_WIDTH = NSA_HPG * NSA_HEAD_DIM
NSA_KV_WIDTH = NSA_KV_GROUPS * NSA_HEAD_DIM
NSA_BRANCHES = 3
GATE_ROWS = 16
CMP_BLOCK = 32
CMP_STRIDE = 16
CMP_HIDDEN = 128
SEL_BLOCK = 64
SEL_TOPK = 16
SEL_PAD = 64
WINDOW = 512
Q_BLOCK = 256
FORCE_SCORE = 1.0e4
ROPE_THETA = 500000.0
ROPE_DIM = NSA_HEAD_DIM // 4
RET_HEADS = 4
RET_HEAD_DIM = 128
RET_WIDTH = RET_HEADS * RET_HEAD_DIM
RET_CHUNK = 128
RET_THETA = 10000.0
ODD_MIX = NSA_WIDTH + RET_WIDTH
NEG_INF = -1.0e30
LOG2_E = 1.4426950408889634
LANES = 128
SUBLANES = 8
BF16_SUBLANES = 2 * SUBLANES
MXU_COLS = 256

VMEM_LIMIT = 56 * 1024 * 1024
FFN_ROWS = 512
EVEN_ROWS = 512
PROJ_ROWS = 512
SEL_KEYS = Q_BLOCK
RANK_CHUNK = 8
RET_ROWS = 512


def _cparams(sem):
    return pltpu.CompilerParams(dimension_semantics=sem, vmem_limit_bytes=VMEM_LIMIT)


def _const_spec(shape):
    nd = len(shape)
    return pl.BlockSpec(shape, lambda *_: (0,) * nd, pipeline_mode=pl.Buffered(1))


def _dot(a, b):
    return jnp.dot(a, b, preferred_element_type=F32)


def _dot_nt(a, b):
    return lax.dot_general(a, b, (((1,), (1,)), ((), ())), preferred_element_type=F32)


def _dot_tn(a, b):
    return lax.dot_general(a, b, (((0,), (0,)), ((), ())), preferred_element_type=F32)


def _rms(x, g, eps=1e-6):
    return x * lax.rsqrt(jnp.mean(x * x, axis=-1, keepdims=True) + eps) * g


def _eye(n):
    return jnp.where(lax.broadcasted_iota(jnp.int32, (n, n), 0) == lax.broadcasted_iota(jnp.int32, (n, n), 1),
                     1.0, 0.0).astype(BF16)


def _ffn_tile(x, gpre_ref, gpost_ref, wg_ref, wu_ref, wd_ref):
    h = _rms(x, gpre_ref[...]).astype(BF16)
    f_total = wg_ref.shape[1]
    fc = MXU_COLS
    assert f_total % fc == 0
    acc = None
    for c in range(f_total // fc):
        gate = _dot(h, wg_ref[:, c * fc:(c + 1) * fc])
        up = _dot(h, wu_ref[:, c * fc:(c + 1) * fc])
        act = (gate * jax.nn.sigmoid(gate) * up).astype(BF16)
        part = _dot(act, wd_ref[c * fc:(c + 1) * fc, :])
        acc = part if acc is None else acc + part
    return x + _rms(acc, gpost_ref[...])


def _layer_spec(stacked_shape, layer):
    nd = len(stacked_shape) - 1
    return pl.BlockSpec((None,) + tuple(stacked_shape[1:]), lambda *_: (layer,) + (0,) * nd,
                        pipeline_mode=pl.Buffered(1))


def _weight_spec(w):
    return _layer_spec(w[0].shape, w[1]) if isinstance(w, tuple) else _const_spec(w.shape)


def _weight_array(w):
    return w[0] if isinstance(w, tuple) else w


def _ffn_specs(ffn):
    layer, fpre, fpost, wg, wu, wd = ffn
    return [_const_spec(fpre.shape), _const_spec(fpost.shape), _layer_spec(wg.shape, layer),
            _layer_spec(wu.shape, layer), _layer_spec(wd.shape, layer)]


def _even_body(x_ref, gpre_ref, win_ref, poolw_ref, pscale_ref, lng_ref, lnb_ref, sguw_ref, sgub_ref,
               wout_ref, gpost_ref, fpre_ref, fpost_ref, wg_ref, wu_ref, wd_ref, o_ref, *levels):
    tt = x_ref.shape[1]
    j = pl.program_id(1)

    @pl.when(j == 0)
    def _():
        for level in levels:
            level[0:POOL_HIST, :] = jnp.zeros((POOL_HIST, level.shape[1]), F32)

    x = x_ref[0]
    h = _rms(x, gpre_ref[...]).astype(BF16)
    z = _dot(h, win_ref[...])
    a = z[:, :POOL_WIDTH]
    u = z[:, POOL_WIDTH:POOL_WIDTH + SGU_WIDTH]
    v = z[:, POOL_WIDTH + SGU_WIDTH:]

    assert POOL_WINDOWS == tuple(2 ** (k + 1) for k in range(POOL_GROUPS))
    t_pos = j * tt + lax.broadcasted_iota(jnp.int32, (tt, 1), 0)
    ya = []
    cur = a
    for gi, w in enumerate(POOL_WINDOWS):
        level = levels[gi]
        level[POOL_HIST:POOL_HIST + tt, :] = cur
        shift = w // 2
        sums = cur + level[POOL_HIST - shift:POOL_HIST - shift + tt, :]
        level[0:POOL_HIST, :] = level[tt:tt + POOL_HIST, :]
        cols = slice(gi * POOL_GDIM, (gi + 1) * POOL_GDIM)
        cnt = jnp.minimum(t_pos + 1, w).astype(F32)
        diff = (sums[:, 0:POOL_GDIM] / cnt - a[:, cols]).astype(BF16)
        ya.append(_dot(diff, poolw_ref[gi]))
        if gi + 1 < POOL_GROUPS:
            cur = sums[:, POOL_GDIM:]
    ya = jnp.concatenate(ya, axis=1) * pscale_ref[...]

    ug = jax.nn.gelu(u)
    vg = jax.nn.gelu(v)
    mu = jnp.mean(vg, axis=-1, keepdims=True)
    var = jnp.mean(jnp.square(vg - mu), axis=-1, keepdims=True)
    vn = ((vg - mu) * lax.rsqrt(var + 1e-5) * lng_ref[...] + lnb_ref[...]).astype(BF16)
    r_i = lax.broadcasted_iota(jnp.int32, (SGU_CHUNK, SGU_CHUNK), 0)
    c_i = lax.broadcasted_iota(jnp.int32, (SGU_CHUNK, SGU_CHUNK), 1)
    yb = []
    for g in range(SGU_GROUPS):
        cols = slice(g * SGU_GDIM, (g + 1) * SGU_GDIM)
        wm = jnp.where(r_i >= c_i, sguw_ref[g], 0.0).astype(BF16)
        bcol = sgub_ref[:, g:g + 1]
        parts = []
        for c in range(tt // SGU_CHUNK):
            rows = slice(c * SGU_CHUNK, (c + 1) * SGU_CHUNK)
            parts.append(ug[rows, cols] * (_dot(wm, vn[rows, cols]) + bcol))
        yb.append(jnp.concatenate(parts, axis=0))
    yb = jnp.concatenate(yb, axis=1)

    m = _dot(ya.astype(BF16), wout_ref[0:POOL_WIDTH, :]) + _dot(yb.astype(BF16), wout_ref[POOL_WIDTH:, :])
    o_ref[0] = _ffn_tile(x + _rms(m, gpost_ref[...]), fpre_ref, fpost_ref, wg_ref, wu_ref, wd_ref)


def _even_layer(x, gpre, win, poolw, pscale, lng, lnb, sguw, sgub_t, wout, gpost, ffn):
    b, t, d = x.shape
    tt = EVEN_ROWS
    row = pl.BlockSpec((1, tt, d), lambda bi, j: (bi, j, 0))
    return pl.pallas_call(
        _even_body,
        grid=(b, t // tt),
        in_specs=[row, _const_spec((1, d)), _weight_spec(win), _const_spec(poolw.shape),
                  _const_spec(pscale.shape), _const_spec(lng.shape), _const_spec(lnb.shape),
                  _const_spec(sguw.shape), _const_spec(sgub_t.shape), _weight_spec(wout),
                  _const_spec((1, d))] + _ffn_specs(ffn),
        out_specs=row,
        out_shape=jax.ShapeDtypeStruct((b, t, d), F32),
        scratch_shapes=[pltpu.VMEM((POOL_HIST + tt, POOL_WIDTH - k * POOL_GDIM), F32) for k in range(POOL_GROUPS)],
        compiler_params=_cparams(("arbitrary", "arbitrary")),
        name="even_layer",
    )(x, gpre, _weight_array(win), poolw, pscale, lng, lnb, sguw, sgub_t, _weight_array(wout), gpost, *ffn[1:])


_C_Q = 0
_C_KC = _C_Q + NSA_WIDTH
_C_VC = _C_KC + NSA_KV_WIDTH
_C_KS = _C_VC + NSA_KV_WIDTH
_C_KW = _C_KS + NSA_KV_WIDTH
_C_RQ = _C_KW + NSA_KV_WIDTH
_C_RK = _C_RQ + RET_WIDTH
_C_RV = _C_RK + RET_WIDTH
_C_RG = _C_RV + RET_WIDTH
ODD_COLS = _C_RG + RET_WIDTH
_R_VS = 0
_R_VW = _R_VS + NSA_KV_WIDTH
_R_GT = _R_VW + NSA_KV_WIDTH
ODD_TROWS = _R_GT + NSA_KV_GROUPS * GATE_ROWS
V_AUG = NSA_HEAD_DIM + BF16_SUBLANES
VT_ROWS = 2 * NSA_KV_GROUPS * V_AUG


def _rope_nsa(z, c, sa, sb):
    half = ROPE_DIM // 2
    return z * c + pltpu.roll(z, LANES - half, 1) * sa + pltpu.roll(z, half, 1) * sb


def _split3(x):
    a = x.astype(BF16)
    r = x - a.astype(F32)
    b = r.astype(BF16)
    return a, b, (r - b.astype(F32)).astype(BF16)


def _proj_body(x_ref, gpre_ref, w_ref, wr_ref, wt_ref, cs_ref, place_ref, base_ref, cr_ref, sr_ref,
               q_ref, kc_ref, vc_ref, ksa_ref, kw_ref, vt_ref, gt_ref, rq_ref, rk_ref, rv_ref, rg_ref):
    tm = x_ref.shape[1]
    j = pl.program_id(1)
    x = x_ref[0]
    h = _rms(x, gpre_ref[...]).astype(BF16)
    tables = base_ref[...]
    for part in _split3(cs_ref[0]):
        tables = tables + _dot(part, place_ref[...])
    nc, nsa, nsb = tables[:, 0:LANES], tables[:, LANES:2 * LANES], tables[:, 2 * LANES:3 * LANES]
    cr, sr = cr_ref[0], sr_ref[0]
    rc = jnp.concatenate([cr, cr], axis=1)
    rs = jnp.concatenate([-sr, sr], axis=1)

    def cols(start, width):
        if start >= _C_RQ:
            return _dot(h, wr_ref[:, start - _C_RQ:start - _C_RQ + width])
        return _dot(h, w_ref[:, start:start + width])

    zq = cols(_C_Q, NSA_WIDTH)
    q_scale = NSA_HEAD_DIM ** -0.5 * LOG2_E
    for s in range(NSA_WIDTH // LANES):
        sl = slice(s * LANES, (s + 1) * LANES)
        q_ref[0, :, sl] = (_rope_nsa(zq[:, sl], nc, nsa, nsb) * q_scale).astype(BF16)

    zkv = cols(_C_KC, _C_RQ - _C_KC)
    kc_ref[0] = _rope_nsa(zkv[:, _C_KC - _C_KC:_C_VC - _C_KC], nc, nsa, nsb)
    vc_ref[0] = zkv[:, _C_VC - _C_KC:_C_KS - _C_KC]
    ks = _rope_nsa(zkv[:, _C_KS - _C_KC:_C_KW - _C_KC], nc, nsa, nsb)
    kw = _rope_nsa(zkv[:, _C_KW - _C_KC:_C_RQ - _C_KC], nc, nsa, nsb)
    t_pos = j * tm + lax.broadcasted_iota(jnp.int32, (tm, SEL_PAD), 0)
    blk = lax.broadcasted_iota(jnp.int32, (tm, SEL_PAD), 1)
    onehot = jnp.where(t_pos // SEL_BLOCK == blk, 1.0, 0.0).astype(BF16)
    for g in range(NSA_KV_GROUPS):
        sl = slice(g * NSA_HEAD_DIM, (g + 1) * NSA_HEAD_DIM)
        ksa_ref[0, g] = jnp.concatenate([ks[:, sl].astype(BF16), onehot], axis=1)
        kw_ref[0, g] = kw[:, sl].astype(BF16)

    zt = _dot_nt(wt_ref[...], h)
    ones_rows = jnp.where(lax.broadcasted_iota(jnp.int32, (V_AUG - NSA_HEAD_DIM, tm), 0) == 0, 1.0, 0.0).astype(BF16)
    for k in range(2 * NSA_KV_GROUPS):
        vt_ref[0, k * V_AUG:k * V_AUG + NSA_HEAD_DIM] = zt[k * NSA_HEAD_DIM:(k + 1) * NSA_HEAD_DIM].astype(BF16)
        vt_ref[0, k * V_AUG + NSA_HEAD_DIM:(k + 1) * V_AUG] = ones_rows
    gt_ref[0] = jax.nn.sigmoid(zt[_R_GT:])

    k_scale = RET_HEAD_DIM ** -0.5
    zrq = cols(_C_RQ, RET_WIDTH)
    zrk = cols(_C_RK, RET_WIDTH)
    for hh in range(RET_HEADS):
        sl = slice(hh * RET_HEAD_DIM, (hh + 1) * RET_HEAD_DIM)
        zq_h, zk_h = zrq[:, sl], zrk[:, sl]
        rq_ref[0, :, sl] = (zq_h * rc + pltpu.roll(zq_h, RET_HEAD_DIM // 2, 1) * rs).astype(BF16)
        rk_ref[0, :, sl] = ((zk_h * rc + pltpu.roll(zk_h, RET_HEAD_DIM // 2, 1) * rs) * k_scale).astype(BF16)
    rv_ref[0] = cols(_C_RV, RET_WIDTH).astype(BF16)
    rg_ref[0] = cols(_C_RG, RET_WIDTH)


def _odd_proj(x, gpre, w, wr, wt, cs, place, base, cr, sr):
    b, t, d = x.shape
    tm = PROJ_ROWS
    G = NSA_KV_GROUPS

    def row(width):
        return pl.BlockSpec((1, tm, width), lambda bi, j: (bi, j, 0))

    def grp(width):
        return pl.BlockSpec((1, G, tm, width), lambda bi, j: (bi, 0, j, 0))

    def feat(rows):
        return pl.BlockSpec((1, rows, tm), lambda bi, j: (bi, 0, j))

    out_shape = [
        jax.ShapeDtypeStruct((b, t, NSA_WIDTH), BF16),
        jax.ShapeDtypeStruct((b, t, LANES), F32),
        jax.ShapeDtypeStruct((b, t, LANES), F32),
        jax.ShapeDtypeStruct((b, G, t, LANES), BF16),
        jax.ShapeDtypeStruct((b, G, t, NSA_HEAD_DIM), BF16),
        jax.ShapeDtypeStruct((b, VT_ROWS, t), BF16),
        jax.ShapeDtypeStruct((b, G * GATE_ROWS, t), F32),
        jax.ShapeDtypeStruct((b, t, RET_WIDTH), BF16),
        jax.ShapeDtypeStruct((b, t, RET_WIDTH), BF16),
        jax.ShapeDtypeStruct((b, t, RET_WIDTH), BF16),
        jax.ShapeDtypeStruct((b, t, RET_WIDTH), F32),
    ]
    out_specs = [row(NSA_WIDTH), row(LANES), row(LANES), grp(LANES), grp(NSA_HEAD_DIM), feat(VT_ROWS),
                 feat(G * GATE_ROWS), row(RET_WIDTH), row(RET_WIDTH), row(RET_WIDTH), row(RET_WIDTH)]
    return pl.pallas_call(
        _proj_body,
        grid=(b, t // tm),
        in_specs=[row(d), _const_spec((1, d)), _const_spec(w.shape), _const_spec(wr.shape), _const_spec(wt.shape),
                  row(cs.shape[2]),
                  _const_spec(place.shape), _const_spec(base.shape), row(cr.shape[2]), row(sr.shape[2])],
        out_specs=out_specs,
        out_shape=out_shape,
        compiler_params=_cparams(("arbitrary", "arbitrary")),
        name="odd_proj",
    )(x, gpre, w, wr, wt, cs, place, base, cr, sr)


def _compress_body(k_ref, v_ref, kpos_ref, vpos_ref, kw1_ref, vw1_ref, kw2_ref, vw2t_ref, kcm_ref, vct_ref):
    nc = k_ref.shape[1] // CMP_STRIDE
    half = CMP_BLOCK // CMP_STRIDE

    def hidden(x_ref, pos_ref, w1_ref):
        parts = [None] * half
        for off in range(CMP_STRIDE):
            tok = x_ref[0, pl.ds(off, nc, stride=CMP_STRIDE), :]
            for h in range(half):
                l = h * CMP_STRIDE + off
                term = _dot((tok + pos_ref[l:l + 1, :]).astype(BF16), w1_ref[l])
                parts[h] = term if parts[h] is None else parts[h] + term
        pre = parts[0]
        for h in range(1, half):
            pre = pre + pltpu.roll(parts[h], nc - h, 0)
        return jax.nn.gelu(pre).astype(BF16)

    hk = hidden(k_ref, kpos_ref, kw1_ref)
    hv = hidden(v_ref, vpos_ref, vw1_ref)
    for g in range(NSA_KV_GROUPS):
        sl = slice(g * CMP_HIDDEN, (g + 1) * CMP_HIDDEN)
        kcm_ref[0, g] = _dot(hk[:, sl], kw2_ref[...]).astype(BF16)
        vct_ref[0, g] = _dot_nt(vw2t_ref[...], hv[:, sl]).astype(BF16)


def _compress(k, v, kpos, vpos, kw1, vw1, kw2, vw2t):
    b, t, width = k.shape
    nc = t // CMP_STRIDE
    G = NSA_KV_GROUPS
    row = pl.BlockSpec((1, t, width), lambda bi: (bi, 0, 0))
    return pl.pallas_call(
        _compress_body,
        grid=(b,),
        in_specs=[row, row, _const_spec(kpos.shape), _const_spec(vpos.shape), _const_spec(kw1.shape),
                  _const_spec(vw1.shape), _const_spec(kw2.shape), _const_spec(vw2t.shape)],
        out_specs=[pl.BlockSpec((1, G, nc, NSA_HEAD_DIM), lambda bi: (bi, 0, 0, 0)),
                   pl.BlockSpec((1, G, NSA_HEAD_DIM, nc), lambda bi: (bi, 0, 0, 0))],
        out_shape=[jax.ShapeDtypeStruct((b, G, nc, NSA_HEAD_DIM), BF16),
                   jax.ShapeDtypeStruct((b, G, NSA_HEAD_DIM, nc), BF16)],
        compiler_params=_cparams(("arbitrary",)),
        name="compress",
    )(k, v, kpos, vpos, kw1, vw1, kw2, vw2t)


def _nsa_body(q_ref, kcm_ref, vct_ref, c2st_ref, wmask_ref, ksa_ref, kw_ref, vt_ref, gt_ref,
              rq_ref, rk_ref, rv_ref, rg_ref, decay_ref, zeta_ref, xi_ref, gchunk_ref, gn_ref, o_ref, yd_ref,
              score_ref, rank_ref, sc_ref, sw_ref, pw_ref, sa_ref, sb_ref, pa_ref, pb_ref, lhs_ref,
              m_ref, mta_ref, mtb_ref, mw_ref, acc_ref, state_ref, *, top_k):
    _ret_body(rq_ref, rk_ref, rv_ref, rg_ref, decay_ref, zeta_ref, xi_ref, gchunk_ref, gn_ref, yd_ref, state_ref)
    G, M, Q, dh = NSA_KV_GROUPS, NSA_HPG, Q_BLOCK, NSA_HEAD_DIM
    cols = M * Q
    nc = kcm_ref.shape[2]
    i = pl.program_id(1)
    t0 = i * Q
    tq = t0 + (lax.broadcasted_iota(jnp.int32, (1, cols), 1) & (Q - 1))
    groups = range(G)

    q_heads, q_rows = [], []
    for g in groups:
        qb = q_ref[0, :, g * NSA_GROUP_WIDTH:(g + 1) * NSA_GROUP_WIDTH]
        q_heads.append([qb[:, m * dh:(m + 1) * dh] for m in range(M)])
        q_rows.append(jnp.concatenate(q_heads[g], axis=0))

    n_idx = lax.broadcasted_iota(jnp.int32, (nc, 1), 0)
    cmp_ok = (n_idx * CMP_STRIDE + (CMP_BLOCK - 1) <= tq) & (n_idx < nc - 1)
    j_idx = lax.broadcasted_iota(jnp.int32, (SEL_PAD, Q), 0)
    cur = (t0 + lax.broadcasted_iota(jnp.int32, (SEL_PAD, Q), 1)) // SEL_BLOCK
    forced = (j_idx == 0) | (j_idx == cur) | (j_idx == cur - 1)
    valid_tok = (tq >= CMP_BLOCK - 1).astype(F32)
    o_cmp = []
    for g in groups:
        sc = sc_ref.at[g]
        sc[...] = jnp.where(cmp_ok, _dot_nt(kcm_ref[0, g], q_rows[g]), NEG_INF)
        sc[...] = jnp.exp2(sc[...] - jnp.max(sc[...], axis=0, keepdims=True))
        norm = valid_tok / jnp.sum(sc[...], axis=0, keepdims=True)
        o_cmp.append(_dot(vct_ref[0, g], sc[...].astype(BF16)) * norm)
        p_grp = sc[:, 0:Q] * norm[:, 0:Q]
        for m in range(1, M):
            p_grp = p_grp + sc[:, m * Q:(m + 1) * Q] * norm[:, m * Q:(m + 1) * Q]
        p_hi = p_grp.astype(BF16)
        p_lo = (p_grp - p_hi.astype(F32)).astype(BF16)
        imp = _dot(c2st_ref[...], p_hi) + _dot(c2st_ref[...], p_lo)
        score_ref[g] = jnp.where(j_idx <= cur, jnp.where(forced, FORCE_SCORE, imp), -1.0)

    def v_aug(branch, g, k0, n):
        r0 = (branch * G + g) * V_AUG
        return vt_ref[0, r0:r0 + V_AUG, pl.ds(k0, n)]

    span = WINDOW + Q
    w0 = pl.multiple_of(jnp.maximum(t0 - WINDOW, 0), Q)
    n_wb = span // Q
    diag_b = jnp.minimum(i, n_wb - 1)
    tri_causal = wmask_ref[n_wb - 1, (n_wb - 1) * Q:n_wb * Q, :]
    for g in groups:
        s_win = _dot_nt(kw_ref[0, g, pl.ds(w0, span), :], q_rows[g]) + wmask_ref[diag_b]
        sw_ref[g] = s_win
        mw_ref[g] = jnp.max(s_win, axis=0, keepdims=True)

    n_causal = (t0 + Q) // SEL_BLOCK
    rank_ref[...] = jnp.zeros(rank_ref.shape, jnp.int32)
    row_in_tile = lax.broadcasted_iota(jnp.int32, (SUBLANES, LANES), 0)

    def rank_rows(c0, row_lo, row_hi):
        for g in groups:
            for l0 in range(0, Q, LANES):
                ln = slice(l0, l0 + LANES)
                others = [score_ref[g, k:k + 1, ln] for k in range(c0, c0 + RANK_CHUNK)]
                for r0 in range(row_lo, row_hi, SUBLANES):
                    mine = score_ref[g, r0:r0 + SUBLANES, ln]
                    count = rank_ref[g, r0:r0 + SUBLANES, ln]
                    for k, other in zip(range(c0, c0 + RANK_CHUNK), others):
                        if r0 > k:
                            beats = other >= mine
                        elif r0 + SUBLANES <= k:
                            beats = other > mine
                        else:
                            beats = (other > mine) | ((other == mine) & (k < r0 + row_in_tile))
                        count = count + beats.astype(jnp.int32)
                    rank_ref[g, r0:r0 + SUBLANES, ln] = count

    for c0 in range(0, SEL_PAD, RANK_CHUNK):
        @pl.when(c0 < n_causal)
        def _(c0=c0):
            rank_rows(c0, 0, SEL_PAD // 2)

        @pl.when(n_causal > max(c0, SEL_PAD // 2))
        def _(c0=c0):
            rank_rows(c0, SEL_PAD // 2, SEL_PAD)

    eye_q = _eye(Q)
    for g in groups:
        chosen = jnp.where((rank_ref[g] < top_k) & (score_ref[g] >= 0.0), 1.0, 0.0).astype(BF16)
        chosen_t = _dot_nt(eye_q, chosen)
        bias = jnp.where(chosen_t > 0.5, 0.0, NEG_INF).astype(BF16)
        for m in range(M):
            lhs_ref[g, m * Q:(m + 1) * Q, :] = jnp.concatenate([q_heads[g][m], bias], axis=1)
        m_ref[g] = jnp.full((1, cols), NEG_INF, F32)
        acc_ref[g] = jnp.zeros((V_AUG, cols), F32)

    buf_a = (sa_ref, mta_ref, pa_ref)
    buf_b = (sb_ref, mtb_ref, pb_ref)

    def put_scores(buf, g, kt):
        s_dst, max_dst, _ = buf
        k0 = pl.multiple_of(kt * SEL_KEYS, SEL_KEYS)
        s = _dot_nt(ksa_ref[0, g, pl.ds(k0, SEL_KEYS), :], lhs_ref[g])
        s_dst[g] = s
        max_dst[g] = jnp.max(s, axis=0, keepdims=True)

    def sel_update(buf, g, kt, bias=None):
        src, tile_max, p_dst = buf
        k0 = pl.multiple_of(kt * SEL_KEYS, SEL_KEYS)
        m_i = m_ref[g]
        if bias is None:
            m_new = jnp.maximum(m_i, tile_max[g])
            p_dst[g] = jnp.exp2(src[g] - m_new).astype(BF16)
        else:
            m_new = jnp.maximum(m_i, jnp.max(src[g] + bias, axis=0, keepdims=True))
            p_dst[g] = jnp.exp2(src[g] + bias - m_new).astype(BF16)
        acc_ref[g] = jnp.exp2(m_i - m_new) * acc_ref[g] + _dot(v_aug(0, g, k0, SEL_KEYS), p_dst[g])
        m_ref[g] = m_new

    n_full = t0 // SEL_KEYS
    odd = n_full & 1
    for g in groups:
        put_scores(buf_a, g, 0)

    o_win = []
    for g in groups:
        pw_ref[g] = jnp.exp2(sw_ref[g] - mw_ref[g]).astype(BF16)
        acc_win = _dot(v_aug(1, g, w0, span), pw_ref[g])
        o_win.append(acc_win[0:dh] / acc_win[dh:dh + 1])

    @pl.when(odd == 1)
    def _():
        for g in groups:
            sel_update(buf_a, g, 0)
            put_scores(buf_a, g, 1)

    def sel_pair(pair, _):
        ta = odd + 2 * pair
        for g in groups:
            put_scores(buf_b, g, ta + 1)
        for g in groups:
            sel_update(buf_a, g, ta)
        for g in groups:
            put_scores(buf_a, g, ta + 2)
        for g in groups:
            sel_update(buf_b, g, ta + 1)
        return 0

    lax.fori_loop(0, n_full // 2, sel_pair, 0)
    o_sel = []
    for g in groups:
        sel_update(buf_a, g, n_full, bias=tri_causal)
        o_sel.append(acc_ref[g, 0:dh, :] / acc_ref[g, dh:dh + 1, :])

    for g in groups:
        gt = gt_ref[0, g * GATE_ROWS:(g + 1) * GATE_ROWS, :]
        heads = []
        for m in range(M):
            c = slice(m * Q, (m + 1) * Q)
            r = NSA_BRANCHES * m
            mixed = (o_cmp[g][:, c] * gt[r:r + 1] + o_sel[g][:, c] * gt[r + 1:r + 2]
                     + o_win[g][:, c] * gt[r + 2:r + 3])
            heads.append(mixed.astype(BF16))
        o_ref[0, :, g * NSA_GROUP_WIDTH:(g + 1) * NSA_GROUP_WIDTH] = _dot_nt(
            eye_q, jnp.concatenate(heads, axis=0)).astype(BF16)


def _nsa(q, kcm, vct, c2st, wmask, ksa, kw, vt, gt, ret):
    rq, rk, rv, rg, decay, zeta, xi, gchunk, gn = ret
    b, t, _ = q.shape
    G = NSA_KV_GROUPS
    nc = kcm.shape[2]
    n_sel = t // SEL_BLOCK
    assert n_sel <= SEL_PAD and t >= WINDOW + Q_BLOCK and t % SEL_KEYS == 0
    top_k = min(SEL_TOPK, n_sel)
    cols, span = NSA_HPG * Q_BLOCK, WINDOW + Q_BLOCK

    def per_batch(shape):
        nd = len(shape)
        return pl.BlockSpec((1,) + shape, lambda bi, i: (bi,) + (0,) * nd)

    token_rows = pl.BlockSpec((1, Q_BLOCK, NSA_WIDTH), lambda bi, i: (bi, i, 0))
    ret_rows = pl.BlockSpec((1, Q_BLOCK, RET_WIDTH), lambda bi, i: (bi, i, 0))
    return pl.pallas_call(
        functools.partial(_nsa_body, top_k=top_k),
        grid=(b, t // Q_BLOCK),
        in_specs=[token_rows, per_batch((G, nc, NSA_HEAD_DIM)), per_batch((G, NSA_HEAD_DIM, nc)),
                  _const_spec(c2st.shape), _const_spec(wmask.shape), per_batch((G, t, LANES)), per_batch((G, t, NSA_HEAD_DIM)),
                  per_batch((vt.shape[1], t)),
                  pl.BlockSpec((1, G * GATE_ROWS, Q_BLOCK), lambda bi, i: (bi, 0, i)),
                  ret_rows, ret_rows, ret_rows, ret_rows, _const_spec(decay.shape), _const_spec(zeta.shape),
                  _const_spec(xi.shape), _const_spec(gchunk.shape), _const_spec(gn.shape)],
        out_specs=[token_rows, ret_rows],
        out_shape=[jax.ShapeDtypeStruct((b, t, NSA_WIDTH), BF16), jax.ShapeDtypeStruct((b, t, RET_WIDTH), BF16)],
        scratch_shapes=[pltpu.VMEM((G, SEL_PAD, Q_BLOCK), F32),
                        pltpu.VMEM((G, SEL_PAD, Q_BLOCK), jnp.int32),
                        pltpu.VMEM((G, nc, cols), F32),
                        pltpu.VMEM((G, span, cols), F32),
                        pltpu.VMEM((G, span, cols), BF16),
                        pltpu.VMEM((G, SEL_KEYS, cols), F32),
                        pltpu.VMEM((G, SEL_KEYS, cols), F32),
                        pltpu.VMEM((G, SEL_KEYS, cols), BF16),
                        pltpu.VMEM((G, SEL_KEYS, cols), BF16),
                        pltpu.VMEM((G, cols, 2 * NSA_HEAD_DIM), BF16),
                        pltpu.VMEM((G, 1, cols), F32),
                        pltpu.VMEM((G, 1, cols), F32),
                        pltpu.VMEM((G, 1, cols), F32),
                        pltpu.VMEM((G, 1, cols), F32),
                        pltpu.VMEM((G, V_AUG, cols), F32),
                        pltpu.VMEM((RET_HEADS, RET_HEAD_DIM, RET_HEAD_DIM), F32)],
        compiler_params=_cparams(("arbitrary", "arbitrary")),
        name="nsa_retention",
    )(q, kcm, vct, c2st, wmask, ksa, kw, vt, gt, rq, rk, rv, rg, decay, zeta, xi, gchunk, gn)


def _ret_body(q_ref, k_ref, v_ref, g_ref, decay_ref, zeta_ref, xi_ref, gchunk_ref, gn_ref, o_ref, state_ref):
    C, d = RET_CHUNK, RET_HEAD_DIM

    @pl.when(pl.program_id(1) == 0)
    def _():
        state_ref[...] = jnp.zeros_like(state_ref)

    n_chunks = q_ref.shape[1] // C
    for h in range(RET_HEADS):
        hc = slice(h * d, (h + 1) * d)
        state = state_ref[h]
        before = []
        for c in range(n_chunks):
            rows = slice(c * C, (c + 1) * C)
            kz = (k_ref[0, rows, hc].astype(F32) * zeta_ref[h]).astype(BF16)
            before.append(state.astype(BF16))
            state = state * gchunk_ref[h] + _dot_tn(kz, v_ref[0, rows, hc])
        state_ref[h] = state
        for c in range(n_chunks):
            rows = slice(c * C, (c + 1) * C)
            q, k, v = q_ref[0, rows, hc], k_ref[0, rows, hc], v_ref[0, rows, hc]
            s = (_dot_nt(q, k) * decay_ref[h]).astype(BF16)
            o = _dot(s, v) + _dot((q.astype(F32) * xi_ref[h]).astype(BF16), before[c])
            mu = jnp.mean(o, axis=-1, keepdims=True)
            var = jnp.mean(jnp.square(o - mu), axis=-1, keepdims=True)
            o = (o - mu) * lax.rsqrt(var + 1e-5) * gn_ref[h]
            gate = g_ref[0, rows, hc]
            o_ref[0, rows, hc] = (gate * jax.nn.sigmoid(gate) * o).astype(BF16)


def _oproj_body(x_ref, yc_ref, yd_ref, w_ref, gpost_ref, fpre_ref, fpost_ref, wg_ref, wu_ref, wd_ref, o_ref):
    m = _dot(yc_ref[...], w_ref[0:NSA_WIDTH, :]) + _dot(yd_ref[...], w_ref[NSA_WIDTH:, :])
    o_ref[...] = _ffn_tile(x_ref[...] + _rms(m, gpost_ref[...]), fpre_ref, fpost_ref, wg_ref, wu_ref, wd_ref)


def _odd_out(x2, yc2, yd2, w, gpost, ffn):
    n, d = x2.shape
    tm = FFN_ROWS

    def row(width):
        return pl.BlockSpec((tm, width), lambda i: (i, 0))

    return pl.pallas_call(
        _oproj_body,
        grid=(n // tm,),
        in_specs=[row(d), row(NSA_WIDTH), row(RET_WIDTH), _weight_spec(w), _const_spec((1, d))]
        + _ffn_specs(ffn),
        out_specs=row(d),
        out_shape=jax.ShapeDtypeStruct((n, d), F32),
        compiler_params=_cparams(("arbitrary",)),
        name="odd_out_ffn",
    )(x2, yc2, yd2, _weight_array(w), gpost, *ffn[1:])


def _odd_in_weight(w_in):
    sizes = [NSA_WIDTH] + [NSA_KV_WIDTH] * 6 + [NSA_BRANCHES * NSA_HEADS] + [RET_WIDTH] * 4
    offs = np.concatenate([[0], np.cumsum(sizes)])
    w_in = w_in.astype(BF16)
    q, kc, vc, ks, vs, kw, vw, gt, rq, rk, rv, rg = [w_in[:, offs[n]:offs[n + 1]] for n in range(len(sizes))]
    w = jnp.concatenate([w_in[:, offs[0]:offs[4]], kw], axis=1)
    wr = w_in[:, offs[8]:offs[12]]
    per_group = NSA_BRANCHES * NSA_HPG
    gates = [jnp.pad(gt[:, g * per_group:(g + 1) * per_group], ((0, 0), (0, GATE_ROWS - per_group)))
             for g in range(NSA_KV_GROUPS)]
    wt = jnp.concatenate([vs, vw] + gates, axis=1).T
    return w, wr, wt


def _rope_tables(positions):
    pos = positions.astype(F32)[..., None]
    half = ROPE_DIM // 2
    inv = 1.0 / (ROPE_THETA ** (jnp.arange(0, ROPE_DIM, 2, dtype=F32) / ROPE_DIM))
    ang = pos * inv
    cs = jnp.concatenate([jnp.cos(ang), jnp.sin(ang)], axis=-1)
    place = np.zeros((ROPE_DIM, 3 * LANES), np.float32)
    base = np.zeros((1, 3 * LANES), np.float32)
    for head0 in range(0, LANES, NSA_HEAD_DIM):
        base[0, head0 + ROPE_DIM:head0 + NSA_HEAD_DIM] = 1.0
        for f in range(half):
            place[f, head0 + f] = 1.0
            place[f, head0 + half + f] = 1.0
            place[half + f, LANES + head0 + f] = -1.0
            place[half + f, 2 * LANES + head0 + half + f] = 1.0
    inv_r = 1.0 / (RET_THETA ** (jnp.arange(0, RET_HEAD_DIM, 2, dtype=F32) / RET_HEAD_DIM))
    ang_r = pos * inv_r
    return cs, jnp.asarray(place, dtype=BF16), jnp.asarray(base), jnp.cos(ang_r), jnp.sin(ang_r)


def _retention_tables():
    H, C, d = RET_HEADS, RET_CHUNK, RET_HEAD_DIM
    log_gamma = np.log1p(-np.exp2(-5.0 - np.arange(H, dtype=np.float64)))
    idx = np.arange(C, dtype=np.float64)
    rel = idx[:, None] - idx[None, :]
    decay = np.where(rel >= 0, np.exp(np.maximum(rel, 0.0)[None] * log_gamma[:, None, None]), 0.0)
    zeta = np.exp((C - 1 - idx)[None, :] * log_gamma[:, None])
    xi = np.exp((idx + 1.0)[None, :] * log_gamma[:, None])
    gchunk = np.exp(C * log_gamma)
    zeta_b = np.broadcast_to(zeta[:, :, None], (H, C, d))
    xi_b = np.broadcast_to(xi[:, :, None], (H, C, d))
    gchunk_b = np.broadcast_to(gchunk[:, None, None], (H, 1, d))
    return tuple(jnp.asarray(a, dtype=F32) for a in (decay, zeta_b, xi_b, gchunk_b))


def _cmp_to_sel_t(t_len):
    nc = t_len // CMP_STRIDE
    n_cmp = (t_len - CMP_BLOCK) // CMP_STRIDE + 1
    n_sel = t_len // SEL_BLOCK
    c_start = np.arange(nc) * CMP_STRIDE
    s_start = np.arange(SEL_PAD) * SEL_BLOCK
    hit = ((c_start[None, :] < s_start[:, None] + SEL_BLOCK) & (c_start[None, :] + CMP_BLOCK > s_start[:, None])
           & (np.arange(nc)[None, :] < n_cmp) & (np.arange(SEL_PAD)[:, None] < n_sel))
    return jnp.asarray(hit.astype(np.float32), dtype=BF16)


def _window_masks():
    Q = Q_BLOCK
    n_wb = (WINDOW + Q) // Q
    r = np.arange(Q)[:, None]
    tok = np.arange(NSA_HPG * Q)[None, :] % Q
    causal = np.where(r <= tok, 0.0, NEG_INF)
    leaving = np.where(r > tok, 0.0, NEG_INF)
    masks = np.zeros((n_wb, n_wb * Q, NSA_HPG * Q), np.float32)
    for d in range(n_wb):
        masks[d, d * Q:(d + 1) * Q] = causal
        masks[d, (d + 1) * Q:] = NEG_INF
    masks[n_wb - 1, 0:Q] = leaving
    return jnp.asarray(masks)


def _compress_weights(pos, w1, w2):
    G, dh = NSA_KV_GROUPS, NSA_HEAD_DIM
    pos_rows = jnp.tile(pos, (1, G))
    w1r = w1.astype(BF16).reshape(CMP_BLOCK, dh, CMP_HIDDEN)
    w1_bd = jnp.einsum('ab,lij->laibj', jnp.eye(G, dtype=BF16), w1r).reshape(CMP_BLOCK, G * dh, G * CMP_HIDDEN)
    return pos_rows, w1_bd, w2.astype(BF16)


def _odd_layer(x, positions_tables, ret_tables, c2st, wmask, gpre, w_in, cmp_k_pos, cmp_k_w1, cmp_k_w2,
               cmp_v_pos, cmp_v_w1, cmp_v_w2, gn_g, w_out, gpost, ffn):
    b, t, d = x.shape
    w, wr, wt = _odd_in_weight(w_in)
    (q, kc, vc, ksa, kw, vt, gt, rq, rk, rv, rg) = _odd_proj(x, gpre, w, wr, wt, *positions_tables)
    kpos, kw1, kw2 = _compress_weights(cmp_k_pos, cmp_k_w1, cmp_k_w2)
    vpos, vw1, vw2 = _compress_weights(cmp_v_pos, cmp_v_w1, cmp_v_w2)
    kcm, vct = _compress(kc, vc, kpos, vpos, kw1, vw1, kw2, vw2.T)
    yc, yd = _nsa(q, kcm, vct, c2st, wmask, ksa, kw, vt, gt,
                  (rq, rk, rv, rg, *ret_tables, gn_g.reshape(RET_HEADS, 1, RET_HEAD_DIM)))
    out = _odd_out(x.reshape(b * t, d), yc.reshape(b * t, NSA_WIDTH), yd.reshape(b * t, RET_WIDTH),
                   w_out, gpost, ffn)
    return out.reshape(b, t, d)


def kernel(x, positions, ln_mix_pre, ln_mix_post, ln_ffn_pre, ln_ffn_post, ffn_w_gate, ffn_w_up, ffn_w_down,
           ev_w_in, ev_pool_w, ev_pool_scale, ev_sgu_ln_g, ev_sgu_ln_b, ev_sgu_w, ev_sgu_b, ev_w_out,
           od_w_in, od_cmp_k_pos, od_cmp_k_w1, od_cmp_k_w2, od_cmp_v_pos, od_cmp_v_w1, od_cmp_v_w2,
           od_ret_gn_g, od_w_out):
    b, t, d = x.shape
    depth = ln_mix_pre.shape[0]
    rope = _rope_tables(positions)
    ret_tables = _retention_tables()
    c2st = _cmp_to_sel_t(t)
    wmask = _window_masks()
    wg_all, wu_all, wd_all = ffn_w_gate.astype(BF16), ffn_w_up.astype(BF16), ffn_w_down.astype(BF16)
    ev_in_all, ev_out_all, od_out_all = ev_w_in.astype(BF16), ev_w_out.astype(BF16), od_w_out.astype(BF16)
    for layer in range(depth):
        gpre = ln_mix_pre[layer].reshape(1, d)
        gpost = ln_mix_post[layer].reshape(1, d)
        ffn = (layer, ln_ffn_pre[layer].reshape(1, d), ln_ffn_post[layer].reshape(1, d), wg_all, wu_all, wd_all)
        if layer % 2 == 0:
            e = layer // 2
            x = _even_layer(x, gpre, (ev_in_all, e), ev_pool_w[e].astype(BF16),
                            ev_pool_scale[e].reshape(1, POOL_WIDTH), ev_sgu_ln_g[e].reshape(1, SGU_WIDTH),
                            ev_sgu_ln_b[e].reshape(1, SGU_WIDTH), ev_sgu_w[e], ev_sgu_b[e].T,
                            (ev_out_all, e), gpost, ffn)
        else:
            o = layer // 2
            x = _odd_layer(x, rope, ret_tables, c2st, wmask, gpre, od_w_in[o], od_cmp_k_pos[o], od_cmp_k_w1[o],
                           od_cmp_k_w2[o], od_cmp_v_pos[o], od_cmp_v_w1[o], od_cmp_v_w2[o], od_ret_gn_g[o],
                           (od_out_all, o), gpost, ffn)
    return x
```
